```python
import jax, jax.numpy as jnp
from jax import lax
import numpy as np

D_MODEL = 1024
BATCH = 32
SEQ = 2048
DEPTH = 1

CONV_WIDTH = D_MODEL
CONV_K = 3
RWKV_HEAD = 64
RWKV_HEADS = D_MODEL // RWKV_HEAD
RWKV_WIDTH = RWKV_HEADS * RWKV_HEAD
DECAY_LORA = 64
AAA_LORA = 64
GATE_LORA = 160
N_EXPERTS = 32
TOP_K = 4
D_FF = D_MODEL
SWIGLU_ALPHA = 1.702
SWIGLU_LIMIT = 7.0
MOE_BLOCK = 256
NORM_EPS = 1e-5
GN_EPS = 64e-5
CONV_COLS = 3 * CONV_WIDTH
RWKV_COLS = 3 * RWKV_WIDTH + DECAY_LORA + AAA_LORA + GATE_LORA
GATE_COLS = 2 * D_MODEL
PROJ_COLS = CONV_COLS + RWKV_COLS + GATE_COLS

kernel_name = "hybrid_conv_rwkv7_moe_block"


def rmsnorm(x, g):
    xf = x.astype(jnp.float32)
    y = xf * lax.rsqrt(jnp.mean(xf * xf, axis=-1, keepdims=True) + NORM_EPS)
    return (y * g.astype(jnp.float32)).astype(x.dtype)


def short_conv_mixer(p, conv_w):
    gb, gc, xin = jnp.split(p, 3, axis=-1)
    u = gc * xin
    s = u.shape[1]
    up = jnp.pad(u, ((0, 0), (CONV_K - 1, 0), (0, 0)))
    conv = sum(conv_w[j] * up[:, j:j + s] for j in range(CONV_K))
    return gb * conv


def rwkv7_time_mix(p, mu, w0, w2, a0, a2, g2, k_k, k_a, r_k, ln_g, ln_b):
    f32 = jnp.float32
    bsz, s, _ = p.shape
    p = p.astype(f32)
    p_prev = jnp.pad(p, ((0, 0), (1, 0), (0, 0)))[:, :-1]
    p = p + (p_prev - p) * mu
    r, k, v, wd, ad, gd = jnp.split(
        p, [RWKV_WIDTH, 2 * RWKV_WIDTH, 3 * RWKV_WIDTH,
            3 * RWKV_WIDTH + DECAY_LORA, 3 * RWKV_WIDTH + DECAY_LORA + AAA_LORA], axis=-1)
    w = -jax.nn.softplus(-(w0 + jnp.tanh(wd) @ w2)) - 0.5
    decay = jnp.exp(-jnp.exp(w))
    a = jax.nn.sigmoid(a0 + ad @ a2)
    g = jax.nn.sigmoid(gd) @ g2
    heads = lambda t: t.reshape(bsz, s, RWKV_HEADS, RWKV_HEAD)
    kk = heads(k * k_k)
    kk = kk / jnp.maximum(jnp.sqrt(jnp.sum(kk * kk, axis=-1, keepdims=True)), 1e-12)
    k = k * (1.0 + (a - 1.0) * k_a)
    r_h, k_h, v_h, w_h, a_h = heads(r), heads(k), heads(v), heads(decay), heads(a)

    def step(state, inp):
        r_t, w_t, k_t, v_t, kk_t, a_t = inp
        sa = jnp.einsum('bhij,bhj->bhi', state, kk_t)
        state = (state * w_t[:, :, None, :]
                 + jnp.einsum('bhi,bhj->bhij', -sa, kk_t * a_t)
                 + jnp.einsum('bhi,bhj->bhij', v_t, k_t))
        return state, jnp.einsum('bhij,bhj->bhi', state, r_t)

    xs = tuple(jnp.moveaxis(t, 1, 0) for t in (r_h, w_h, k_h, v_h, kk, a_h))
    s0 = jnp.zeros((bsz, RWKV_HEADS, RWKV_HEAD, RWKV_HEAD), f32)
    _, y = lax.scan(step, s0, xs)
    y = jnp.moveaxis(y, 0, 1)
    mean = jnp.mean(y, axis=-1, keepdims=True)
    var = jnp.mean(jnp.square(y - mean), axis=-1, keepdims=True)
    y = ((y - mean) * lax.rsqrt(var + GN_EPS)).reshape(bsz, s, RWKV_WIDTH) * ln_g + ln_b
    bonus = (jnp.sum(r_h * k_h * r_k, axis=-1, keepdims=True) * v_h).reshape(bsz, s, RWKV_WIDTH)
    return (y + bonus) * g


def clamped_swiglu(u):
    glu, lin = jnp.split(u, 2, axis=-1)
    glu = jnp.minimum(glu, SWIGLU_LIMIT)
    lin = jnp.clip(lin, -SWIGLU_LIMIT, SWIGLU_LIMIT)
    return glu * jax.nn.sigmoid(SWIGLU_ALPHA * glu) * (lin + 1.0)


def moe_ffn(h, router_w, router_b, w1, b1, w2, b2):
    t, d = h.shape
    logits = (h @ router_w + router_b).astype(jnp.float32)
    top_val, top_idx = lax.top_k(logits, TOP_K)
    gates = jax.nn.softmax(top_val, axis=-1).astype(h.dtype)
    n_assign = t * TOP_K
    flat_e = top_idx.reshape(-1)
    order = jnp.argsort(flat_e)
    sorted_e = flat_e[order]
    counts = jnp.bincount(flat_e, length=N_EXPERTS).astype(jnp.int32)
    padded = (counts + MOE_BLOCK - 1) // MOE_BLOCK * MOE_BLOCK
    start = jnp.cumsum(counts) - counts
    padded_end = jnp.cumsum(padded)
    padded_start = padded_end - padded
    slot_sorted = (padded_start[sorted_e] + jnp.arange(n_assign, dtype=jnp.int32) - start[sorted_e]).astype(jnp.int32)
    slot = jnp.zeros((n_assign,), jnp.int32).at[order].set(slot_sorted)
    n_blocks = -(-n_assign // MOE_BLOCK) + N_EXPERTS
    n_slots = n_blocks * MOE_BLOCK
    slot_token = jnp.full((n_slots,), t, jnp.int32).at[slot].set(
        jnp.arange(n_assign, dtype=jnp.int32) // TOP_K)
    block_expert = jnp.minimum(
        jnp.searchsorted(padded_end, jnp.arange(n_blocks, dtype=jnp.int32) * MOE_BLOCK, side='right'),
        N_EXPERTS - 1)
    h_pad = jnp.concatenate([h, jnp.zeros((1, d), h.dtype)], axis=0)
    xb = h_pad[slot_token].reshape(n_blocks, MOE_BLOCK, d)

    def expert_block(args):
        xblk, e = args
        u = xblk @ w1[e] + b1[e]
        return clamped_swiglu(u) @ w2[e] + b2[e]

    yb = lax.map(expert_block, (xb, block_expert)).reshape(n_slots, d)
    return jnp.einsum('tkd,tk->td', yb[slot].reshape(t, TOP_K, d), gates)


def setup_inputs(seed: int = 0) -> dict:
    key = jax.random.key(seed)
    ks = jax.random.split(key, 24)
    L = DEPTH
    nrm = lambda k, shape, scale: jax.random.normal(k, shape, jnp.float32) * scale
    return {
        "x": nrm(ks[0], (BATCH, SEQ, D_MODEL), 1.0),
        "norm_mix_g": 1.0 + nrm(ks[1], (L, D_MODEL), 0.02),
        "w_in": nrm(ks[2], (L, D_MODEL, PROJ_COLS), D_MODEL ** -0.5),
        "conv_w": nrm(ks[3], (L, CONV_K, CONV_WIDTH), CONV_K ** -0.5),
        "shift_mu": jax.random.uniform(ks[4], (L, RWKV_COLS), jnp.float32),
        "decay_w0": jax.random.uniform(ks[5], (L, RWKV_WIDTH), jnp.float32, minval=-6.0, maxval=-1.0),
        "decay_w2": nrm(ks[6], (L, DECAY_LORA, RWKV_WIDTH), 0.1 * DECAY_LORA ** -0.5),
        "iclr_a0": nrm(ks[7], (L, RWKV_WIDTH), 0.5),
        "iclr_a2": nrm(ks[8], (L, AAA_LORA, RWKV_WIDTH), 0.1 * AAA_LORA ** -0.5),
        "gate_g2": nrm(ks[9], (L, GATE_LORA, RWKV_WIDTH), GATE_LORA ** -0.5),
        "k_k": 0.85 + nrm(ks[10], (L, RWKV_WIDTH), 0.02),
        "k_a": 1.0 + nrm(ks[11], (L, RWKV_WIDTH), 0.02),
        "r_k": nrm(ks[12], (L, RWKV_HEADS, RWKV_HEAD), 0.1),
        "ln_x_g": 1.0 + nrm(ks[13], (L, RWKV_WIDTH), 0.02),
        "ln_x_b": nrm(ks[14], (L, RWKV_WIDTH), 0.01),
        "w_out": nrm(ks[15], (L, D_MODEL, D_MODEL), D_MODEL ** -0.5),
        "norm_ffn_g": 1.0 + nrm(ks[16], (L, D_MODEL), 0.02),
        "router_w": nrm(ks[17], (L, D_MODEL, N_EXPERTS), D_MODEL ** -0.5),
        "router_b": nrm(ks[18], (L, N_EXPERTS), 0.01),
        "exp_w1": nrm(ks[19], (L, N_EXPERTS, D_MODEL, 2 * D_FF), D_MODEL ** -0.5),
        "exp_b1": nrm(ks[20], (L, N_EXPERTS, 2 * D_FF), 0.01),
        "exp_w2": nrm(ks[21], (L, N_EXPERTS, D_FF, D_MODEL), D_FF ** -0.5),
        "exp_b2": nrm(ks[22], (L, N_EXPERTS, D_MODEL), 0.01),
        "norm_final_g": 1.0 + nrm(ks[23], (D_MODEL,), 0.02),
    }


def reference(x, norm_mix_g, w_in, conv_w, shift_mu, decay_w0, decay_w2, iclr_a0, iclr_a2,
              gate_g2, k_k, k_a, r_k, ln_x_g, ln_x_b, w_out, norm_ffn_g, router_w, router_b,
              exp_w1, exp_b1, exp_w2, exp_b2, norm_final_g):
    bsz, s, d = x.shape
    h = x
    for l in range(DEPTH):
        xn = rmsnorm(h, norm_mix_g[l])
        p = xn @ w_in[l]
        p_conv, p_rwkv, p_gate = jnp.split(p, [CONV_COLS, CONV_COLS + RWKV_COLS], axis=-1)
        y_conv = short_conv_mixer(p_conv, conv_w[l])
        y_rwkv = rwkv7_time_mix(p_rwkv, shift_mu[l], decay_w0[l], decay_w2[l], iclr_a0[l],
                                iclr_a2[l], gate_g2[l], k_k[l], k_a[l], r_k[l],
                                ln_x_g[l], ln_x_b[l]).astype(h.dtype)
        gate_conv, gate_rwkv = jnp.split(jax.nn.sigmoid(p_gate), 2, axis=-1)
        h = h + (gate_conv * y_conv + gate_rwkv * y_rwkv) @ w_out[l]
        hn = rmsnorm(h, norm_ffn_g[l]).reshape(bsz * s, d)
        h = h + moe_ffn(hn, router_w[l], router_b[l], exp_w1[l], exp_b1[l],
                        exp_w2[l], exp_b2[l]).reshape(bsz, s, d)
    return rmsnorm(h, norm_final_g)
```

```python
import functools

import jax
import jax.numpy as jnp
from jax import lax
from jax.experimental import pallas as pl
from jax.experimental.pallas import tpu as pltpu

HEAD = 64
DECAY_LORA = 64
AAA_LORA = 64
GATE_LORA = 160
TOP_K = 4
SWIGLU_ALPHA = 1.702
SWIGLU_LIMIT = 7.0
NORM_EPS = 1e-5
GN_EPS = 64e-5

LANES = 128
SUBLANES = 8
VMEM_LIMIT_BYTES = 56 * 1024 * 1024

RWKV_CHUNK = 64
MIX_TOKENS = 256
ROUTE_TOKENS = 512
MOVE_TOKENS = 256
SLOT_TILE = 256

BF16 = jnp.bfloat16
F32 = jnp.float32


def _dot(a, b):
    return jnp.dot(a.astype(BF16), b.astype(BF16), preferred_element_type=F32)


def _dot_nt(a, b):
    return lax.dot_general(a.astype(BF16), b.astype(BF16), (((1,), (1,)), ((), ())),
                           preferred_element_type=F32)


def _dot_tn(a, b):
    return lax.dot_general(a.astype(BF16), b.astype(BF16), (((0,), (0,)), ((), ())),
                           preferred_element_type=F32)


def _sigmoid(x):
    return 1.0 / (1.0 + jnp.exp(-x))


def _softplus(x):
    return jnp.maximum(x, 0.0) + jnp.log1p(jnp.exp(-jnp.abs(x)))


def _shift_rows(x, n, carry):
    rolled = pltpu.roll(x, n, axis=0)
    row = lax.broadcasted_iota(jnp.int32, x.shape, 0)
    out = rolled
    for i in range(n):
        out = jnp.where(row == i, carry[SUBLANES - n + i:SUBLANES - n + i + 1, :], out)
    return out


def _mixer_kernel(x_ref, g_ref, wmain_ref, wlora_ref, wgate_ref, cw_ref, mu_rkv_ref, mu_lora_ref,
                  w0_ref, w2_ref, a0_ref, a2_ref, g2_ref, kk_ref, ka_ref, rk_ref, lng_ref, lnb_ref,
                  wout_ref, o_ref,
                  state_s, cu_s, cp_s, cl_s, ar_s, bk_s, v_s, rkb_s, y_s, gam_s,
                  *, n_heads, d_model):
    tc = x_ref.shape[1]
    L = RWKV_CHUNK
    n_sub = tc // L
    D = d_model

    @pl.when(pl.program_id(1) == 0)
    def _():
        state_s[...] = jnp.zeros_like(state_s)
        cu_s[...] = jnp.zeros_like(cu_s)
        cp_s[...] = jnp.zeros_like(cp_s)
        cl_s[...] = jnp.zeros_like(cl_s)

    x = x_ref[0]
    ms = jnp.mean(x * x, axis=-1, keepdims=True)
    xn = (x * lax.rsqrt(ms + NORM_EPS) * g_ref[...]).astype(BF16)

    pc = jnp.dot(xn, wmain_ref[:, 0:3 * D], preferred_element_type=F32)
    u = pc[:, D:2 * D] * pc[:, 2 * D:3 * D]
    cu = cu_s[...]
    conv = (cw_ref[0:1, :] * _shift_rows(u, 2, cu) + cw_ref[1:2, :] * _shift_rows(u, 1, cu)
            + cw_ref[2:3, :] * u)
    y_conv = pc[:, 0:D] * conv
    cu_s[...] = u[tc - SUBLANES:tc, :]

    pr = jnp.dot(xn, wmain_ref[:, 3 * D:6 * D], preferred_element_type=F32)
    cp = cp_s[...]
    cp_s[...] = pr[tc - SUBLANES:tc, :]
    pr = pr + (_shift_rows(pr, 1, cp) - pr) * mu_rkv_ref[...]
    plo = jnp.dot(xn, wlora_ref[...], preferred_element_type=F32)
    cl = cl_s[...]
    cl_s[...] = plo[tc - SUBLANES:tc, :]
    plo = plo + (_shift_rows(plo, 1, cl) - plo) * mu_lora_ref[...]

    r = pr[:, 0:D]
    k = pr[:, D:2 * D]
    v = pr[:, 2 * D:3 * D]
    wd = plo[:, 0:LANES]
    ad = plo[:, LANES:2 * LANES]
    gd = plo[:, 2 * LANES:4 * LANES]

    w_raw = -_softplus(-(w0_ref[...] + _dot(jnp.tanh(wd), w2_ref[...]))) - 0.5
    lw = -jnp.exp(w_raw)
    a = _sigmoid(a0_ref[...] + _dot(ad, a2_ref[...]))
    g = _dot(_sigmoid(gd), g2_ref[...])

    row = lax.broadcasted_iota(jnp.int32, (tc, tc), 0)
    col = lax.broadcasted_iota(jnp.int32, (tc, tc), 1)
    tri = jnp.where((row >= col) & ((row // L) == (col // L)), 1.0, 0.0).astype(BF16)
    lw_hi = lw.astype(BF16)
    lw_lo = (lw - lw_hi.astype(F32)).astype(BF16)
    cum = (jnp.dot(tri, lw_hi, preferred_element_type=F32)
           + jnp.dot(tri, lw_lo, preferred_element_type=F32))
    e_inv = jnp.exp(-cum)

    kkraw = k * kk_ref[...]
    for h in range(n_heads):
        hs = slice(h * HEAD, (h + 1) * HEAD)
        kh = kkraw[:, hs]
        ss = jnp.sum(kh * kh, axis=-1, keepdims=True)
        y_s[:, hs] = kh / jnp.maximum(jnp.sqrt(ss), 1e-12)
    kkn = y_s[...]

    k2 = k * (1.0 + (a - 1.0) * ka_ref[...])
    a_t = kkn * jnp.exp(cum - lw)
    r_t = r * jnp.exp(cum)
    b_t = kkn * a * e_inv
    k_t = k2 * e_inv
    for c in range(n_sub):
        rows = slice(c * L, (c + 1) * L)
        ar_s[c, 0:L, :] = a_t[rows]
        ar_s[c, L:2 * L, :] = r_t[rows]
        bk_s[c, 0:L, :] = b_t[rows]
        bk_s[c, L:2 * L, :] = k_t[rows]
        gam_s[c] = jnp.exp(cum[(c + 1) * L - 1:(c + 1) * L, :])
    v_s[...] = v
    rkb_s[...] = r * k2 * rk_ref[...]

    ri = lax.broadcasted_iota(jnp.int32, (L, L), 0)
    ci = lax.broadcasted_iota(jnp.int32, (L, L), 1)
    strict = ri > ci
    eye = jnp.where(ri == ci, 1.0, 0.0)
    ri2 = lax.broadcasted_iota(jnp.int32, (L, 2 * L), 0)
    ci2 = lax.broadcasted_iota(jnp.int32, (L, 2 * L), 1)
    incl2 = ri2 >= (ci2 % L)
    n_double = L.bit_length() - 2

    def sub_chunk(c, carry):
        r0 = pl.multiple_of(c * L, L)
        for h in range(n_heads):
            hs = slice(h * HEAD, (h + 1) * HEAD)
            ar = ar_s[c, :, hs]
            bk = bk_s[c, :, hs]
            vh = v_s[pl.ds(r0, L), hs]
            st = state_s[h]
            gm = _dot_nt(ar, bk)
            nm = jnp.where(strict, gm[0:L, 0:L], 0.0)
            mak = jnp.where(strict, gm[0:L, L:2 * L], 0.0)
            q = jnp.where(incl2, gm[L:2 * L, :], 0.0)
            xinv = eye - nm
            p = nm
            for _ in range(n_double):
                p = _dot(p, p)
                xinv = xinv + _dot(xinv, p)
            ars = _dot_nt(ar, st)
            uu = _dot(xinv, ars[0:L] + _dot(mak, vh))
            uv = jnp.concatenate([-uu, vh], axis=0)
            yh = ars[L:2 * L] + _dot(q, uv)
            state_s[h] = (st + _dot_tn(uv, bk)) * gam_s[c, :, hs]
            mean = jnp.mean(yh, axis=-1, keepdims=True)
            yc = yh - mean
            var = jnp.mean(yc * yc, axis=-1, keepdims=True)
            bonus = jnp.sum(rkb_s[pl.ds(r0, L), hs], axis=-1, keepdims=True) * vh
            y_s[pl.ds(r0, L), hs] = (yc * lax.rsqrt(var + GN_EPS) * lng_ref[:, hs]
                                     + lnb_ref[:, hs] + bonus)
        return carry

    lax.fori_loop(0, n_sub, sub_chunk, 0)

    y_rwkv = y_s[...] * g
    gates = _sigmoid(jnp.dot(xn, wgate_ref[...], preferred_element_type=F32))
    mix = gates[:, 0:D] * y_conv + gates[:, D:2 * D] * y_rwkv
    o_ref[0] = x + jnp.dot(mix.astype(BF16), wout_ref[...], preferred_element_type=F32)


def _const_spec(shape):
    nd = len(shape)
    return pl.BlockSpec(shape, lambda *_: (0,) * nd, pipeline_mode=pl.Buffered(1))


def _mixer(x, norm_g, w_in, conv_w, shift_mu, w0, w2, a0, a2, g2, k_k, k_a, r_k, ln_g, ln_b, w_out):
    bsz, s, d = x.shape
    n_heads = d // HEAD
    tc = min(MIX_TOKENS, s)
    L = RWKV_CHUNK
    n_sub = tc // L
    lora0 = 6 * d
    w_main = w_in[:, 0:6 * d].astype(BF16)
    pad = lambda t, n: jnp.pad(t, ((0, 0), (0, n - t.shape[1])))
    lora_cols = (DECAY_LORA, AAA_LORA, GATE_LORA)
    lora_pads = (LANES, LANES, 2 * LANES)
    pieces_w, pieces_mu, off = [], [], lora0
    for n, p in zip(lora_cols, lora_pads):
        pieces_w.append(pad(w_in[:, off:off + n], p))
        pieces_mu.append(pad(shift_mu[None, off - 3 * d:off - 3 * d + n], p))
        off += n
    w_lora = jnp.concatenate(pieces_w, axis=1).astype(BF16)
    mu_lora = jnp.concatenate(pieces_mu, axis=1)
    w_gate = w_in[:, off:off + 2 * d].astype(BF16)
    mu_rkv = shift_mu[None, 0:3 * d]
    padr = lambda t, n: jnp.pad(t, ((0, n - t.shape[0]), (0, 0)))
    w2p = padr(w2, LANES).astype(BF16)
    a2p = padr(a2, LANES).astype(BF16)
    g2p = padr(g2, 2 * LANES).astype(BF16)
    row = lambda t: t.reshape(1, -1)
    consts = [row(norm_g), w_main, w_lora, w_gate, conv_w, mu_rkv, mu_lora, row(w0), w2p, row(a0), a2p,
              g2p, row(k_k), row(k_a), row(r_k), row(ln_g), row(ln_b), w_out.astype(BF16)]
    kern = functools.partial(_mixer_kernel, n_heads=n_heads, d_model=d)
    return pl.pallas_call(
        kern,
        grid=(bsz, s // tc),
        in_specs=[pl.BlockSpec((1, tc, d), lambda b, c: (b, c, 0))] + [_const_spec(t.shape) for t in consts],
        out_specs=pl.BlockSpec((1, tc, d), lambda b, c: (b, c, 0)),
        out_shape=jax.ShapeDtypeStruct((bsz, s, d), F32),
        scratch_shapes=[
            pltpu.VMEM((n_heads, HEAD, HEAD), F32),
            pltpu.VMEM((SUBLANES, d), F32),
            pltpu.VMEM((SUBLANES, 3 * d), F32),
            pltpu.VMEM((SUBLANES, 4 * LANES), F32),
            pltpu.VMEM((n_sub, 2 * L, d), F32),
            pltpu.VMEM((n_sub, 2 * L, d), F32),
            pltpu.VMEM((tc, d), F32),
            pltpu.VMEM((tc, d), F32),
            pltpu.VMEM((tc, d), F32),
            pltpu.VMEM((n_sub, 1, d), F32),
        ],
        compiler_params=pltpu.CompilerParams(
            dimension_semantics=("arbitrary", "arbitrary"), vmem_limit_bytes=VMEM_LIMIT_BYTES),
        name="mixer",
    )(x, *consts)


def _router_kernel(h_ref, g_ref, rw_ref, rb_ref, meta_ref, cnt_ref, run_s, *, n_experts):
    tt = h_ref.shape[0]

    @pl.when(pl.program_id(0) == 0)
    def _():
        run_s[...] = jnp.zeros_like(run_s)

    h = h_ref[...]
    ms = jnp.mean(h * h, axis=-1, keepdims=True)
    hn = h * lax.rsqrt(ms + NORM_EPS) * g_ref[...]
    logits = jnp.dot(hn, rw_ref[...], preferred_element_type=F32,
                     precision=lax.Precision.HIGHEST) + rb_ref[...]
    lane = lax.broadcasted_iota(jnp.int32, (tt, LANES), 1)
    neg = jnp.float32(-jnp.inf)
    work = jnp.where(lane < n_experts, logits, neg)
    vals, idxs = [], []
    for _ in range(TOP_K):
        m = jnp.max(work, axis=-1, keepdims=True)
        i = jnp.min(jnp.where(work == m, lane, LANES), axis=-1, keepdims=True)
        vals.append(m)
        idxs.append(i)
        work = jnp.where(lane == i, neg, work)
    ex = [jnp.exp(vv - vals[0]) for vv in vals]
    den = ex[0] + ex[1] + ex[2] + ex[3]
    gates = [e / den for e in ex]

    onehot = jnp.zeros((tt, LANES), jnp.bool_)
    for kk in range(TOP_K):
        onehot = onehot | (lane == (idxs[kk] + kk * n_experts))
    oh = jnp.where(onehot, 1.0, 0.0)
    ri = lax.broadcasted_iota(jnp.int32, (tt, tt), 0)
    ci = lax.broadcasted_iota(jnp.int32, (tt, tt), 1)
    tri = jnp.where(ri > ci, 1.0, 0.0).astype(BF16)
    cnt = jnp.dot(tri, oh.astype(BF16), preferred_element_type=F32)
    tot = jnp.broadcast_to(jnp.sum(oh, axis=0, keepdims=True), (SUBLANES, LANES))
    lane8 = lax.broadcasted_iota(jnp.int32, (SUBLANES, LANES), 1)
    pk = jnp.zeros_like(tot)
    te = tot
    for j in range(1, TOP_K):
        rolled = pltpu.roll(tot, j * n_experts, axis=1)
        pk = pk + jnp.where(lane8 >= j * n_experts, rolled, 0.0)
        te = te + rolled
    before = cnt + (run_s[...] + pk)[0:1, :]
    ranks = [jnp.sum(jnp.where(onehot & (lane // n_experts == kk), before, 0.0), axis=-1, keepdims=True)
             for kk in range(TOP_K)]
    run_s[...] = run_s[...] + te
    cnt_ref[...] = run_s[...]

    meta = jnp.zeros((tt, LANES), F32)
    for kk in range(TOP_K):
        meta = jnp.where(lane == kk, idxs[kk].astype(F32), meta)
        meta = jnp.where(lane == TOP_K + kk, gates[kk], meta)
        meta = jnp.where(lane == 2 * TOP_K + kk, ranks[kk], meta)
    meta_ref[...] = meta


def _router(h2, norm_g, router_w, router_b):
    t, d = h2.shape
    n_experts = router_w.shape[1]
    tt = min(ROUTE_TOKENS, t)
    rw = jnp.pad(router_w, ((0, 0), (0, LANES - n_experts)))
    rb = jnp.pad(router_b, (0, LANES - n_experts)).reshape(1, LANES)
    kern = functools.partial(_router_kernel, n_experts=n_experts)
    return pl.pallas_call(
        kern,
        grid=(t // tt,),
        in_specs=[pl.BlockSpec((tt, d), lambda i: (i, 0)),
                  pl.BlockSpec((1, d), lambda i: (0, 0)),
                  pl.BlockSpec((d, LANES), lambda i: (0, 0)),
                  pl.BlockSpec((1, LANES), lambda i: (0, 0))],
        out_specs=[pl.BlockSpec((tt, LANES), lambda i: (i, 0)),
                   pl.BlockSpec((SUBLANES, LANES), lambda i: (0, 0))],
        out_shape=[jax.ShapeDtypeStruct((t, LANES), F32),
                   jax.ShapeDtypeStruct((SUBLANES, LANES), F32)],
        scratch_shapes=[pltpu.VMEM((SUBLANES, LANES), F32)],
        compiler_params=pltpu.CompilerParams(
            dimension_semantics=("arbitrary",), vmem_limit_bytes=VMEM_LIMIT_BYTES),
        name="router",
    )(h2, norm_g.reshape(1, d), rw, rb)


def _dispatch_kernel(pos_ref, h_ref, g_ref, xs_ref, hn_s, sem):
    tt = h_ref.shape[0]
    h = h_ref[...]
    ms = jnp.mean(h * h, axis=-1, keepdims=True)
    hn_s[...] = h * lax.rsqrt(ms + NORM_EPS) * g_ref[...]

    def row_copy(t, slot):
        return pltpu.make_async_copy(hn_s.at[pl.ds(t, 1), :], xs_ref.at[pl.ds(slot, 1), :], sem)

    def issue(t, carry):
        for kk in range(TOP_K):
            row_copy(t, pos_ref[0, 0, t * TOP_K + kk]).start()
        return carry

    lax.fori_loop(0, tt, issue, 0)

    def drain(t, carry):
        for kk in range(TOP_K):
            row_copy(t, pos_ref[0, 0, t * TOP_K + kk]).wait()
        return carry

    lax.fori_loop(0, tt, drain, 0)


def _dispatch(h2, norm_g, pos, n_slots):
    t, d = h2.shape
    tt = min(MOVE_TOKENS, t)
    pos3 = pos.reshape(t // tt, 1, tt * TOP_K)
    return pl.pallas_call(
        _dispatch_kernel,
        grid=(t // tt,),
        in_specs=[pl.BlockSpec((1, 1, tt * TOP_K), lambda i: (i, 0, 0), memory_space=pltpu.SMEM),
                  pl.BlockSpec((tt, d), lambda i: (i, 0)),
                  pl.BlockSpec((1, d), lambda i: (0, 0))],
        out_specs=pl.BlockSpec(memory_space=pl.ANY),
        out_shape=jax.ShapeDtypeStruct((n_slots, d), F32),
        scratch_shapes=[pltpu.VMEM((tt, d), F32), pltpu.SemaphoreType.DMA(())],
        compiler_params=pltpu.CompilerParams(
            dimension_semantics=("arbitrary",), vmem_limit_bytes=VMEM_LIMIT_BYTES,
            has_side_effects=True),
        name="dispatch",
    )(pos3, h2, norm_g.reshape(1, d))


def _experts_kernel(te_ref, nv_ref, xs_ref, w1_ref, b1_ref, w2_ref, b2_ref, ys_ref, w1_s, w2_s, *, d_ff):
    i = pl.program_id(0)
    prev = te_ref[jnp.maximum(i - 1, 0)]
    changed = (i == 0) | (te_ref[i] != prev)

    @pl.when(changed)
    def _():
        w1_s[...] = w1_ref[0].astype(BF16)
        w2_s[...] = w2_ref[0].astype(BF16)

    nvalid = nv_ref[i]

    @pl.when(nvalid > 0)
    def _():
        tm = xs_ref.shape[0]
        row = lax.broadcasted_iota(jnp.int32, (tm, 1), 0)
        x = jnp.where(row < nvalid, xs_ref[...], 0.0).astype(BF16)
        u = jnp.dot(x, w1_s[...], preferred_element_type=F32) + b1_ref[0]
        glu = jnp.minimum(u[:, 0:d_ff], SWIGLU_LIMIT)
        lin = jnp.clip(u[:, d_ff:2 * d_ff], -SWIGLU_LIMIT, SWIGLU_LIMIT)
        act = glu * _sigmoid(SWIGLU_ALPHA * glu) * (lin + 1.0)
        ys_ref[...] = jnp.dot(act.astype(BF16), w2_s[...], preferred_element_type=F32) + b2_ref[0]

    @pl.when(nvalid <= 0)
    def _():
        ys_ref[...] = jnp.zeros_like(ys_ref)


def _experts(xs, tile_expert, tile_valid, w1, b1, w2, b2):
    n_slots, d = xs.shape
    n_exp, _, two_ff = w1.shape
    d_ff = two_ff // 2
    tm = SLOT_TILE
    n_tiles = n_slots // tm
    kern = functools.partial(_experts_kernel, d_ff=d_ff)
    grid_spec = pltpu.PrefetchScalarGridSpec(
        num_scalar_prefetch=2,
        grid=(n_tiles,),
        in_specs=[pl.BlockSpec((tm, d), lambda i, te, nv: (i, 0)),
                  pl.BlockSpec((1, d, two_ff), lambda i, te, nv: (te[i], 0, 0)),
                  pl.BlockSpec((1, 1, two_ff), lambda i, te, nv: (te[i], 0, 0)),
                  pl.BlockSpec((1, d_ff, d), lambda i, te, nv: (te[i], 0, 0)),
                  pl.BlockSpec((1, 1, d), lambda i, te, nv: (te[i], 0, 0))],
        out_specs=pl.BlockSpec((tm, d), lambda i, te, nv: (i, 0)),
        scratch_shapes=[pltpu.VMEM((d, two_ff), BF16), pltpu.VMEM((d_ff, d), BF16)],
    )
    return pl.pallas_call(
        kern,
        grid_spec=grid_spec,
        out_shape=jax.ShapeDtypeStruct((n_slots, d), F32),
        compiler_params=pltpu.CompilerParams(
            dimension_semantics=("arbitrary",), vmem_limit_bytes=VMEM_LIMIT_BYTES),
        name="experts",
    )(tile_expert, tile_valid, xs, w1, b1.reshape(n_exp, 1, two_ff), w2, b2.reshape(n_exp, 1, d))


def _combine_kernel(pos_ref, h_ref, meta_ref, g_ref, ys_ref, o_ref, buf_s, sem):
    tt = h_ref.shape[0]

    def row_copy(t, kk, slot):
        return pltpu.make_async_copy(ys_ref.at[pl.ds(slot, 1), :], buf_s.at[kk, pl.ds(t, 1), :], sem)

    def issue(t, carry):
        for kk in range(TOP_K):
            row_copy(t, kk, pos_ref[0, 0, t * TOP_K + kk]).start()
        return carry

    lax.fori_loop(0, tt, issue, 0)

    def drain(t, carry):
        for kk in range(TOP_K):
            row_copy(t, kk, pos_ref[0, 0, t * TOP_K + kk]).wait()
        return carry

    lax.fori_loop(0, tt, drain, 0)

    acc = h_ref[...]
    for kk in range(TOP_K):
        acc = acc + meta_ref[:, TOP_K + kk:TOP_K + kk + 1] * buf_s[kk]
    ms = jnp.mean(acc * acc, axis=-1, keepdims=True)
    o_ref[...] = acc * lax.rsqrt(ms + NORM_EPS) * g_ref[...]


def _combine(h2, meta, pos, ys, norm_g):
    t, d = h2.shape
    tt = min(MOVE_TOKENS, t)
    pos3 = pos.reshape(t // tt, 1, tt * TOP_K)
    return pl.pallas_call(
        _combine_kernel,
        grid=(t // tt,),
        in_specs=[pl.BlockSpec((1, 1, tt * TOP_K), lambda i: (i, 0, 0), memory_space=pltpu.SMEM),
                  pl.BlockSpec((tt, d), lambda i: (i, 0)),
                  pl.BlockSpec((tt, LANES), lambda i: (i, 0)),
                  pl.BlockSpec((1, d), lambda i: (0, 0)),
                  pl.BlockSpec(memory_space=pl.ANY)],
        out_specs=pl.BlockSpec((tt, d), lambda i: (i, 0)),
        out_shape=jax.ShapeDtypeStruct((t, d), F32),
        scratch_shapes=[pltpu.VMEM((TOP_K, tt, d), F32), pltpu.SemaphoreType.DMA(())],
        compiler_params=pltpu.CompilerParams(
            dimension_semantics=("arbitrary",), vmem_limit_bytes=VMEM_LIMIT_BYTES),
        name="combine",
    )(pos3, h2, meta, norm_g.reshape(1, d), ys)


def _moe(h2, norm_ffn_g, router_w, router_b, w1, b1, w2, b2, norm_final_g):
    t, d = h2.shape
    n_exp = router_w.shape[1]
    tm = SLOT_TILE
    meta, counts = _router(h2, norm_ffn_g, router_w, router_b)
    eidx = meta[:, 0:TOP_K].astype(jnp.int32)
    rank = meta[:, 2 * TOP_K:3 * TOP_K].astype(jnp.int32)
    cnt = counts[0, 0:n_exp].astype(jnp.int32)
    padded = (cnt + tm - 1) // tm * tm
    seg_end = jnp.cumsum(padded)
    seg_start = seg_end - padded
    pos = (seg_start[eidx] + rank).reshape(-1)
    n_tiles = -(-(t * TOP_K) // tm) + n_exp
    tile_start = jnp.arange(n_tiles, dtype=jnp.int32) * tm
    tile_expert = jnp.minimum(jnp.searchsorted(seg_end, tile_start, side="right"), n_exp - 1).astype(jnp.int32)
    tile_valid = jnp.clip(seg_start[tile_expert] + cnt[tile_expert] - tile_start, 0, tm).astype(jnp.int32)
    xs = _dispatch(h2, norm_ffn_g, pos, n_tiles * tm)
    ys = _experts(xs, tile_expert, tile_valid, w1, b1, w2, b2)
    return _combine(h2, meta, pos, ys, norm_final_g)


def kernel(x, norm_mix_g, w_in, conv_w, shift_mu, decay_w0, decay_w2, iclr_a0, iclr_a2, gate_g2, k_k, k_a,
           r_k, ln_x_g, ln_x_b, w_out, norm_ffn_g, router_w, router_b, exp_w1, exp_b1, exp_w2, exp_b2,
           norm_final_g):
    bsz, s, d = x.shape
    depth = w_in.shape[0]
    assert depth == 1, "final norm is fused into the last layer's combine kernel"
    h = _mixer(x, norm_mix_g[0], w_in[0], conv_w[0], shift_mu[0], decay_w0[0], decay_w2[0], iclr_a0[0],
               iclr_a2[0], gate_g2[0], k_k[0], k_a[0], r_k[0], ln_x_g[0], ln_x_b[0], w_out[0])
    out = _moe(h.reshape(bsz * s, d), norm_ffn_g[0], router_w[0], router_b[0], exp_w1[0], exp_b1[0],
               exp_w2[0], exp_b2[0], norm_final_g)
    return out.reshape(bsz, s, d)
```

```python
import functools

import jax
import jax.numpy as jnp
from jax import lax
from jax.experimental import pallas as pl
from jax.experimental.pallas import tpu as pltpu

HEAD = 64
DECAY_LORA = 64
AAA_LORA = 64
GATE_LORA = 160
TOP_K = 4
SWIGLU_ALPHA = 1.702
SWIGLU_LIMIT = 7.0
NORM_EPS = 1e-5
GN_EPS = 64e-5

LANES = 128
SUBLANES = 8
VMEM_LIMIT_BYTES = 56 * 1024 * 1024

RWKV_CHUNK = 64
MIX_TOKENS = 256
ROUTE_TOKENS = 512
MOVE_TOKENS = 256
SLOT_TILE = 256

BF16 = jnp.bfloat16
F32 = jnp.float32


def _dot(a, b):
    return jnp.dot(a.astype(BF16), b.astype(BF16), preferred_element_type=F32)


def _dot_nt(a, b):
    return lax.dot_general(a.astype(BF16), b.astype(BF16), (((1,), (1,)), ((), ())),
                           preferred_element_type=F32)


def _dot_tn(a, b):
    return lax.dot_general(a.astype(BF16), b.astype(BF16), (((0,), (0,)), ((), ())),
                           preferred_element_type=F32)


def _sigmoid(x):
    return 1.0 / (1.0 + jnp.exp(-x))


def _softplus(x):
    return jnp.maximum(x, 0.0) + jnp.log1p(jnp.exp(-jnp.abs(x)))


def _shift_rows(x, n, carry):
    rolled = pltpu.roll(x, n, axis=0)
    row = lax.broadcasted_iota(jnp.int32, x.shape, 0)
    out = rolled
    for i in range(n):
        out = jnp.where(row == i, carry[SUBLANES - n + i:SUBLANES - n + i + 1, :], out)
    return out


def _mixer_kernel(x_ref, g_ref, wmain_ref, wlora_ref, wgate_ref, cw_ref, mu_rkv_ref, mu_lora_ref,
                  w0_ref, w2_ref, a0_ref, a2_ref, g2_ref, kk_ref, ka_ref, rk_ref, lng_ref, lnb_ref,
                  wout_ref, o_ref,
                  cu_s, cp_s, cl_s, ar_s, bk_s, v_s, rkb_s, y_s, gam_s, *state_s,
                  n_heads, d_model):
    tc = x_ref.shape[1]
    L = RWKV_CHUNK
    n_sub = tc // L
    D = d_model

    @pl.when(pl.program_id(1) == 0)
    def _():
        for st_ref in state_s:
            st_ref[...] = jnp.zeros_like(st_ref)
        cu_s[...] = jnp.zeros_like(cu_s)
        cp_s[...] = jnp.zeros_like(cp_s)
        cl_s[...] = jnp.zeros_like(cl_s)

    x = x_ref[0]
    ms = jnp.mean(x * x, axis=-1, keepdims=True)
    xn = (x * lax.rsqrt(ms + NORM_EPS) * g_ref[...]).astype(BF16)

    pc = jnp.dot(xn, wmain_ref[:, 0:3 * D], preferred_element_type=F32)
    u = pc[:, D:2 * D] * pc[:, 2 * D:3 * D]
    cu = cu_s[...]
    conv = (cw_ref[0:1, :] * _shift_rows(u, 2, cu) + cw_ref[1:2, :] * _shift_rows(u, 1, cu)
            + cw_ref[2:3, :] * u)
    y_conv = pc[:, 0:D] * conv
    cu_s[...] = u[tc - SUBLANES:tc, :]

    pr = jnp.dot(xn, wmain_ref[:, 3 * D:6 * D], preferred_element_type=F32)
    cp = cp_s[...]
    cp_s[...] = pr[tc - SUBLANES:tc, :]
    pr = pr + (_shift_rows(pr, 1, cp) - pr) * mu_rkv_ref[...]
    plo = jnp.dot(xn, wlora_ref[...], preferred_element_type=F32)
    cl = cl_s[...]
    cl_s[...] = plo[tc - SUBLANES:tc, :]
    plo = plo + (_shift_rows(plo, 1, cl) - plo) * mu_lora_ref[...]

    r = pr[:, 0:D]
    k = pr[:, D:2 * D]
    v = pr[:, 2 * D:3 * D]
    wd = plo[:, 0:LANES]
    ad = plo[:, LANES:2 * LANES]
    gd = plo[:, 2 * LANES:4 * LANES]

    w_raw = -_softplus(-(w0_ref[...] + _dot(jnp.tanh(wd), w2_ref[...]))) - 0.5
    lw = -jnp.exp(w_raw)
    a = _sigmoid(a0_ref[...] + _dot(ad, a2_ref[...]))
    g = _dot(_sigmoid(gd), g2_ref[...])

    row = lax.broadcasted_iota(jnp.int32, (tc, tc), 0)
    col = lax.broadcasted_iota(jnp.int32, (tc, tc), 1)
    tri = jnp.where((row >= col) & ((row // L) == (col // L)), 1.0, 0.0).astype(BF16)
    lw_hi = lw.astype(BF16)
    lw_lo = (lw - lw_hi.astype(F32)).astype(BF16)
    cum = (jnp.dot(tri, lw_hi, preferred_element_type=F32)
           + jnp.dot(tri, lw_lo, preferred_element_type=F32))
    e_inv = jnp.exp(-cum)

    kkraw = k * kk_ref[...]
    for h in range(n_heads):
        hs = slice(h * HEAD, (h + 1) * HEAD)
        kh = kkraw[:, hs]
        ss = jnp.sum(kh * kh, axis=-1, keepdims=True)
        y_s[:, hs] = kh / jnp.maximum(jnp.sqrt(ss), 1e-12)
    kkn = y_s[...]

    k2 = k * (1.0 + (a - 1.0) * ka_ref[...])
    a_t = kkn * jnp.exp(cum - lw)
    r_t = r * jnp.exp(cum)
    b_t = kkn * a * e_inv
    k_t = k2 * e_inv
    for c in range(n_sub):
        rows = slice(c * L, (c + 1) * L)
        ar_s[c, 0:L, :] = a_t[rows]
        ar_s[c, L:2 * L, :] = r_t[rows]
        bk_s[c, 0:L, :] = b_t[rows]
        bk_s[c, L:2 * L, :] = k_t[rows]
        gam_s[c] = jnp.exp(cum[(c + 1) * L - 1:(c + 1) * L, :])
    v_s[...] = v
    rkb_s[...] = r * k2 * rk_ref[...]

    ri = lax.broadcasted_iota(jnp.int32, (L, L), 0)
    ci = lax.broadcasted_iota(jnp.int32, (L, L), 1)
    strict = ri > ci
    eye = jnp.where(ri == ci, 1.0, 0.0)
    ri2 = lax.broadcasted_iota(jnp.int32, (L, 2 * L), 0)
    ci2 = lax.broadcasted_iota(jnp.int32, (L, 2 * L), 1)
    incl2 = ri2 >= (ci2 % L)
    n_double = L.bit_length() - 2

    def sub_chunk(c, carry):
        r0 = pl.multiple_of(c * L, L)
        heads = range(n_heads)
        hsl = [slice(h * HEAD, (h + 1) * HEAD) for h in heads]
        ar = [ar_s[c, :, hsl[h]].astype(BF16) for h in heads]
        bk = [bk_s[c, :, hsl[h]].astype(BF16) for h in heads]
        vh = [v_s[pl.ds(r0, L), hsl[h]] for h in heads]
        st = [state_s[h][...] for h in heads]
        gm = [_dot_nt(ar[h], bk[h]) for h in heads]
        nm = [jnp.where(strict, gm[h][0:L, 0:L], 0.0) for h in heads]
        mak = [jnp.where(strict, gm[h][0:L, L:2 * L], 0.0) for h in heads]
        q = [jnp.where(incl2, gm[h][L:2 * L, :], 0.0) for h in heads]
        xinv = [eye - nm[h] for h in heads]
        p = nm
        for _ in range(n_double):
            p = [_dot(p[h], p[h]) for h in heads]
            xinv = [xinv[h] + _dot(xinv[h], p[h]) for h in heads]
        ars = [_dot_nt(ar[h], st[h]) for h in heads]
        mv = [_dot(mak[h], vh[h]) for h in heads]
        uu = [_dot(xinv[h], ars[h][0:L] + mv[h]) for h in heads]
        uv = [jnp.concatenate([-uu[h], vh[h]], axis=0).astype(BF16) for h in heads]
        yh = [ars[h][L:2 * L] + _dot(q[h], uv[h]) for h in heads]
        for h in heads:
            state_s[h][...] = (st[h] + _dot_tn(uv[h], bk[h])) * gam_s[c, :, hsl[h]]
        for h in heads:
            mean = jnp.mean(yh[h], axis=-1, keepdims=True)
            yc = yh[h] - mean
            var = jnp.mean(yc * yc, axis=-1, keepdims=True)
            bonus = jnp.sum(rkb_s[pl.ds(r0, L), hsl[h]], axis=-1, keepdims=True) * vh[h]
            y_s[pl.ds(r0, L), hsl[h]] = (yc * lax.rsqrt(var + GN_EPS) * lng_ref[:, hsl[h]]
                                         + lnb_ref[:, hsl[h]] + bonus)
        return carry

    lax.fori_loop(0, n_sub, sub_chunk, 0)

    y_rwkv = y_s[...] * g
    gates = _sigmoid(jnp.dot(xn, wgate_ref[...], preferred_element_type=F32))
    mix = gates[:, 0:D] * y_conv + gates[:, D:2 * D] * y_rwkv
    o_ref[0] = x + jnp.dot(mix.astype(BF16), wout_ref[...], preferred_element_type=F32)


def _const_spec(shape):
    nd = len(shape)
    return pl.BlockSpec(shape, lambda *_: (0,) * nd, pipeline_mode=pl.Buffered(1))


def _mixer(x, norm_g, w_in, conv_w, shift_mu, w0, w2, a0, a2, g2, k_k, k_a, r_k, ln_g, ln_b, w_out):
    bsz, s, d = x.shape
    n_heads = d // HEAD
    tc = min(MIX_TOKENS, s)
    L = RWKV_CHUNK
    n_sub = tc // L
    lora0 = 6 * d
    w_main = w_in[:, 0:6 * d].astype(BF16)
    pad = lambda t, n: jnp.pad(t, ((0, 0), (0, n - t.shape[1])))
    lora_cols = (DECAY_LORA, AAA_LORA, GATE_LORA)
    lora_pads = (LANES, LANES, 2 * LANES)
    pieces_w, pieces_mu, off = [], [], lora0
    for n, p in zip(lora_cols, lora_pads):
        pieces_w.append(pad(w_in[:, off:off + n], p))
        pieces_mu.append(pad(shift_mu[None, off - 3 * d:off - 3 * d + n], p))
        off += n
    w_lora = jnp.concatenate(pieces_w, axis=1).astype(BF16)
    mu_lora = jnp.concatenate(pieces_mu, axis=1)
    w_gate = w_in[:, off:off + 2 * d].astype(BF16)
    mu_rkv = shift_mu[None, 0:3 * d]
    padr = lambda t, n: jnp.pad(t, ((0, n - t.shape[0]), (0, 0)))
    w2p = padr(w2, LANES).astype(BF16)
    a2p = padr(a2, LANES).astype(BF16)
    g2p = padr(g2, 2 * LANES).astype(BF16)
    row = lambda t: t.reshape(1, -1)
    consts = [row(norm_g), w_main, w_lora, w_gate, conv_w, mu_rkv, mu_lora, row(w0), w2p, row(a0), a2p,
              g2p, row(k_k), row(k_a), row(r_k), row(ln_g), row(ln_b), w_out.astype(BF16)]
    kern = functools.partial(_mixer_kernel, n_heads=n_heads, d_model=d)
    return pl.pallas_call(
        kern,
        grid=(bsz, s // tc),
        in_specs=[pl.BlockSpec((1, tc, d), lambda b, c: (b, c, 0))] + [_const_spec(t.shape) for t in consts],
        out_specs=pl.BlockSpec((1, tc, d), lambda b, c: (b, c, 0)),
        out_shape=jax.ShapeDtypeStruct((bsz, s, d), F32),
        scratch_shapes=[
            pltpu.VMEM((SUBLANES, d), F32),
            pltpu.VMEM((SUBLANES, 3 * d), F32),
            pltpu.VMEM((SUBLANES, 4 * LANES), F32),
            pltpu.VMEM((n_sub, 2 * L, d), F32),
            pltpu.VMEM((n_sub, 2 * L, d), F32),
            pltpu.VMEM((tc, d), F32),
            pltpu.VMEM((tc, d), F32),
            pltpu.VMEM((tc, d), F32),
            pltpu.VMEM((n_sub, 1, d), F32),
        ] + [pltpu.VMEM((HEAD, HEAD), F32)] * n_heads,
        compiler_params=pltpu.CompilerParams(
            dimension_semantics=("arbitrary", "arbitrary"), vmem_limit_bytes=VMEM_LIMIT_BYTES),
        name="mixer",
    )(x, *consts)


def _router_kernel(h_ref, g_ref, rw_ref, rb_ref, meta_ref, cnt_ref, run_s, *, n_experts):
    tt = h_ref.shape[0]

    @pl.when(pl.program_id(0) == 0)
    def _():
        run_s[...] = jnp.zeros_like(run_s)

    h = h_ref[...]
    ms = jnp.mean(h * h, axis=-1, keepdims=True)
    hn = h * lax.rsqrt(ms + NORM_EPS) * g_ref[...]
    logits = jnp.dot(hn, rw_ref[...], preferred_element_type=F32,
                     precision=lax.Precision.HIGHEST) + rb_ref[...]
    lane = lax.broadcasted_iota(jnp.int32, (tt, LANES), 1)
    neg = jnp.float32(-jnp.inf)
    work = jnp.where(lane < n_experts, logits, neg)
    vals, idxs = [], []
    for _ in range(TOP_K):
        m = jnp.max(work, axis=-1, keepdims=True)
        i = jnp.min(jnp.where(work == m, lane, LANES), axis=-1, keepdims=True)
        vals.append(m)
        idxs.append(i)
        work = jnp.where(lane == i, neg, work)
    ex = [jnp.exp(vv - vals[0]) for vv in vals]
    den = ex[0] + ex[1] + ex[2] + ex[3]
    gates = [e / den for e in ex]

    onehot = jnp.zeros((tt, LANES), jnp.bool_)
    for kk in range(TOP_K):
        onehot = onehot | (lane == (idxs[kk] + kk * n_experts))
    oh = jnp.where(onehot, 1.0, 0.0)
    ri = lax.broadcasted_iota(jnp.int32, (tt, tt), 0)
    ci = lax.broadcasted_iota(jnp.int32, (tt, tt), 1)
    tri = jnp.where(ri > ci, 1.0, 0.0).astype(BF16)
    cnt = jnp.dot(tri, oh.astype(BF16), preferred_element_type=F32)
    tot = jnp.broadcast_to(jnp.sum(oh, axis=0, keepdims=True), (SUBLANES, LANES))
    lane8 = lax.broadcasted_iota(jnp.int32, (SUBLANES, LANES), 1)
    pk = jnp.zeros_like(tot)
    te = tot
    for j in range(1, TOP_K):
        rolled = pltpu.roll(tot, j * n_experts, axis=1)
        pk = pk + jnp.where(lane8 >= j * n_experts, rolled, 0.0)
        te = te + rolled
    before = cnt + (run_s[...] + pk)[0:1, :]
    ranks = [jnp.sum(jnp.where(onehot & (lane // n_experts == kk), before, 0.0), axis=-1, keepdims=True)
             for kk in range(TOP_K)]
    run_s[...] = run_s[...] + te
    cnt_ref[...] = run_s[...]

    meta = jnp.zeros((tt, LANES), F32)
    for kk in range(TOP_K):
        meta = jnp.where(lane == kk, idxs[kk].astype(F32), meta)
        meta = jnp.where(lane == TOP_K + kk, gates[kk], meta)
        meta = jnp.where(lane == 2 * TOP_K + kk, ranks[kk], meta)
    meta_ref[...] = meta


def _router(h2, norm_g, router_w, router_b):
    t, d = h2.shape
    n_experts = router_w.shape[1]
    tt = min(ROUTE_TOKENS, t)
    rw = jnp.pad(router_w, ((0, 0), (0, LANES - n_experts)))
    rb = jnp.pad(router_b, (0, LANES - n_experts)).reshape(1, LANES)
    kern = functools.partial(_router_kernel, n_experts=n_experts)
    return pl.pallas_call(
        kern,
        grid=(t // tt,),
        in_specs=[pl.BlockSpec((tt, d), lambda i: (i, 0)),
                  pl.BlockSpec((1, d), lambda i: (0, 0)),
                  pl.BlockSpec((d, LANES), lambda i: (0, 0)),
                  pl.BlockSpec((1, LANES), lambda i: (0, 0))],
        out_specs=[pl.BlockSpec((tt, LANES), lambda i: (i, 0)),
                   pl.BlockSpec((SUBLANES, LANES), lambda i: (0, 0))],
        out_shape=[jax.ShapeDtypeStruct((t, LANES), F32),
                   jax.ShapeDtypeStruct((SUBLANES, LANES), F32)],
        scratch_shapes=[pltpu.VMEM((SUBLANES, LANES), F32)],
        compiler_params=pltpu.CompilerParams(
            dimension_semantics=("arbitrary",), vmem_limit_bytes=VMEM_LIMIT_BYTES),
        name="router",
    )(h2, norm_g.reshape(1, d), rw, rb)


def _dispatch_kernel(pos_ref, h_ref, g_ref, xs_ref, hn_s, sem):
    tt = h_ref.shape[0]
    h = h_ref[...]
    ms = jnp.mean(h * h, axis=-1, keepdims=True)
    hn_s[...] = h * lax.rsqrt(ms + NORM_EPS) * g_ref[...]

    def row_copy(t, slot):
        return pltpu.make_async_copy(hn_s.at[pl.ds(t, 1), :], xs_ref.at[pl.ds(slot, 1), :], sem)

    def issue(t, carry):
        for kk in range(TOP_K):
            row_copy(t, pos_ref[0, 0, t * TOP_K + kk]).start()
        return carry

    lax.fori_loop(0, tt, issue, 0)

    def drain(t, carry):
        for kk in range(TOP_K):
            row_copy(t, pos_ref[0, 0, t * TOP_K + kk]).wait()
        return carry

    lax.fori_loop(0, tt, drain, 0)


def _dispatch(h2, norm_g, pos, n_slots):
    t, d = h2.shape
    tt = min(MOVE_TOKENS, t)
    pos3 = pos.reshape(t // tt, 1, tt * TOP_K)
    return pl.pallas_call(
        _dispatch_kernel,
        grid=(t // tt,),
        in_specs=[pl.BlockSpec((1, 1, tt * TOP_K), lambda i: (i, 0, 0), memory_space=pltpu.SMEM),
                  pl.BlockSpec((tt, d), lambda i: (i, 0)),
                  pl.BlockSpec((1, d), lambda i: (0, 0))],
        out_specs=pl.BlockSpec(memory_space=pl.ANY),
        out_shape=jax.ShapeDtypeStruct((n_slots, d), F32),
        scratch_shapes=[pltpu.VMEM((tt, d), F32), pltpu.SemaphoreType.DMA(())],
        compiler_params=pltpu.CompilerParams(
            dimension_semantics=("arbitrary",), vmem_limit_bytes=VMEM_LIMIT_BYTES,
            has_side_effects=True),
        name="dispatch",
    )(pos3, h2, norm_g.reshape(1, d))


def _experts_kernel(te_ref, nv_ref, xs_ref, w1_ref, b1_ref, w2_ref, b2_ref, ys_ref, w1_s, w2_s, *, d_ff):
    i = pl.program_id(0)
    prev = te_ref[jnp.maximum(i - 1, 0)]
    changed = (i == 0) | (te_ref[i] != prev)

    @pl.when(changed)
    def _():
        w1_s[...] = w1_ref[0].astype(BF16)
        w2_s[...] = w2_ref[0].astype(BF16)

    nvalid = nv_ref[i]

    @pl.when(nvalid > 0)
    def _():
        tm = xs_ref.shape[0]
        row = lax.broadcasted_iota(jnp.int32, (tm, 1), 0)
        x = jnp.where(row < nvalid, xs_ref[...], 0.0).astype(BF16)
        u = jnp.dot(x, w1_s[...], preferred_element_type=F32) + b1_ref[0]
        glu = jnp.minimum(u[:, 0:d_ff], SWIGLU_LIMIT)
        lin = jnp.clip(u[:, d_ff:2 * d_ff], -SWIGLU_LIMIT, SWIGLU_LIMIT)
        act = glu * _sigmoid(SWIGLU_ALPHA * glu) * (lin + 1.0)
        ys_ref[...] = jnp.dot(act.astype(BF16), w2_s[...], preferred_element_type=F32) + b2_ref[0]

    @pl.when(nvalid <= 0)
    def _():
        ys_ref[...] = jnp.zeros_like(ys_ref)


def _experts(xs, tile_expert, tile_valid, w1, b1, w2, b2):
    n_slots, d = xs.shape
    n_exp, _, two_ff = w1.shape
    d_ff = two_ff // 2
    tm = SLOT_TILE
    n_tiles = n_slots // tm
    kern = functools.partial(_experts_kernel, d_ff=d_ff)
    grid_spec = pltpu.PrefetchScalarGridSpec(
        num_scalar_prefetch=2,
        grid=(n_tiles,),
        in_specs=[pl.BlockSpec((tm, d), lambda i, te, nv: (i, 0)),
                  pl.BlockSpec((1, d, two_ff), lambda i, te, nv: (te[i], 0, 0)),
                  pl.BlockSpec((1, 1, two_ff), lambda i, te, nv: (te[i], 0, 0)),
                  pl.BlockSpec((1, d_ff, d), lambda i, te, nv: (te[i], 0, 0)),
                  pl.BlockSpec((1, 1, d), lambda i, te, nv: (te[i], 0, 0))],
        out_specs=pl.BlockSpec((tm, d), lambda i, te, nv: (i, 0)),
        scratch_shapes=[pltpu.VMEM((d, two_ff), BF16), pltpu.VMEM((d_ff, d), BF16)],
    )
    return pl.pallas_call(
        kern,
        grid_spec=grid_spec,
        out_shape=jax.ShapeDtypeStruct((n_slots, d), F32),
        compiler_params=pltpu.CompilerParams(
            dimension_semantics=("arbitrary",), vmem_limit_bytes=VMEM_LIMIT_BYTES),
        name="experts",
    )(tile_expert, tile_valid, xs, w1, b1.reshape(n_exp, 1, two_ff), w2, b2.reshape(n_exp, 1, d))


def _combine_kernel(pos_ref, h_ref, meta_ref, g_ref, ys_ref, o_ref, buf_s, sem):
    tt = h_ref.shape[0]

    def row_copy(t, kk, slot):
        return pltpu.make_async_copy(ys_ref.at[pl.ds(slot, 1), :], buf_s.at[kk, pl.ds(t, 1), :], sem)

    def issue(t, carry):
        for kk in range(TOP_K):
            row_copy(t, kk, pos_ref[0, 0, t * TOP_K + kk]).start()
        return carry

    lax.fori_loop(0, tt, issue, 0)

    def drain(t, carry):
        for kk in range(TOP_K):
            row_copy(t, kk, pos_ref[0, 0, t * TOP_K + kk]).wait()
        return carry

    lax.fori_loop(0, tt, drain, 0)

    acc = h_ref[...]
    for kk in range(TOP_K):
        acc = acc + meta_ref[:, TOP_K + kk:TOP_K + kk + 1] * buf_s[kk]
    ms = jnp.mean(acc * acc, axis=-1, keepdims=True)
    o_ref[...] = acc * lax.rsqrt(ms + NORM_EPS) * g_ref[...]


def _combine(h2, meta, pos, ys, norm_g):
    t, d = h2.shape
    tt = min(MOVE_TOKENS, t)
    pos3 = pos.reshape(t // tt, 1, tt * TOP_K)
    return pl.pallas_call(
        _combine_kernel,
        grid=(t // tt,),
        in_specs=[pl.BlockSpec((1, 1, tt * TOP_K), lambda i: (i, 0, 0), memory_space=pltpu.SMEM),
                  pl.BlockSpec((tt, d), lambda i: (i, 0)),
                  pl.BlockSpec((tt, LANES), lambda i: (i, 0)),
                  pl.BlockSpec((1, d), lambda i: (0, 0)),
                  pl.BlockSpec(memory_space=pl.ANY)],
        out_specs=pl.BlockSpec((tt, d), lambda i: (i, 0)),
        out_shape=jax.ShapeDtypeStruct((t, d), F32),
        scratch_shapes=[pltpu.VMEM((TOP_K, tt, d), F32), pltpu.SemaphoreType.DMA(())],
        compiler_params=pltpu.CompilerParams(
            dimension_semantics=("arbitrary",), vmem_limit_bytes=VMEM_LIMIT_BYTES),
        name="combine",
    )(pos3, h2, meta, norm_g.reshape(1, d), ys)


def _moe(h2, norm_ffn_g, router_w, router_b, w1, b1, w2, b2, norm_final_g):
    t, d = h2.shape
    n_exp = router_w.shape[1]
    tm = SLOT_TILE
    meta, counts = _router(h2, norm_ffn_g, router_w, router_b)
    eidx = meta[:, 0:TOP_K].astype(jnp.int32)
    rank = meta[:, 2 * TOP_K:3 * TOP_K].astype(jnp.int32)
    cnt = counts[0, 0:n_exp].astype(jnp.int32)
    padded = (cnt + tm - 1) // tm * tm
    seg_end = jnp.cumsum(padded)
    seg_start = seg_end - padded
    pos = (seg_start[eidx] + rank).reshape(-1)
    n_tiles = -(-(t * TOP_K) // tm) + n_exp
    tile_start = jnp.arange(n_tiles, dtype=jnp.int32) * tm
    tile_expert = jnp.minimum(jnp.searchsorted(seg_end, tile_start, side="right"), n_exp - 1).astype(jnp.int32)
    tile_valid = jnp.clip(seg_start[tile_expert] + cnt[tile_expert] - tile_start, 0, tm).astype(jnp.int32)
    xs = _dispatch(h2, norm_ffn_g, pos, n_tiles * tm)
    ys = _experts(xs, tile_expert, tile_valid, w1, b1, w2, b2)
    return _combine(h2, meta, pos, ys, norm_final_g)


def kernel(x, norm_mix_g, w_in, conv_w, shift_mu, decay_w0, decay_w2, iclr_a0, iclr_a2, gate_g2, k_k, k_a,
           r_k, ln_x_g, ln_x_b, w_out, norm_ffn_g, router_w, router_b, exp_w1, exp_b1, exp_w2, exp_b2,
           norm_final_g):
    bsz, s, d = x.shape
    depth = w_in.shape[0]
    assert depth == 1, "final norm is fused into the last layer's combine kernel"
    h = _mixer(x, norm_mix_g[0], w_in[0], conv_w[0], shift_mu[0], decay_w0[0], decay_w2[0], iclr_a0[0],
               iclr_a2[0], gate_g2[0], k_k[0], k_a[0], r_k[0], ln_x_g[0], ln_x_b[0], w_out[0])
    out = _moe(h.reshape(bsz * s, d), norm_ffn_g[0], router_w[0], router_b[0], exp_w1[0], exp_b1[0],
               exp_w2[0], exp_b2[0], norm_final_g)
    return out.reshape(bsz, s, d)
```

```python
import functools

import jax
import jax.numpy as jnp
from jax import lax
from jax.experimental import pallas as pl
from jax.experimental.pallas import tpu as pltpu

HEAD = 64
DECAY_LORA = 64
AAA_LORA = 64
GATE_LORA = 160
TOP_K = 4
SWIGLU_ALPHA = 1.702
SWIGLU_LIMIT = 7.0
NORM_EPS = 1e-5
GN_EPS = 64e-5
DECAY_SCALE = 0.6065306597126334

LANES = 128
SUBLANES = 8
VMEM_LIMIT_BYTES = 56 * 1024 * 1024

RWKV_CHUNK = 64
MIX_TOKENS = 256
ROUTE_TOKENS = 512
MOVE_TOKENS = 256
SLOT_TILE = 256

BF16 = jnp.bfloat16
F32 = jnp.float32


def _dot(a, b):
    return jnp.dot(a.astype(BF16), b.astype(BF16), preferred_element_type=F32)


def _dot_nt(a, b):
    return lax.dot_general(a.astype(BF16), b.astype(BF16), (((1,), (1,)), ((), ())),
                           preferred_element_type=F32)


def _dot_tn(a, b):
    return lax.dot_general(a.astype(BF16), b.astype(BF16), (((0,), (0,)), ((), ())),
                           preferred_element_type=F32)


def _sigmoid(x):
    return 0.5 * jnp.tanh(0.5 * x) + 0.5


def _shift_rows(x, n, carry):
    rolled = pltpu.roll(x, n, axis=0)
    row = lax.broadcasted_iota(jnp.int32, x.shape, 0)
    out = rolled
    for i in range(n):
        out = jnp.where(row == i, carry[SUBLANES - n + i:SUBLANES - n + i + 1, :], out)
    return out


def _mixer_kernel(x_ref, g_ref, wmain_ref, wlora_ref, wgate_ref, cw_ref, mu_rkv_ref, mu_lora_ref,
                  w0_ref, w2_ref, a0_ref, a2_ref, g2_ref, kk_ref, ka_ref, rk_ref, lng_ref, lnb_ref,
                  wout_ref, hsum_ref, o_ref,
                  cu_s, cp_s, cl_s, ar_s, bk_s, v_s, rkb_s, y_s, gam_s, *state_s,
                  n_heads, d_model):
    tc = x_ref.shape[1]
    L = RWKV_CHUNK
    n_sub = tc // L
    D = d_model

    @pl.when(pl.program_id(1) == 0)
    def _():
        for st_ref in state_s:
            st_ref[...] = jnp.zeros_like(st_ref)
        cu_s[...] = jnp.zeros_like(cu_s)
        cp_s[...] = jnp.zeros_like(cp_s)
        cl_s[...] = jnp.zeros_like(cl_s)

    x = x_ref[0]
    ms = jnp.mean(x * x, axis=-1, keepdims=True)
    xn = (x * lax.rsqrt(ms + NORM_EPS) * g_ref[...]).astype(BF16)

    pc = jnp.dot(xn, wmain_ref[:, 0:3 * D], preferred_element_type=F32)
    u = pc[:, D:2 * D] * pc[:, 2 * D:3 * D]
    cu = cu_s[...]
    conv = (cw_ref[0:1, :] * _shift_rows(u, 2, cu) + cw_ref[1:2, :] * _shift_rows(u, 1, cu)
            + cw_ref[2:3, :] * u)
    y_conv = pc[:, 0:D] * conv
    cu_s[...] = u[tc - SUBLANES:tc, :]

    pr = jnp.dot(xn, wmain_ref[:, 3 * D:6 * D], preferred_element_type=F32)
    cp = cp_s[...]
    cp_s[...] = pr[tc - SUBLANES:tc, :]
    pr = pr + (_shift_rows(pr, 1, cp) - pr) * mu_rkv_ref[...]
    plo = jnp.dot(xn, wlora_ref[...], preferred_element_type=F32)
    cl = cl_s[...]
    cl_s[...] = plo[tc - SUBLANES:tc, :]
    plo = plo + (_shift_rows(plo, 1, cl) - plo) * mu_lora_ref[...]

    r = pr[:, 0:D]
    k = pr[:, D:2 * D]
    v = pr[:, 2 * D:3 * D]
    wd = plo[:, 0:LANES]
    ad = plo[:, LANES:2 * LANES]
    gd = plo[:, 2 * LANES:4 * LANES]

    lw = -DECAY_SCALE * _sigmoid(w0_ref[...] + _dot(jnp.tanh(wd), w2_ref[...]))
    a = _sigmoid(a0_ref[...] + _dot(ad, a2_ref[...]))
    g = _dot(_sigmoid(gd), g2_ref[...])

    row = lax.broadcasted_iota(jnp.int32, (tc, tc), 0)
    col = lax.broadcasted_iota(jnp.int32, (tc, tc), 1)
    tri = jnp.where((row >= col) & ((row // L) == (col // L)), 1.0, 0.0).astype(BF16)
    lw_hi = lw.astype(BF16)
    lw_lo = (lw - lw_hi.astype(F32)).astype(BF16)
    cum = (jnp.dot(tri, lw_hi, preferred_element_type=F32)
           + jnp.dot(tri, lw_lo, preferred_element_type=F32))
    e_inv = jnp.exp(-cum)

    kkraw = k * kk_ref[...]
    ss = _dot(kkraw * kkraw, hsum_ref[...])
    kkn = kkraw * jnp.minimum(lax.rsqrt(ss), 1e12)

    k2 = k * (1.0 + (a - 1.0) * ka_ref[...])
    a_t = kkn * jnp.exp(cum - lw)
    r_t = r * jnp.exp(cum)
    b_t = kkn * a * e_inv
    k_t = k2 * e_inv
    for c in range(n_sub):
        rows = slice(c * L, (c + 1) * L)
        ar_s[c, 0:L, :] = a_t[rows]
        ar_s[c, L:2 * L, :] = r_t[rows]
        bk_s[c, 0:L, :] = b_t[rows]
        bk_s[c, L:2 * L, :] = k_t[rows]
        gam_s[c] = jnp.exp(cum[(c + 1) * L - 1:(c + 1) * L, :])
    v_s[...] = v
    rkb_s[...] = r * k2 * rk_ref[...]

    ri = lax.broadcasted_iota(jnp.int32, (L, L), 0)
    ci = lax.broadcasted_iota(jnp.int32, (L, L), 1)
    strict = ri > ci
    eye = jnp.where(ri == ci, 1.0, 0.0)
    ri2 = lax.broadcasted_iota(jnp.int32, (L, 2 * L), 0)
    ci2 = lax.broadcasted_iota(jnp.int32, (L, 2 * L), 1)
    incl2 = ri2 >= (ci2 % L)
    n_double = L.bit_length() - 2

    def sub_chunk(c, carry):
        r0 = pl.multiple_of(c * L, L)
        heads = range(n_heads)
        hsl = [slice(h * HEAD, (h + 1) * HEAD) for h in heads]
        ar = [ar_s[c, :, hsl[h]].astype(BF16) for h in heads]
        bk = [bk_s[c, :, hsl[h]].astype(BF16) for h in heads]
        vh = [v_s[pl.ds(r0, L), hsl[h]] for h in heads]
        st = [state_s[h][...] for h in heads]
        gm = [_dot_nt(ar[h], bk[h]) for h in heads]
        nm = [jnp.where(strict, gm[h][0:L, 0:L], 0.0) for h in heads]
        mak = [jnp.where(strict, gm[h][0:L, L:2 * L], 0.0) for h in heads]
        q = [jnp.where(incl2, gm[h][L:2 * L, :], 0.0) for h in heads]
        xinv = [eye - nm[h] for h in heads]
        p = nm
        for _ in range(n_double):
            p = [_dot(p[h], p[h]) for h in heads]
            xinv = [xinv[h] + _dot(xinv[h], p[h]) for h in heads]
        ars = [_dot_nt(ar[h], st[h]) for h in heads]
        mv = [_dot(mak[h], vh[h]) for h in heads]
        uu = [_dot(xinv[h], ars[h][0:L] + mv[h]) for h in heads]
        uv = [jnp.concatenate([-uu[h], vh[h]], axis=0).astype(BF16) for h in heads]
        yh = [ars[h][L:2 * L] + _dot(q[h], uv[h]) for h in heads]
        for h in heads:
            state_s[h][...] = (st[h] + _dot_tn(uv[h], bk[h])) * gam_s[c, :, hsl[h]]
        for h in heads:
            mean = jnp.mean(yh[h], axis=-1, keepdims=True)
            yc = yh[h] - mean
            var = jnp.mean(yc * yc, axis=-1, keepdims=True)
            bonus = jnp.sum(rkb_s[pl.ds(r0, L), hsl[h]], axis=-1, keepdims=True) * vh[h]
            y_s[pl.ds(r0, L), hsl[h]] = (yc * lax.rsqrt(var + GN_EPS) * lng_ref[:, hsl[h]]
                                         + lnb_ref[:, hsl[h]] + bonus)
        return carry

    lax.fori_loop(0, n_sub, sub_chunk, 0)

    y_rwkv = y_s[...] * g
    gates = _sigmoid(jnp.dot(xn, wgate_ref[...], preferred_element_type=F32))
    mix = gates[:, 0:D] * y_conv + gates[:, D:2 * D] * y_rwkv
    o_ref[0] = x + jnp.dot(mix.astype(BF16), wout_ref[...], preferred_element_type=F32)


def _const_spec(shape):
    nd = len(shape)
    return pl.BlockSpec(shape, lambda *_: (0,) * nd, pipeline_mode=pl.Buffered(1))


def _mixer(x, norm_g, w_in, conv_w, shift_mu, w0, w2, a0, a2, g2, k_k, k_a, r_k, ln_g, ln_b, w_out):
    bsz, s, d = x.shape
    n_heads = d // HEAD
    tc = min(MIX_TOKENS, s)
    L = RWKV_CHUNK
    n_sub = tc // L
    lora0 = 6 * d
    w_main = w_in[:, 0:6 * d].astype(BF16)
    pad = lambda t, n: jnp.pad(t, ((0, 0), (0, n - t.shape[1])))
    lora_cols = (DECAY_LORA, AAA_LORA, GATE_LORA)
    lora_pads = (LANES, LANES, 2 * LANES)
    pieces_w, pieces_mu, off = [], [], lora0
    for n, p in zip(lora_cols, lora_pads):
        pieces_w.append(pad(w_in[:, off:off + n], p))
        pieces_mu.append(pad(shift_mu[None, off - 3 * d:off - 3 * d + n], p))
        off += n
    w_lora = jnp.concatenate(pieces_w, axis=1).astype(BF16)
    mu_lora = jnp.concatenate(pieces_mu, axis=1)
    w_gate = w_in[:, off:off + 2 * d].astype(BF16)
    mu_rkv = shift_mu[None, 0:3 * d]
    padr = lambda t, n: jnp.pad(t, ((0, n - t.shape[0]), (0, 0)))
    w2p = padr(w2, LANES).astype(BF16)
    a2p = padr(a2, LANES).astype(BF16)
    g2p = padr(g2, 2 * LANES).astype(BF16)
    row = lambda t: t.reshape(1, -1)
    head_of = jnp.arange(d, dtype=jnp.int32) // HEAD
    head_sum = (head_of[:, None] == head_of[None, :]).astype(BF16)
    consts = [row(norm_g), w_main, w_lora, w_gate, conv_w, mu_rkv, mu_lora, row(w0), w2p, row(a0), a2p,
              g2p, row(k_k), row(k_a), row(r_k), row(ln_g), row(ln_b), w_out.astype(BF16), head_sum]
    kern = functools.partial(_mixer_kernel, n_heads=n_heads, d_model=d)
    return pl.pallas_call(
        kern,
        grid=(bsz, s // tc),
        in_specs=[pl.BlockSpec((1, tc, d), lambda b, c: (b, c, 0))] + [_const_spec(t.shape) for t in consts],
        out_specs=pl.BlockSpec((1, tc, d), lambda b, c: (b, c, 0)),
        out_shape=jax.ShapeDtypeStruct((bsz, s, d), F32),
        scratch_shapes=[
            pltpu.VMEM((SUBLANES, d), F32),
            pltpu.VMEM((SUBLANES, 3 * d), F32),
            pltpu.VMEM((SUBLANES, 4 * LANES), F32),
            pltpu.VMEM((n_sub, 2 * L, d), F32),
            pltpu.VMEM((n_sub, 2 * L, d), F32),
            pltpu.VMEM((tc, d), F32),
            pltpu.VMEM((tc, d), F32),
            pltpu.VMEM((tc, d), F32),
            pltpu.VMEM((n_sub, 1, d), F32),
        ] + [pltpu.VMEM((HEAD, HEAD), F32)] * n_heads,
        compiler_params=pltpu.CompilerParams(
            dimension_semantics=("arbitrary", "arbitrary"), vmem_limit_bytes=VMEM_LIMIT_BYTES),
        name="mixer",
    )(x, *consts)


def _router_kernel(h_ref, g_ref, rw_ref, rb_ref, meta_ref, cnt_ref, run_s, *, n_experts):
    tt = h_ref.shape[0]

    @pl.when(pl.program_id(0) == 0)
    def _():
        run_s[...] = jnp.zeros_like(run_s)

    h = h_ref[...]
    ms = jnp.mean(h * h, axis=-1, keepdims=True)
    hn = h * lax.rsqrt(ms + NORM_EPS) * g_ref[...]
    logits = jnp.dot(hn, rw_ref[...], preferred_element_type=F32,
                     precision=lax.Precision.HIGHEST) + rb_ref[...]
    lane = lax.broadcasted_iota(jnp.int32, (tt, LANES), 1)
    neg = jnp.float32(-jnp.inf)
    work = jnp.where(lane < n_experts, logits, neg)
    vals, idxs = [], []
    for _ in range(TOP_K):
        m = jnp.max(work, axis=-1, keepdims=True)
        i = jnp.min(jnp.where(work == m, lane, LANES), axis=-1, keepdims=True)
        vals.append(m)
        idxs.append(i)
        work = jnp.where(lane == i, neg, work)
    ex = [jnp.exp(vv - vals[0]) for vv in vals]
    den = ex[0] + ex[1] + ex[2] + ex[3]
    gates = [e / den for e in ex]

    onehot = jnp.zeros((tt, LANES), jnp.bool_)
    for kk in range(TOP_K):
        onehot = onehot | (lane == (idxs[kk] + kk * n_experts))
    oh = jnp.where(onehot, 1.0, 0.0)
    ri = lax.broadcasted_iota(jnp.int32, (tt, tt), 0)
    ci = lax.broadcasted_iota(jnp.int32, (tt, tt), 1)
    tri = jnp.where(ri > ci, 1.0, 0.0).astype(BF16)
    cnt = jnp.dot(tri, oh.astype(BF16), preferred_element_type=F32)
    tot = jnp.broadcast_to(jnp.sum(oh, axis=0, keepdims=True), (SUBLANES, LANES))
    lane8 = lax.broadcasted_iota(jnp.int32, (SUBLANES, LANES), 1)
    pk = jnp.zeros_like(tot)
    te = tot
    for j in range(1, TOP_K):
        rolled = pltpu.roll(tot, j * n_experts, axis=1)
        pk = pk + jnp.where(lane8 >= j * n_experts, rolled, 0.0)
        te = te + rolled
    before = cnt + (run_s[...] + pk)[0:1, :]
    ranks = [jnp.sum(jnp.where(onehot & (lane // n_experts == kk), before, 0.0), axis=-1, keepdims=True)
             for kk in range(TOP_K)]
    run_s[...] = run_s[...] + te
    cnt_ref[...] = run_s[...]

    meta = jnp.zeros((tt, LANES), F32)
    for kk in range(TOP_K):
        meta = jnp.where(lane == kk, idxs[kk].astype(F32), meta)
        meta = jnp.where(lane == TOP_K + kk, gates[kk], meta)
        meta = jnp.where(lane == 2 * TOP_K + kk, ranks[kk], meta)
    meta_ref[...] = meta


def _router(h2, norm_g, router_w, router_b):
    t, d = h2.shape
    n_experts = router_w.shape[1]
    tt = min(ROUTE_TOKENS, t)
    rw = jnp.pad(router_w, ((0, 0), (0, LANES - n_experts)))
    rb = jnp.pad(router_b, (0, LANES - n_experts)).reshape(1, LANES)
    kern = functools.partial(_router_kernel, n_experts=n_experts)
    return pl.pallas_call(
        kern,
        grid=(t // tt,),
        in_specs=[pl.BlockSpec((tt, d), lambda i: (i, 0)),
                  pl.BlockSpec((1, d), lambda i: (0, 0)),
                  pl.BlockSpec((d, LANES), lambda i: (0, 0)),
                  pl.BlockSpec((1, LANES), lambda i: (0, 0))],
        out_specs=[pl.BlockSpec((tt, LANES), lambda i: (i, 0)),
                   pl.BlockSpec((SUBLANES, LANES), lambda i: (0, 0))],
        out_shape=[jax.ShapeDtypeStruct((t, LANES), F32),
                   jax.ShapeDtypeStruct((SUBLANES, LANES), F32)],
        scratch_shapes=[pltpu.VMEM((SUBLANES, LANES), F32)],
        compiler_params=pltpu.CompilerParams(
            dimension_semantics=("arbitrary",), vmem_limit_bytes=VMEM_LIMIT_BYTES),
        name="router",
    )(h2, norm_g.reshape(1, d), rw, rb)


def _dispatch_kernel(pos_ref, h_ref, g_ref, xs_ref, hn_s, sem):
    tt = h_ref.shape[0]
    h = h_ref[...]
    ms = jnp.mean(h * h, axis=-1, keepdims=True)
    hn_s[...] = h * lax.rsqrt(ms + NORM_EPS) * g_ref[...]

    def row_copy(t, slot):
        return pltpu.make_async_copy(hn_s.at[pl.ds(t, 1), :], xs_ref.at[pl.ds(slot, 1), :], sem)

    def issue(t, carry):
        for kk in range(TOP_K):
            row_copy(t, pos_ref[0, 0, t * TOP_K + kk]).start()
        return carry

    lax.fori_loop(0, tt, issue, 0)

    def drain(t, carry):
        for kk in range(TOP_K):
            row_copy(t, pos_ref[0, 0, t * TOP_K + kk]).wait()
        return carry

    lax.fori_loop(0, tt, drain, 0)


def _dispatch(h2, norm_g, pos, n_slots):
    t, d = h2.shape
    tt = min(MOVE_TOKENS, t)
    pos3 = pos.reshape(t // tt, 1, tt * TOP_K)
    return pl.pallas_call(
        _dispatch_kernel,
        grid=(t // tt,),
        in_specs=[pl.BlockSpec((1, 1, tt * TOP_K), lambda i: (i, 0, 0), memory_space=pltpu.SMEM),
                  pl.BlockSpec((tt, d), lambda i: (i, 0)),
                  pl.BlockSpec((1, d), lambda i: (0, 0))],
        out_specs=pl.BlockSpec(memory_space=pl.ANY),
        out_shape=jax.ShapeDtypeStruct((n_slots, d), F32),
        scratch_shapes=[pltpu.VMEM((tt, d), F32), pltpu.SemaphoreType.DMA(())],
        compiler_params=pltpu.CompilerParams(
            dimension_semantics=("arbitrary",), vmem_limit_bytes=VMEM_LIMIT_BYTES,
            has_side_effects=True),
        name="dispatch",
    )(pos3, h2, norm_g.reshape(1, d))


def _experts_kernel(te_ref, nv_ref, xs_ref, w1_ref, b1_ref, w2_ref, b2_ref, ys_ref, w1_s, w2_s, *, d_ff):
    i = pl.program_id(0)
    prev = te_ref[jnp.maximum(i - 1, 0)]
    changed = (i == 0) | (te_ref[i] != prev)

    @pl.when(changed)
    def _():
        w1_s[...] = w1_ref[0].astype(BF16)
        w2_s[...] = w2_ref[0].astype(BF16)

    nvalid = nv_ref[i]

    @pl.when(nvalid > 0)
    def _():
        tm = xs_ref.shape[0]
        row = lax.broadcasted_iota(jnp.int32, (tm, 1), 0)
        x = jnp.where(row < nvalid, xs_ref[...], 0.0).astype(BF16)
        u = jnp.dot(x, w1_s[...], preferred_element_type=F32) + b1_ref[0]
        glu = jnp.minimum(u[:, 0:d_ff], SWIGLU_LIMIT)
        lin = jnp.clip(u[:, d_ff:2 * d_ff], -SWIGLU_LIMIT, SWIGLU_LIMIT)
        act = glu * _sigmoid(SWIGLU_ALPHA * glu) * (lin + 1.0)
        ys_ref[...] = jnp.dot(act.astype(BF16), w2_s[...], preferred_element_type=F32) + b2_ref[0]

    @pl.when(nvalid <= 0)
    def _():
        ys_ref[...] = jnp.zeros_like(ys_ref)


def _experts(xs, tile_expert, tile_valid, w1, b1, w2, b2):
    n_slots, d = xs.shape
    n_exp, _, two_ff = w1.shape
    d_ff = two_ff // 2
    tm = SLOT_TILE
    n_tiles = n_slots // tm
    kern = functools.partial(_experts_kernel, d_ff=d_ff)
    grid_spec = pltpu.PrefetchScalarGridSpec(
        num_scalar_prefetch=2,
        grid=(n_tiles,),
        in_specs=[pl.BlockSpec((tm, d), lambda i, te, nv: (i, 0)),
                  pl.BlockSpec((1, d, two_ff), lambda i, te, nv: (te[i], 0, 0)),
                  pl.BlockSpec((1, 1, two_ff), lambda i, te, nv: (te[i], 0, 0)),
                  pl.BlockSpec((1, d_ff, d), lambda i, te, nv: (te[i], 0, 0)),
                  pl.BlockSpec((1, 1, d), lambda i, te, nv: (te[i], 0, 0))],
        out_specs=pl.BlockSpec((tm, d), lambda i, te, nv: (i, 0)),
        scratch_shapes=[pltpu.VMEM((d, two_ff), BF16), pltpu.VMEM((d_ff, d), BF16)],
    )
    return pl.pallas_call(
        kern,
        grid_spec=grid_spec,
        out_shape=jax.ShapeDtypeStruct((n_slots, d), F32),
        compiler_params=pltpu.CompilerParams(
            dimension_semantics=("arbitrary",), vmem_limit_bytes=VMEM_LIMIT_BYTES),
        name="experts",
    )(tile_expert, tile_valid, xs, w1, b1.reshape(n_exp, 1, two_ff), w2, b2.reshape(n_exp, 1, d))


def _combine_kernel(pos_ref, h_ref, meta_ref, g_ref, ys_ref, o_ref, buf_s, sem):
    tt = h_ref.shape[0]

    def row_copy(t, kk, slot):
        return pltpu.make_async_copy(ys_ref.at[pl.ds(slot, 1), :], buf_s.at[kk, pl.ds(t, 1), :], sem)

    def issue(t, carry):
        for kk in range(TOP_K):
            row_copy(t, kk, pos_ref[0, 0, t * TOP_K + kk]).start()
        return carry

    lax.fori_loop(0, tt, issue, 0)

    def drain(t, carry):
        for kk in range(TOP_K):
            row_copy(t, kk, pos_ref[0, 0, t * TOP_K + kk]).wait()
        return carry

    lax.fori_loop(0, tt, drain, 0)

    acc = h_ref[...]
    for kk in range(TOP_K):
        acc = acc + meta_ref[:, TOP_K + kk:TOP_K + kk + 1] * buf_s[kk]
    ms = jnp.mean(acc * acc, axis=-1, keepdims=True)
    o_ref[...] = acc * lax.rsqrt(ms + NORM_EPS) * g_ref[...]


def _combine(h2, meta, pos, ys, norm_g):
    t, d = h2.shape
    tt = min(MOVE_TOKENS, t)
    pos3 = pos.reshape(t // tt, 1, tt * TOP_K)
    return pl.pallas_call(
        _combine_kernel,
        grid=(t // tt,),
        in_specs=[pl.BlockSpec((1, 1, tt * TOP_K), lambda i: (i, 0, 0), memory_space=pltpu.SMEM),
                  pl.BlockSpec((tt, d), lambda i: (i, 0)),
                  pl.BlockSpec((tt, LANES), lambda i: (i, 0)),
                  pl.BlockSpec((1, d), lambda i: (0, 0)),
                  pl.BlockSpec(memory_space=pl.ANY)],
        out_specs=pl.BlockSpec((tt, d), lambda i: (i, 0)),
        out_shape=jax.ShapeDtypeStruct((t, d), F32),
        scratch_shapes=[pltpu.VMEM((TOP_K, tt, d), F32), pltpu.SemaphoreType.DMA(())],
        compiler_params=pltpu.CompilerParams(
            dimension_semantics=("arbitrary",), vmem_limit_bytes=VMEM_LIMIT_BYTES),
        name="combine",
    )(pos3, h2, meta, norm_g.reshape(1, d), ys)


def _moe(h2, norm_ffn_g, router_w, router_b, w1, b1, w2, b2, norm_final_g):
    t, d = h2.shape
    n_exp = router_w.shape[1]
    tm = SLOT_TILE
    meta, counts = _router(h2, norm_ffn_g, router_w, router_b)
    eidx = meta[:, 0:TOP_K].astype(jnp.int32)
    rank = meta[:, 2 * TOP_K:3 * TOP_K].astype(jnp.int32)
    cnt = counts[0, 0:n_exp].astype(jnp.int32)
    padded = (cnt + tm - 1) // tm * tm
    seg_end = jnp.cumsum(padded)
    seg_start = seg_end - padded
    pos = (seg_start[eidx] + rank).reshape(-1)
    n_tiles = -(-(t * TOP_K) // tm) + n_exp
    tile_start = jnp.arange(n_tiles, dtype=jnp.int32) * tm
    tile_expert = jnp.minimum(jnp.searchsorted(seg_end, tile_start, side="right"), n_exp - 1).astype(jnp.int32)
    tile_valid = jnp.clip(seg_start[tile_expert] + cnt[tile_expert] - tile_start, 0, tm).astype(jnp.int32)
    xs = _dispatch(h2, norm_ffn_g, pos, n_tiles * tm)
    ys = _experts(xs, tile_expert, tile_valid, w1, b1, w2, b2)
    return _combine(h2, meta, pos, ys, norm_final_g)


def kernel(x, norm_mix_g, w_in, conv_w, shift_mu, decay_w0, decay_w2, iclr_a0, iclr_a2, gate_g2, k_k, k_a,
           r_k, ln_x_g, ln_x_b, w_out, norm_ffn_g, router_w, router_b, exp_w1, exp_b1, exp_w2, exp_b2,
           norm_final_g):
    bsz, s, d = x.shape
    depth = w_in.shape[0]
    assert depth == 1, "final norm is fused into the last layer's combine kernel"
    h = _mixer(x, norm_mix_g[0], w_in[0], conv_w[0], shift_mu[0], decay_w0[0], decay_w2[0], iclr_a0[0],
               iclr_a2[0], gate_g2[0], k_k[0], k_a[0], r_k[0], ln_x_g[0], ln_x_b[0], w_out[0])
    out = _moe(h.reshape(bsz * s, d), norm_ffn_g[0], router_w[0], router_b[0], exp_w1[0], exp_b1[0],
               exp_w2[0], exp_b2[0], norm_final_g)
    return out.reshape(bsz, s, d)
```

```python
import functools

import jax
import jax.numpy as jnp
from jax import lax
from jax.experimental import pallas as pl
from jax.experimental.pallas import tpu as pltpu
from jax.experimental.pallas import tpu_sc as plsc

HEAD = 64
DECAY_LORA = 64
AAA_LORA = 64
GATE_LORA = 160
TOP_K = 4
SWIGLU_ALPHA = 1.702
SWIGLU_LIMIT = 7.0
NORM_EPS = 1e-5
GN_EPS = 64e-5
DECAY_SCALE = 0.6065306597126334

LANES = 128
SUBLANES = 8
VMEM_LIMIT_BYTES = 56 * 1024 * 1024

RWKV_CHUNK = 64
MIX_TOKENS = 256
ROUTE_TOKENS = 512
MOVE_TOKENS = 256
GATHER_WINDOW = 32
SLOT_TILE = 256

BF16 = jnp.bfloat16
F32 = jnp.float32


def _dot(a, b):
    return jnp.dot(a.astype(BF16), b.astype(BF16), preferred_element_type=F32)


def _dot_nt(a, b):
    return lax.dot_general(a.astype(BF16), b.astype(BF16), (((1,), (1,)), ((), ())),
                           preferred_element_type=F32)


def _dot_tn(a, b):
    return lax.dot_general(a.astype(BF16), b.astype(BF16), (((0,), (0,)), ((), ())),
                           preferred_element_type=F32)


def _sigmoid(x):
    return 0.5 * jnp.tanh(0.5 * x) + 0.5


def _shift_rows(x, n, carry):
    rolled = pltpu.roll(x, n, axis=0)
    row = lax.broadcasted_iota(jnp.int32, x.shape, 0)
    out = rolled
    for i in range(n):
        out = jnp.where(row == i, carry[SUBLANES - n + i:SUBLANES - n + i + 1, :], out)
    return out


def _mixer_kernel(x_ref, g_ref, wmain_ref, wlora_ref, wgate_ref, cw_ref, mu_rkv_ref, mu_lora_ref,
                  w0_ref, w2_ref, a0_ref, a2_ref, g2_ref, kk_ref, ka_ref, rk_ref, lng_ref, lnb_ref,
                  wout_ref, hsum_ref, o_ref,
                  cu_s, cp_s, cl_s, ar_s, bk_s, v_s, rkb_s, y_s, gam_s, *state_s,
                  n_heads, d_model):
    tc = x_ref.shape[1]
    L = RWKV_CHUNK
    n_sub = tc // L
    D = d_model

    @pl.when(pl.program_id(1) == 0)
    def _():
        for st_ref in state_s:
            st_ref[...] = jnp.zeros_like(st_ref)
        cu_s[...] = jnp.zeros_like(cu_s)
        cp_s[...] = jnp.zeros_like(cp_s)
        cl_s[...] = jnp.zeros_like(cl_s)

    x = x_ref[0]
    ms = jnp.mean(x * x, axis=-1, keepdims=True)
    xn = (x * lax.rsqrt(ms + NORM_EPS) * g_ref[...]).astype(BF16)

    pc = jnp.dot(xn, wmain_ref[:, 0:3 * D], preferred_element_type=F32)
    u = pc[:, D:2 * D] * pc[:, 2 * D:3 * D]
    cu = cu_s[...]
    conv = (cw_ref[0:1, :] * _shift_rows(u, 2, cu) + cw_ref[1:2, :] * _shift_rows(u, 1, cu)
            + cw_ref[2:3, :] * u)
    y_conv = pc[:, 0:D] * conv
    cu_s[...] = u[tc - SUBLANES:tc, :]

    pr = jnp.dot(xn, wmain_ref[:, 3 * D:6 * D], preferred_element_type=F32)
    cp = cp_s[...]
    cp_s[...] = pr[tc - SUBLANES:tc, :]
    pr = pr + (_shift_rows(pr, 1, cp) - pr) * mu_rkv_ref[...]
    plo = jnp.dot(xn, wlora_ref[...], preferred_element_type=F32)
    cl = cl_s[...]
    cl_s[...] = plo[tc - SUBLANES:tc, :]
    plo = plo + (_shift_rows(plo, 1, cl) - plo) * mu_lora_ref[...]

    r = pr[:, 0:D]
    k = pr[:, D:2 * D]
    v = pr[:, 2 * D:3 * D]
    wd = plo[:, 0:LANES]
    ad = plo[:, LANES:2 * LANES]
    gd = plo[:, 2 * LANES:4 * LANES]

    lw = -DECAY_SCALE * _sigmoid(w0_ref[...] + _dot(jnp.tanh(wd), w2_ref[...]))
    a = _sigmoid(a0_ref[...] + _dot(ad, a2_ref[...]))
    g = _dot(_sigmoid(gd), g2_ref[...])

    row = lax.broadcasted_iota(jnp.int32, (tc, tc), 0)
    col = lax.broadcasted_iota(jnp.int32, (tc, tc), 1)
    tri = jnp.where((row >= col) & ((row // L) == (col // L)), 1.0, 0.0).astype(BF16)
    lw_hi = lw.astype(BF16)
    lw_lo = (lw - lw_hi.astype(F32)).astype(BF16)
    cum = (jnp.dot(tri, lw_hi, preferred_element_type=F32)
           + jnp.dot(tri, lw_lo, preferred_element_type=F32))
    e_inv = jnp.exp(-cum)

    kkraw = k * kk_ref[...]
    ss = _dot(kkraw * kkraw, hsum_ref[...])
    kkn = kkraw * jnp.minimum(lax.rsqrt(ss), 1e12)

    k2 = k * (1.0 + (a - 1.0) * ka_ref[...])
    a_t = kkn * jnp.exp(cum - lw)
    r_t = r * jnp.exp(cum)
    b_t = kkn * a * e_inv
    k_t = k2 * e_inv
    for c in range(n_sub):
        rows = slice(c * L, (c + 1) * L)
        ar_s[c, 0:L, :] = a_t[rows]
        ar_s[c, L:2 * L, :] = r_t[rows]
        bk_s[c, 0:L, :] = b_t[rows]
        bk_s[c, L:2 * L, :] = k_t[rows]
        gam_s[c] = jnp.exp(cum[(c + 1) * L - 1:(c + 1) * L, :])
    v_s[...] = v
    rkb_s[...] = r * k2 * rk_ref[...]

    ri = lax.broadcasted_iota(jnp.int32, (L, L), 0)
    ci = lax.broadcasted_iota(jnp.int32, (L, L), 1)
    strict = ri > ci
    eye = jnp.where(ri == ci, 1.0, 0.0)
    ri2 = lax.broadcasted_iota(jnp.int32, (L, 2 * L), 0)
    ci2 = lax.broadcasted_iota(jnp.int32, (L, 2 * L), 1)
    incl2 = ri2 >= (ci2 % L)
    n_double = L.bit_length() - 2

    def sub_chunk(c, carry):
        r0 = pl.multiple_of(c * L, L)
        heads = range(n_heads)
        hsl = [slice(h * HEAD, (h + 1) * HEAD) for h in heads]
        ar = [ar_s[c, :, hsl[h]].astype(BF16) for h in heads]
        bk = [bk_s[c, :, hsl[h]].astype(BF16) for h in heads]
        vh = [v_s[pl.ds(r0, L), hsl[h]] for h in heads]
        st = [state_s[h][...] for h in heads]
        gm = [_dot_nt(ar[h], bk[h]) for h in heads]
        nm = [jnp.where(strict, gm[h][0:L, 0:L], 0.0) for h in heads]
        mak = [jnp.where(strict, gm[h][0:L, L:2 * L], 0.0) for h in heads]
        q = [jnp.where(incl2, gm[h][L:2 * L, :], 0.0) for h in heads]
        xinv = [eye - nm[h] for h in heads]
        p = nm
        for _ in range(n_double):
            p = [_dot(p[h], p[h]) for h in heads]
            xinv = [xinv[h] + _dot(xinv[h], p[h]) for h in heads]
        ars = [_dot_nt(ar[h], st[h]) for h in heads]
        mv = [_dot(mak[h], vh[h]) for h in heads]
        uu = [_dot(xinv[h], ars[h][0:L] + mv[h]) for h in heads]
        uv = [jnp.concatenate([-uu[h], vh[h]], axis=0).astype(BF16) for h in heads]
        yh = [ars[h][L:2 * L] + _dot(q[h], uv[h]) for h in heads]
        for h in heads:
            state_s[h][...] = (st[h] + _dot_tn(uv[h], bk[h])) * gam_s[c, :, hsl[h]]
        for h in heads:
            mean = jnp.mean(yh[h], axis=-1, keepdims=True)
            yc = yh[h] - mean
            var = jnp.mean(yc * yc, axis=-1, keepdims=True)
            bonus = jnp.sum(rkb_s[pl.ds(r0, L), hsl[h]], axis=-1, keepdims=True) * vh[h]
            y_s[pl.ds(r0, L), hsl[h]] = (yc * lax.rsqrt(var + GN_EPS) * lng_ref[:, hsl[h]]
                                         + lnb_ref[:, hsl[h]] + bonus)
        return carry

    lax.fori_loop(0, n_sub, sub_chunk, 0)

    y_rwkv = y_s[...] * g
    gates = _sigmoid(jnp.dot(xn, wgate_ref[...], preferred_element_type=F32))
    mix = gates[:, 0:D] * y_conv + gates[:, D:2 * D] * y_rwkv
    o_ref[0] = x + jnp.dot(mix.astype(BF16), wout_ref[...], preferred_element_type=F32)


def _const_spec(shape):
    nd = len(shape)
    return pl.BlockSpec(shape, lambda *_: (0,) * nd, pipeline_mode=pl.Buffered(1))


def _mixer(x, norm_g, w_in, conv_w, shift_mu, w0, w2, a0, a2, g2, k_k, k_a, r_k, ln_g, ln_b, w_out):
    bsz, s, d = x.shape
    n_heads = d // HEAD
    tc = min(MIX_TOKENS, s)
    L = RWKV_CHUNK
    n_sub = tc // L
    lora0 = 6 * d
    w_main = w_in[:, 0:6 * d].astype(BF16)
    pad = lambda t, n: jnp.pad(t, ((0, 0), (0, n - t.shape[1])))
    lora_cols = (DECAY_LORA, AAA_LORA, GATE_LORA)
    lora_pads = (LANES, LANES, 2 * LANES)
    pieces_w, pieces_mu, off = [], [], lora0
    for n, p in zip(lora_cols, lora_pads):
        pieces_w.append(pad(w_in[:, off:off + n], p))
        pieces_mu.append(pad(shift_mu[None, off - 3 * d:off - 3 * d + n], p))
        off += n
    w_lora = jnp.concatenate(pieces_w, axis=1).astype(BF16)
    mu_lora = jnp.concatenate(pieces_mu, axis=1)
    w_gate = w_in[:, off:off + 2 * d].astype(BF16)
    mu_rkv = shift_mu[None, 0:3 * d]
    padr = lambda t, n: jnp.pad(t, ((0, n - t.shape[0]), (0, 0)))
    w2p = padr(w2, LANES).astype(BF16)
    a2p = padr(a2, LANES).astype(BF16)
    g2p = padr(g2, 2 * LANES).astype(BF16)
    row = lambda t: t.reshape(1, -1)
    head_of = jnp.arange(d, dtype=jnp.int32) // HEAD
    head_sum = (head_of[:, None] == head_of[None, :]).astype(BF16)
    consts = [row(norm_g), w_main, w_lora, w_gate, conv_w, mu_rkv, mu_lora, row(w0), w2p, row(a0), a2p,
              g2p, row(k_k), row(k_a), row(r_k), row(ln_g), row(ln_b), w_out.astype(BF16), head_sum]
    kern = functools.partial(_mixer_kernel, n_heads=n_heads, d_model=d)
    return pl.pallas_call(
        kern,
        grid=(bsz, s // tc),
        in_specs=[pl.BlockSpec((1, tc, d), lambda b, c: (b, c, 0))] + [_const_spec(t.shape) for t in consts],
        out_specs=pl.BlockSpec((1, tc, d), lambda b, c: (b, c, 0)),
        out_shape=jax.ShapeDtypeStruct((bsz, s, d), F32),
        scratch_shapes=[
            pltpu.VMEM((SUBLANES, d), F32),
            pltpu.VMEM((SUBLANES, 3 * d), F32),
            pltpu.VMEM((SUBLANES, 4 * LANES), F32),
            pltpu.VMEM((n_sub, 2 * L, d), F32),
            pltpu.VMEM((n_sub, 2 * L, d), F32),
            pltpu.VMEM((tc, d), F32),
            pltpu.VMEM((tc, d), F32),
            pltpu.VMEM((tc, d), F32),
            pltpu.VMEM((n_sub, 1, d), F32),
        ] + [pltpu.VMEM((HEAD, HEAD), F32)] * n_heads,
        compiler_params=pltpu.CompilerParams(
            dimension_semantics=("arbitrary", "arbitrary"), vmem_limit_bytes=VMEM_LIMIT_BYTES),
        name="mixer",
    )(x, *consts)


def _router_kernel(h_ref, g_ref, rw_ref, rb_ref, meta_ref, cnt_ref, run_s, *, n_experts):
    tt = h_ref.shape[0]

    @pl.when(pl.program_id(0) == 0)
    def _():
        run_s[...] = jnp.zeros_like(run_s)

    h = h_ref[...]
    ms = jnp.mean(h * h, axis=-1, keepdims=True)
    hn = h * lax.rsqrt(ms + NORM_EPS) * g_ref[...]
    logits = jnp.dot(hn, rw_ref[...], preferred_element_type=F32,
                     precision=lax.Precision.HIGHEST) + rb_ref[...]
    lane = lax.broadcasted_iota(jnp.int32, (tt, LANES), 1)
    neg = jnp.float32(-jnp.inf)
    work = jnp.where(lane < n_experts, logits, neg)
    vals, idxs = [], []
    for _ in range(TOP_K):
        m = jnp.max(work, axis=-1, keepdims=True)
        i = jnp.min(jnp.where(work == m, lane, LANES), axis=-1, keepdims=True)
        vals.append(m)
        idxs.append(i)
        work = jnp.where(lane == i, neg, work)
    ex = [jnp.exp(vv - vals[0]) for vv in vals]
    den = ex[0] + ex[1] + ex[2] + ex[3]
    gates = [e / den for e in ex]

    onehot = jnp.zeros((tt, LANES), jnp.bool_)
    for kk in range(TOP_K):
        onehot = onehot | (lane == (idxs[kk] + kk * n_experts))
    oh = jnp.where(onehot, 1.0, 0.0)
    ri = lax.broadcasted_iota(jnp.int32, (tt, tt), 0)
    ci = lax.broadcasted_iota(jnp.int32, (tt, tt), 1)
    tri = jnp.where(ri > ci, 1.0, 0.0).astype(BF16)
    cnt = jnp.dot(tri, oh.astype(BF16), preferred_element_type=F32)
    tot = jnp.broadcast_to(jnp.sum(oh, axis=0, keepdims=True), (SUBLANES, LANES))
    lane8 = lax.broadcasted_iota(jnp.int32, (SUBLANES, LANES), 1)
    pk = jnp.zeros_like(tot)
    te = tot
    for j in range(1, TOP_K):
        rolled = pltpu.roll(tot, j * n_experts, axis=1)
        pk = pk + jnp.where(lane8 >= j * n_experts, rolled, 0.0)
        te = te + rolled
    before = cnt + (run_s[...] + pk)[0:1, :]
    ranks = [jnp.sum(jnp.where(onehot & (lane // n_experts == kk), before, 0.0), axis=-1, keepdims=True)
             for kk in range(TOP_K)]
    run_s[...] = run_s[...] + te
    cnt_ref[...] = run_s[...]

    meta = jnp.zeros((tt, LANES), F32)
    for kk in range(TOP_K):
        meta = jnp.where(lane == kk, idxs[kk].astype(F32), meta)
        meta = jnp.where(lane == TOP_K + kk, gates[kk], meta)
        meta = jnp.where(lane == 2 * TOP_K + kk, ranks[kk], meta)
    meta_ref[...] = meta


def _router(h2, norm_g, router_w, router_b):
    t, d = h2.shape
    n_experts = router_w.shape[1]
    tt = min(ROUTE_TOKENS, t)
    rw = jnp.pad(router_w, ((0, 0), (0, LANES - n_experts)))
    rb = jnp.pad(router_b, (0, LANES - n_experts)).reshape(1, LANES)
    kern = functools.partial(_router_kernel, n_experts=n_experts)
    return pl.pallas_call(
        kern,
        grid=(t // tt,),
        in_specs=[pl.BlockSpec((tt, d), lambda i: (i, 0)),
                  pl.BlockSpec((1, d), lambda i: (0, 0)),
                  pl.BlockSpec((d, LANES), lambda i: (0, 0)),
                  pl.BlockSpec((1, LANES), lambda i: (0, 0))],
        out_specs=[pl.BlockSpec((tt, LANES), lambda i: (i, 0)),
                   pl.BlockSpec((SUBLANES, LANES), lambda i: (0, 0))],
        out_shape=[jax.ShapeDtypeStruct((t, LANES), F32),
                   jax.ShapeDtypeStruct((SUBLANES, LANES), F32)],
        scratch_shapes=[pltpu.VMEM((SUBLANES, LANES), F32)],
        compiler_params=pltpu.CompilerParams(
            dimension_semantics=("arbitrary",), vmem_limit_bytes=VMEM_LIMIT_BYTES),
        name="router",
    )(h2, norm_g.reshape(1, d), rw, rb)


def _row_gather(table, idx):
    n_idx = idx.shape[0]
    d = table.shape[1]
    w = GATHER_WINDOW
    sc = plsc.get_sparse_core_info()
    n_workers = sc.num_cores * sc.num_subcores
    per_worker = n_idx // n_workers
    n_win = per_worker // w
    assert per_worker * n_workers == n_idx and n_win * w == per_worker and n_win % 2 == 0
    mesh = plsc.VectorSubcoreMesh(core_axis_name="core", subcore_axis_name="subcore")

    @functools.partial(
        pl.kernel, out_type=jax.ShapeDtypeStruct((n_idx, d), table.dtype), mesh=mesh,
        scratch_types=[pltpu.VMEM((per_worker,), jnp.int32),
                       pltpu.VMEM((w, d), table.dtype), pltpu.VMEM((w, d), table.dtype),
                       pltpu.SemaphoreType.DMA, pltpu.SemaphoreType.DMA,
                       pltpu.SemaphoreType.DMA, pltpu.SemaphoreType.DMA])
    def gather_kernel(table_hbm, idx_hbm, out_hbm, idx_v, buf_a, buf_b, gsem_a, gsem_b, psem_a, psem_b):
        worker = lax.axis_index("subcore") * sc.num_cores + lax.axis_index("core")
        base = worker * per_worker
        pltpu.sync_copy(idx_hbm.at[pl.ds(base, per_worker)], idx_v)

        def gather(j, buf, sem):
            return pltpu.make_async_copy(table_hbm.at[idx_v.at[pl.ds(j * w, w)]], buf, sem)

        def put(j, buf, sem):
            return pltpu.make_async_copy(buf, out_hbm.at[pl.ds(base + j * w, w)], sem)

        gather(0, buf_a, gsem_a).start()

        @pl.loop(0, n_win, step=2)
        def _(j):
            gather(j, buf_a, gsem_a).wait()

            @pl.when(j > 0)
            def _():
                put(j - 1, buf_b, psem_b).wait()

            gather(j + 1, buf_b, gsem_b).start()
            put(j, buf_a, psem_a).start()
            gather(j + 1, buf_b, gsem_b).wait()
            put(j, buf_a, psem_a).wait()

            @pl.when(j + 2 < n_win)
            def _():
                gather(j + 2, buf_a, gsem_a).start()

            put(j + 1, buf_b, psem_b).start()

        put(n_win - 1, buf_b, psem_b).wait()

    return gather_kernel(table, idx)


def _experts_kernel(te_ref, nv_ref, xs_ref, g_ref, w1_ref, b1_ref, w2_ref, b2_ref, ys_ref, w1_s, w2_s, *, d_ff):
    i = pl.program_id(0)
    prev = te_ref[jnp.maximum(i - 1, 0)]
    changed = (i == 0) | (te_ref[i] != prev)

    @pl.when(changed)
    def _():
        w1_s[...] = w1_ref[0].astype(BF16)
        w2_s[...] = w2_ref[0].astype(BF16)

    nvalid = nv_ref[i]

    @pl.when(nvalid > 0)
    def _():
        tm = xs_ref.shape[0]
        row = lax.broadcasted_iota(jnp.int32, (tm, 1), 0)
        h = jnp.where(row < nvalid, xs_ref[...], 0.0)
        ms = jnp.mean(h * h, axis=-1, keepdims=True)
        x = (h * lax.rsqrt(ms + NORM_EPS) * g_ref[...]).astype(BF16)
        u = jnp.dot(x, w1_s[...], preferred_element_type=F32) + b1_ref[0]
        glu = jnp.minimum(u[:, 0:d_ff], SWIGLU_LIMIT)
        lin = jnp.clip(u[:, d_ff:2 * d_ff], -SWIGLU_LIMIT, SWIGLU_LIMIT)
        act = glu * _sigmoid(SWIGLU_ALPHA * glu) * (lin + 1.0)
        ys_ref[...] = jnp.dot(act.astype(BF16), w2_s[...], preferred_element_type=F32) + b2_ref[0]

    @pl.when(nvalid <= 0)
    def _():
        ys_ref[...] = jnp.zeros_like(ys_ref)


def _experts(xs, norm_g, tile_expert, tile_valid, w1, b1, w2, b2):
    n_slots, d = xs.shape
    n_exp, _, two_ff = w1.shape
    d_ff = two_ff // 2
    tm = SLOT_TILE
    n_tiles = n_slots // tm
    kern = functools.partial(_experts_kernel, d_ff=d_ff)
    grid_spec = pltpu.PrefetchScalarGridSpec(
        num_scalar_prefetch=2,
        grid=(n_tiles,),
        in_specs=[pl.BlockSpec((tm, d), lambda i, te, nv: (i, 0)),
                  pl.BlockSpec((1, d), lambda i, te, nv: (0, 0)),
                  pl.BlockSpec((1, d, two_ff), lambda i, te, nv: (te[i], 0, 0)),
                  pl.BlockSpec((1, 1, two_ff), lambda i, te, nv: (te[i], 0, 0)),
                  pl.BlockSpec((1, d_ff, d), lambda i, te, nv: (te[i], 0, 0)),
                  pl.BlockSpec((1, 1, d), lambda i, te, nv: (te[i], 0, 0))],
        out_specs=pl.BlockSpec((tm, d), lambda i, te, nv: (i, 0)),
        scratch_shapes=[pltpu.VMEM((d, two_ff), BF16), pltpu.VMEM((d_ff, d), BF16)],
    )
    return pl.pallas_call(
        kern,
        grid_spec=grid_spec,
        out_shape=jax.ShapeDtypeStruct((n_slots, d), F32),
        compiler_params=pltpu.CompilerParams(
            dimension_semantics=("arbitrary",), vmem_limit_bytes=VMEM_LIMIT_BYTES),
        name="experts",
    )(tile_expert, tile_valid, xs, norm_g.reshape(1, d), w1, b1.reshape(n_exp, 1, two_ff), w2,
      b2.reshape(n_exp, 1, d))


def _combine_kernel(h_ref, y4_ref, meta_ref, g_ref, o_ref):
    d = h_ref.shape[1]
    acc = h_ref[...]
    for kk in range(TOP_K):
        acc = acc + meta_ref[:, TOP_K + kk:TOP_K + kk + 1] * y4_ref[:, kk * d:(kk + 1) * d]
    ms = jnp.mean(acc * acc, axis=-1, keepdims=True)
    o_ref[...] = acc * lax.rsqrt(ms + NORM_EPS) * g_ref[...]


def _combine(h2, y4, meta, norm_g):
    t, d = h2.shape
    tt = min(MOVE_TOKENS, t)
    return pl.pallas_call(
        _combine_kernel,
        grid=(t // tt,),
        in_specs=[pl.BlockSpec((tt, d), lambda i: (i, 0)),
                  pl.BlockSpec((tt, TOP_K * d), lambda i: (i, 0)),
                  pl.BlockSpec((tt, LANES), lambda i: (i, 0)),
                  pl.BlockSpec((1, d), lambda i: (0, 0))],
        out_specs=pl.BlockSpec((tt, d), lambda i: (i, 0)),
        out_shape=jax.ShapeDtypeStruct((t, d), F32),
        compiler_params=pltpu.CompilerParams(
            dimension_semantics=("arbitrary",), vmem_limit_bytes=VMEM_LIMIT_BYTES),
        name="combine",
    )(h2, y4, meta, norm_g.reshape(1, d))


def _moe(h2, norm_ffn_g, router_w, router_b, w1, b1, w2, b2, norm_final_g):
    t, d = h2.shape
    n_exp = router_w.shape[1]
    tm = SLOT_TILE
    meta, counts = _router(h2, norm_ffn_g, router_w, router_b)
    eidx = meta[:, 0:TOP_K].astype(jnp.int32)
    rank = meta[:, 2 * TOP_K:3 * TOP_K].astype(jnp.int32)
    cnt = counts[0, 0:n_exp].astype(jnp.int32)
    padded = (cnt + tm - 1) // tm * tm
    seg_end = jnp.cumsum(padded)
    seg_start = seg_end - padded
    pos = (seg_start[eidx] + rank).reshape(-1)
    n_tiles = -(-(t * TOP_K) // tm) + n_exp
    n_slots = n_tiles * tm
    tile_start = jnp.arange(n_tiles, dtype=jnp.int32) * tm
    tile_expert = jnp.minimum(jnp.searchsorted(seg_end, tile_start, side="right"), n_exp - 1).astype(jnp.int32)
    tile_valid = jnp.clip(seg_start[tile_expert] + cnt[tile_expert] - tile_start, 0, tm).astype(jnp.int32)
    slot_token = jnp.zeros((n_slots,), jnp.int32).at[pos].set(
        jnp.arange(t * TOP_K, dtype=jnp.int32) // TOP_K)
    xs = _row_gather(h2, slot_token)
    ys = _experts(xs, norm_ffn_g, tile_expert, tile_valid, w1, b1, w2, b2)
    y4 = _row_gather(ys, pos).reshape(t, TOP_K * d)
    return _combine(h2, y4, meta, norm_final_g)


def kernel(x, norm_mix_g, w_in, conv_w, shift_mu, decay_w0, decay_w2, iclr_a0, iclr_a2, gate_g2, k_k, k_a,
           r_k, ln_x_g, ln_x_b, w_out, norm_ffn_g, router_w, router_b, exp_w1, exp_b1, exp_w2, exp_b2,
           norm_final_g):
    bsz, s, d = x.shape
    depth = w_in.shape[0]
    assert depth == 1, "final norm is fused into the last layer's combine kernel"
    h = _mixer(x, norm_mix_g[0], w_in[0], conv_w[0], shift_mu[0], decay_w0[0], decay_w2[0], iclr_a0[0],
               iclr_a2[0], gate_g2[0], k_k[0], k_a[0], r_k[0], ln_x_g[0], ln_x_b[0], w_out[0])
    out = _moe(h.reshape(bsz * s, d), norm_ffn_g[0], router_w[0], router_b[0], exp_w1[0], exp_b1[0],
               exp_w2[0], exp_b2[0], norm_final_g)
    return out.reshape(bsz, s, d)
```

```python
import functools

import jax
import jax.numpy as jnp
from jax import lax
from jax.experimental import pallas as pl
from jax.experimental.pallas import tpu as pltpu
from jax.experimental.pallas import tpu_sc as plsc

HEAD = 64
DECAY_LORA = 64
AAA_LORA = 64
GATE_LORA = 160
TOP_K = 4
SWIGLU_ALPHA = 1.702
SWIGLU_LIMIT = 7.0
NORM_EPS = 1e-5
GN_EPS = 64e-5
DECAY_SCALE = 0.6065306597126334

LANES = 128
SUBLANES = 8
VMEM_LIMIT_BYTES = 56 * 1024 * 1024

RWKV_CHUNK = 64
MIX_TOKENS = 256
ROUTE_TOKENS = 512
MOVE_TOKENS = 256
GATHER_WINDOW = 32
SLOT_TILE = 256

BF16 = jnp.bfloat16
F32 = jnp.float32


def _dot(a, b):
    return jnp.dot(a.astype(BF16), b.astype(BF16), preferred_element_type=F32)


def _dot_nt(a, b):
    return lax.dot_general(a.astype(BF16), b.astype(BF16), (((1,), (1,)), ((), ())),
                           preferred_element_type=F32)


def _dot_tn(a, b):
    return lax.dot_general(a.astype(BF16), b.astype(BF16), (((0,), (0,)), ((), ())),
                           preferred_element_type=F32)


def _sigmoid(x):
    return 0.5 * jnp.tanh(0.5 * x) + 0.5


def _shift_rows(x, n, carry):
    rolled = pltpu.roll(x, n, axis=0)
    row = lax.broadcasted_iota(jnp.int32, x.shape, 0)
    out = rolled
    for i in range(n):
        out = jnp.where(row == i, carry[SUBLANES - n + i:SUBLANES - n + i + 1, :], out)
    return out


def _mixer_kernel(x_ref, g_ref, wmain_ref, wlora_ref, wgate_ref, cw_ref, mu_rkv_ref, mu_lora_ref,
                  w0_ref, w2_ref, a0_ref, a2_ref, g2_ref, kk_ref, ka_ref, rk_ref, lng_ref, lnb_ref,
                  wout_ref, hsum_ref, o_ref,
                  cu_s, cp_s, cl_s, ar_s, bk_s, v_s, rkb_s, y_s, gam_s, *state_s,
                  n_heads, d_model):
    tc = x_ref.shape[1]
    L = RWKV_CHUNK
    n_sub = tc // L
    D = d_model

    @pl.when(pl.program_id(1) == 0)
    def _():
        for st_ref in state_s:
            st_ref[...] = jnp.zeros_like(st_ref)
        cu_s[...] = jnp.zeros_like(cu_s)
        cp_s[...] = jnp.zeros_like(cp_s)
        cl_s[...] = jnp.zeros_like(cl_s)

    x = x_ref[0]
    ms = jnp.mean(x * x, axis=-1, keepdims=True)
    xn = (x * lax.rsqrt(ms + NORM_EPS) * g_ref[...]).astype(BF16)

    pc = jnp.dot(xn, wmain_ref[:, 0:3 * D], preferred_element_type=F32)
    u = pc[:, D:2 * D] * pc[:, 2 * D:3 * D]
    cu = cu_s[...]
    conv = (cw_ref[0:1, :] * _shift_rows(u, 2, cu) + cw_ref[1:2, :] * _shift_rows(u, 1, cu)
            + cw_ref[2:3, :] * u)
    y_conv = pc[:, 0:D] * conv
    cu_s[...] = u[tc - SUBLANES:tc, :]

    pr = jnp.dot(xn, wmain_ref[:, 3 * D:6 * D], preferred_element_type=F32)
    cp = cp_s[...]
    cp_s[...] = pr[tc - SUBLANES:tc, :]
    pr = pr + (_shift_rows(pr, 1, cp) - pr) * mu_rkv_ref[...]
    plo = jnp.dot(xn, wlora_ref[...], preferred_element_type=F32)
    cl = cl_s[...]
    cl_s[...] = plo[tc - SUBLANES:tc, :]
    plo = plo + (_shift_rows(plo, 1, cl) - plo) * mu_lora_ref[...]

    r = pr[:, 0:D]
    k = pr[:, D:2 * D]
    v = pr[:, 2 * D:3 * D]
    wd = plo[:, 0:LANES]
    ad = plo[:, LANES:2 * LANES]
    gd = plo[:, 2 * LANES:4 * LANES]

    lw = -DECAY_SCALE * _sigmoid(w0_ref[...] + _dot(jnp.tanh(wd), w2_ref[...]))
    a = _sigmoid(a0_ref[...] + _dot(ad, a2_ref[...]))
    g = _dot(_sigmoid(gd), g2_ref[...])

    row = lax.broadcasted_iota(jnp.int32, (tc, tc), 0)
    col = lax.broadcasted_iota(jnp.int32, (tc, tc), 1)
    tri = jnp.where((row >= col) & ((row // L) == (col // L)), 1.0, 0.0).astype(BF16)
    lw_hi = lw.astype(BF16)
    lw_lo = (lw - lw_hi.astype(F32)).astype(BF16)
    cum = (jnp.dot(tri, lw_hi, preferred_element_type=F32)
           + jnp.dot(tri, lw_lo, preferred_element_type=F32))
    e_inv = jnp.exp(-cum)

    kkraw = k * kk_ref[...]
    ss = _dot(kkraw * kkraw, hsum_ref[...])
    kkn = kkraw * jnp.minimum(lax.rsqrt(ss), 1e12)

    k2 = k * (1.0 + (a - 1.0) * ka_ref[...])
    a_t = kkn * jnp.exp(cum - lw)
    r_t = r * jnp.exp(cum)
    b_t = kkn * a * e_inv
    k_t = k2 * e_inv
    for c in range(n_sub):
        rows = slice(c * L, (c + 1) * L)
        ar_s[c, 0:L, :] = a_t[rows]
        ar_s[c, L:2 * L, :] = r_t[rows]
        bk_s[c, 0:L, :] = b_t[rows]
        bk_s[c, L:2 * L, :] = k_t[rows]
        gam_s[c] = jnp.exp(cum[(c + 1) * L - 1:(c + 1) * L, :])
    v_s[...] = v
    rkb_s[...] = r * k2 * rk_ref[...]

    ri = lax.broadcasted_iota(jnp.int32, (L, L), 0)
    ci = lax.broadcasted_iota(jnp.int32, (L, L), 1)
    strict = ri > ci
    eye = jnp.where(ri == ci, 1.0, 0.0)
    ri2 = lax.broadcasted_iota(jnp.int32, (L, 2 * L), 0)
    ci2 = lax.broadcasted_iota(jnp.int32, (L, 2 * L), 1)
    incl2 = ri2 >= (ci2 % L)
    n_double = L.bit_length() - 2

    def sub_chunk(c, carry):
        r0 = pl.multiple_of(c * L, L)
        heads = range(n_heads)
        hsl = [slice(h * HEAD, (h + 1) * HEAD) for h in heads]
        ar = [ar_s[c, :, hsl[h]].astype(BF16) for h in heads]
        bk = [bk_s[c, :, hsl[h]].astype(BF16) for h in heads]
        vh = [v_s[pl.ds(r0, L), hsl[h]] for h in heads]
        st = [state_s[h][...] for h in heads]
        gm = [_dot_nt(ar[h], bk[h]) for h in heads]
        nm = [jnp.where(strict, gm[h][0:L, 0:L], 0.0) for h in heads]
        mak = [jnp.where(strict, gm[h][0:L, L:2 * L], 0.0) for h in heads]
        q = [jnp.where(incl2, gm[h][L:2 * L, :], 0.0) for h in heads]
        xinv = [eye - nm[h] for h in heads]
        p = nm
        for _ in range(n_double):
            p = [_dot(p[h], p[h]) for h in heads]
            xinv = [xinv[h] + _dot(xinv[h], p[h]) for h in heads]
        ars = [_dot_nt(ar[h], st[h]) for h in heads]
        mv = [_dot(mak[h], vh[h]) for h in heads]
        uu = [_dot(xinv[h], ars[h][0:L] + mv[h]) for h in heads]
        uv = [jnp.concatenate([-uu[h], vh[h]], axis=0).astype(BF16) for h in heads]
        yh = [ars[h][L:2 * L] + _dot(q[h], uv[h]) for h in heads]
        for h in heads:
            state_s[h][...] = (st[h] + _dot_tn(uv[h], bk[h])) * gam_s[c, :, hsl[h]]
        for h in heads:
            mean = jnp.mean(yh[h], axis=-1, keepdims=True)
            yc = yh[h] - mean
            var = jnp.mean(yc * yc, axis=-1, keepdims=True)
            bonus = jnp.sum(rkb_s[pl.ds(r0, L), hsl[h]], axis=-1, keepdims=True) * vh[h]
            y_s[pl.ds(r0, L), hsl[h]] = (yc * lax.rsqrt(var + GN_EPS) * lng_ref[:, hsl[h]]
                                         + lnb_ref[:, hsl[h]] + bonus)
        return carry

    lax.fori_loop(0, n_sub, sub_chunk, 0)

    y_rwkv = y_s[...] * g
    gates = _sigmoid(jnp.dot(xn, wgate_ref[...], preferred_element_type=F32))
    mix = gates[:, 0:D] * y_conv + gates[:, D:2 * D] * y_rwkv
    o_ref[0] = x + jnp.dot(mix.astype(BF16), wout_ref[...], preferred_element_type=F32)


def _const_spec(shape):
    nd = len(shape)
    return pl.BlockSpec(shape, lambda *_: (0,) * nd, pipeline_mode=pl.Buffered(1))


def _mixer(x, norm_g, w_in, conv_w, shift_mu, w0, w2, a0, a2, g2, k_k, k_a, r_k, ln_g, ln_b, w_out):
    bsz, s, d = x.shape
    n_heads = d // HEAD
    tc = min(MIX_TOKENS, s)
    L = RWKV_CHUNK
    n_sub = tc // L
    lora0 = 6 * d
    w_main = w_in[:, 0:6 * d].astype(BF16)
    pad = lambda t, n: jnp.pad(t, ((0, 0), (0, n - t.shape[1])))
    lora_cols = (DECAY_LORA, AAA_LORA, GATE_LORA)
    lora_pads = (LANES, LANES, 2 * LANES)
    pieces_w, pieces_mu, off = [], [], lora0
    for n, p in zip(lora_cols, lora_pads):
        pieces_w.append(pad(w_in[:, off:off + n], p))
        pieces_mu.append(pad(shift_mu[None, off - 3 * d:off - 3 * d + n], p))
        off += n
    w_lora = jnp.concatenate(pieces_w, axis=1).astype(BF16)
    mu_lora = jnp.concatenate(pieces_mu, axis=1)
    w_gate = w_in[:, off:off + 2 * d].astype(BF16)
    mu_rkv = shift_mu[None, 0:3 * d]
    padr = lambda t, n: jnp.pad(t, ((0, n - t.shape[0]), (0, 0)))
    w2p = padr(w2, LANES).astype(BF16)
    a2p = padr(a2, LANES).astype(BF16)
    g2p = padr(g2, 2 * LANES).astype(BF16)
    row = lambda t: t.reshape(1, -1)
    head_of = jnp.arange(d, dtype=jnp.int32) // HEAD
    head_sum = (head_of[:, None] == head_of[None, :]).astype(BF16)
    consts = [row(norm_g), w_main, w_lora, w_gate, conv_w, mu_rkv, mu_lora, row(w0), w2p, row(a0), a2p,
              g2p, row(k_k), row(k_a), row(r_k), row(ln_g), row(ln_b), w_out.astype(BF16), head_sum]
    kern = functools.partial(_mixer_kernel, n_heads=n_heads, d_model=d)
    return pl.pallas_call(
        kern,
        grid=(bsz, s // tc),
        in_specs=[pl.BlockSpec((1, tc, d), lambda b, c: (b, c, 0))] + [_const_spec(t.shape) for t in consts],
        out_specs=pl.BlockSpec((1, tc, d), lambda b, c: (b, c, 0)),
        out_shape=jax.ShapeDtypeStruct((bsz, s, d), F32),
        scratch_shapes=[
            pltpu.VMEM((SUBLANES, d), F32),
            pltpu.VMEM((SUBLANES, 3 * d), F32),
            pltpu.VMEM((SUBLANES, 4 * LANES), F32),
            pltpu.VMEM((n_sub, 2 * L, d), F32),
            pltpu.VMEM((n_sub, 2 * L, d), F32),
            pltpu.VMEM((tc, d), F32),
            pltpu.VMEM((tc, d), F32),
            pltpu.VMEM((tc, d), F32),
            pltpu.VMEM((n_sub, 1, d), F32),
        ] + [pltpu.VMEM((HEAD, HEAD), F32)] * n_heads,
        compiler_params=pltpu.CompilerParams(
            dimension_semantics=("arbitrary", "arbitrary"), vmem_limit_bytes=VMEM_LIMIT_BYTES),
        name="mixer",
    )(x, *consts)


def _router_kernel(h_ref, g_ref, rw_ref, rb_ref, meta_ref, cnt_ref, run_s, *, n_experts):
    tt = h_ref.shape[0]

    @pl.when(pl.program_id(0) == 0)
    def _():
        run_s[...] = jnp.zeros_like(run_s)

    h = h_ref[...]
    ms = jnp.mean(h * h, axis=-1, keepdims=True)
    hn = h * lax.rsqrt(ms + NORM_EPS) * g_ref[...]
    logits = jnp.dot(hn, rw_ref[...], preferred_element_type=F32,
                     precision=lax.Precision.HIGHEST) + rb_ref[...]
    lane = lax.broadcasted_iota(jnp.int32, (tt, LANES), 1)
    neg = jnp.float32(-jnp.inf)
    work = jnp.where(lane < n_experts, logits, neg)
    vals, idxs = [], []
    for _ in range(TOP_K):
        m = jnp.max(work, axis=-1, keepdims=True)
        i = jnp.min(jnp.where(work == m, lane, LANES), axis=-1, keepdims=True)
        vals.append(m)
        idxs.append(i)
        work = jnp.where(lane == i, neg, work)
    ex = [jnp.exp(vv - vals[0]) for vv in vals]
    den = ex[0] + ex[1] + ex[2] + ex[3]
    gates = [e / den for e in ex]

    onehot = jnp.zeros((tt, LANES), jnp.bool_)
    for kk in range(TOP_K):
        onehot = onehot | (lane == (idxs[kk] + kk * n_experts))
    oh = jnp.where(onehot, 1.0, 0.0)
    ri = lax.broadcasted_iota(jnp.int32, (tt, tt), 0)
    ci = lax.broadcasted_iota(jnp.int32, (tt, tt), 1)
    tri = jnp.where(ri > ci, 1.0, 0.0).astype(BF16)
    cnt = jnp.dot(tri, oh.astype(BF16), preferred_element_type=F32)
    tot = jnp.broadcast_to(jnp.sum(oh, axis=0, keepdims=True), (SUBLANES, LANES))
    lane8 = lax.broadcasted_iota(jnp.int32, (SUBLANES, LANES), 1)
    pk = jnp.zeros_like(tot)
    te = tot
    for j in range(1, TOP_K):
        rolled = pltpu.roll(tot, j * n_experts, axis=1)
        pk = pk + jnp.where(lane8 >= j * n_experts, rolled, 0.0)
        te = te + rolled
    before = cnt + (run_s[...] + pk)[0:1, :]
    ranks = [jnp.sum(jnp.where(onehot & (lane // n_experts == kk), before, 0.0), axis=-1, keepdims=True)
             for kk in range(TOP_K)]
    run_s[...] = run_s[...] + te
    cnt_ref[...] = run_s[...]

    meta = jnp.zeros((tt, LANES), F32)
    for kk in range(TOP_K):
        meta = jnp.where(lane == kk, idxs[kk].astype(F32), meta)
        meta = jnp.where(lane == TOP_K + kk, gates[kk], meta)
        meta = jnp.where(lane == 2 * TOP_K + kk, ranks[kk], meta)
    meta_ref[...] = meta


def _router(h2, norm_g, router_w, router_b):
    t, d = h2.shape
    n_experts = router_w.shape[1]
    tt = min(ROUTE_TOKENS, t)
    rw = jnp.pad(router_w, ((0, 0), (0, LANES - n_experts)))
    rb = jnp.pad(router_b, (0, LANES - n_experts)).reshape(1, LANES)
    kern = functools.partial(_router_kernel, n_experts=n_experts)
    return pl.pallas_call(
        kern,
        grid=(t // tt,),
        in_specs=[pl.BlockSpec((tt, d), lambda i: (i, 0)),
                  pl.BlockSpec((1, d), lambda i: (0, 0)),
                  pl.BlockSpec((d, LANES), lambda i: (0, 0)),
                  pl.BlockSpec((1, LANES), lambda i: (0, 0))],
        out_specs=[pl.BlockSpec((tt, LANES), lambda i: (i, 0)),
                   pl.BlockSpec((SUBLANES, LANES), lambda i: (0, 0))],
        out_shape=[jax.ShapeDtypeStruct((t, LANES), F32),
                   jax.ShapeDtypeStruct((SUBLANES, LANES), F32)],
        scratch_shapes=[pltpu.VMEM((SUBLANES, LANES), F32)],
        compiler_params=pltpu.CompilerParams(
            dimension_semantics=("arbitrary",), vmem_limit_bytes=VMEM_LIMIT_BYTES),
        name="router",
    )(h2, norm_g.reshape(1, d), rw, rb)


def _row_gather(table, idx):
    n_idx = idx.shape[0]
    d = table.shape[1]
    w = GATHER_WINDOW
    sc = plsc.get_sparse_core_info()
    n_workers = sc.num_cores * sc.num_subcores
    per_worker = n_idx // n_workers
    n_win = per_worker // w
    assert per_worker * n_workers == n_idx and n_win * w == per_worker and n_win % 2 == 0
    mesh = plsc.VectorSubcoreMesh(core_axis_name="core", subcore_axis_name="subcore")

    @functools.partial(
        pl.kernel, out_type=jax.ShapeDtypeStruct((n_idx, d), table.dtype), mesh=mesh,
        scratch_types=[pltpu.VMEM((per_worker,), jnp.int32),
                       pltpu.VMEM((w, d), table.dtype), pltpu.VMEM((w, d), table.dtype),
                       pltpu.SemaphoreType.DMA, pltpu.SemaphoreType.DMA,
                       pltpu.SemaphoreType.DMA, pltpu.SemaphoreType.DMA])
    def gather_kernel(table_hbm, idx_hbm, out_hbm, idx_v, buf_a, buf_b, gsem_a, gsem_b, psem_a, psem_b):
        worker = lax.axis_index("subcore") * sc.num_cores + lax.axis_index("core")
        base = worker * per_worker
        pltpu.sync_copy(idx_hbm.at[pl.ds(base, per_worker)], idx_v)

        def gather(j, buf, sem):
            return pltpu.make_async_copy(table_hbm.at[idx_v.at[pl.ds(j * w, w)]], buf, sem)

        def put(j, buf, sem):
            return pltpu.make_async_copy(buf, out_hbm.at[pl.ds(base + j * w, w)], sem)

        gather(0, buf_a, gsem_a).start()

        @pl.loop(0, n_win, step=2)
        def _(j):
            gather(j, buf_a, gsem_a).wait()

            @pl.when(j > 0)
            def _():
                put(j - 1, buf_b, psem_b).wait()

            gather(j + 1, buf_b, gsem_b).start()
            put(j, buf_a, psem_a).start()
            gather(j + 1, buf_b, gsem_b).wait()
            put(j, buf_a, psem_a).wait()

            @pl.when(j + 2 < n_win)
            def _():
                gather(j + 2, buf_a, gsem_a).start()

            put(j + 1, buf_b, psem_b).start()

        put(n_win - 1, buf_b, psem_b).wait()

    return gather_kernel(table, idx)


def _slot_tokens(pos, n_slots):
    n = pos.shape[0]
    sc = plsc.get_sparse_core_info()
    n_cores, n_sub = sc.num_cores, sc.num_subcores
    rows = n // LANES
    rows_per_tile = rows // n_sub
    out_per_worker = n_slots // (n_cores * n_sub)
    init_per_tile = n_slots // n_sub
    assert rows_per_tile * n_sub * LANES == n and out_per_worker * n_cores * n_sub == n_slots
    mesh = plsc.VectorSubcoreMesh(core_axis_name="core", subcore_axis_name="subcore")
    tokens = (jnp.arange(n, dtype=jnp.int32) // TOP_K).reshape(rows, LANES)
    zeros = jnp.zeros((n_slots,), jnp.int32)

    @functools.partial(
        pl.kernel, out_type=jax.ShapeDtypeStruct((n_slots,), jnp.int32), mesh=mesh,
        scratch_types=[pltpu.VMEM((rows_per_tile, LANES), jnp.int32),
                       pltpu.VMEM((rows_per_tile, LANES), jnp.int32),
                       pltpu.VMEM_SHARED((n_slots,), jnp.int32)])
    def invert_kernel(pos_hbm, tok_hbm, zero_hbm, out_hbm, idx_v, val_v, shared):
        cid = lax.axis_index("core")
        sid = lax.axis_index("subcore")
        pltpu.sync_copy(zero_hbm.at[pl.ds(sid * init_per_tile, init_per_tile)],
                        shared.at[pl.ds(sid * init_per_tile, init_per_tile)])
        pltpu.sync_copy(pos_hbm.at[pl.ds(sid * rows_per_tile, rows_per_tile)], idx_v)
        pltpu.sync_copy(tok_hbm.at[pl.ds(sid * rows_per_tile, rows_per_tile)], val_v)
        plsc.subcore_barrier()

        @pl.loop(0, rows_per_tile)
        def _(j):
            pltpu.sync_copy(val_v.at[j], shared.at[idx_v.at[j]])

        plsc.subcore_barrier()
        off = (cid * n_sub + sid) * out_per_worker
        pltpu.sync_copy(shared.at[pl.ds(off, out_per_worker)], out_hbm.at[pl.ds(off, out_per_worker)])

    return invert_kernel(pos.reshape(rows, LANES), tokens, zeros)


def _experts_kernel(te_ref, nv_ref, xs_ref, g_ref, w1_ref, b1_ref, w2_ref, b2_ref, ys_ref, w1_s, w2_s, *, d_ff):
    i = pl.program_id(0)
    prev = te_ref[jnp.maximum(i - 1, 0)]
    changed = (i == 0) | (te_ref[i] != prev)

    @pl.when(changed)
    def _():
        w1_s[...] = w1_ref[0].astype(BF16)
        w2_s[...] = w2_ref[0].astype(BF16)

    nvalid = nv_ref[i]

    @pl.when(nvalid > 0)
    def _():
        tm = xs_ref.shape[0]
        row = lax.broadcasted_iota(jnp.int32, (tm, 1), 0)
        h = jnp.where(row < nvalid, xs_ref[...], 0.0)
        ms = jnp.mean(h * h, axis=-1, keepdims=True)
        x = (h * lax.rsqrt(ms + NORM_EPS) * g_ref[...]).astype(BF16)
        u = jnp.dot(x, w1_s[...], preferred_element_type=F32) + b1_ref[0]
        glu = jnp.minimum(u[:, 0:d_ff], SWIGLU_LIMIT)
        lin = jnp.clip(u[:, d_ff:2 * d_ff], -SWIGLU_LIMIT, SWIGLU_LIMIT)
        act = glu * _sigmoid(SWIGLU_ALPHA * glu) * (lin + 1.0)
        ys_ref[...] = jnp.dot(act.astype(BF16), w2_s[...], preferred_element_type=F32) + b2_ref[0]

    @pl.when(nvalid <= 0)
    def _():
        ys_ref[...] = jnp.zeros_like(ys_ref)


def _experts(xs, norm_g, tile_expert, tile_valid, w1, b1, w2, b2):
    n_slots, d = xs.shape
    n_exp, _, two_ff = w1.shape
    d_ff = two_ff // 2
    tm = SLOT_TILE
    n_tiles = n_slots // tm
    kern = functools.partial(_experts_kernel, d_ff=d_ff)
    grid_spec = pltpu.PrefetchScalarGridSpec(
        num_scalar_prefetch=2,
        grid=(n_tiles,),
        in_specs=[pl.BlockSpec((tm, d), lambda i, te, nv: (i, 0)),
                  pl.BlockSpec((1, d), lambda i, te, nv: (0, 0)),
                  pl.BlockSpec((1, d, two_ff), lambda i, te, nv: (te[i], 0, 0)),
                  pl.BlockSpec((1, 1, two_ff), lambda i, te, nv: (te[i], 0, 0)),
                  pl.BlockSpec((1, d_ff, d), lambda i, te, nv: (te[i], 0, 0)),
                  pl.BlockSpec((1, 1, d), lambda i, te, nv: (te[i], 0, 0))],
        out_specs=pl.BlockSpec((tm, d), lambda i, te, nv: (i, 0)),
        scratch_shapes=[pltpu.VMEM((d, two_ff), BF16), pltpu.VMEM((d_ff, d), BF16)],
    )
    return pl.pallas_call(
        kern,
        grid_spec=grid_spec,
        out_shape=jax.ShapeDtypeStruct((n_slots, d), F32),
        compiler_params=pltpu.CompilerParams(
            dimension_semantics=("arbitrary",), vmem_limit_bytes=VMEM_LIMIT_BYTES),
        name="experts",
    )(tile_expert, tile_valid, xs, norm_g.reshape(1, d), w1, b1.reshape(n_exp, 1, two_ff), w2,
      b2.reshape(n_exp, 1, d))


def _combine_kernel(h_ref, meta_ref, g_ref, *rest):
    y_refs, o_ref = rest[:TOP_K], rest[TOP_K]
    acc = h_ref[...]
    for kk in range(TOP_K):
        acc = acc + meta_ref[:, TOP_K + kk:TOP_K + kk + 1] * y_refs[kk][...]
    ms = jnp.mean(acc * acc, axis=-1, keepdims=True)
    o_ref[...] = acc * lax.rsqrt(ms + NORM_EPS) * g_ref[...]


def _combine(h2, y4, meta, norm_g):
    t, d = h2.shape
    tt = min(MOVE_TOKENS, t)
    n_blk = t // tt
    y_specs = [pl.BlockSpec((tt, d), functools.partial(lambda i, kk: (kk * n_blk + i, 0), kk=kk))
               for kk in range(TOP_K)]
    return pl.pallas_call(
        _combine_kernel,
        grid=(n_blk,),
        in_specs=[pl.BlockSpec((tt, d), lambda i: (i, 0)),
                  pl.BlockSpec((tt, LANES), lambda i: (i, 0)),
                  pl.BlockSpec((1, d), lambda i: (0, 0))] + y_specs,
        out_specs=pl.BlockSpec((tt, d), lambda i: (i, 0)),
        out_shape=jax.ShapeDtypeStruct((t, d), F32),
        compiler_params=pltpu.CompilerParams(
            dimension_semantics=("arbitrary",), vmem_limit_bytes=VMEM_LIMIT_BYTES),
        name="combine",
    )(h2, meta, norm_g.reshape(1, d), *([y4] * TOP_K))


def _moe(h2, norm_ffn_g, router_w, router_b, w1, b1, w2, b2, norm_final_g):
    t, d = h2.shape
    n_exp = router_w.shape[1]
    tm = SLOT_TILE
    meta, counts = _router(h2, norm_ffn_g, router_w, router_b)
    eidx = meta[:, 0:TOP_K].astype(jnp.int32)
    rank = meta[:, 2 * TOP_K:3 * TOP_K].astype(jnp.int32)
    cnt = counts[0, 0:n_exp].astype(jnp.int32)
    padded = (cnt + tm - 1) // tm * tm
    seg_end = jnp.cumsum(padded)
    seg_start = seg_end - padded
    pos = (seg_start[eidx] + rank).reshape(-1)
    n_tiles = -(-(t * TOP_K) // tm) + n_exp
    n_slots = n_tiles * tm
    tile_start = jnp.arange(n_tiles, dtype=jnp.int32) * tm
    tile_expert = jnp.minimum(jnp.sum(tile_start[:, None] >= seg_end[None, :], axis=1), n_exp - 1).astype(jnp.int32)
    tile_valid = jnp.clip(seg_start[tile_expert] + cnt[tile_expert] - tile_start, 0, tm).astype(jnp.int32)
    slot_token = _slot_tokens(pos, n_slots)
    xs = _row_gather(h2, slot_token)
    ys = _experts(xs, norm_ffn_g, tile_expert, tile_valid, w1, b1, w2, b2)
    y4 = _row_gather(ys, pos.reshape(t, TOP_K).T.reshape(-1))
    return _combine(h2, y4, meta, norm_final_g)


def kernel(x, norm_mix_g, w_in, conv_w, shift_mu, decay_w0, decay_w2, iclr_a0, iclr_a2, gate_g2, k_k, k_a,
           r_k, ln_x_g, ln_x_b, w_out, norm_ffn_g, router_w, router_b, exp_w1, exp_b1, exp_w2, exp_b2,
           norm_final_g):
    bsz, s, d = x.shape
    depth = w_in.shape[0]
    assert depth == 1, "final norm is fused into the last layer's combine kernel"
    h = _mixer(x, norm_mix_g[0], w_in[0], conv_w[0], shift_mu[0], decay_w0[0], decay_w2[0], iclr_a0[0],
               iclr_a2[0], gate_g2[0], k_k[0], k_a[0], r_k[0], ln_x_g[0], ln_x_b[0], w_out[0])
    out = _moe(h.reshape(bsz * s, d), norm_ffn_g[0], router_w[0], router_b[0], exp_w1[0], exp_b1[0],
               exp_w2[0], exp_b2[0], norm_final_g)
    return out.reshape(bsz, s, d)
```

```python
import functools

import jax
import jax.numpy as jnp
from jax import lax
from jax.experimental import pallas as pl
from jax.experimental.pallas import tpu as pltpu
from jax.experimental.pallas import tpu_sc as plsc

HEAD = 64
DECAY_LORA = 64
AAA_LORA = 64
GATE_LORA = 160
TOP_K = 4
SWIGLU_ALPHA = 1.702
SWIGLU_LIMIT = 7.0
NORM_EPS = 1e-5
GN_EPS = 64e-5
DECAY_SCALE = 0.6065306597126334

LANES = 128
SUBLANES = 8
VMEM_LIMIT_BYTES = 56 * 1024 * 1024

RWKV_CHUNK = 64
MIX_TOKENS = 256
ROUTE_TOKENS = 512
MOVE_TOKENS = 256
GATHER_WINDOW = 32
SLOT_TILE = 512
EXPERT_ROWS = 256

BF16 = jnp.bfloat16
F32 = jnp.float32


def _dot(a, b):
    return jnp.dot(a.astype(BF16), b.astype(BF16), preferred_element_type=F32)


def _dot_nt(a, b):
    return lax.dot_general(a.astype(BF16), b.astype(BF16), (((1,), (1,)), ((), ())),
                           preferred_element_type=F32)


def _dot_tn(a, b):
    return lax.dot_general(a.astype(BF16), b.astype(BF16), (((0,), (0,)), ((), ())),
                           preferred_element_type=F32)


def _sigmoid(x):
    return 0.5 * jnp.tanh(0.5 * x) + 0.5


def _shift_rows(x, n, carry):
    rolled = pltpu.roll(x, n, axis=0)
    row = lax.broadcasted_iota(jnp.int32, x.shape, 0)
    out = rolled
    for i in range(n):
        out = jnp.where(row == i, carry[SUBLANES - n + i:SUBLANES - n + i + 1, :], out)
    return out


def _mixer_kernel(x_ref, g_ref, wmain_ref, wlora_ref, wgate_ref, cw_ref, mu_rkv_ref, mu_lora_ref,
                  w0_ref, w2_ref, a0_ref, a2_ref, g2_ref, kk_ref, ka_ref, rk_ref, lng_ref, lnb_ref,
                  wout_ref, hsum_ref, o_ref,
                  cu_s, cp_s, cl_s, ar_s, bk_s, v_s, rkb_s, y_s, gam_s, *state_s,
                  n_heads, d_model):
    tc = x_ref.shape[1]
    L = RWKV_CHUNK
    n_sub = tc // L
    D = d_model

    @pl.when(pl.program_id(1) == 0)
    def _():
        for st_ref in state_s:
            st_ref[...] = jnp.zeros_like(st_ref)
        cu_s[...] = jnp.zeros_like(cu_s)
        cp_s[...] = jnp.zeros_like(cp_s)
        cl_s[...] = jnp.zeros_like(cl_s)

    x = x_ref[0]
    ms = jnp.mean(x * x, axis=-1, keepdims=True)
    xn = (x * lax.rsqrt(ms + NORM_EPS) * g_ref[...]).astype(BF16)

    pc = jnp.dot(xn, wmain_ref[:, 0:3 * D], preferred_element_type=F32)
    u = pc[:, D:2 * D] * pc[:, 2 * D:3 * D]
    cu = cu_s[...]
    conv = (cw_ref[0:1, :] * _shift_rows(u, 2, cu) + cw_ref[1:2, :] * _shift_rows(u, 1, cu)
            + cw_ref[2:3, :] * u)
    y_conv = pc[:, 0:D] * conv
    cu_s[...] = u[tc - SUBLANES:tc, :]

    pr = jnp.dot(xn, wmain_ref[:, 3 * D:6 * D], preferred_element_type=F32)
    cp = cp_s[...]
    cp_s[...] = pr[tc - SUBLANES:tc, :]
    pr = pr + (_shift_rows(pr, 1, cp) - pr) * mu_rkv_ref[...]
    plo = jnp.dot(xn, wlora_ref[...], preferred_element_type=F32)
    cl = cl_s[...]
    cl_s[...] = plo[tc - SUBLANES:tc, :]
    plo = plo + (_shift_rows(plo, 1, cl) - plo) * mu_lora_ref[...]

    r = pr[:, 0:D]
    k = pr[:, D:2 * D]
    v = pr[:, 2 * D:3 * D]
    wd = plo[:, 0:LANES]
    ad = plo[:, LANES:2 * LANES]
    gd = plo[:, 2 * LANES:4 * LANES]

    lw = -DECAY_SCALE * _sigmoid(w0_ref[...] + _dot(jnp.tanh(wd), w2_ref[...]))
    a = _sigmoid(a0_ref[...] + _dot(ad, a2_ref[...]))
    g = _dot(_sigmoid(gd), g2_ref[...])

    row = lax.broadcasted_iota(jnp.int32, (tc, tc), 0)
    col = lax.broadcasted_iota(jnp.int32, (tc, tc), 1)
    tri = jnp.where((row >= col) & ((row // L) == (col // L)), 1.0, 0.0).astype(BF16)
    lw_hi = lw.astype(BF16)
    lw_lo = (lw - lw_hi.astype(F32)).astype(BF16)
    cum = (jnp.dot(tri, lw_hi, preferred_element_type=F32)
           + jnp.dot(tri, lw_lo, preferred_element_type=F32))
    e_inv = jnp.exp(-cum)

    kkraw = k * kk_ref[...]
    ss = _dot(kkraw * kkraw, hsum_ref[...])
    kkn = kkraw * jnp.minimum(lax.rsqrt(ss), 1e12)

    k2 = k * (1.0 + (a - 1.0) * ka_ref[...])
    a_t = kkn * jnp.exp(cum - lw)
    r_t = r * jnp.exp(cum)
    b_t = kkn * a * e_inv
    k_t = k2 * e_inv
    for c in range(n_sub):
        rows = slice(c * L, (c + 1) * L)
        ar_s[c, 0:L, :] = a_t[rows]
        ar_s[c, L:2 * L, :] = r_t[rows]
        bk_s[c, 0:L, :] = b_t[rows]
        bk_s[c, L:2 * L, :] = k_t[rows]
        gam_s[c] = jnp.exp(cum[(c + 1) * L - 1:(c + 1) * L, :])
    v_s[...] = v
    rkb_s[...] = r * k2 * rk_ref[...]

    ri = lax.broadcasted_iota(jnp.int32, (L, L), 0)
    ci = lax.broadcasted_iota(jnp.int32, (L, L), 1)
    strict = ri > ci
    eye = jnp.where(ri == ci, 1.0, 0.0)
    ri2 = lax.broadcasted_iota(jnp.int32, (L, 2 * L), 0)
    ci2 = lax.broadcasted_iota(jnp.int32, (L, 2 * L), 1)
    incl2 = ri2 >= (ci2 % L)
    n_double = L.bit_length() - 2

    def sub_chunk(c, carry):
        r0 = pl.multiple_of(c * L, L)
        heads = range(n_heads)
        hsl = [slice(h * HEAD, (h + 1) * HEAD) for h in heads]
        ar = [ar_s[c, :, hsl[h]].astype(BF16) for h in heads]
        bk = [bk_s[c, :, hsl[h]].astype(BF16) for h in heads]
        vh = [v_s[pl.ds(r0, L), hsl[h]] for h in heads]
        st = [state_s[h][...] for h in heads]
        gm = [_dot_nt(ar[h], bk[h]) for h in heads]
        nm = [jnp.where(strict, gm[h][0:L, 0:L], 0.0) for h in heads]
        mak = [jnp.where(strict, gm[h][0:L, L:2 * L], 0.0) for h in heads]
        q = [jnp.where(incl2, gm[h][L:2 * L, :], 0.0) for h in heads]
        xinv = [eye - nm[h] for h in heads]
        p = nm
        for _ in range(n_double):
            p = [_dot(p[h], p[h]) for h in heads]
            xinv = [xinv[h] + _dot(xinv[h], p[h]) for h in heads]
        ars = [_dot_nt(ar[h], st[h]) for h in heads]
        mv = [_dot(mak[h], vh[h]) for h in heads]
        uu = [_dot(xinv[h], ars[h][0:L] + mv[h]) for h in heads]
        uv = [jnp.concatenate([-uu[h], vh[h]], axis=0).astype(BF16) for h in heads]
        yh = [ars[h][L:2 * L] + _dot(q[h], uv[h]) for h in heads]
        for h in heads:
            state_s[h][...] = (st[h] + _dot_tn(uv[h], bk[h])) * gam_s[c, :, hsl[h]]
        for h in heads:
            mean = jnp.mean(yh[h], axis=-1, keepdims=True)
            yc = yh[h] - mean
            var = jnp.mean(yc * yc, axis=-1, keepdims=True)
            bonus = jnp.sum(rkb_s[pl.ds(r0, L), hsl[h]], axis=-1, keepdims=True) * vh[h]
            y_s[pl.ds(r0, L), hsl[h]] = (yc * lax.rsqrt(var + GN_EPS) * lng_ref[:, hsl[h]]
                                         + lnb_ref[:, hsl[h]] + bonus)
        return carry

    lax.fori_loop(0, n_sub, sub_chunk, 0)

    y_rwkv = y_s[...] * g
    gates = _sigmoid(jnp.dot(xn, wgate_ref[...], preferred_element_type=F32))
    mix = gates[:, 0:D] * y_conv + gates[:, D:2 * D] * y_rwkv
    o_ref[0] = x + jnp.dot(mix.astype(BF16), wout_ref[...], preferred_element_type=F32)


def _const_spec(shape):
    nd = len(shape)
    return pl.BlockSpec(shape, lambda *_: (0,) * nd, pipeline_mode=pl.Buffered(1))


def _mixer(x, norm_g, w_in, conv_w, shift_mu, w0, w2, a0, a2, g2, k_k, k_a, r_k, ln_g, ln_b, w_out):
    bsz, s, d = x.shape
    n_heads = d // HEAD
    tc = min(MIX_TOKENS, s)
    L = RWKV_CHUNK
    n_sub = tc // L
    lora0 = 6 * d
    w_main = w_in[:, 0:6 * d].astype(BF16)
    pad = lambda t, n: jnp.pad(t, ((0, 0), (0, n - t.shape[1])))
    lora_cols = (DECAY_LORA, AAA_LORA, GATE_LORA)
    lora_pads = (LANES, LANES, 2 * LANES)
    pieces_w, pieces_mu, off = [], [], lora0
    for n, p in zip(lora_cols, lora_pads):
        pieces_w.append(pad(w_in[:, off:off + n], p))
        pieces_mu.append(pad(shift_mu[None, off - 3 * d:off - 3 * d + n], p))
        off += n
    w_lora = jnp.concatenate(pieces_w, axis=1).astype(BF16)
    mu_lora = jnp.concatenate(pieces_mu, axis=1)
    w_gate = w_in[:, off:off + 2 * d].astype(BF16)
    mu_rkv = shift_mu[None, 0:3 * d]
    padr = lambda t, n: jnp.pad(t, ((0, n - t.shape[0]), (0, 0)))
    w2p = padr(w2, LANES).astype(BF16)
    a2p = padr(a2, LANES).astype(BF16)
    g2p = padr(g2, 2 * LANES).astype(BF16)
    row = lambda t: t.reshape(1, -1)
    head_of = jnp.arange(d, dtype=jnp.int32) // HEAD
    head_sum = (head_of[:, None] == head_of[None, :]).astype(BF16)
    consts = [row(norm_g), w_main, w_lora, w_gate, conv_w, mu_rkv, mu_lora, row(w0), w2p, row(a0), a2p,
              g2p, row(k_k), row(k_a), row(r_k), row(ln_g), row(ln_b), w_out.astype(BF16), head_sum]
    kern = functools.partial(_mixer_kernel, n_heads=n_heads, d_model=d)
    return pl.pallas_call(
        kern,
        grid=(bsz, s // tc),
        in_specs=[pl.BlockSpec((1, tc, d), lambda b, c: (b, c, 0))] + [_const_spec(t.shape) for t in consts],
        out_specs=pl.BlockSpec((1, tc, d), lambda b, c: (b, c, 0)),
        out_shape=jax.ShapeDtypeStruct((bsz, s, d), F32),
        scratch_shapes=[
            pltpu.VMEM((SUBLANES, d), F32),
            pltpu.VMEM((SUBLANES, 3 * d), F32),
            pltpu.VMEM((SUBLANES, 4 * LANES), F32),
            pltpu.VMEM((n_sub, 2 * L, d), F32),
            pltpu.VMEM((n_sub, 2 * L, d), F32),
            pltpu.VMEM((tc, d), F32),
            pltpu.VMEM((tc, d), F32),
            pltpu.VMEM((tc, d), F32),
            pltpu.VMEM((n_sub, 1, d), F32),
        ] + [pltpu.VMEM((HEAD, HEAD), F32)] * n_heads,
        compiler_params=pltpu.CompilerParams(
            dimension_semantics=("arbitrary", "arbitrary"), vmem_limit_bytes=VMEM_LIMIT_BYTES),
        name="mixer",
    )(x, *consts)


def _router_kernel(h_ref, g_ref, rw_ref, rb_ref, meta_ref, cnt_ref, run_s, *, n_experts):
    tt = h_ref.shape[0]

    @pl.when(pl.program_id(0) == 0)
    def _():
        run_s[...] = jnp.zeros_like(run_s)

    h = h_ref[...]
    ms = jnp.mean(h * h, axis=-1, keepdims=True)
    hn = h * lax.rsqrt(ms + NORM_EPS) * g_ref[...]
    logits = jnp.dot(hn, rw_ref[...], preferred_element_type=F32,
                     precision=lax.Precision.HIGHEST) + rb_ref[...]
    lane = lax.broadcasted_iota(jnp.int32, (tt, LANES), 1)
    neg = jnp.float32(-jnp.inf)
    work = jnp.where(lane < n_experts, logits, neg)
    vals, idxs = [], []
    for _ in range(TOP_K):
        m = jnp.max(work, axis=-1, keepdims=True)
        i = jnp.min(jnp.where(work == m, lane, LANES), axis=-1, keepdims=True)
        vals.append(m)
        idxs.append(i)
        work = jnp.where(lane == i, neg, work)
    ex = [jnp.exp(vv - vals[0]) for vv in vals]
    den = ex[0] + ex[1] + ex[2] + ex[3]
    gates = [e / den for e in ex]

    onehot = jnp.zeros((tt, LANES), jnp.bool_)
    for kk in range(TOP_K):
        onehot = onehot | (lane == (idxs[kk] + kk * n_experts))
    oh = jnp.where(onehot, 1.0, 0.0)
    ri = lax.broadcasted_iota(jnp.int32, (tt, tt), 0)
    ci = lax.broadcasted_iota(jnp.int32, (tt, tt), 1)
    tri = jnp.where(ri > ci, 1.0, 0.0).astype(BF16)
    cnt = jnp.dot(tri, oh.astype(BF16), preferred_element_type=F32)
    tot = jnp.broadcast_to(jnp.sum(oh, axis=0, keepdims=True), (SUBLANES, LANES))
    lane8 = lax.broadcasted_iota(jnp.int32, (SUBLANES, LANES), 1)
    pk = jnp.zeros_like(tot)
    te = tot
    for j in range(1, TOP_K):
        rolled = pltpu.roll(tot, j * n_experts, axis=1)
        pk = pk + jnp.where(lane8 >= j * n_experts, rolled, 0.0)
        te = te + rolled
    before = cnt + (run_s[...] + pk)[0:1, :]
    ranks = [jnp.sum(jnp.where(onehot & (lane // n_experts == kk), before, 0.0), axis=-1, keepdims=True)
             for kk in range(TOP_K)]
    run_s[...] = run_s[...] + te
    cnt_ref[...] = run_s[...]

    meta = jnp.zeros((tt, LANES), F32)
    for kk in range(TOP_K):
        meta = jnp.where(lane == kk, idxs[kk].astype(F32), meta)
        meta = jnp.where(lane == TOP_K + kk, gates[kk], meta)
        meta = jnp.where(lane == 2 * TOP_K + kk, ranks[kk], meta)
    meta_ref[...] = meta


def _router(h2, norm_g, router_w, router_b):
    t, d = h2.shape
    n_experts = router_w.shape[1]
    tt = min(ROUTE_TOKENS, t)
    rw = jnp.pad(router_w, ((0, 0), (0, LANES - n_experts)))
    rb = jnp.pad(router_b, (0, LANES - n_experts)).reshape(1, LANES)
    kern = functools.partial(_router_kernel, n_experts=n_experts)
    return pl.pallas_call(
        kern,
        grid=(t // tt,),
        in_specs=[pl.BlockSpec((tt, d), lambda i: (i, 0)),
                  pl.BlockSpec((1, d), lambda i: (0, 0)),
                  pl.BlockSpec((d, LANES), lambda i: (0, 0)),
                  pl.BlockSpec((1, LANES), lambda i: (0, 0))],
        out_specs=[pl.BlockSpec((tt, LANES), lambda i: (i, 0)),
                   pl.BlockSpec((SUBLANES, LANES), lambda i: (0, 0))],
        out_shape=[jax.ShapeDtypeStruct((t, LANES), F32),
                   jax.ShapeDtypeStruct((SUBLANES, LANES), F32)],
        scratch_shapes=[pltpu.VMEM((SUBLANES, LANES), F32)],
        compiler_params=pltpu.CompilerParams(
            dimension_semantics=("arbitrary",), vmem_limit_bytes=VMEM_LIMIT_BYTES),
        name="router",
    )(h2, norm_g.reshape(1, d), rw, rb)


def _row_gather(table, idx):
    n_idx = idx.shape[0]
    d = table.shape[1]
    w = GATHER_WINDOW
    sc = plsc.get_sparse_core_info()
    n_workers = sc.num_cores * sc.num_subcores
    per_worker = n_idx // n_workers
    n_win = per_worker // w
    assert per_worker * n_workers == n_idx and n_win * w == per_worker and n_win % 2 == 0
    mesh = plsc.VectorSubcoreMesh(core_axis_name="core", subcore_axis_name="subcore")

    @functools.partial(
        pl.kernel, out_type=jax.ShapeDtypeStruct((n_idx, d), table.dtype), mesh=mesh,
        scratch_types=[pltpu.VMEM((per_worker,), jnp.int32),
                       pltpu.VMEM((w, d), table.dtype), pltpu.VMEM((w, d), table.dtype),
                       pltpu.SemaphoreType.DMA, pltpu.SemaphoreType.DMA,
                       pltpu.SemaphoreType.DMA, pltpu.SemaphoreType.DMA])
    def gather_kernel(table_hbm, idx_hbm, out_hbm, idx_v, buf_a, buf_b, gsem_a, gsem_b, psem_a, psem_b):
        worker = lax.axis_index("subcore") * sc.num_cores + lax.axis_index("core")
        base = worker * per_worker
        pltpu.sync_copy(idx_hbm.at[pl.ds(base, per_worker)], idx_v)

        def gather(j, buf, sem):
            return pltpu.make_async_copy(table_hbm.at[idx_v.at[pl.ds(j * w, w)]], buf, sem)

        def put(j, buf, sem):
            return pltpu.make_async_copy(buf, out_hbm.at[pl.ds(base + j * w, w)], sem)

        gather(0, buf_a, gsem_a).start()

        @pl.loop(0, n_win, step=2)
        def _(j):
            gather(j, buf_a, gsem_a).wait()

            @pl.when(j > 0)
            def _():
                put(j - 1, buf_b, psem_b).wait()

            gather(j + 1, buf_b, gsem_b).start()
            put(j, buf_a, psem_a).start()
            gather(j + 1, buf_b, gsem_b).wait()
            put(j, buf_a, psem_a).wait()

            @pl.when(j + 2 < n_win)
            def _():
                gather(j + 2, buf_a, gsem_a).start()

            put(j + 1, buf_b, psem_b).start()

        put(n_win - 1, buf_b, psem_b).wait()

    return gather_kernel(table, idx)


def _slot_tokens(pos, n_slots):
    n = pos.shape[0]
    sc = plsc.get_sparse_core_info()
    n_cores, n_sub = sc.num_cores, sc.num_subcores
    rows = n // LANES
    rows_per_tile = rows // n_sub
    out_per_worker = n_slots // (n_cores * n_sub)
    init_per_tile = n_slots // n_sub
    assert rows_per_tile * n_sub * LANES == n and out_per_worker * n_cores * n_sub == n_slots
    mesh = plsc.VectorSubcoreMesh(core_axis_name="core", subcore_axis_name="subcore")
    tokens = (jnp.arange(n, dtype=jnp.int32) // TOP_K).reshape(rows, LANES)
    zeros = jnp.zeros((n_slots,), jnp.int32)

    @functools.partial(
        pl.kernel, out_type=jax.ShapeDtypeStruct((n_slots,), jnp.int32), mesh=mesh,
        scratch_types=[pltpu.VMEM((rows_per_tile, LANES), jnp.int32),
                       pltpu.VMEM((rows_per_tile, LANES), jnp.int32),
                       pltpu.VMEM_SHARED((n_slots,), jnp.int32)])
    def invert_kernel(pos_hbm, tok_hbm, zero_hbm, out_hbm, idx_v, val_v, shared):
        cid = lax.axis_index("core")
        sid = lax.axis_index("subcore")
        pltpu.sync_copy(zero_hbm.at[pl.ds(sid * init_per_tile, init_per_tile)],
                        shared.at[pl.ds(sid * init_per_tile, init_per_tile)])
        pltpu.sync_copy(pos_hbm.at[pl.ds(sid * rows_per_tile, rows_per_tile)], idx_v)
        pltpu.sync_copy(tok_hbm.at[pl.ds(sid * rows_per_tile, rows_per_tile)], val_v)
        plsc.subcore_barrier()

        @pl.loop(0, rows_per_tile)
        def _(j):
            pltpu.sync_copy(val_v.at[j], shared.at[idx_v.at[j]])

        plsc.subcore_barrier()
        off = (cid * n_sub + sid) * out_per_worker
        pltpu.sync_copy(shared.at[pl.ds(off, out_per_worker)], out_hbm.at[pl.ds(off, out_per_worker)])

    return invert_kernel(pos.reshape(rows, LANES), tokens, zeros)


def _experts_kernel(te_ref, nv_ref, xs_ref, g_ref, w1_ref, b1_ref, w2_ref, b2_ref, ys_ref, w1_s, w2_s, *, d_ff):
    i = pl.program_id(0)
    prev = te_ref[jnp.maximum(i - 1, 0)]
    changed = (i == 0) | (te_ref[i] != prev)

    @pl.when(changed)
    def _():
        w1_s[...] = w1_ref[0].astype(BF16)
        w2_s[...] = w2_ref[0].astype(BF16)

    nvalid = nv_ref[i]

    @pl.when(nvalid > 0)
    def _():
        tm = xs_ref.shape[0]
        groups = [slice(r, r + EXPERT_ROWS) for r in range(0, tm, EXPERT_ROWS)]
        xg = []
        for rows in groups:
            row = rows.start + lax.broadcasted_iota(jnp.int32, (EXPERT_ROWS, 1), 0)
            h = jnp.where(row < nvalid, xs_ref[rows, :], 0.0)
            ms = jnp.mean(h * h, axis=-1, keepdims=True)
            xg.append((h * lax.rsqrt(ms + NORM_EPS) * g_ref[...]).astype(BF16))
        ug = [jnp.dot(x, w1_s[...], preferred_element_type=F32) + b1_ref[0] for x in xg]
        ag = []
        for u in ug:
            glu = jnp.minimum(u[:, 0:d_ff], SWIGLU_LIMIT)
            lin = jnp.clip(u[:, d_ff:2 * d_ff], -SWIGLU_LIMIT, SWIGLU_LIMIT)
            ag.append((glu * _sigmoid(SWIGLU_ALPHA * glu) * (lin + 1.0)).astype(BF16))
        for rows, act in zip(groups, ag):
            ys_ref[rows, :] = jnp.dot(act, w2_s[...], preferred_element_type=F32) + b2_ref[0]

    @pl.when(nvalid <= 0)
    def _():
        ys_ref[...] = jnp.zeros_like(ys_ref)


def _experts(xs, norm_g, tile_expert, tile_valid, w1, b1, w2, b2):
    n_slots, d = xs.shape
    n_exp, _, two_ff = w1.shape
    d_ff = two_ff // 2
    tm = SLOT_TILE
    n_tiles = n_slots // tm
    kern = functools.partial(_experts_kernel, d_ff=d_ff)
    grid_spec = pltpu.PrefetchScalarGridSpec(
        num_scalar_prefetch=2,
        grid=(n_tiles,),
        in_specs=[pl.BlockSpec((tm, d), lambda i, te, nv: (i, 0)),
                  pl.BlockSpec((1, d), lambda i, te, nv: (0, 0)),
                  pl.BlockSpec((1, d, two_ff), lambda i, te, nv: (te[i], 0, 0)),
                  pl.BlockSpec((1, 1, two_ff), lambda i, te, nv: (te[i], 0, 0)),
                  pl.BlockSpec((1, d_ff, d), lambda i, te, nv: (te[i], 0, 0)),
                  pl.BlockSpec((1, 1, d), lambda i, te, nv: (te[i], 0, 0))],
        out_specs=pl.BlockSpec((tm, d), lambda i, te, nv: (i, 0)),
        scratch_shapes=[pltpu.VMEM((d, two_ff), BF16), pltpu.VMEM((d_ff, d), BF16)],
    )
    return pl.pallas_call(
        kern,
        grid_spec=grid_spec,
        out_shape=jax.ShapeDtypeStruct((n_slots, d), F32),
        compiler_params=pltpu.CompilerParams(
            dimension_semantics=("arbitrary",), vmem_limit_bytes=VMEM_LIMIT_BYTES),
        name="experts",
    )(tile_expert, tile_valid, xs, norm_g.reshape(1, d), w1, b1.reshape(n_exp, 1, two_ff), w2,
      b2.reshape(n_exp, 1, d))


def _combine_kernel(h_ref, meta_ref, g_ref, *rest):
    y_refs, o_ref = rest[:TOP_K], rest[TOP_K]
    acc = h_ref[...]
    for kk in range(TOP_K):
        acc = acc + meta_ref[:, TOP_K + kk:TOP_K + kk + 1] * y_refs[kk][...]
    ms = jnp.mean(acc * acc, axis=-1, keepdims=True)
    o_ref[...] = acc * lax.rsqrt(ms + NORM_EPS) * g_ref[...]


def _combine(h2, y4, meta, norm_g):
    t, d = h2.shape
    tt = min(MOVE_TOKENS, t)
    n_blk = t // tt
    y_specs = [pl.BlockSpec((tt, d), functools.partial(lambda i, kk: (kk * n_blk + i, 0), kk=kk))
               for kk in range(TOP_K)]
    return pl.pallas_call(
        _combine_kernel,
        grid=(n_blk,),
        in_specs=[pl.BlockSpec((tt, d), lambda i: (i, 0)),
                  pl.BlockSpec((tt, LANES), lambda i: (i, 0)),
                  pl.BlockSpec((1, d), lambda i: (0, 0))] + y_specs,
        out_specs=pl.BlockSpec((tt, d), lambda i: (i, 0)),
        out_shape=jax.ShapeDtypeStruct((t, d), F32),
        compiler_params=pltpu.CompilerParams(
            dimension_semantics=("arbitrary",), vmem_limit_bytes=VMEM_LIMIT_BYTES),
        name="combine",
    )(h2, meta, norm_g.reshape(1, d), *([y4] * TOP_K))


def _moe(h2, norm_ffn_g, router_w, router_b, w1, b1, w2, b2, norm_final_g):
    t, d = h2.shape
    n_exp = router_w.shape[1]
    tm = SLOT_TILE
    meta, counts = _router(h2, norm_ffn_g, router_w, router_b)
    eidx = meta[:, 0:TOP_K].astype(jnp.int32)
    rank = meta[:, 2 * TOP_K:3 * TOP_K].astype(jnp.int32)
    cnt = counts[0, 0:n_exp].astype(jnp.int32)
    padded = (cnt + tm - 1) // tm * tm
    seg_end = jnp.cumsum(padded)
    seg_start = seg_end - padded
    pos = (seg_start[eidx] + rank).reshape(-1)
    n_tiles = -(-(t * TOP_K) // tm) + n_exp
    n_slots = n_tiles * tm
    tile_start = jnp.arange(n_tiles, dtype=jnp.int32) * tm
    tile_expert = jnp.minimum(jnp.sum(tile_start[:, None] >= seg_end[None, :], axis=1), n_exp - 1).astype(jnp.int32)
    tile_valid = jnp.clip(seg_start[tile_expert] + cnt[tile_expert] - tile_start, 0, tm).astype(jnp.int32)
    slot_token = _slot_tokens(pos, n_slots)
    xs = _row_gather(h2, slot_token)
    ys = _experts(xs, norm_ffn_g, tile_expert, tile_valid, w1, b1, w2, b2)
    y4 = _row_gather(ys, pos.reshape(t, TOP_K).T.reshape(-1))
    return _combine(h2, y4, meta, norm_final_g)


def kernel(x, norm_mix_g, w_in, conv_w, shift_mu, decay_w0, decay_w2, iclr_a0, iclr_a2, gate_g2, k_k, k_a,
           r_k, ln_x_g, ln_x_b, w_out, norm_ffn_g, router_w, router_b, exp_w1, exp_b1, exp_w2, exp_b2,
           norm_final_g):
    bsz, s, d = x.shape
    depth = w_in.shape[0]
    assert depth == 1, "final norm is fused into the last layer's combine kernel"
    h = _mixer(x, norm_mix_g[0], w_in[0], conv_w[0], shift_mu[0], decay_w0[0], decay_w2[0], iclr_a0[0],
               iclr_a2[0], gate_g2[0], k_k[0], k_a[0], r_k[0], ln_x_g[0], ln_x_b[0], w_out[0])
    out = _moe(h.reshape(bsz * s, d), norm_ffn_g[0], router_w[0], router_b[0], exp_w1[0], exp_b1[0],
               exp_w2[0], exp_b2[0], norm_final_g)
    return out.reshape(bsz, s, d)
```

```python
import functools

import jax
import jax.numpy as jnp
from jax import lax
from jax.experimental import pallas as pl
from jax.experimental.pallas import tpu as pltpu
from jax.experimental.pallas import tpu_sc as plsc

HEAD = 64
DECAY_LORA = 64
AAA_LORA = 64
GATE_LORA = 160
TOP_K = 4
SWIGLU_ALPHA = 1.702
SWIGLU_LIMIT = 7.0
NORM_EPS = 1e-5
GN_EPS = 64e-5
DECAY_SCALE = 0.6065306597126334

LANES = 128
SUBLANES = 8
VMEM_LIMIT_BYTES = 56 * 1024 * 1024

RWKV_CHUNK = 64
MIX_TOKENS = 256
ROUTE_TOKENS = 512
MOVE_TOKENS = 256
GATHER_WINDOW = 32
SLOT_TILE = 512
EXPERT_ROWS = 256

BF16 = jnp.bfloat16
F32 = jnp.float32


def _dot(a, b):
    return jnp.dot(a.astype(BF16), b.astype(BF16), preferred_element_type=F32)


def _dot_nt(a, b):
    return lax.dot_general(a.astype(BF16), b.astype(BF16), (((1,), (1,)), ((), ())),
                           preferred_element_type=F32)


def _dot_tn(a, b):
    return lax.dot_general(a.astype(BF16), b.astype(BF16), (((0,), (0,)), ((), ())),
                           preferred_element_type=F32)


def _sigmoid(x):
    return 0.5 * jnp.tanh(0.5 * x) + 0.5


def _shift_rows(x, n, carry):
    rolled = pltpu.roll(x, n, axis=0)
    row = lax.broadcasted_iota(jnp.int32, x.shape, 0)
    out = rolled
    for i in range(n):
        out = jnp.where(row == i, carry[SUBLANES - n + i:SUBLANES - n + i + 1, :], out)
    return out


def _mixer_kernel(x_ref, g_ref, wmain_ref, wlora_ref, wgate_ref, cw_ref, mu_rkv_ref, mu_lora_ref,
                  w0_ref, w2_ref, a0_ref, a2_ref, g2_ref, kk_ref, ka_ref, rk_ref, lng_ref, lnb_ref,
                  wout_ref, hsum_ref, o_ref,
                  cu_s, cp_s, cl_s, ar_s, bk_s, v_s, rkb_s, y_s, gam_s, *state_s,
                  n_heads, d_model):
    tc = x_ref.shape[1]
    L = RWKV_CHUNK
    n_sub = tc // L
    D = d_model

    @pl.when(pl.program_id(1) == 0)
    def _():
        for st_ref in state_s:
            st_ref[...] = jnp.zeros_like(st_ref)
        cu_s[...] = jnp.zeros_like(cu_s)
        cp_s[...] = jnp.zeros_like(cp_s)
        cl_s[...] = jnp.zeros_like(cl_s)

    x = x_ref[0]
    ms = jnp.mean(x * x, axis=-1, keepdims=True)
    xn = (x * lax.rsqrt(ms + NORM_EPS) * g_ref[...]).astype(BF16)

    pc = jnp.dot(xn, wmain_ref[:, 0:3 * D], preferred_element_type=F32)
    u = pc[:, D:2 * D] * pc[:, 2 * D:3 * D]
    cu = cu_s[...]
    conv = (cw_ref[0:1, :] * _shift_rows(u, 2, cu) + cw_ref[1:2, :] * _shift_rows(u, 1, cu)
            + cw_ref[2:3, :] * u)
    y_conv = pc[:, 0:D] * conv
    cu_s[...] = u[tc - SUBLANES:tc, :]

    pr = jnp.dot(xn, wmain_ref[:, 3 * D:6 * D], preferred_element_type=F32)
    cp = cp_s[...]
    cp_s[...] = pr[tc - SUBLANES:tc, :]
    pr = pr + (_shift_rows(pr, 1, cp) - pr) * mu_rkv_ref[...]
    plo = jnp.dot(xn, wlora_ref[...], preferred_element_type=F32)
    cl = cl_s[...]
    cl_s[...] = plo[tc - SUBLANES:tc, :]
    plo = plo + (_shift_rows(plo, 1, cl) - plo) * mu_lora_ref[...]

    r = pr[:, 0:D]
    k = pr[:, D:2 * D]
    v = pr[:, 2 * D:3 * D]
    wd = plo[:, 0:LANES]
    ad = plo[:, LANES:2 * LANES]
    gd = plo[:, 2 * LANES:4 * LANES]

    lw = -DECAY_SCALE * _sigmoid(w0_ref[...] + _dot(jnp.tanh(wd), w2_ref[...]))
    a = _sigmoid(a0_ref[...] + _dot(ad, a2_ref[...]))
    g = _dot(_sigmoid(gd), g2_ref[...])

    row = lax.broadcasted_iota(jnp.int32, (tc, tc), 0)
    col = lax.broadcasted_iota(jnp.int32, (tc, tc), 1)
    tri = jnp.where((row >= col) & ((row // L) == (col // L)), 1.0, 0.0).astype(BF16)
    lw_hi = lw.astype(BF16)
    lw_lo = (lw - lw_hi.astype(F32)).astype(BF16)
    cum = (jnp.dot(tri, lw_hi, preferred_element_type=F32)
           + jnp.dot(tri, lw_lo, preferred_element_type=F32))
    e_inv = jnp.exp(-cum)

    kkraw = k * kk_ref[...]
    ss = _dot(kkraw * kkraw, hsum_ref[...])
    kkn = kkraw * jnp.minimum(lax.rsqrt(ss), 1e12)

    k2 = k * (1.0 + (a - 1.0) * ka_ref[...])
    a_t = kkn * jnp.exp(cum - lw)
    r_t = r * jnp.exp(cum)
    b_t = kkn * a * e_inv
    k_t = k2 * e_inv
    for c in range(n_sub):
        rows = slice(c * L, (c + 1) * L)
        ar_s[c, 0:L, :] = a_t[rows]
        ar_s[c, L:2 * L, :] = r_t[rows]
        bk_s[c, 0:L, :] = b_t[rows]
        bk_s[c, L:2 * L, :] = k_t[rows]
        gam_s[c] = jnp.exp(cum[(c + 1) * L - 1:(c + 1) * L, :])
    v_s[...] = v
    rkb_s[...] = r * k2 * rk_ref[...]

    ri = lax.broadcasted_iota(jnp.int32, (L, L), 0)
    ci = lax.broadcasted_iota(jnp.int32, (L, L), 1)
    strict = ri > ci
    eye = jnp.where(ri == ci, 1.0, 0.0)
    ri2 = lax.broadcasted_iota(jnp.int32, (L, 2 * L), 0)
    ci2 = lax.broadcasted_iota(jnp.int32, (L, 2 * L), 1)
    incl2 = ri2 >= (ci2 % L)
    n_double = L.bit_length() - 2

    def sub_chunk(c, carry):
        r0 = pl.multiple_of(c * L, L)
        heads = range(n_heads)
        hsl = [slice(h * HEAD, (h + 1) * HEAD) for h in heads]
        ar = [ar_s[c, :, hsl[h]].astype(BF16) for h in heads]
        bk = [bk_s[c, :, hsl[h]].astype(BF16) for h in heads]
        vh = [v_s[pl.ds(r0, L), hsl[h]] for h in heads]
        st = [state_s[h][...] for h in heads]
        gm = [_dot_nt(ar[h], bk[h]) for h in heads]
        nm = [jnp.where(strict, gm[h][0:L, 0:L], 0.0) for h in heads]
        mak = [jnp.where(strict, gm[h][0:L, L:2 * L], 0.0) for h in heads]
        q = [jnp.where(incl2, gm[h][L:2 * L, :], 0.0) for h in heads]
        xinv = [eye - nm[h] for h in heads]
        p = nm
        for _ in range(n_double):
            p = [_dot(p[h], p[h]) for h in heads]
            xinv = [xinv[h] + _dot(xinv[h], p[h]) for h in heads]
        ars = [_dot_nt(ar[h], st[h]) for h in heads]
        mv = [_dot(mak[h], vh[h]) for h in heads]
        uu = [_dot(xinv[h], ars[h][0:L] + mv[h]) for h in heads]
        uv = [jnp.concatenate([-uu[h], vh[h]], axis=0).astype(BF16) for h in heads]
        yh = [ars[h][L:2 * L] + _dot(q[h], uv[h]) for h in heads]
        for h in heads:
            state_s[h][...] = (st[h] + _dot_tn(uv[h], bk[h])) * gam_s[c, :, hsl[h]]
        for h in heads:
            mean = jnp.mean(yh[h], axis=-1, keepdims=True)
            yc = yh[h] - mean
            var = jnp.mean(yc * yc, axis=-1, keepdims=True)
            bonus = jnp.sum(rkb_s[pl.ds(r0, L), hsl[h]], axis=-1, keepdims=True) * vh[h]
            y_s[pl.ds(r0, L), hsl[h]] = (yc * lax.rsqrt(var + GN_EPS) * lng_ref[:, hsl[h]]
                                         + lnb_ref[:, hsl[h]] + bonus)
        return carry

    lax.fori_loop(0, n_sub, sub_chunk, 0)

    y_rwkv = y_s[...] * g
    gates = _sigmoid(jnp.dot(xn, wgate_ref[...], preferred_element_type=F32))
    mix = gates[:, 0:D] * y_conv + gates[:, D:2 * D] * y_rwkv
    o_ref[0] = x + jnp.dot(mix.astype(BF16), wout_ref[...], preferred_element_type=F32)


def _const_spec(shape):
    nd = len(shape)
    return pl.BlockSpec(shape, lambda *_: (0,) * nd, pipeline_mode=pl.Buffered(1))


def _mixer(x, norm_g, w_in, conv_w, shift_mu, w0, w2, a0, a2, g2, k_k, k_a, r_k, ln_g, ln_b, w_out):
    bsz, s, d = x.shape
    n_heads = d // HEAD
    tc = min(MIX_TOKENS, s)
    L = RWKV_CHUNK
    n_sub = tc // L
    lora0 = 6 * d
    w_main = w_in[:, 0:6 * d].astype(BF16)
    pad = lambda t, n: jnp.pad(t, ((0, 0), (0, n - t.shape[1])))
    lora_cols = (DECAY_LORA, AAA_LORA, GATE_LORA)
    lora_pads = (LANES, LANES, 2 * LANES)
    pieces_w, pieces_mu, off = [], [], lora0
    for n, p in zip(lora_cols, lora_pads):
        pieces_w.append(pad(w_in[:, off:off + n], p))
        pieces_mu.append(pad(shift_mu[None, off - 3 * d:off - 3 * d + n], p))
        off += n
    w_lora = jnp.concatenate(pieces_w, axis=1).astype(BF16)
    mu_lora = jnp.concatenate(pieces_mu, axis=1)
    w_gate = w_in[:, off:off + 2 * d].astype(BF16)
    mu_rkv = shift_mu[None, 0:3 * d]
    padr = lambda t, n: jnp.pad(t, ((0, n - t.shape[0]), (0, 0)))
    w2p = padr(w2, LANES).astype(BF16)
    a2p = padr(a2, LANES).astype(BF16)
    g2p = padr(g2, 2 * LANES).astype(BF16)
    row = lambda t: t.reshape(1, -1)
    head_of = jnp.arange(d, dtype=jnp.int32) // HEAD
    head_sum = (head_of[:, None] == head_of[None, :]).astype(BF16)
    consts = [row(norm_g), w_main, w_lora, w_gate, conv_w, mu_rkv, mu_lora, row(w0), w2p, row(a0), a2p,
              g2p, row(k_k), row(k_a), row(r_k), row(ln_g), row(ln_b), w_out.astype(BF16), head_sum]
    kern = functools.partial(_mixer_kernel, n_heads=n_heads, d_model=d)
    return pl.pallas_call(
        kern,
        grid=(bsz, s // tc),
        in_specs=[pl.BlockSpec((1, tc, d), lambda b, c: (b, c, 0))] + [_const_spec(t.shape) for t in consts],
        out_specs=pl.BlockSpec((1, tc, d), lambda b, c: (b, c, 0)),
        out_shape=jax.ShapeDtypeStruct((bsz, s, d), F32),
        scratch_shapes=[
            pltpu.VMEM((SUBLANES, d), F32),
            pltpu.VMEM((SUBLANES, 3 * d), F32),
            pltpu.VMEM((SUBLANES, 4 * LANES), F32),
            pltpu.VMEM((n_sub, 2 * L, d), F32),
            pltpu.VMEM((n_sub, 2 * L, d), F32),
            pltpu.VMEM((tc, d), F32),
            pltpu.VMEM((tc, d), F32),
            pltpu.VMEM((tc, d), F32),
            pltpu.VMEM((n_sub, 1, d), F32),
        ] + [pltpu.VMEM((HEAD, HEAD), F32)] * n_heads,
        compiler_params=pltpu.CompilerParams(
            dimension_semantics=("arbitrary", "arbitrary"), vmem_limit_bytes=VMEM_LIMIT_BYTES),
        name="mixer",
    )(x, *consts)


def _router_kernel(h_ref, g_ref, rw_ref, rb_ref, meta_ref, cnt_ref, run_s, *, n_experts):
    tt = h_ref.shape[0]

    @pl.when(pl.program_id(0) == 0)
    def _():
        run_s[...] = jnp.zeros_like(run_s)

    h = h_ref[...]
    ms = jnp.mean(h * h, axis=-1, keepdims=True)
    hn = h * lax.rsqrt(ms + NORM_EPS) * g_ref[...]
    logits = jnp.dot(hn, rw_ref[...], preferred_element_type=F32,
                     precision=lax.Precision.HIGHEST) + rb_ref[...]
    lane = lax.broadcasted_iota(jnp.int32, (tt, LANES), 1)
    neg = jnp.float32(-jnp.inf)
    work = jnp.where(lane < n_experts, logits, neg)
    vals, idxs = [], []
    for _ in range(TOP_K):
        m = jnp.max(work, axis=-1, keepdims=True)
        i = jnp.min(jnp.where(work == m, lane, LANES), axis=-1, keepdims=True)
        vals.append(m)
        idxs.append(i)
        work = jnp.where(lane == i, neg, work)
    ex = [jnp.exp(vv - vals[0]) for vv in vals]
    den = ex[0] + ex[1] + ex[2] + ex[3]
    gates = [e / den for e in ex]

    onehot = jnp.zeros((tt, LANES), jnp.bool_)
    for kk in range(TOP_K):
        onehot = onehot | (lane == (idxs[kk] + kk * n_experts))
    oh = jnp.where(onehot, 1.0, 0.0)
    ri = lax.broadcasted_iota(jnp.int32, (tt, tt), 0)
    ci = lax.broadcasted_iota(jnp.int32, (tt, tt), 1)
    tri = jnp.where(ri > ci, 1.0, 0.0).astype(BF16)
    cnt = jnp.dot(tri, oh.astype(BF16), preferred_element_type=F32)
    tot = jnp.broadcast_to(jnp.sum(oh, axis=0, keepdims=True), (SUBLANES, LANES))
    lane8 = lax.broadcasted_iota(jnp.int32, (SUBLANES, LANES), 1)
    pk = jnp.zeros_like(tot)
    te = tot
    for j in range(1, TOP_K):
        rolled = pltpu.roll(tot, j * n_experts, axis=1)
        pk = pk + jnp.where(lane8 >= j * n_experts, rolled, 0.0)
        te = te + rolled
    before = cnt + (run_s[...] + pk)[0:1, :]
    ranks = [jnp.sum(jnp.where(onehot & (lane // n_experts == kk), before, 0.0), axis=-1, keepdims=True)
             for kk in range(TOP_K)]
    run_s[...] = run_s[...] + te
    cnt_ref[...] = run_s[...]

    meta = jnp.zeros((tt, LANES), F32)
    for kk in range(TOP_K):
        meta = jnp.where(lane == kk, idxs[kk].astype(F32), meta)
        meta = jnp.where(lane == TOP_K + kk, gates[kk], meta)
        meta = jnp.where(lane == 2 * TOP_K + kk, ranks[kk], meta)
    meta_ref[...] = meta


def _router(h2, norm_g, router_w, router_b):
    t, d = h2.shape
    n_experts = router_w.shape[1]
    tt = min(ROUTE_TOKENS, t)
    rw = jnp.pad(router_w, ((0, 0), (0, LANES - n_experts)))
    rb = jnp.pad(router_b, (0, LANES - n_experts)).reshape(1, LANES)
    kern = functools.partial(_router_kernel, n_experts=n_experts)
    return pl.pallas_call(
        kern,
        grid=(t // tt,),
        in_specs=[pl.BlockSpec((tt, d), lambda i: (i, 0)),
                  pl.BlockSpec((1, d), lambda i: (0, 0)),
                  pl.BlockSpec((d, LANES), lambda i: (0, 0)),
                  pl.BlockSpec((1, LANES), lambda i: (0, 0))],
        out_specs=[pl.BlockSpec((tt, LANES), lambda i: (i, 0)),
                   pl.BlockSpec((SUBLANES, LANES), lambda i: (0, 0))],
        out_shape=[jax.ShapeDtypeStruct((t, LANES), F32),
                   jax.ShapeDtypeStruct((SUBLANES, LANES), F32)],
        scratch_shapes=[pltpu.VMEM((SUBLANES, LANES), F32)],
        compiler_params=pltpu.CompilerParams(
            dimension_semantics=("arbitrary",), vmem_limit_bytes=VMEM_LIMIT_BYTES),
        name="router",
    )(h2, norm_g.reshape(1, d), rw, rb)


def _row_gather(table, idx):
    n_idx = idx.shape[0]
    d = table.shape[1]
    w = GATHER_WINDOW
    sc = plsc.get_sparse_core_info()
    n_workers = sc.num_cores * sc.num_subcores
    per_worker = n_idx // n_workers
    n_win = per_worker // w
    assert per_worker * n_workers == n_idx and n_win * w == per_worker and n_win % 2 == 0
    mesh = plsc.VectorSubcoreMesh(core_axis_name="core", subcore_axis_name="subcore")

    @functools.partial(
        pl.kernel, out_type=jax.ShapeDtypeStruct((n_idx, d), table.dtype), mesh=mesh,
        scratch_types=[pltpu.VMEM((per_worker,), jnp.int32),
                       pltpu.VMEM((w, d), table.dtype), pltpu.VMEM((w, d), table.dtype),
                       pltpu.SemaphoreType.DMA, pltpu.SemaphoreType.DMA,
                       pltpu.SemaphoreType.DMA, pltpu.SemaphoreType.DMA])
    def gather_kernel(table_hbm, idx_hbm, out_hbm, idx_v, buf_a, buf_b, gsem_a, gsem_b, psem_a, psem_b):
        worker = lax.axis_index("subcore") * sc.num_cores + lax.axis_index("core")
        base = worker * per_worker
        pltpu.sync_copy(idx_hbm.at[pl.ds(base, per_worker)], idx_v)

        def gather(j, buf, sem):
            return pltpu.make_async_copy(table_hbm.at[idx_v.at[pl.ds(j * w, w)]], buf, sem)

        def put(j, buf, sem):
            return pltpu.make_async_copy(buf, out_hbm.at[pl.ds(base + j * w, w)], sem)

        gather(0, buf_a, gsem_a).start()

        @pl.loop(0, n_win, step=2)
        def _(j):
            gather(j, buf_a, gsem_a).wait()

            @pl.when(j > 0)
            def _():
                put(j - 1, buf_b, psem_b).wait()

            gather(j + 1, buf_b, gsem_b).start()
            put(j, buf_a, psem_a).start()
            gather(j + 1, buf_b, gsem_b).wait()
            put(j, buf_a, psem_a).wait()

            @pl.when(j + 2 < n_win)
            def _():
                gather(j + 2, buf_a, gsem_a).start()

            put(j + 1, buf_b, psem_b).start()

        put(n_win - 1, buf_b, psem_b).wait()

    return gather_kernel(table, idx)


def _slot_tokens(pos, n_slots):
    n = pos.shape[0]
    sc = plsc.get_sparse_core_info()
    n_cores, n_sub = sc.num_cores, sc.num_subcores
    rows = n // LANES
    rows_per_tile = rows // n_sub
    out_per_worker = n_slots // (n_cores * n_sub)
    init_per_tile = n_slots // n_sub
    assert rows_per_tile * n_sub * LANES == n and out_per_worker * n_cores * n_sub == n_slots
    mesh = plsc.VectorSubcoreMesh(core_axis_name="core", subcore_axis_name="subcore")
    tokens = (jnp.arange(n, dtype=jnp.int32) // TOP_K).reshape(rows, LANES)
    fill = jnp.arange(n_slots, dtype=jnp.int32) % (n // TOP_K)

    @functools.partial(
        pl.kernel, out_type=jax.ShapeDtypeStruct((n_slots,), jnp.int32), mesh=mesh,
        scratch_types=[pltpu.VMEM((rows_per_tile, LANES), jnp.int32),
                       pltpu.VMEM((rows_per_tile, LANES), jnp.int32),
                       pltpu.VMEM_SHARED((n_slots,), jnp.int32)])
    def invert_kernel(pos_hbm, tok_hbm, fill_hbm, out_hbm, idx_v, val_v, shared):
        cid = lax.axis_index("core")
        sid = lax.axis_index("subcore")
        pltpu.sync_copy(fill_hbm.at[pl.ds(sid * init_per_tile, init_per_tile)],
                        shared.at[pl.ds(sid * init_per_tile, init_per_tile)])
        pltpu.sync_copy(pos_hbm.at[pl.ds(sid * rows_per_tile, rows_per_tile)], idx_v)
        pltpu.sync_copy(tok_hbm.at[pl.ds(sid * rows_per_tile, rows_per_tile)], val_v)
        plsc.subcore_barrier()

        @pl.loop(0, rows_per_tile)
        def _(j):
            pltpu.sync_copy(val_v.at[j], shared.at[idx_v.at[j]])

        plsc.subcore_barrier()
        off = (cid * n_sub + sid) * out_per_worker
        pltpu.sync_copy(shared.at[pl.ds(off, out_per_worker)], out_hbm.at[pl.ds(off, out_per_worker)])

    return invert_kernel(pos.reshape(rows, LANES), tokens, fill)


def _experts_kernel(te_ref, nv_ref, xs_ref, g_ref, w1_ref, b1_ref, w2_ref, b2_ref, ys_ref, w1_s, w2_s, *, d_ff):
    i = pl.program_id(0)
    prev = te_ref[jnp.maximum(i - 1, 0)]
    changed = (i == 0) | (te_ref[i] != prev)

    @pl.when(changed)
    def _():
        w1_s[...] = w1_ref[0].astype(BF16)
        w2_s[...] = w2_ref[0].astype(BF16)

    nvalid = nv_ref[i]

    @pl.when(nvalid > 0)
    def _():
        tm = xs_ref.shape[0]
        groups = [slice(r, r + EXPERT_ROWS) for r in range(0, tm, EXPERT_ROWS)]
        xg = []
        for rows in groups:
            row = rows.start + lax.broadcasted_iota(jnp.int32, (EXPERT_ROWS, 1), 0)
            h = jnp.where(row < nvalid, xs_ref[rows, :], 0.0)
            ms = jnp.mean(h * h, axis=-1, keepdims=True)
            xg.append((h * lax.rsqrt(ms + NORM_EPS) * g_ref[...]).astype(BF16))
        ug = [jnp.dot(x, w1_s[...], preferred_element_type=F32) + b1_ref[0] for x in xg]
        ag = []
        for u in ug:
            glu = jnp.minimum(u[:, 0:d_ff], SWIGLU_LIMIT)
            lin = jnp.clip(u[:, d_ff:2 * d_ff], -SWIGLU_LIMIT, SWIGLU_LIMIT)
            ag.append((glu * _sigmoid(SWIGLU_ALPHA * glu) * (lin + 1.0)).astype(BF16))
        for rows, act in zip(groups, ag):
            ys_ref[rows, :] = jnp.dot(act, w2_s[...], preferred_element_type=F32) + b2_ref[0]

    @pl.when(nvalid <= 0)
    def _():
        ys_ref[...] = jnp.zeros_like(ys_ref)


def _experts(xs, norm_g, tile_expert, tile_valid, w1, b1, w2, b2):
    n_slots, d = xs.shape
    n_exp, _, two_ff = w1.shape
    d_ff = two_ff // 2
    tm = SLOT_TILE
    n_tiles = n_slots // tm
    kern = functools.partial(_experts_kernel, d_ff=d_ff)
    grid_spec = pltpu.PrefetchScalarGridSpec(
        num_scalar_prefetch=2,
        grid=(n_tiles,),
        in_specs=[pl.BlockSpec((tm, d), lambda i, te, nv: (i, 0)),
                  pl.BlockSpec((1, d), lambda i, te, nv: (0, 0)),
                  pl.BlockSpec((1, d, two_ff), lambda i, te, nv: (te[i], 0, 0)),
                  pl.BlockSpec((1, 1, two_ff), lambda i, te, nv: (te[i], 0, 0)),
                  pl.BlockSpec((1, d_ff, d), lambda i, te, nv: (te[i], 0, 0)),
                  pl.BlockSpec((1, 1, d), lambda i, te, nv: (te[i], 0, 0))],
        out_specs=pl.BlockSpec((tm, d), lambda i, te, nv: (i, 0)),
        scratch_shapes=[pltpu.VMEM((d, two_ff), BF16), pltpu.VMEM((d_ff, d), BF16)],
    )
    return pl.pallas_call(
        kern,
        grid_spec=grid_spec,
        out_shape=jax.ShapeDtypeStruct((n_slots, d), F32),
        compiler_params=pltpu.CompilerParams(
            dimension_semantics=("arbitrary",), vmem_limit_bytes=VMEM_LIMIT_BYTES),
        name="experts",
    )(tile_expert, tile_valid, xs, norm_g.reshape(1, d), w1, b1.reshape(n_exp, 1, two_ff), w2,
      b2.reshape(n_exp, 1, d))


def _combine_kernel(h_ref, meta_ref, g_ref, *rest):
    y_refs, o_ref = rest[:TOP_K], rest[TOP_K]
    acc = h_ref[...]
    for kk in range(TOP_K):
        acc = acc + meta_ref[:, TOP_K + kk:TOP_K + kk + 1] * y_refs[kk][...]
    ms = jnp.mean(acc * acc, axis=-1, keepdims=True)
    o_ref[...] = acc * lax.rsqrt(ms + NORM_EPS) * g_ref[...]


def _combine(h2, y4, meta, norm_g):
    t, d = h2.shape
    tt = min(MOVE_TOKENS, t)
    n_blk = t // tt
    y_specs = [pl.BlockSpec((tt, d), functools.partial(lambda i, kk: (kk * n_blk + i, 0), kk=kk))
               for kk in range(TOP_K)]
    return pl.pallas_call(
        _combine_kernel,
        grid=(n_blk,),
        in_specs=[pl.BlockSpec((tt, d), lambda i: (i, 0)),
                  pl.BlockSpec((tt, LANES), lambda i: (i, 0)),
                  pl.BlockSpec((1, d), lambda i: (0, 0))] + y_specs,
        out_specs=pl.BlockSpec((tt, d), lambda i: (i, 0)),
        out_shape=jax.ShapeDtypeStruct((t, d), F32),
        compiler_params=pltpu.CompilerParams(
            dimension_semantics=("arbitrary",), vmem_limit_bytes=VMEM_LIMIT_BYTES),
        name="combine",
    )(h2, meta, norm_g.reshape(1, d), *([y4] * TOP_K))


def _moe(h2, norm_ffn_g, router_w, router_b, w1, b1, w2, b2, norm_final_g):
    t, d = h2.shape
    n_exp = router_w.shape[1]
    tm = SLOT_TILE
    meta, counts = _router(h2, norm_ffn_g, router_w, router_b)
    eidx = meta[:, 0:TOP_K].astype(jnp.int32)
    rank = meta[:, 2 * TOP_K:3 * TOP_K].astype(jnp.int32)
    cnt = counts[0, 0:n_exp].astype(jnp.int32)
    padded = (cnt + tm - 1) // tm * tm
    seg_end = jnp.cumsum(padded)
    seg_start = seg_end - padded
    pos = (seg_start[eidx] + rank).reshape(-1)
    n_tiles = -(-(t * TOP_K) // tm) + n_exp
    n_slots = n_tiles * tm
    tile_start = jnp.arange(n_tiles, dtype=jnp.int32) * tm
    tile_expert = jnp.minimum(jnp.sum(tile_start[:, None] >= seg_end[None, :], axis=1), n_exp - 1).astype(jnp.int32)
    tile_valid = jnp.clip(seg_start[tile_expert] + cnt[tile_expert] - tile_start, 0, tm).astype(jnp.int32)
    slot_token = _slot_tokens(pos, n_slots)
    xs = _row_gather(h2, slot_token)
    ys = _experts(xs, norm_ffn_g, tile_expert, tile_valid, w1, b1, w2, b2)
    y4 = _row_gather(ys, pos.reshape(t, TOP_K).T.reshape(-1))
    return _combine(h2, y4, meta, norm_final_g)


def kernel(x, norm_mix_g, w_in, conv_w, shift_mu, decay_w0, decay_w2, iclr_a0, iclr_a2, gate_g2, k_k, k_a,
           r_k, ln_x_g, ln_x_b, w_out, norm_ffn_g, router_w, router_b, exp_w1, exp_b1, exp_w2, exp_b2,
           norm_final_g):
    bsz, s, d = x.shape
    depth = w_in.shape[0]
    assert depth == 1, "final norm is fused into the last layer's combine kernel"
    h = _mixer(x, norm_mix_g[0], w_in[0], conv_w[0], shift_mu[0], decay_w0[0], decay_w2[0], iclr_a0[0],
               iclr_a2[0], gate_g2[0], k_k[0], k_a[0], r_k[0], ln_x_g[0], ln_x_b[0], w_out[0])
    out = _moe(h.reshape(bsz * s, d), norm_ffn_g[0], router_w[0], router_b[0], exp_w1[0], exp_b1[0],
               exp_w2[0], exp_b2[0], norm_final_g)
    return out.reshape(bsz, s, d)
```

```python
import functools

import jax
import jax.numpy as jnp
from jax import lax
from jax.experimental import pallas as pl
from jax.experimental.pallas import tpu as pltpu
from jax.experimental.pallas import tpu_sc as plsc

HEAD = 64
DECAY_LORA = 64
AAA_LORA = 64
GATE_LORA = 160
TOP_K = 4
SWIGLU_ALPHA = 1.702
SWIGLU_LIMIT = 7.0
NORM_EPS = 1e-5
GN_EPS = 64e-5
DECAY_SCALE = 0.6065306597126334

LANES = 128
SUBLANES = 8
VMEM_LIMIT_BYTES = 56 * 1024 * 1024

RWKV_CHUNK = 64
MIX_TOKENS = 256
ROUTE_TOKENS = 512
MOVE_TOKENS = 256
GATHER_WINDOW = 32
SLOT_TILE = 512
EXPERT_ROWS = 256
BATCH_GROUPS = 2

BF16 = jnp.bfloat16
F32 = jnp.float32


def _dot(a, b):
    return jnp.dot(a.astype(BF16), b.astype(BF16), preferred_element_type=F32)


def _dot_nt(a, b):
    return lax.dot_general(a.astype(BF16), b.astype(BF16), (((1,), (1,)), ((), ())),
                           preferred_element_type=F32)


def _dot_tn(a, b):
    return lax.dot_general(a.astype(BF16), b.astype(BF16), (((0,), (0,)), ((), ())),
                           preferred_element_type=F32)


def _sigmoid(x):
    return 0.5 * jnp.tanh(0.5 * x) + 0.5


def _shift_rows(x, n, carry):
    rolled = pltpu.roll(x, n, axis=0)
    row = lax.broadcasted_iota(jnp.int32, x.shape, 0)
    out = rolled
    for i in range(n):
        out = jnp.where(row == i, carry[SUBLANES - n + i:SUBLANES - n + i + 1, :], out)
    return out


def _mixer_kernel(x_ref, g_ref, wmain_ref, wlora_ref, wgate_ref, cw_ref, mu_rkv_ref, mu_lora_ref,
                  w0_ref, w2_ref, a0_ref, a2_ref, g2_ref, kk_ref, ka_ref, rk_ref, lng_ref, lnb_ref,
                  wout_ref, hsum_ref, o_ref,
                  cu_s, cp_s, cl_s, ar_s, bk_s, v_s, rkb_s, y_s, gam_s, *state_s,
                  n_heads, d_model):
    tc = x_ref.shape[1]
    L = RWKV_CHUNK
    n_sub = tc // L
    D = d_model

    @pl.when(pl.program_id(1) == 0)
    def _():
        for st_ref in state_s:
            st_ref[...] = jnp.zeros_like(st_ref)
        cu_s[...] = jnp.zeros_like(cu_s)
        cp_s[...] = jnp.zeros_like(cp_s)
        cl_s[...] = jnp.zeros_like(cl_s)

    x = x_ref[0]
    ms = jnp.mean(x * x, axis=-1, keepdims=True)
    xn = (x * lax.rsqrt(ms + NORM_EPS) * g_ref[...]).astype(BF16)

    pc = jnp.dot(xn, wmain_ref[:, 0:3 * D], preferred_element_type=F32)
    u = pc[:, D:2 * D] * pc[:, 2 * D:3 * D]
    cu = cu_s[...]
    conv = (cw_ref[0:1, :] * _shift_rows(u, 2, cu) + cw_ref[1:2, :] * _shift_rows(u, 1, cu)
            + cw_ref[2:3, :] * u)
    y_conv = pc[:, 0:D] * conv
    cu_s[...] = u[tc - SUBLANES:tc, :]

    pr = jnp.dot(xn, wmain_ref[:, 3 * D:6 * D], preferred_element_type=F32)
    cp = cp_s[...]
    cp_s[...] = pr[tc - SUBLANES:tc, :]
    pr = pr + (_shift_rows(pr, 1, cp) - pr) * mu_rkv_ref[...]
    plo = jnp.dot(xn, wlora_ref[...], preferred_element_type=F32)
    cl = cl_s[...]
    cl_s[...] = plo[tc - SUBLANES:tc, :]
    plo = plo + (_shift_rows(plo, 1, cl) - plo) * mu_lora_ref[...]

    r = pr[:, 0:D]
    k = pr[:, D:2 * D]
    v = pr[:, 2 * D:3 * D]
    wd = plo[:, 0:LANES]
    ad = plo[:, LANES:2 * LANES]
    gd = plo[:, 2 * LANES:4 * LANES]

    lw = -DECAY_SCALE * _sigmoid(w0_ref[...] + _dot(jnp.tanh(wd), w2_ref[...]))
    a = _sigmoid(a0_ref[...] + _dot(ad, a2_ref[...]))
    g = _dot(_sigmoid(gd), g2_ref[...])

    row = lax.broadcasted_iota(jnp.int32, (tc, tc), 0)
    col = lax.broadcasted_iota(jnp.int32, (tc, tc), 1)
    tri = jnp.where((row >= col) & ((row // L) == (col // L)), 1.0, 0.0).astype(BF16)
    lw_hi = lw.astype(BF16)
    lw_lo = (lw - lw_hi.astype(F32)).astype(BF16)
    cum = (jnp.dot(tri, lw_hi, preferred_element_type=F32)
           + jnp.dot(tri, lw_lo, preferred_element_type=F32))
    e_inv = jnp.exp(-cum)

    kkraw = k * kk_ref[...]
    ss = _dot(kkraw * kkraw, hsum_ref[...])
    kkn = kkraw * jnp.minimum(lax.rsqrt(ss), 1e12)

    k2 = k * (1.0 + (a - 1.0) * ka_ref[...])
    a_t = kkn * jnp.exp(cum - lw)
    r_t = r * jnp.exp(cum)
    b_t = kkn * a * e_inv
    k_t = k2 * e_inv
    for c in range(n_sub):
        rows = slice(c * L, (c + 1) * L)
        ar_s[c, 0:L, :] = a_t[rows]
        ar_s[c, L:2 * L, :] = r_t[rows]
        bk_s[c, 0:L, :] = b_t[rows]
        bk_s[c, L:2 * L, :] = k_t[rows]
        gam_s[c] = jnp.exp(cum[(c + 1) * L - 1:(c + 1) * L, :])
    v_s[...] = v
    rkb_s[...] = r * k2 * rk_ref[...]

    ri = lax.broadcasted_iota(jnp.int32, (L, L), 0)
    ci = lax.broadcasted_iota(jnp.int32, (L, L), 1)
    strict = ri > ci
    eye = jnp.where(ri == ci, 1.0, 0.0)
    ri2 = lax.broadcasted_iota(jnp.int32, (L, 2 * L), 0)
    ci2 = lax.broadcasted_iota(jnp.int32, (L, 2 * L), 1)
    incl2 = ri2 >= (ci2 % L)
    n_double = L.bit_length() - 2

    def sub_chunk(c, carry):
        r0 = pl.multiple_of(c * L, L)
        heads = range(n_heads)
        hsl = [slice(h * HEAD, (h + 1) * HEAD) for h in heads]
        ar = [ar_s[c, :, hsl[h]].astype(BF16) for h in heads]
        bk = [bk_s[c, :, hsl[h]].astype(BF16) for h in heads]
        vh = [v_s[pl.ds(r0, L), hsl[h]] for h in heads]
        st = [state_s[h][...] for h in heads]
        gm = [_dot_nt(ar[h], bk[h]) for h in heads]
        nm = [jnp.where(strict, gm[h][0:L, 0:L], 0.0) for h in heads]
        mak = [jnp.where(strict, gm[h][0:L, L:2 * L], 0.0) for h in heads]
        q = [jnp.where(incl2, gm[h][L:2 * L, :], 0.0) for h in heads]
        xinv = [eye - nm[h] for h in heads]
        p = nm
        for _ in range(n_double):
            p = [_dot(p[h], p[h]) for h in heads]
            xinv = [xinv[h] + _dot(xinv[h], p[h]) for h in heads]
        ars = [_dot_nt(ar[h], st[h]) for h in heads]
        mv = [_dot(mak[h], vh[h]) for h in heads]
        uu = [_dot(xinv[h], ars[h][0:L] + mv[h]) for h in heads]
        uv = [jnp.concatenate([-uu[h], vh[h]], axis=0).astype(BF16) for h in heads]
        yh = [ars[h][L:2 * L] + _dot(q[h], uv[h]) for h in heads]
        for h in heads:
            state_s[h][...] = (st[h] + _dot_tn(uv[h], bk[h])) * gam_s[c, :, hsl[h]]
        for h in heads:
            mean = jnp.mean(yh[h], axis=-1, keepdims=True)
            yc = yh[h] - mean
            var = jnp.mean(yc * yc, axis=-1, keepdims=True)
            bonus = jnp.sum(rkb_s[pl.ds(r0, L), hsl[h]], axis=-1, keepdims=True) * vh[h]
            y_s[pl.ds(r0, L), hsl[h]] = (yc * lax.rsqrt(var + GN_EPS) * lng_ref[:, hsl[h]]
                                         + lnb_ref[:, hsl[h]] + bonus)
        return carry

    lax.fori_loop(0, n_sub, sub_chunk, 0)

    y_rwkv = y_s[...] * g
    gates = _sigmoid(jnp.dot(xn, wgate_ref[...], preferred_element_type=F32))
    mix = gates[:, 0:D] * y_conv + gates[:, D:2 * D] * y_rwkv
    o_ref[0] = x + jnp.dot(mix.astype(BF16), wout_ref[...], preferred_element_type=F32)


def _const_spec(shape):
    nd = len(shape)
    return pl.BlockSpec(shape, lambda *_: (0,) * nd, pipeline_mode=pl.Buffered(1))


def _mixer(x, n_groups, norm_g, w_in, conv_w, shift_mu, w0, w2, a0, a2, g2, k_k, k_a, r_k, ln_g, ln_b, w_out):
    bsz, s, d = x.shape
    gb = bsz // n_groups
    n_heads = d // HEAD
    tc = min(MIX_TOKENS, s)
    L = RWKV_CHUNK
    n_sub = tc // L
    lora0 = 6 * d
    w_main = w_in[:, 0:6 * d].astype(BF16)
    pad = lambda t, n: jnp.pad(t, ((0, 0), (0, n - t.shape[1])))
    lora_cols = (DECAY_LORA, AAA_LORA, GATE_LORA)
    lora_pads = (LANES, LANES, 2 * LANES)
    pieces_w, pieces_mu, off = [], [], lora0
    for n, p in zip(lora_cols, lora_pads):
        pieces_w.append(pad(w_in[:, off:off + n], p))
        pieces_mu.append(pad(shift_mu[None, off - 3 * d:off - 3 * d + n], p))
        off += n
    w_lora = jnp.concatenate(pieces_w, axis=1).astype(BF16)
    mu_lora = jnp.concatenate(pieces_mu, axis=1)
    w_gate = w_in[:, off:off + 2 * d].astype(BF16)
    mu_rkv = shift_mu[None, 0:3 * d]
    padr = lambda t, n: jnp.pad(t, ((0, n - t.shape[0]), (0, 0)))
    w2p = padr(w2, LANES).astype(BF16)
    a2p = padr(a2, LANES).astype(BF16)
    g2p = padr(g2, 2 * LANES).astype(BF16)
    row = lambda t: t.reshape(1, -1)
    head_of = jnp.arange(d, dtype=jnp.int32) // HEAD
    head_sum = (head_of[:, None] == head_of[None, :]).astype(BF16)
    consts = [row(norm_g), w_main, w_lora, w_gate, conv_w, mu_rkv, mu_lora, row(w0), w2p, row(a0), a2p,
              g2p, row(k_k), row(k_a), row(r_k), row(ln_g), row(ln_b), w_out.astype(BF16), head_sum]
    kern = functools.partial(_mixer_kernel, n_heads=n_heads, d_model=d)
    call = lambda first: pl.pallas_call(
        kern,
        grid=(gb, s // tc),
        in_specs=[pl.BlockSpec((1, tc, d), lambda b, c: (b + first, c, 0))]
        + [_const_spec(t.shape) for t in consts],
        out_specs=pl.BlockSpec((1, tc, d), lambda b, c: (b, c, 0)),
        out_shape=jax.ShapeDtypeStruct((gb, s, d), F32),
        scratch_shapes=[
            pltpu.VMEM((SUBLANES, d), F32),
            pltpu.VMEM((SUBLANES, 3 * d), F32),
            pltpu.VMEM((SUBLANES, 4 * LANES), F32),
            pltpu.VMEM((n_sub, 2 * L, d), F32),
            pltpu.VMEM((n_sub, 2 * L, d), F32),
            pltpu.VMEM((tc, d), F32),
            pltpu.VMEM((tc, d), F32),
            pltpu.VMEM((tc, d), F32),
            pltpu.VMEM((n_sub, 1, d), F32),
        ] + [pltpu.VMEM((HEAD, HEAD), F32)] * n_heads,
        compiler_params=pltpu.CompilerParams(
            dimension_semantics=("arbitrary", "arbitrary"), vmem_limit_bytes=VMEM_LIMIT_BYTES),
        name="mixer",
    )(x, *consts)
    return [call(g * gb) for g in range(n_groups)]


def _router_kernel(h_ref, g_ref, rw_ref, rb_ref, meta_ref, cnt_ref, run_s, *, n_experts):
    tt = h_ref.shape[0]

    @pl.when(pl.program_id(0) == 0)
    def _():
        run_s[...] = jnp.zeros_like(run_s)

    h = h_ref[...]
    ms = jnp.mean(h * h, axis=-1, keepdims=True)
    hn = h * lax.rsqrt(ms + NORM_EPS) * g_ref[...]
    logits = jnp.dot(hn, rw_ref[...], preferred_element_type=F32,
                     precision=lax.Precision.HIGHEST) + rb_ref[...]
    lane = lax.broadcasted_iota(jnp.int32, (tt, LANES), 1)
    neg = jnp.float32(-jnp.inf)
    work = jnp.where(lane < n_experts, logits, neg)
    vals, idxs = [], []
    for _ in range(TOP_K):
        m = jnp.max(work, axis=-1, keepdims=True)
        i = jnp.min(jnp.where(work == m, lane, LANES), axis=-1, keepdims=True)
        vals.append(m)
        idxs.append(i)
        work = jnp.where(lane == i, neg, work)
    ex = [jnp.exp(vv - vals[0]) for vv in vals]
    den = ex[0] + ex[1] + ex[2] + ex[3]
    gates = [e / den for e in ex]

    onehot = jnp.zeros((tt, LANES), jnp.bool_)
    for kk in range(TOP_K):
        onehot = onehot | (lane == (idxs[kk] + kk * n_experts))
    oh = jnp.where(onehot, 1.0, 0.0)
    ri = lax.broadcasted_iota(jnp.int32, (tt, tt), 0)
    ci = lax.broadcasted_iota(jnp.int32, (tt, tt), 1)
    tri = jnp.where(ri > ci, 1.0, 0.0).astype(BF16)
    cnt = jnp.dot(tri, oh.astype(BF16), preferred_element_type=F32)
    tot = jnp.broadcast_to(jnp.sum(oh, axis=0, keepdims=True), (SUBLANES, LANES))
    lane8 = lax.broadcasted_iota(jnp.int32, (SUBLANES, LANES), 1)
    pk = jnp.zeros_like(tot)
    te = tot
    for j in range(1, TOP_K):
        rolled = pltpu.roll(tot, j * n_experts, axis=1)
        pk = pk + jnp.where(lane8 >= j * n_experts, rolled, 0.0)
        te = te + rolled
    before = cnt + (run_s[...] + pk)[0:1, :]
    ranks = [jnp.sum(jnp.where(onehot & (lane // n_experts == kk), before, 0.0), axis=-1, keepdims=True)
             for kk in range(TOP_K)]
    run_s[...] = run_s[...] + te
    cnt_ref[...] = run_s[...]

    meta = jnp.zeros((tt, LANES), F32)
    for kk in range(TOP_K):
        meta = jnp.where(lane == kk, idxs[kk].astype(F32), meta)
        meta = jnp.where(lane == TOP_K + kk, gates[kk], meta)
        meta = jnp.where(lane == 2 * TOP_K + kk, ranks[kk], meta)
    meta_ref[...] = meta


def _router(h2, norm_g, router_w, router_b):
    t, d = h2.shape
    n_experts = router_w.shape[1]
    tt = min(ROUTE_TOKENS, t)
    rw = jnp.pad(router_w, ((0, 0), (0, LANES - n_experts)))
    rb = jnp.pad(router_b, (0, LANES - n_experts)).reshape(1, LANES)
    kern = functools.partial(_router_kernel, n_experts=n_experts)
    return pl.pallas_call(
        kern,
        grid=(t // tt,),
        in_specs=[pl.BlockSpec((tt, d), lambda i: (i, 0)),
                  pl.BlockSpec((1, d), lambda i: (0, 0)),
                  pl.BlockSpec((d, LANES), lambda i: (0, 0)),
                  pl.BlockSpec((1, LANES), lambda i: (0, 0))],
        out_specs=[pl.BlockSpec((tt, LANES), lambda i: (i, 0)),
                   pl.BlockSpec((SUBLANES, LANES), lambda i: (0, 0))],
        out_shape=[jax.ShapeDtypeStruct((t, LANES), F32),
                   jax.ShapeDtypeStruct((SUBLANES, LANES), F32)],
        scratch_shapes=[pltpu.VMEM((SUBLANES, LANES), F32)],
        compiler_params=pltpu.CompilerParams(
            dimension_semantics=("arbitrary",), vmem_limit_bytes=VMEM_LIMIT_BYTES),
        name="router",
    )(h2, norm_g.reshape(1, d), rw, rb)


def _row_gather(table, idx):
    n_idx = idx.shape[0]
    d = table.shape[1]
    w = GATHER_WINDOW
    sc = plsc.get_sparse_core_info()
    n_workers = sc.num_cores * sc.num_subcores
    per_worker = n_idx // n_workers
    n_win = per_worker // w
    assert per_worker * n_workers == n_idx and n_win * w == per_worker and n_win % 2 == 0
    mesh = plsc.VectorSubcoreMesh(core_axis_name="core", subcore_axis_name="subcore")

    @functools.partial(
        pl.kernel, out_type=jax.ShapeDtypeStruct((n_idx, d), table.dtype), mesh=mesh,
        scratch_types=[pltpu.VMEM((per_worker,), jnp.int32),
                       pltpu.VMEM((w, d), table.dtype), pltpu.VMEM((w, d), table.dtype),
                       pltpu.SemaphoreType.DMA, pltpu.SemaphoreType.DMA,
                       pltpu.SemaphoreType.DMA, pltpu.SemaphoreType.DMA])
    def gather_kernel(table_hbm, idx_hbm, out_hbm, idx_v, buf_a, buf_b, gsem_a, gsem_b, psem_a, psem_b):
        worker = lax.axis_index("subcore") * sc.num_cores + lax.axis_index("core")
        base = worker * per_worker
        pltpu.sync_copy(idx_hbm.at[pl.ds(base, per_worker)], idx_v)

        def gather(j, buf, sem):
            return pltpu.make_async_copy(table_hbm.at[idx_v.at[pl.ds(j * w, w)]], buf, sem)

        def put(j, buf, sem):
            return pltpu.make_async_copy(buf, out_hbm.at[pl.ds(base + j * w, w)], sem)

        gather(0, buf_a, gsem_a).start()

        @pl.loop(0, n_win, step=2)
        def _(j):
            gather(j, buf_a, gsem_a).wait()

            @pl.when(j > 0)
            def _():
                put(j - 1, buf_b, psem_b).wait()

            gather(j + 1, buf_b, gsem_b).start()
            put(j, buf_a, psem_a).start()
            gather(j + 1, buf_b, gsem_b).wait()
            put(j, buf_a, psem_a).wait()

            @pl.when(j + 2 < n_win)
            def _():
                gather(j + 2, buf_a, gsem_a).start()

            put(j + 1, buf_b, psem_b).start()

        put(n_win - 1, buf_b, psem_b).wait()

    return gather_kernel(table, idx)


def _slot_tokens(pos, n_slots):
    n = pos.shape[0]
    sc = plsc.get_sparse_core_info()
    n_cores, n_sub = sc.num_cores, sc.num_subcores
    rows = n // LANES
    rows_per_tile = rows // n_sub
    out_per_worker = n_slots // (n_cores * n_sub)
    init_per_tile = n_slots // n_sub
    assert rows_per_tile * n_sub * LANES == n and out_per_worker * n_cores * n_sub == n_slots
    mesh = plsc.VectorSubcoreMesh(core_axis_name="core", subcore_axis_name="subcore")
    tokens = (jnp.arange(n, dtype=jnp.int32) // TOP_K).reshape(rows, LANES)
    fill = jnp.arange(n_slots, dtype=jnp.int32) % (n // TOP_K)

    @functools.partial(
        pl.kernel, out_type=jax.ShapeDtypeStruct((n_slots,), jnp.int32), mesh=mesh,
        scratch_types=[pltpu.VMEM((rows_per_tile, LANES), jnp.int32),
                       pltpu.VMEM((rows_per_tile, LANES), jnp.int32),
                       pltpu.VMEM_SHARED((n_slots,), jnp.int32)])
    def invert_kernel(pos_hbm, tok_hbm, fill_hbm, out_hbm, idx_v, val_v, shared):
        cid = lax.axis_index("core")
        sid = lax.axis_index("subcore")
        pltpu.sync_copy(fill_hbm.at[pl.ds(sid * init_per_tile, init_per_tile)],
                        shared.at[pl.ds(sid * init_per_tile, init_per_tile)])
        pltpu.sync_copy(pos_hbm.at[pl.ds(sid * rows_per_tile, rows_per_tile)], idx_v)
        pltpu.sync_copy(tok_hbm.at[pl.ds(sid * rows_per_tile, rows_per_tile)], val_v)
        plsc.subcore_barrier()

        @pl.loop(0, rows_per_tile)
        def _(j):
            pltpu.sync_copy(val_v.at[j], shared.at[idx_v.at[j]])

        plsc.subcore_barrier()
        off = (cid * n_sub + sid) * out_per_worker
        pltpu.sync_copy(shared.at[pl.ds(off, out_per_worker)], out_hbm.at[pl.ds(off, out_per_worker)])

    return invert_kernel(pos.reshape(rows, LANES), tokens, fill)


def _experts_kernel(te_ref, nv_ref, xs_ref, g_ref, w1_ref, b1_ref, w2_ref, b2_ref, ys_ref, w1_s, w2_s, *, d_ff):
    i = pl.program_id(0)
    prev = te_ref[jnp.maximum(i - 1, 0)]
    changed = (i == 0) | (te_ref[i] != prev)

    @pl.when(changed)
    def _():
        w1_s[...] = w1_ref[0].astype(BF16)
        w2_s[...] = w2_ref[0].astype(BF16)

    nvalid = nv_ref[i]

    @pl.when(nvalid > 0)
    def _():
        tm = xs_ref.shape[0]
        groups = [slice(r, r + EXPERT_ROWS) for r in range(0, tm, EXPERT_ROWS)]
        xg = []
        for rows in groups:
            row = rows.start + lax.broadcasted_iota(jnp.int32, (EXPERT_ROWS, 1), 0)
            h = jnp.where(row < nvalid, xs_ref[rows, :], 0.0)
            ms = jnp.mean(h * h, axis=-1, keepdims=True)
            xg.append((h * lax.rsqrt(ms + NORM_EPS) * g_ref[...]).astype(BF16))
        ug = [jnp.dot(x, w1_s[...], preferred_element_type=F32) + b1_ref[0] for x in xg]
        ag = []
        for u in ug:
            glu = jnp.minimum(u[:, 0:d_ff], SWIGLU_LIMIT)
            lin = jnp.clip(u[:, d_ff:2 * d_ff], -SWIGLU_LIMIT, SWIGLU_LIMIT)
            ag.append((glu * _sigmoid(SWIGLU_ALPHA * glu) * (lin + 1.0)).astype(BF16))
        for rows, act in zip(groups, ag):
            ys_ref[rows, :] = jnp.dot(act, w2_s[...], preferred_element_type=F32) + b2_ref[0]

    @pl.when(nvalid <= 0)
    def _():
        ys_ref[...] = jnp.zeros_like(ys_ref)


def _experts(xs, norm_g, tile_expert, tile_valid, w1, b1, w2, b2):
    n_slots, d = xs.shape
    n_exp, _, two_ff = w1.shape
    d_ff = two_ff // 2
    tm = SLOT_TILE
    n_tiles = n_slots // tm
    kern = functools.partial(_experts_kernel, d_ff=d_ff)
    grid_spec = pltpu.PrefetchScalarGridSpec(
        num_scalar_prefetch=2,
        grid=(n_tiles,),
        in_specs=[pl.BlockSpec((tm, d), lambda i, te, nv: (i, 0)),
                  pl.BlockSpec((1, d), lambda i, te, nv: (0, 0)),
                  pl.BlockSpec((1, d, two_ff), lambda i, te, nv: (te[i], 0, 0)),
                  pl.BlockSpec((1, 1, two_ff), lambda i, te, nv: (te[i], 0, 0)),
                  pl.BlockSpec((1, d_ff, d), lambda i, te, nv: (te[i], 0, 0)),
                  pl.BlockSpec((1, 1, d), lambda i, te, nv: (te[i], 0, 0))],
        out_specs=pl.BlockSpec((tm, d), lambda i, te, nv: (i, 0)),
        scratch_shapes=[pltpu.VMEM((d, two_ff), BF16), pltpu.VMEM((d_ff, d), BF16)],
    )
    return pl.pallas_call(
        kern,
        grid_spec=grid_spec,
        out_shape=jax.ShapeDtypeStruct((n_slots, d), F32),
        compiler_params=pltpu.CompilerParams(
            dimension_semantics=("arbitrary",), vmem_limit_bytes=VMEM_LIMIT_BYTES),
        name="experts",
    )(tile_expert, tile_valid, xs, norm_g.reshape(1, d), w1, b1.reshape(n_exp, 1, two_ff), w2,
      b2.reshape(n_exp, 1, d))


def _combine_kernel(h_ref, meta_ref, g_ref, *rest):
    y_refs, o_ref = rest[:TOP_K], rest[-1]
    acc = h_ref[...]
    for kk in range(TOP_K):
        acc = acc + meta_ref[:, TOP_K + kk:TOP_K + kk + 1] * y_refs[kk][...]
    ms = jnp.mean(acc * acc, axis=-1, keepdims=True)
    o_ref[...] = acc * lax.rsqrt(ms + NORM_EPS) * g_ref[...]


def _combine(h2, y4, meta, norm_g, out_prev, group, n_groups):
    t, d = h2.shape
    tt = min(MOVE_TOKENS, t)
    n_blk = t // tt
    prev_specs = [] if out_prev is None else [pl.BlockSpec(memory_space=pl.ANY)]
    prev_args = [] if out_prev is None else [out_prev]
    n_in = 3 + TOP_K
    y_specs = [pl.BlockSpec((tt, d), functools.partial(lambda i, kk: (kk * n_blk + i, 0), kk=kk))
               for kk in range(TOP_K)]
    return pl.pallas_call(
        _combine_kernel,
        grid=(n_blk,),
        in_specs=[pl.BlockSpec((tt, d), lambda i: (i, 0)),
                  pl.BlockSpec((tt, LANES), lambda i: (i, 0)),
                  pl.BlockSpec((1, d), lambda i: (0, 0))] + y_specs + prev_specs,
        out_specs=pl.BlockSpec((tt, d), lambda i: (group * n_blk + i, 0)),
        out_shape=jax.ShapeDtypeStruct((n_groups * t, d), F32),
        input_output_aliases={} if out_prev is None else {n_in: 0},
        compiler_params=pltpu.CompilerParams(
            dimension_semantics=("arbitrary",), vmem_limit_bytes=VMEM_LIMIT_BYTES),
        name="combine",
    )(h2, meta, norm_g.reshape(1, d), *([y4] * TOP_K), *prev_args)


def _moe(h2, norm_ffn_g, router_w, router_b, w1, b1, w2, b2, norm_final_g, out_prev, group, n_groups):
    t, d = h2.shape
    n_exp = router_w.shape[1]
    tm = SLOT_TILE
    meta, counts = _router(h2, norm_ffn_g, router_w, router_b)
    eidx = meta[:, 0:TOP_K].astype(jnp.int32)
    rank = meta[:, 2 * TOP_K:3 * TOP_K].astype(jnp.int32)
    cnt = counts[0, 0:n_exp].astype(jnp.int32)
    padded = (cnt + tm - 1) // tm * tm
    seg_end = jnp.cumsum(padded)
    seg_start = seg_end - padded
    pos = (seg_start[eidx] + rank).reshape(-1)
    n_tiles = -(-(t * TOP_K) // tm) + n_exp
    n_slots = n_tiles * tm
    tile_start = jnp.arange(n_tiles, dtype=jnp.int32) * tm
    tile_expert = jnp.minimum(jnp.sum(tile_start[:, None] >= seg_end[None, :], axis=1), n_exp - 1).astype(jnp.int32)
    tile_valid = jnp.clip(seg_start[tile_expert] + cnt[tile_expert] - tile_start, 0, tm).astype(jnp.int32)
    slot_token = _slot_tokens(pos, n_slots)
    xs = _row_gather(h2, slot_token)
    ys = _experts(xs, norm_ffn_g, tile_expert, tile_valid, w1, b1, w2, b2)
    y4 = _row_gather(ys, pos.reshape(t, TOP_K).T.reshape(-1))
    return _combine(h2, y4, meta, norm_final_g, out_prev, group, n_groups)


def kernel(x, norm_mix_g, w_in, conv_w, shift_mu, decay_w0, decay_w2, iclr_a0, iclr_a2, gate_g2, k_k, k_a,
           r_k, ln_x_g, ln_x_b, w_out, norm_ffn_g, router_w, router_b, exp_w1, exp_b1, exp_w2, exp_b2,
           norm_final_g):
    bsz, s, d = x.shape
    depth = w_in.shape[0]
    assert depth == 1, "final norm is fused into the last layer's combine kernel"
    n_groups = BATCH_GROUPS if bsz % BATCH_GROUPS == 0 else 1
    hs = _mixer(x, n_groups, norm_mix_g[0], w_in[0], conv_w[0], shift_mu[0], decay_w0[0], decay_w2[0],
                iclr_a0[0], iclr_a2[0], gate_g2[0], k_k[0], k_a[0], r_k[0], ln_x_g[0], ln_x_b[0], w_out[0])
    out = None
    for g, h in enumerate(hs):
        out = _moe(h.reshape(-1, d), norm_ffn_g[0], router_w[0], router_b[0], exp_w1[0], exp_b1[0],
                   exp_w2[0], exp_b2[0], norm_final_g, out, g, n_groups)
    return out.reshape(bsz, s, d)
```

```python
import functools

import jax
import jax.numpy as jnp
from jax import lax
from jax.experimental import pallas as pl
from jax.experimental.pallas import tpu as pltpu
from jax.experimental.pallas import tpu_sc as plsc

HEAD = 64
DECAY_LORA = 64
AAA_LORA = 64
GATE_LORA = 160
TOP_K = 4
SWIGLU_ALPHA = 1.702
SWIGLU_LIMIT = 7.0
NORM_EPS = 1e-5
GN_EPS = 64e-5
DECAY_SCALE = 0.6065306597126334

LANES = 128
SUBLANES = 8
VMEM_LIMIT_BYTES = 56 * 1024 * 1024

RWKV_CHUNK = 64
MIX_TOKENS = 256
ROUTE_TOKENS = 512
MOVE_TOKENS = 256
GATHER_WINDOW = 64
SLOT_TILE = 512
EXPERT_ROWS = 256
BATCH_GROUPS = 2

BF16 = jnp.bfloat16
F32 = jnp.float32


def _dot(a, b):
    return jnp.dot(a.astype(BF16), b.astype(BF16), preferred_element_type=F32)


def _dot_nt(a, b):
    return lax.dot_general(a.astype(BF16), b.astype(BF16), (((1,), (1,)), ((), ())),
                           preferred_element_type=F32)


def _dot_tn(a, b):
    return lax.dot_general(a.astype(BF16), b.astype(BF16), (((0,), (0,)), ((), ())),
                           preferred_element_type=F32)


def _sigmoid(x):
    return 0.5 * jnp.tanh(0.5 * x) + 0.5


def _pack_bf16_pairs(x):
    n = x.shape[1] // 2
    bits = pltpu.bitcast(x.astype(BF16).astype(F32), jnp.uint32)
    packed = (bits[:, n:] & jnp.uint32(0xFFFF0000)) | (bits[:, :n] >> 16)
    return pltpu.bitcast(packed, jnp.int32)


def _unpack_bf16_pairs(w):
    bits = pltpu.bitcast(w, jnp.uint32)
    return pltpu.bitcast(bits << 16, F32), pltpu.bitcast(bits & jnp.uint32(0xFFFF0000), F32)


def _shift_rows(x, n, carry):
    rolled = pltpu.roll(x, n, axis=0)
    row = lax.broadcasted_iota(jnp.int32, x.shape, 0)
    out = rolled
    for i in range(n):
        out = jnp.where(row == i, carry[SUBLANES - n + i:SUBLANES - n + i + 1, :], out)
    return out


def _mixer_kernel(x_ref, g_ref, wmain_ref, wlora_ref, wgate_ref, cw_ref, mu_rkv_ref, mu_lora_ref,
                  w0_ref, w2_ref, a0_ref, a2_ref, g2_ref, kk_ref, ka_ref, rk_ref, lng_ref, lnb_ref,
                  wout_ref, hsum_ref, gffn_ref, o_ref, hp_ref,
                  cu_s, cp_s, cl_s, ar_s, bk_s, v_s, rkb_s, y_s, gam_s, *state_s,
                  n_heads, d_model):
    tc = x_ref.shape[1]
    L = RWKV_CHUNK
    n_sub = tc // L
    D = d_model

    @pl.when(pl.program_id(1) == 0)
    def _():
        for st_ref in state_s:
            st_ref[...] = jnp.zeros_like(st_ref)
        cu_s[...] = jnp.zeros_like(cu_s)
        cp_s[...] = jnp.zeros_like(cp_s)
        cl_s[...] = jnp.zeros_like(cl_s)

    x = x_ref[0]
    ms = jnp.mean(x * x, axis=-1, keepdims=True)
    xn = (x * lax.rsqrt(ms + NORM_EPS) * g_ref[...]).astype(BF16)

    pc = jnp.dot(xn, wmain_ref[:, 0:3 * D], preferred_element_type=F32)
    u = pc[:, D:2 * D] * pc[:, 2 * D:3 * D]
    cu = cu_s[...]
    conv = (cw_ref[0:1, :] * _shift_rows(u, 2, cu) + cw_ref[1:2, :] * _shift_rows(u, 1, cu)
            + cw_ref[2:3, :] * u)
    y_conv = pc[:, 0:D] * conv
    cu_s[...] = u[tc - SUBLANES:tc, :]

    pr = jnp.dot(xn, wmain_ref[:, 3 * D:6 * D], preferred_element_type=F32)
    cp = cp_s[...]
    cp_s[...] = pr[tc - SUBLANES:tc, :]
    pr = pr + (_shift_rows(pr, 1, cp) - pr) * mu_rkv_ref[...]
    plo = jnp.dot(xn, wlora_ref[...], preferred_element_type=F32)
    cl = cl_s[...]
    cl_s[...] = plo[tc - SUBLANES:tc, :]
    plo = plo + (_shift_rows(plo, 1, cl) - plo) * mu_lora_ref[...]

    r = pr[:, 0:D]
    k = pr[:, D:2 * D]
    v = pr[:, 2 * D:3 * D]
    wd = plo[:, 0:LANES]
    ad = plo[:, LANES:2 * LANES]
    gd = plo[:, 2 * LANES:4 * LANES]

    lw = -DECAY_SCALE * _sigmoid(w0_ref[...] + _dot(jnp.tanh(wd), w2_ref[...]))
    a = _sigmoid(a0_ref[...] + _dot(ad, a2_ref[...]))
    g = _dot(_sigmoid(gd), g2_ref[...])

    row = lax.broadcasted_iota(jnp.int32, (tc, tc), 0)
    col = lax.broadcasted_iota(jnp.int32, (tc, tc), 1)
    tri = jnp.where((row >= col) & ((row // L) == (col // L)), 1.0, 0.0).astype(BF16)
    lw_hi = lw.astype(BF16)
    lw_lo = (lw - lw_hi.astype(F32)).astype(BF16)
    cum = (jnp.dot(tri, lw_hi, preferred_element_type=F32)
           + jnp.dot(tri, lw_lo, preferred_element_type=F32))
    e_inv = jnp.exp(-cum)

    kkraw = k * kk_ref[...]
    ss = _dot(kkraw * kkraw, hsum_ref[...])
    kkn = kkraw * jnp.minimum(lax.rsqrt(ss), 1e12)

    k2 = k * (1.0 + (a - 1.0) * ka_ref[...])
    a_t = kkn * jnp.exp(cum - lw)
    r_t = r * jnp.exp(cum)
    b_t = kkn * a * e_inv
    k_t = k2 * e_inv
    for c in range(n_sub):
        rows = slice(c * L, (c + 1) * L)
        ar_s[c, 0:L, :] = a_t[rows]
        ar_s[c, L:2 * L, :] = r_t[rows]
        bk_s[c, 0:L, :] = b_t[rows]
        bk_s[c, L:2 * L, :] = k_t[rows]
        gam_s[c] = jnp.exp(cum[(c + 1) * L - 1:(c + 1) * L, :])
    v_s[...] = v
    rkb_s[...] = r * k2 * rk_ref[...]

    ri = lax.broadcasted_iota(jnp.int32, (L, L), 0)
    ci = lax.broadcasted_iota(jnp.int32, (L, L), 1)
    strict = ri > ci
    eye = jnp.where(ri == ci, 1.0, 0.0)
    ri2 = lax.broadcasted_iota(jnp.int32, (L, 2 * L), 0)
    ci2 = lax.broadcasted_iota(jnp.int32, (L, 2 * L), 1)
    incl2 = ri2 >= (ci2 % L)
    n_double = L.bit_length() - 2

    def sub_chunk(c, carry):
        r0 = pl.multiple_of(c * L, L)
        heads = range(n_heads)
        hsl = [slice(h * HEAD, (h + 1) * HEAD) for h in heads]
        ar = [ar_s[c, :, hsl[h]].astype(BF16) for h in heads]
        bk = [bk_s[c, :, hsl[h]].astype(BF16) for h in heads]
        vh = [v_s[pl.ds(r0, L), hsl[h]] for h in heads]
        st = [state_s[h][...] for h in heads]
        gm = [_dot_nt(ar[h], bk[h]) for h in heads]
        nm = [jnp.where(strict, gm[h][0:L, 0:L], 0.0) for h in heads]
        mak = [jnp.where(strict, gm[h][0:L, L:2 * L], 0.0) for h in heads]
        q = [jnp.where(incl2, gm[h][L:2 * L, :], 0.0) for h in heads]
        xinv = [eye - nm[h] for h in heads]
        p = nm
        for _ in range(n_double):
            p = [_dot(p[h], p[h]) for h in heads]
            xinv = [xinv[h] + _dot(xinv[h], p[h]) for h in heads]
        ars = [_dot_nt(ar[h], st[h]) for h in heads]
        mv = [_dot(mak[h], vh[h]) for h in heads]
        uu = [_dot(xinv[h], ars[h][0:L] + mv[h]) for h in heads]
        uv = [jnp.concatenate([-uu[h], vh[h]], axis=0).astype(BF16) for h in heads]
        yh = [ars[h][L:2 * L] + _dot(q[h], uv[h]) for h in heads]
        for h in heads:
            state_s[h][...] = (st[h] + _dot_tn(uv[h], bk[h])) * gam_s[c, :, hsl[h]]
        for h in heads:
            mean = jnp.mean(yh[h], axis=-1, keepdims=True)
            yc = yh[h] - mean
            var = jnp.mean(yc * yc, axis=-1, keepdims=True)
            bonus = jnp.sum(rkb_s[pl.ds(r0, L), hsl[h]], axis=-1, keepdims=True) * vh[h]
            y_s[pl.ds(r0, L), hsl[h]] = (yc * lax.rsqrt(var + GN_EPS) * lng_ref[:, hsl[h]]
                                         + lnb_ref[:, hsl[h]] + bonus)
        return carry

    lax.fori_loop(0, n_sub, sub_chunk, 0)

    y_rwkv = y_s[...] * g
    gates = _sigmoid(jnp.dot(xn, wgate_ref[...], preferred_element_type=F32))
    mix = gates[:, 0:D] * y_conv + gates[:, D:2 * D] * y_rwkv
    h = x + jnp.dot(mix.astype(BF16), wout_ref[...], preferred_element_type=F32)
    o_ref[0] = h
    hn = h * lax.rsqrt(jnp.mean(h * h, axis=-1, keepdims=True) + NORM_EPS) * gffn_ref[...]
    hp_ref[0] = _pack_bf16_pairs(hn)


def _const_spec(shape):
    nd = len(shape)
    return pl.BlockSpec(shape, lambda *_: (0,) * nd, pipeline_mode=pl.Buffered(1))


def _mixer(x, n_groups, norm_g, norm_ffn_g, w_in, conv_w, shift_mu, w0, w2, a0, a2, g2, k_k, k_a, r_k, ln_g, ln_b, w_out):
    bsz, s, d = x.shape
    gb = bsz // n_groups
    n_heads = d // HEAD
    tc = min(MIX_TOKENS, s)
    L = RWKV_CHUNK
    n_sub = tc // L
    lora0 = 6 * d
    w_main = w_in[:, 0:6 * d].astype(BF16)
    pad = lambda t, n: jnp.pad(t, ((0, 0), (0, n - t.shape[1])))
    lora_cols = (DECAY_LORA, AAA_LORA, GATE_LORA)
    lora_pads = (LANES, LANES, 2 * LANES)
    pieces_w, pieces_mu, off = [], [], lora0
    for n, p in zip(lora_cols, lora_pads):
        pieces_w.append(pad(w_in[:, off:off + n], p))
        pieces_mu.append(pad(shift_mu[None, off - 3 * d:off - 3 * d + n], p))
        off += n
    w_lora = jnp.concatenate(pieces_w, axis=1).astype(BF16)
    mu_lora = jnp.concatenate(pieces_mu, axis=1)
    w_gate = w_in[:, off:off + 2 * d].astype(BF16)
    mu_rkv = shift_mu[None, 0:3 * d]
    padr = lambda t, n: jnp.pad(t, ((0, n - t.shape[0]), (0, 0)))
    w2p = padr(w2, LANES).astype(BF16)
    a2p = padr(a2, LANES).astype(BF16)
    g2p = padr(g2, 2 * LANES).astype(BF16)
    row = lambda t: t.reshape(1, -1)
    head_of = jnp.arange(d, dtype=jnp.int32) // HEAD
    head_sum = (head_of[:, None] == head_of[None, :]).astype(BF16)
    consts = [row(norm_g), w_main, w_lora, w_gate, conv_w, mu_rkv, mu_lora, row(w0), w2p, row(a0), a2p,
              g2p, row(k_k), row(k_a), row(r_k), row(ln_g), row(ln_b), w_out.astype(BF16), head_sum,
              row(norm_ffn_g)]
    kern = functools.partial(_mixer_kernel, n_heads=n_heads, d_model=d)
    call = lambda first: pl.pallas_call(
        kern,
        grid=(gb, s // tc),
        in_specs=[pl.BlockSpec((1, tc, d), lambda b, c: (b + first, c, 0))]
        + [_const_spec(t.shape) for t in consts],
        out_specs=[pl.BlockSpec((1, tc, d), lambda b, c: (b, c, 0)),
                   pl.BlockSpec((1, tc, d // 2), lambda b, c: (b, c, 0))],
        out_shape=[jax.ShapeDtypeStruct((gb, s, d), F32), jax.ShapeDtypeStruct((gb, s, d // 2), jnp.int32)],
        scratch_shapes=[
            pltpu.VMEM((SUBLANES, d), F32),
            pltpu.VMEM((SUBLANES, 3 * d), F32),
            pltpu.VMEM((SUBLANES, 4 * LANES), F32),
            pltpu.VMEM((n_sub, 2 * L, d), F32),
            pltpu.VMEM((n_sub, 2 * L, d), F32),
            pltpu.VMEM((tc, d), F32),
            pltpu.VMEM((tc, d), F32),
            pltpu.VMEM((tc, d), F32),
            pltpu.VMEM((n_sub, 1, d), F32),
        ] + [pltpu.VMEM((HEAD, HEAD), F32)] * n_heads,
        compiler_params=pltpu.CompilerParams(
            dimension_semantics=("arbitrary", "arbitrary"), vmem_limit_bytes=VMEM_LIMIT_BYTES),
        name="mixer",
    )(x, *consts)
    return [call(g * gb) for g in range(n_groups)]


def _router_kernel(h_ref, g_ref, rw_ref, rb_ref, meta_ref, cnt_ref, run_s, *, n_experts):
    tt = h_ref.shape[0]

    @pl.when(pl.program_id(0) == 0)
    def _():
        run_s[...] = jnp.zeros_like(run_s)

    h = h_ref[...]
    ms = jnp.mean(h * h, axis=-1, keepdims=True)
    hn = h * lax.rsqrt(ms + NORM_EPS) * g_ref[...]
    logits = jnp.dot(hn, rw_ref[...], preferred_element_type=F32,
                     precision=lax.Precision.HIGHEST) + rb_ref[...]
    lane = lax.broadcasted_iota(jnp.int32, (tt, LANES), 1)
    neg = jnp.float32(-jnp.inf)
    work = jnp.where(lane < n_experts, logits, neg)
    vals, idxs = [], []
    for _ in range(TOP_K):
        m = jnp.max(work, axis=-1, keepdims=True)
        i = jnp.min(jnp.where(work == m, lane, LANES), axis=-1, keepdims=True)
        vals.append(m)
        idxs.append(i)
        work = jnp.where(lane == i, neg, work)
    ex = [jnp.exp(vv - vals[0]) for vv in vals]
    den = ex[0] + ex[1] + ex[2] + ex[3]
    gates = [e / den for e in ex]

    onehot = jnp.zeros((tt, LANES), jnp.bool_)
    for kk in range(TOP_K):
        onehot = onehot | (lane == (idxs[kk] + kk * n_experts))
    oh = jnp.where(onehot, 1.0, 0.0)
    ri = lax.broadcasted_iota(jnp.int32, (tt, tt), 0)
    ci = lax.broadcasted_iota(jnp.int32, (tt, tt), 1)
    tri = jnp.where(ri > ci, 1.0, 0.0).astype(BF16)
    cnt = jnp.dot(tri, oh.astype(BF16), preferred_element_type=F32)
    tot = jnp.broadcast_to(jnp.sum(oh, axis=0, keepdims=True), (SUBLANES, LANES))
    lane8 = lax.broadcasted_iota(jnp.int32, (SUBLANES, LANES), 1)
    pk = jnp.zeros_like(tot)
    te = tot
    for j in range(1, TOP_K):
        rolled = pltpu.roll(tot, j * n_experts, axis=1)
        pk = pk + jnp.where(lane8 >= j * n_experts, rolled, 0.0)
        te = te + rolled
    before = cnt + (run_s[...] + pk)[0:1, :]
    ranks = [jnp.sum(jnp.where(onehot & (lane // n_experts == kk), before, 0.0), axis=-1, keepdims=True)
             for kk in range(TOP_K)]
    run_s[...] = run_s[...] + te
    cnt_ref[...] = run_s[...]

    meta = jnp.zeros((tt, LANES), F32)
    for kk in range(TOP_K):
        meta = jnp.where(lane == kk, idxs[kk].astype(F32), meta)
        meta = jnp.where(lane == TOP_K + kk, gates[kk], meta)
        meta = jnp.where(lane == 2 * TOP_K + kk, ranks[kk], meta)
    meta_ref[...] = meta


def _router(h2, norm_g, router_w, router_b):
    t, d = h2.shape
    n_experts = router_w.shape[1]
    tt = min(ROUTE_TOKENS, t)
    rw = jnp.pad(router_w, ((0, 0), (0, LANES - n_experts)))
    rb = jnp.pad(router_b, (0, LANES - n_experts)).reshape(1, LANES)
    kern = functools.partial(_router_kernel, n_experts=n_experts)
    return pl.pallas_call(
        kern,
        grid=(t // tt,),
        in_specs=[pl.BlockSpec((tt, d), lambda i: (i, 0)),
                  pl.BlockSpec((1, d), lambda i: (0, 0)),
                  pl.BlockSpec((d, LANES), lambda i: (0, 0)),
                  pl.BlockSpec((1, LANES), lambda i: (0, 0))],
        out_specs=[pl.BlockSpec((tt, LANES), lambda i: (i, 0)),
                   pl.BlockSpec((SUBLANES, LANES), lambda i: (0, 0))],
        out_shape=[jax.ShapeDtypeStruct((t, LANES), F32),
                   jax.ShapeDtypeStruct((SUBLANES, LANES), F32)],
        scratch_shapes=[pltpu.VMEM((SUBLANES, LANES), F32)],
        compiler_params=pltpu.CompilerParams(
            dimension_semantics=("arbitrary",), vmem_limit_bytes=VMEM_LIMIT_BYTES),
        name="router",
    )(h2, norm_g.reshape(1, d), rw, rb)


def _row_gather(table, idx):
    n_idx = idx.shape[0]
    d = table.shape[1]
    w = GATHER_WINDOW
    sc = plsc.get_sparse_core_info()
    n_workers = sc.num_cores * sc.num_subcores
    per_worker = n_idx // n_workers
    n_win = per_worker // w
    assert per_worker * n_workers == n_idx and n_win * w == per_worker and n_win % 2 == 0
    mesh = plsc.VectorSubcoreMesh(core_axis_name="core", subcore_axis_name="subcore")

    @functools.partial(
        pl.kernel, out_type=jax.ShapeDtypeStruct((n_idx, d), table.dtype), mesh=mesh,
        scratch_types=[pltpu.VMEM((per_worker,), jnp.int32),
                       pltpu.VMEM((w, d), table.dtype), pltpu.VMEM((w, d), table.dtype),
                       pltpu.SemaphoreType.DMA, pltpu.SemaphoreType.DMA,
                       pltpu.SemaphoreType.DMA, pltpu.SemaphoreType.DMA])
    def gather_kernel(table_hbm, idx_hbm, out_hbm, idx_v, buf_a, buf_b, gsem_a, gsem_b, psem_a, psem_b):
        worker = lax.axis_index("subcore") * sc.num_cores + lax.axis_index("core")
        base = worker * per_worker
        pltpu.sync_copy(idx_hbm.at[pl.ds(base, per_worker)], idx_v)

        def gather(j, buf, sem):
            return pltpu.make_async_copy(table_hbm.at[idx_v.at[pl.ds(j * w, w)]], buf, sem)

        def put(j, buf, sem):
            return pltpu.make_async_copy(buf, out_hbm.at[pl.ds(base + j * w, w)], sem)

        gather(0, buf_a, gsem_a).start()

        @pl.loop(0, n_win, step=2)
        def _(j):
            gather(j, buf_a, gsem_a).wait()

            @pl.when(j > 0)
            def _():
                put(j - 1, buf_b, psem_b).wait()

            gather(j + 1, buf_b, gsem_b).start()
            put(j, buf_a, psem_a).start()
            gather(j + 1, buf_b, gsem_b).wait()
            put(j, buf_a, psem_a).wait()

            @pl.when(j + 2 < n_win)
            def _():
                gather(j + 2, buf_a, gsem_a).start()

            put(j + 1, buf_b, psem_b).start()

        put(n_win - 1, buf_b, psem_b).wait()

    return gather_kernel(table, idx)


def _slot_tokens(pos, n_slots):
    n = pos.shape[0]
    sc = plsc.get_sparse_core_info()
    n_cores, n_sub = sc.num_cores, sc.num_subcores
    rows = n // LANES
    rows_per_tile = rows // n_sub
    out_per_worker = n_slots // (n_cores * n_sub)
    init_per_tile = n_slots // n_sub
    assert rows_per_tile * n_sub * LANES == n and out_per_worker * n_cores * n_sub == n_slots
    mesh = plsc.VectorSubcoreMesh(core_axis_name="core", subcore_axis_name="subcore")
    tokens = (jnp.arange(n, dtype=jnp.int32) // TOP_K).reshape(rows, LANES)
    fill = jnp.arange(n_slots, dtype=jnp.int32) % (n // TOP_K)

    @functools.partial(
        pl.kernel, out_type=jax.ShapeDtypeStruct((n_slots,), jnp.int32), mesh=mesh,
        scratch_types=[pltpu.VMEM((rows_per_tile, LANES), jnp.int32),
                       pltpu.VMEM((rows_per_tile, LANES), jnp.int32),
                       pltpu.VMEM_SHARED((n_slots,), jnp.int32)])
    def invert_kernel(pos_hbm, tok_hbm, fill_hbm, out_hbm, idx_v, val_v, shared):
        cid = lax.axis_index("core")
        sid = lax.axis_index("subcore")
        pltpu.sync_copy(fill_hbm.at[pl.ds(sid * init_per_tile, init_per_tile)],
                        shared.at[pl.ds(sid * init_per_tile, init_per_tile)])
        pltpu.sync_copy(pos_hbm.at[pl.ds(sid * rows_per_tile, rows_per_tile)], idx_v)
        pltpu.sync_copy(tok_hbm.at[pl.ds(sid * rows_per_tile, rows_per_tile)], val_v)
        plsc.subcore_barrier()

        @pl.loop(0, rows_per_tile)
        def _(j):
            pltpu.sync_copy(val_v.at[j], shared.at[idx_v.at[j]])

        plsc.subcore_barrier()
        off = (cid * n_sub + sid) * out_per_worker
        pltpu.sync_copy(shared.at[pl.ds(off, out_per_worker)], out_hbm.at[pl.ds(off, out_per_worker)])

    return invert_kernel(pos.reshape(rows, LANES), tokens, fill)


def _experts_kernel(te_ref, nv_ref, xs_ref, w1_ref, b1_ref, w2_ref, b2_ref, ys_ref, w1_s, w2_s, *, d_ff):
    i = pl.program_id(0)
    prev = te_ref[jnp.maximum(i - 1, 0)]
    changed = (i == 0) | (te_ref[i] != prev)

    @pl.when(changed)
    def _():
        w1_s[...] = w1_ref[0].astype(BF16)
        w2_s[...] = w2_ref[0].astype(BF16)

    nvalid = nv_ref[i]

    @pl.when(nvalid > 0)
    def _():
        tm = xs_ref.shape[0]
        groups = [slice(r, r + EXPERT_ROWS) for r in range(0, tm, EXPERT_ROWS)]
        half = xs_ref.shape[1]
        xg = []
        for rows in groups:
            row = rows.start + lax.broadcasted_iota(jnp.int32, (EXPERT_ROWS, 1), 0)
            lo, hi = _unpack_bf16_pairs(jnp.where(row < nvalid, xs_ref[rows, :], 0))
            xg.append((lo.astype(BF16), hi.astype(BF16)))
        ug = [jnp.dot(lo, w1_s[0:half, :], preferred_element_type=F32)
              + jnp.dot(hi, w1_s[half:2 * half, :], preferred_element_type=F32) + b1_ref[0] for lo, hi in xg]
        ag = []
        for u in ug:
            glu = jnp.minimum(u[:, 0:d_ff], SWIGLU_LIMIT)
            lin = jnp.clip(u[:, d_ff:2 * d_ff], -SWIGLU_LIMIT, SWIGLU_LIMIT)
            ag.append((glu * _sigmoid(SWIGLU_ALPHA * glu) * (lin + 1.0)).astype(BF16))
        for rows, act in zip(groups, ag):
            ys_ref[rows, :] = _pack_bf16_pairs(
                jnp.dot(act, w2_s[...], preferred_element_type=F32) + b2_ref[0])

    @pl.when(nvalid <= 0)
    def _():
        ys_ref[...] = jnp.zeros_like(ys_ref)


def _experts(xs, tile_expert, tile_valid, w1, b1, w2, b2):
    n_slots = xs.shape[0]
    n_exp, d, two_ff = w1.shape
    d_ff = two_ff // 2
    tm = SLOT_TILE
    n_tiles = n_slots // tm
    kern = functools.partial(_experts_kernel, d_ff=d_ff)
    grid_spec = pltpu.PrefetchScalarGridSpec(
        num_scalar_prefetch=2,
        grid=(n_tiles,),
        in_specs=[pl.BlockSpec((tm, d // 2), lambda i, te, nv: (i, 0)),
                  pl.BlockSpec((1, d, two_ff), lambda i, te, nv: (te[i], 0, 0)),
                  pl.BlockSpec((1, 1, two_ff), lambda i, te, nv: (te[i], 0, 0)),
                  pl.BlockSpec((1, d_ff, d), lambda i, te, nv: (te[i], 0, 0)),
                  pl.BlockSpec((1, 1, d), lambda i, te, nv: (te[i], 0, 0))],
        out_specs=pl.BlockSpec((tm, d // 2), lambda i, te, nv: (i, 0)),
        scratch_shapes=[pltpu.VMEM((d, two_ff), BF16), pltpu.VMEM((d_ff, d), BF16)],
    )
    return pl.pallas_call(
        kern,
        grid_spec=grid_spec,
        out_shape=jax.ShapeDtypeStruct((n_slots, d // 2), jnp.int32),
        compiler_params=pltpu.CompilerParams(
            dimension_semantics=("arbitrary",), vmem_limit_bytes=VMEM_LIMIT_BYTES),
        name="experts",
    )(tile_expert, tile_valid, xs, w1, b1.reshape(n_exp, 1, two_ff), w2, b2.reshape(n_exp, 1, d))


def _combine_kernel(h_ref, meta_ref, g_ref, *rest):
    y_refs, o_ref = rest[:TOP_K], rest[-1]
    d = h_ref.shape[1]
    half = d // 2
    acc_lo = h_ref[:, 0:half]
    acc_hi = h_ref[:, half:d]
    for kk in range(TOP_K):
        gate = meta_ref[:, TOP_K + kk:TOP_K + kk + 1]
        lo, hi = _unpack_bf16_pairs(y_refs[kk][...])
        acc_lo = acc_lo + gate * lo
        acc_hi = acc_hi + gate * hi
    ms = (jnp.sum(acc_lo * acc_lo, axis=-1, keepdims=True)
          + jnp.sum(acc_hi * acc_hi, axis=-1, keepdims=True)) * (1.0 / d)
    scale = lax.rsqrt(ms + NORM_EPS)
    o_ref[:, 0:half] = acc_lo * scale * g_ref[:, 0:half]
    o_ref[:, half:d] = acc_hi * scale * g_ref[:, half:d]


def _combine(h2, y4, meta, norm_g, out_prev, group, n_groups):
    t, d = h2.shape
    tt = min(MOVE_TOKENS, t)
    n_blk = t // tt
    prev_specs = [] if out_prev is None else [pl.BlockSpec(memory_space=pl.ANY)]
    prev_args = [] if out_prev is None else [out_prev]
    n_in = 3 + TOP_K
    y_specs = [pl.BlockSpec((tt, d // 2), functools.partial(lambda i, kk: (kk * n_blk + i, 0), kk=kk))
               for kk in range(TOP_K)]
    return pl.pallas_call(
        _combine_kernel,
        grid=(n_blk,),
        in_specs=[pl.BlockSpec((tt, d), lambda i: (i, 0)),
                  pl.BlockSpec((tt, LANES), lambda i: (i, 0)),
                  pl.BlockSpec((1, d), lambda i: (0, 0))] + y_specs + prev_specs,
        out_specs=pl.BlockSpec((tt, d), lambda i: (group * n_blk + i, 0)),
        out_shape=jax.ShapeDtypeStruct((n_groups * t, d), F32),
        input_output_aliases={} if out_prev is None else {n_in: 0},
        compiler_params=pltpu.CompilerParams(
            dimension_semantics=("arbitrary",), vmem_limit_bytes=VMEM_LIMIT_BYTES),
        name="combine",
    )(h2, meta, norm_g.reshape(1, d), *([y4] * TOP_K), *prev_args)


def _moe(h2, hp2, norm_ffn_g, router_w, router_b, w1, b1, w2, b2, norm_final_g, out_prev, group, n_groups):
    t, d = h2.shape
    n_exp = router_w.shape[1]
    tm = SLOT_TILE
    meta, counts = _router(h2, norm_ffn_g, router_w, router_b)
    eidx = meta[:, 0:TOP_K].astype(jnp.int32)
    rank = meta[:, 2 * TOP_K:3 * TOP_K].astype(jnp.int32)
    cnt = counts[0, 0:n_exp].astype(jnp.int32)
    padded = (cnt + tm - 1) // tm * tm
    seg_end = jnp.cumsum(padded)
    seg_start = seg_end - padded
    pos = (seg_start[eidx] + rank).reshape(-1)
    n_tiles = -(-(t * TOP_K) // tm) + n_exp
    n_slots = n_tiles * tm
    tile_start = jnp.arange(n_tiles, dtype=jnp.int32) * tm
    tile_expert = jnp.minimum(jnp.sum(tile_start[:, None] >= seg_end[None, :], axis=1), n_exp - 1).astype(jnp.int32)
    tile_valid = jnp.clip(seg_start[tile_expert] + cnt[tile_expert] - tile_start, 0, tm).astype(jnp.int32)
    slot_token = _slot_tokens(pos, n_slots)
    xs = _row_gather(hp2, slot_token)
    ys = _experts(xs, tile_expert, tile_valid, w1, b1, w2, b2)
    y4 = _row_gather(ys, pos.reshape(t, TOP_K).T.reshape(-1))
    return _combine(h2, y4, meta, norm_final_g, out_prev, group, n_groups)


def kernel(x, norm_mix_g, w_in, conv_w, shift_mu, decay_w0, decay_w2, iclr_a0, iclr_a2, gate_g2, k_k, k_a,
           r_k, ln_x_g, ln_x_b, w_out, norm_ffn_g, router_w, router_b, exp_w1, exp_b1, exp_w2, exp_b2,
           norm_final_g):
    bsz, s, d = x.shape
    depth = w_in.shape[0]
    assert depth == 1, "final norm is fused into the last layer's combine kernel"
    n_groups = BATCH_GROUPS if bsz % BATCH_GROUPS == 0 else 1
    hs = _mixer(x, n_groups, norm_mix_g[0], norm_ffn_g[0], w_in[0], conv_w[0], shift_mu[0], decay_w0[0], decay_w2[0],
                iclr_a0[0], iclr_a2[0], gate_g2[0], k_k[0], k_a[0], r_k[0], ln_x_g[0], ln_x_b[0], w_out[0])
    out = None
    for g, (h, hp) in enumerate(hs):
        out = _moe(h.reshape(-1, d), hp.reshape(-1, d // 2), norm_ffn_g[0], router_w[0], router_b[0], exp_w1[0], exp_b1[0],
                   exp_w2[0], exp_b2[0], norm_final_g, out, g, n_groups)
    return out.reshape(bsz, s, d)
```

```python
import functools

import jax
import jax.numpy as jnp
from jax import lax
from jax.experimental import pallas as pl
from jax.experimental.pallas import tpu as pltpu
from jax.experimental.pallas import tpu_sc as plsc

HEAD = 64
DECAY_LORA = 64
AAA_LORA = 64
GATE_LORA = 160
TOP_K = 4
SWIGLU_ALPHA = 1.702
SWIGLU_LIMIT = 7.0
NORM_EPS = 1e-5
GN_EPS = 64e-5
DECAY_SCALE = 0.6065306597126334

LANES = 128
SUBLANES = 8
VMEM_LIMIT_BYTES = 56 * 1024 * 1024

RWKV_CHUNK = 64
MIX_TOKENS = 256
ROUTE_TOKENS = 512
MOVE_TOKENS = 256
GATHER_WINDOW = 64
SLOT_TILE = 512
EXPERT_ROWS = 256
BATCH_GROUPS = 2

BF16 = jnp.bfloat16
F32 = jnp.float32


def _dot(a, b):
    return jnp.dot(a.astype(BF16), b.astype(BF16), preferred_element_type=F32)


def _dot_nt(a, b):
    return lax.dot_general(a.astype(BF16), b.astype(BF16), (((1,), (1,)), ((), ())),
                           preferred_element_type=F32)


def _dot_tn(a, b):
    return lax.dot_general(a.astype(BF16), b.astype(BF16), (((0,), (0,)), ((), ())),
                           preferred_element_type=F32)


def _sigmoid(x):
    return 0.5 * jnp.tanh(0.5 * x) + 0.5


def _pack_bf16_pairs(x):
    n = x.shape[1] // 2
    bits = pltpu.bitcast(x.astype(BF16).astype(F32), jnp.uint32)
    packed = (bits[:, n:] & jnp.uint32(0xFFFF0000)) | (bits[:, :n] >> 16)
    return pltpu.bitcast(packed, jnp.int32)


def _unpack_bf16_pairs(w):
    bits = pltpu.bitcast(w, jnp.uint32)
    return pltpu.bitcast(bits << 16, F32), pltpu.bitcast(bits & jnp.uint32(0xFFFF0000), F32)


def _shift_rows(x, n, carry):
    rolled = pltpu.roll(x, n, axis=0)
    row = lax.broadcasted_iota(jnp.int32, x.shape, 0)
    out = rolled
    for i in range(n):
        out = jnp.where(row == i, carry[SUBLANES - n + i:SUBLANES - n + i + 1, :], out)
    return out


def _mixer_kernel(x_ref, g_ref, wmain_ref, wlora_ref, wgate_ref, cw_ref, mu_rkv_ref, mu_lora_ref,
                  w0_ref, w2_ref, a0_ref, a2_ref, g2_ref, kk_ref, ka_ref, rk_ref, lng_ref, lnb_ref,
                  wout_ref, hsel_ref, hselt_ref, gffn_ref, o_ref, hp_ref,
                  cu_s, cp_s, cl_s, ar_s, bk_s, v_s, y_s, gam_s, *state_s,
                  n_heads, d_model):
    tc = x_ref.shape[1]
    L = RWKV_CHUNK
    n_sub = tc // L
    D = d_model

    @pl.when(pl.program_id(1) == 0)
    def _():
        for st_ref in state_s:
            st_ref[...] = jnp.zeros_like(st_ref)
        cu_s[...] = jnp.zeros_like(cu_s)
        cp_s[...] = jnp.zeros_like(cp_s)
        cl_s[...] = jnp.zeros_like(cl_s)

    def split(t):
        hi = t.astype(BF16)
        return hi, (t - hi.astype(F32)).astype(BF16)

    def head_sums(parts):
        per_head = sum(jnp.dot(p, hsel_ref[...], preferred_element_type=F32) for p in parts)
        return sum(jnp.dot(p, hselt_ref[...], preferred_element_type=F32) for p in split(per_head))

    x = x_ref[0]
    ms = jnp.mean(x * x, axis=-1, keepdims=True)
    xn = (x * lax.rsqrt(ms + NORM_EPS) * g_ref[...]).astype(BF16)

    pc = jnp.dot(xn, wmain_ref[:, 0:3 * D], preferred_element_type=F32)
    u = pc[:, D:2 * D] * pc[:, 2 * D:3 * D]
    cu = cu_s[...]
    conv = (cw_ref[0:1, :] * _shift_rows(u, 2, cu) + cw_ref[1:2, :] * _shift_rows(u, 1, cu)
            + cw_ref[2:3, :] * u)
    y_conv = pc[:, 0:D] * conv
    cu_s[...] = u[tc - SUBLANES:tc, :]

    pr = jnp.dot(xn, wmain_ref[:, 3 * D:6 * D], preferred_element_type=F32)
    cp = cp_s[...]
    cp_s[...] = pr[tc - SUBLANES:tc, :]
    pr = pr + (_shift_rows(pr, 1, cp) - pr) * mu_rkv_ref[...]
    plo = jnp.dot(xn, wlora_ref[...], preferred_element_type=F32)
    cl = cl_s[...]
    cl_s[...] = plo[tc - SUBLANES:tc, :]
    plo = plo + (_shift_rows(plo, 1, cl) - plo) * mu_lora_ref[...]

    r = pr[:, 0:D]
    k = pr[:, D:2 * D]
    v = pr[:, 2 * D:3 * D]
    wd = plo[:, 0:LANES]
    ad = plo[:, LANES:2 * LANES]
    gd = plo[:, 2 * LANES:4 * LANES]

    lw = -DECAY_SCALE * _sigmoid(w0_ref[...] + _dot(jnp.tanh(wd), w2_ref[...]))
    a = _sigmoid(a0_ref[...] + _dot(ad, a2_ref[...]))
    g = _dot(_sigmoid(gd), g2_ref[...])

    row = lax.broadcasted_iota(jnp.int32, (tc, tc), 0)
    col = lax.broadcasted_iota(jnp.int32, (tc, tc), 1)
    tri = jnp.where((row >= col) & ((row // L) == (col // L)), 1.0, 0.0).astype(BF16)
    cum = sum(jnp.dot(tri, p, preferred_element_type=F32) for p in split(lw))
    e_inv = jnp.exp(-cum)

    kkraw = k * kk_ref[...]
    ss = head_sums([(kkraw * kkraw).astype(BF16)])
    kkn = kkraw * jnp.minimum(lax.rsqrt(ss), 1e12)

    k2 = k * (1.0 + (a - 1.0) * ka_ref[...])
    a_t = kkn * jnp.exp(cum - lw)
    r_t = r * jnp.exp(cum)
    b_t = kkn * a * e_inv
    k_t = k2 * e_inv
    for c in range(n_sub):
        rows = slice(c * L, (c + 1) * L)
        ar_s[c, 0:L, :] = a_t[rows]
        ar_s[c, L:2 * L, :] = r_t[rows]
        bk_s[c, 0:L, :] = b_t[rows]
        bk_s[c, L:2 * L, :] = k_t[rows]
        gam_s[c] = jnp.exp(cum[(c + 1) * L - 1:(c + 1) * L, :])
    v_s[...] = v
    bonus = head_sums([(r * k2 * rk_ref[...]).astype(BF16)]) * v

    ri = lax.broadcasted_iota(jnp.int32, (L, L), 0)
    ci = lax.broadcasted_iota(jnp.int32, (L, L), 1)
    strict = ri > ci
    eye = jnp.where(ri == ci, 1.0, 0.0)
    ri2 = lax.broadcasted_iota(jnp.int32, (L, 2 * L), 0)
    ci2 = lax.broadcasted_iota(jnp.int32, (L, 2 * L), 1)
    incl2 = ri2 >= (ci2 % L)
    n_double = L.bit_length() - 2

    def sub_chunk(c, carry):
        r0 = pl.multiple_of(c * L, L)
        heads = range(n_heads)
        hsl = [slice(h * HEAD, (h + 1) * HEAD) for h in heads]
        ar = [ar_s[c, :, hsl[h]].astype(BF16) for h in heads]
        bk = [bk_s[c, :, hsl[h]].astype(BF16) for h in heads]
        vh = [v_s[pl.ds(r0, L), hsl[h]] for h in heads]
        st = [state_s[h][...] for h in heads]
        gm = [_dot_nt(ar[h], bk[h]) for h in heads]
        nm = [jnp.where(strict, gm[h][0:L, 0:L], 0.0) for h in heads]
        mak = [jnp.where(strict, gm[h][0:L, L:2 * L], 0.0) for h in heads]
        q = [jnp.where(incl2, gm[h][L:2 * L, :], 0.0) for h in heads]
        xinv = [eye - nm[h] for h in heads]
        p = nm
        for _ in range(n_double):
            p = [_dot(p[h], p[h]) for h in heads]
            xinv = [xinv[h] + _dot(xinv[h], p[h]) for h in heads]
        ars = [_dot_nt(ar[h], st[h]) for h in heads]
        mv = [_dot(mak[h], vh[h]) for h in heads]
        uu = [_dot(xinv[h], ars[h][0:L] + mv[h]) for h in heads]
        uv = [jnp.concatenate([-uu[h], vh[h]], axis=0).astype(BF16) for h in heads]
        yh = [ars[h][L:2 * L] + _dot(q[h], uv[h]) for h in heads]
        for h in heads:
            state_s[h][...] = (st[h] + _dot_tn(uv[h], bk[h])) * gam_s[c, :, hsl[h]]
        for h in heads:
            y_s[pl.ds(r0, L), hsl[h]] = yh[h]
        return carry

    lax.fori_loop(0, n_sub, sub_chunk, 0)

    y = y_s[...]
    yc = y - head_sums(split(y)) * (1.0 / HEAD)
    var = head_sums([(yc * yc).astype(BF16)]) * (1.0 / HEAD)
    y_rwkv = (yc * lax.rsqrt(var + GN_EPS) * lng_ref[...] + lnb_ref[...] + bonus) * g
    gates = _sigmoid(jnp.dot(xn, wgate_ref[...], preferred_element_type=F32))
    mix = gates[:, 0:D] * y_conv + gates[:, D:2 * D] * y_rwkv
    h = x + jnp.dot(mix.astype(BF16), wout_ref[...], preferred_element_type=F32)
    o_ref[0] = h
    hn = h * lax.rsqrt(jnp.mean(h * h, axis=-1, keepdims=True) + NORM_EPS) * gffn_ref[...]
    hp_ref[0] = _pack_bf16_pairs(hn)


def _const_spec(shape):
    nd = len(shape)
    return pl.BlockSpec(shape, lambda *_: (0,) * nd, pipeline_mode=pl.Buffered(1))


def _mixer(x, n_groups, norm_g, norm_ffn_g, w_in, conv_w, shift_mu, w0, w2, a0, a2, g2, k_k, k_a, r_k, ln_g, ln_b, w_out):
    bsz, s, d = x.shape
    gb = bsz // n_groups
    n_heads = d // HEAD
    tc = min(MIX_TOKENS, s)
    L = RWKV_CHUNK
    n_sub = tc // L
    lora0 = 6 * d
    w_main = w_in[:, 0:6 * d].astype(BF16)
    pad = lambda t, n: jnp.pad(t, ((0, 0), (0, n - t.shape[1])))
    lora_cols = (DECAY_LORA, AAA_LORA, GATE_LORA)
    lora_pads = (LANES, LANES, 2 * LANES)
    pieces_w, pieces_mu, off = [], [], lora0
    for n, p in zip(lora_cols, lora_pads):
        pieces_w.append(pad(w_in[:, off:off + n], p))
        pieces_mu.append(pad(shift_mu[None, off - 3 * d:off - 3 * d + n], p))
        off += n
    w_lora = jnp.concatenate(pieces_w, axis=1).astype(BF16)
    mu_lora = jnp.concatenate(pieces_mu, axis=1)
    w_gate = w_in[:, off:off + 2 * d].astype(BF16)
    mu_rkv = shift_mu[None, 0:3 * d]
    padr = lambda t, n: jnp.pad(t, ((0, n - t.shape[0]), (0, 0)))
    w2p = padr(w2, LANES).astype(BF16)
    a2p = padr(a2, LANES).astype(BF16)
    g2p = padr(g2, 2 * LANES).astype(BF16)
    row = lambda t: t.reshape(1, -1)
    head_of = jnp.arange(d, dtype=jnp.int32) // HEAD
    head_sel = (head_of[:, None] == jnp.arange(LANES)[None, :]).astype(BF16)
    consts = [row(norm_g), w_main, w_lora, w_gate, conv_w, mu_rkv, mu_lora, row(w0), w2p, row(a0), a2p,
              g2p, row(k_k), row(k_a), row(r_k), row(ln_g), row(ln_b), w_out.astype(BF16), head_sel, head_sel.T,
              row(norm_ffn_g)]
    kern = functools.partial(_mixer_kernel, n_heads=n_heads, d_model=d)
    call = lambda first: pl.pallas_call(
        kern,
        grid=(gb, s // tc),
        in_specs=[pl.BlockSpec((1, tc, d), lambda b, c: (b + first, c, 0))]
        + [_const_spec(t.shape) for t in consts],
        out_specs=[pl.BlockSpec((1, tc, d), lambda b, c: (b, c, 0)),
                   pl.BlockSpec((1, tc, d // 2), lambda b, c: (b, c, 0))],
        out_shape=[jax.ShapeDtypeStruct((gb, s, d), F32), jax.ShapeDtypeStruct((gb, s, d // 2), jnp.int32)],
        scratch_shapes=[
            pltpu.VMEM((SUBLANES, d), F32),
            pltpu.VMEM((SUBLANES, 3 * d), F32),
            pltpu.VMEM((SUBLANES, 4 * LANES), F32),
            pltpu.VMEM((n_sub, 2 * L, d), F32),
            pltpu.VMEM((n_sub, 2 * L, d), F32),
            pltpu.VMEM((tc, d), F32),
            pltpu.VMEM((tc, d), F32),
            pltpu.VMEM((n_sub, 1, d), F32),
        ] + [pltpu.VMEM((HEAD, HEAD), F32)] * n_heads,
        compiler_params=pltpu.CompilerParams(
            dimension_semantics=("arbitrary", "arbitrary"), vmem_limit_bytes=VMEM_LIMIT_BYTES),
        name="mixer",
    )(x, *consts)
    return [call(g * gb) for g in range(n_groups)]


def _router_kernel(h_ref, g_ref, rw_ref, rb_ref, meta_ref, cnt_ref, run_s, *, n_experts):
    tt = h_ref.shape[0]

    @pl.when(pl.program_id(0) == 0)
    def _():
        run_s[...] = jnp.zeros_like(run_s)

    h = h_ref[...]
    ms = jnp.mean(h * h, axis=-1, keepdims=True)
    hn = h * lax.rsqrt(ms + NORM_EPS) * g_ref[...]
    logits = jnp.dot(hn, rw_ref[...], preferred_element_type=F32,
                     precision=lax.Precision.HIGHEST) + rb_ref[...]
    lane = lax.broadcasted_iota(jnp.int32, (tt, LANES), 1)
    neg = jnp.float32(-jnp.inf)
    work = jnp.where(lane < n_experts, logits, neg)
    vals, idxs = [], []
    for _ in range(TOP_K):
        m = jnp.max(work, axis=-1, keepdims=True)
        i = jnp.min(jnp.where(work == m, lane, LANES), axis=-1, keepdims=True)
        vals.append(m)
        idxs.append(i)
        work = jnp.where(lane == i, neg, work)
    ex = [jnp.exp(vv - vals[0]) for vv in vals]
    den = ex[0] + ex[1] + ex[2] + ex[3]
    gates = [e / den for e in ex]

    onehot = jnp.zeros((tt, LANES), jnp.bool_)
    for kk in range(TOP_K):
        onehot = onehot | (lane == (idxs[kk] + kk * n_experts))
    oh = jnp.where(onehot, 1.0, 0.0)
    ri = lax.broadcasted_iota(jnp.int32, (tt, tt), 0)
    ci = lax.broadcasted_iota(jnp.int32, (tt, tt), 1)
    tri = jnp.where(ri > ci, 1.0, 0.0).astype(BF16)
    cnt = jnp.dot(tri, oh.astype(BF16), preferred_element_type=F32)
    tot = jnp.broadcast_to(jnp.sum(oh, axis=0, keepdims=True), (SUBLANES, LANES))
    lane8 = lax.broadcasted_iota(jnp.int32, (SUBLANES, LANES), 1)
    pk = jnp.zeros_like(tot)
    te = tot
    for j in range(1, TOP_K):
        rolled = pltpu.roll(tot, j * n_experts, axis=1)
        pk = pk + jnp.where(lane8 >= j * n_experts, rolled, 0.0)
        te = te + rolled
    before = cnt + (run_s[...] + pk)[0:1, :]
    ranks = [jnp.sum(jnp.where(onehot & (lane // n_experts == kk), before, 0.0), axis=-1, keepdims=True)
             for kk in range(TOP_K)]
    run_s[...] = run_s[...] + te
    cnt_ref[...] = run_s[...]

    meta = jnp.zeros((tt, LANES), F32)
    for kk in range(TOP_K):
        meta = jnp.where(lane == kk, idxs[kk].astype(F32), meta)
        meta = jnp.where(lane == TOP_K + kk, gates[kk], meta)
        meta = jnp.where(lane == 2 * TOP_K + kk, ranks[kk], meta)
    meta_ref[...] = meta


def _router(h2, norm_g, router_w, router_b):
    t, d = h2.shape
    n_experts = router_w.shape[1]
    tt = min(ROUTE_TOKENS, t)
    rw = jnp.pad(router_w, ((0, 0), (0, LANES - n_experts)))
    rb = jnp.pad(router_b, (0, LANES - n_experts)).reshape(1, LANES)
    kern = functools.partial(_router_kernel, n_experts=n_experts)
    return pl.pallas_call(
        kern,
        grid=(t // tt,),
        in_specs=[pl.BlockSpec((tt, d), lambda i: (i, 0)),
                  pl.BlockSpec((1, d), lambda i: (0, 0)),
                  pl.BlockSpec((d, LANES), lambda i: (0, 0)),
                  pl.BlockSpec((1, LANES), lambda i: (0, 0))],
        out_specs=[pl.BlockSpec((tt, LANES), lambda i: (i, 0)),
                   pl.BlockSpec((SUBLANES, LANES), lambda i: (0, 0))],
        out_shape=[jax.ShapeDtypeStruct((t, LANES), F32),
                   jax.ShapeDtypeStruct((SUBLANES, LANES), F32)],
        scratch_shapes=[pltpu.VMEM((SUBLANES, LANES), F32)],
        compiler_params=pltpu.CompilerParams(
            dimension_semantics=("arbitrary",), vmem_limit_bytes=VMEM_LIMIT_BYTES),
        name="router",
    )(h2, norm_g.reshape(1, d), rw, rb)


def _row_gather(table, idx):
    n_idx = idx.shape[0]
    d = table.shape[1]
    w = GATHER_WINDOW
    sc = plsc.get_sparse_core_info()
    n_workers = sc.num_cores * sc.num_subcores
    per_worker = n_idx // n_workers
    n_win = per_worker // w
    assert per_worker * n_workers == n_idx and n_win * w == per_worker and n_win % 2 == 0
    mesh = plsc.VectorSubcoreMesh(core_axis_name="core", subcore_axis_name="subcore")

    @functools.partial(
        pl.kernel, out_type=jax.ShapeDtypeStruct((n_idx, d), table.dtype), mesh=mesh,
        scratch_types=[pltpu.VMEM((per_worker,), jnp.int32),
                       pltpu.VMEM((w, d), table.dtype), pltpu.VMEM((w, d), table.dtype),
                       pltpu.SemaphoreType.DMA, pltpu.SemaphoreType.DMA,
                       pltpu.SemaphoreType.DMA, pltpu.SemaphoreType.DMA])
    def gather_kernel(table_hbm, idx_hbm, out_hbm, idx_v, buf_a, buf_b, gsem_a, gsem_b, psem_a, psem_b):
        worker = lax.axis_index("subcore") * sc.num_cores + lax.axis_index("core")
        base = worker * per_worker
        pltpu.sync_copy(idx_hbm.at[pl.ds(base, per_worker)], idx_v)

        def gather(j, buf, sem):
            return pltpu.make_async_copy(table_hbm.at[idx_v.at[pl.ds(j * w, w)]], buf, sem)

        def put(j, buf, sem):
            return pltpu.make_async_copy(buf, out_hbm.at[pl.ds(base + j * w, w)], sem)

        gather(0, buf_a, gsem_a).start()

        @pl.loop(0, n_win, step=2)
        def _(j):
            gather(j, buf_a, gsem_a).wait()

            @pl.when(j > 0)
            def _():
                put(j - 1, buf_b, psem_b).wait()

            gather(j + 1, buf_b, gsem_b).start()
            put(j, buf_a, psem_a).start()
            gather(j + 1, buf_b, gsem_b).wait()
            put(j, buf_a, psem_a).wait()

            @pl.when(j + 2 < n_win)
            def _():
                gather(j + 2, buf_a, gsem_a).start()

            put(j + 1, buf_b, psem_b).start()

        put(n_win - 1, buf_b, psem_b).wait()

    return gather_kernel(table, idx)


def _slot_tokens(pos, n_slots):
    n = pos.shape[0]
    sc = plsc.get_sparse_core_info()
    n_cores, n_sub = sc.num_cores, sc.num_subcores
    rows = n // LANES
    rows_per_tile = rows // n_sub
    out_per_worker = n_slots // (n_cores * n_sub)
    init_per_tile = n_slots // n_sub
    assert rows_per_tile * n_sub * LANES == n and out_per_worker * n_cores * n_sub == n_slots
    mesh = plsc.VectorSubcoreMesh(core_axis_name="core", subcore_axis_name="subcore")
    tokens = (jnp.arange(n, dtype=jnp.int32) // TOP_K).reshape(rows, LANES)
    fill = jnp.arange(n_slots, dtype=jnp.int32) % (n // TOP_K)

    @functools.partial(
        pl.kernel, out_type=jax.ShapeDtypeStruct((n_slots,), jnp.int32), mesh=mesh,
        scratch_types=[pltpu.VMEM((rows_per_tile, LANES), jnp.int32),
                       pltpu.VMEM((rows_per_tile, LANES), jnp.int32),
                       pltpu.VMEM_SHARED((n_slots,), jnp.int32)])
    def invert_kernel(pos_hbm, tok_hbm, fill_hbm, out_hbm, idx_v, val_v, shared):
        cid = lax.axis_index("core")
        sid = lax.axis_index("subcore")
        pltpu.sync_copy(fill_hbm.at[pl.ds(sid * init_per_tile, init_per_tile)],
                        shared.at[pl.ds(sid * init_per_tile, init_per_tile)])
        pltpu.sync_copy(pos_hbm.at[pl.ds(sid * rows_per_tile, rows_per_tile)], idx_v)
        pltpu.sync_copy(tok_hbm.at[pl.ds(sid * rows_per_tile, rows_per_tile)], val_v)
        plsc.subcore_barrier()

        @pl.loop(0, rows_per_tile)
        def _(j):
            pltpu.sync_copy(val_v.at[j], shared.at[idx_v.at[j]])

        plsc.subcore_barrier()
        off = (cid * n_sub + sid) * out_per_worker
        pltpu.sync_copy(shared.at[pl.ds(off, out_per_worker)], out_hbm.at[pl.ds(off, out_per_worker)])

    return invert_kernel(pos.reshape(rows, LANES), tokens, fill)


def _experts_kernel(te_ref, nv_ref, xs_ref, w1_ref, b1_ref, w2_ref, b2_ref, ys_ref, w1_s, w2_s, *, d_ff):
    i = pl.program_id(0)
    prev = te_ref[jnp.maximum(i - 1, 0)]
    changed = (i == 0) | (te_ref[i] != prev)

    @pl.when(changed)
    def _():
        w1_s[...] = w1_ref[0].astype(BF16)
        w2_s[...] = w2_ref[0].astype(BF16)

    nvalid = nv_ref[i]

    @pl.when(nvalid > 0)
    def _():
        tm = xs_ref.shape[0]
        groups = [slice(r, r + EXPERT_ROWS) for r in range(0, tm, EXPERT_ROWS)]
        half = xs_ref.shape[1]
        xg = []
        for rows in groups:
            row = rows.start + lax.broadcasted_iota(jnp.int32, (EXPERT_ROWS, 1), 0)
            lo, hi = _unpack_bf16_pairs(jnp.where(row < nvalid, xs_ref[rows, :], 0))
            xg.append((lo.astype(BF16), hi.astype(BF16)))
        ug = [jnp.dot(lo, w1_s[0:half, :], preferred_element_type=F32)
              + jnp.dot(hi, w1_s[half:2 * half, :], preferred_element_type=F32) + b1_ref[0] for lo, hi in xg]
        ag = []
        for u in ug:
            glu = jnp.minimum(u[:, 0:d_ff], SWIGLU_LIMIT)
            lin = jnp.clip(u[:, d_ff:2 * d_ff], -SWIGLU_LIMIT, SWIGLU_LIMIT)
            ag.append((glu * _sigmoid(SWIGLU_ALPHA * glu) * (lin + 1.0)).astype(BF16))
        for rows, act in zip(groups, ag):
            ys_ref[rows, :] = _pack_bf16_pairs(
                jnp.dot(act, w2_s[...], preferred_element_type=F32) + b2_ref[0])

    @pl.when(nvalid <= 0)
    def _():
        ys_ref[...] = jnp.zeros_like(ys_ref)


def _experts(xs, tile_expert, tile_valid, w1, b1, w2, b2):
    n_slots = xs.shape[0]
    n_exp, d, two_ff = w1.shape
    d_ff = two_ff // 2
    tm = SLOT_TILE
    n_tiles = n_slots // tm
    kern = functools.partial(_experts_kernel, d_ff=d_ff)
    grid_spec = pltpu.PrefetchScalarGridSpec(
        num_scalar_prefetch=2,
        grid=(n_tiles,),
        in_specs=[pl.BlockSpec((tm, d // 2), lambda i, te, nv: (i, 0)),
                  pl.BlockSpec((1, d, two_ff), lambda i, te, nv: (te[i], 0, 0)),
                  pl.BlockSpec((1, 1, two_ff), lambda i, te, nv: (te[i], 0, 0)),
                  pl.BlockSpec((1, d_ff, d), lambda i, te, nv: (te[i], 0, 0)),
                  pl.BlockSpec((1, 1, d), lambda i, te, nv: (te[i], 0, 0))],
        out_specs=pl.BlockSpec((tm, d // 2), lambda i, te, nv: (i, 0)),
        scratch_shapes=[pltpu.VMEM((d, two_ff), BF16), pltpu.VMEM((d_ff, d), BF16)],
    )
    return pl.pallas_call(
        kern,
        grid_spec=grid_spec,
        out_shape=jax.ShapeDtypeStruct((n_slots, d // 2), jnp.int32),
        compiler_params=pltpu.CompilerParams(
            dimension_semantics=("arbitrary",), vmem_limit_bytes=VMEM_LIMIT_BYTES),
        name="experts",
    )(tile_expert, tile_valid, xs, w1, b1.reshape(n_exp, 1, two_ff), w2, b2.reshape(n_exp, 1, d))


def _combine_kernel(h_ref, meta_ref, g_ref, *rest):
    y_refs, o_ref = rest[:TOP_K], rest[-1]
    d = h_ref.shape[1]
    half = d // 2
    acc_lo = h_ref[:, 0:half]
    acc_hi = h_ref[:, half:d]
    for kk in range(TOP_K):
        gate = meta_ref[:, TOP_K + kk:TOP_K + kk + 1]
        lo, hi = _unpack_bf16_pairs(y_refs[kk][...])
        acc_lo = acc_lo + gate * lo
        acc_hi = acc_hi + gate * hi
    ms = (jnp.sum(acc_lo * acc_lo, axis=-1, keepdims=True)
          + jnp.sum(acc_hi * acc_hi, axis=-1, keepdims=True)) * (1.0 / d)
    scale = lax.rsqrt(ms + NORM_EPS)
    o_ref[:, 0:half] = acc_lo * scale * g_ref[:, 0:half]
    o_ref[:, half:d] = acc_hi * scale * g_ref[:, half:d]


def _combine(h2, y4, meta, norm_g, out_prev, group, n_groups):
    t, d = h2.shape
    tt = min(MOVE_TOKENS, t)
    n_blk = t // tt
    prev_specs = [] if out_prev is None else [pl.BlockSpec(memory_space=pl.ANY)]
    prev_args = [] if out_prev is None else [out_prev]
    n_in = 3 + TOP_K
    y_specs = [pl.BlockSpec((tt, d // 2), functools.partial(lambda i, kk: (kk * n_blk + i, 0), kk=kk))
               for kk in range(TOP_K)]
    return pl.pallas_call(
        _combine_kernel,
        grid=(n_blk,),
        in_specs=[pl.BlockSpec((tt, d), lambda i: (i, 0)),
                  pl.BlockSpec((tt, LANES), lambda i: (i, 0)),
                  pl.BlockSpec((1, d), lambda i: (0, 0))] + y_specs + prev_specs,
        out_specs=pl.BlockSpec((tt, d), lambda i: (group * n_blk + i, 0)),
        out_shape=jax.ShapeDtypeStruct((n_groups * t, d), F32),
        input_output_aliases={} if out_prev is None else {n_in: 0},
        compiler_params=pltpu.CompilerParams(
            dimension_semantics=("arbitrary",), vmem_limit_bytes=VMEM_LIMIT_BYTES),
        name="combine",
    )(h2, meta, norm_g.reshape(1, d), *([y4] * TOP_K), *prev_args)


def _moe(h2, hp2, norm_ffn_g, router_w, router_b, w1, b1, w2, b2, norm_final_g, out_prev, group, n_groups):
    t, d = h2.shape
    n_exp = router_w.shape[1]
    tm = SLOT_TILE
    meta, counts = _router(h2, norm_ffn_g, router_w, router_b)
    eidx = meta[:, 0:TOP_K].astype(jnp.int32)
    rank = meta[:, 2 * TOP_K:3 * TOP_K].astype(jnp.int32)
    cnt = counts[0, 0:n_exp].astype(jnp.int32)
    padded = (cnt + tm - 1) // tm * tm
    seg_end = jnp.cumsum(padded)
    seg_start = seg_end - padded
    pos = (seg_start[eidx] + rank).reshape(-1)
    n_tiles = -(-(t * TOP_K) // tm) + n_exp
    n_slots = n_tiles * tm
    tile_start = jnp.arange(n_tiles, dtype=jnp.int32) * tm
    tile_expert = jnp.minimum(jnp.sum(tile_start[:, None] >= seg_end[None, :], axis=1), n_exp - 1).astype(jnp.int32)
    tile_valid = jnp.clip(seg_start[tile_expert] + cnt[tile_expert] - tile_start, 0, tm).astype(jnp.int32)
    slot_token = _slot_tokens(pos, n_slots)
    xs = _row_gather(hp2, slot_token)
    ys = _experts(xs, tile_expert, tile_valid, w1, b1, w2, b2)
    y4 = _row_gather(ys, pos.reshape(t, TOP_K).T.reshape(-1))
    return _combine(h2, y4, meta, norm_final_g, out_prev, group, n_groups)


def kernel(x, norm_mix_g, w_in, conv_w, shift_mu, decay_w0, decay_w2, iclr_a0, iclr_a2, gate_g2, k_k, k_a,
           r_k, ln_x_g, ln_x_b, w_out, norm_ffn_g, router_w, router_b, exp_w1, exp_b1, exp_w2, exp_b2,
           norm_final_g):
    bsz, s, d = x.shape
    depth = w_in.shape[0]
    assert depth == 1, "final norm is fused into the last layer's combine kernel"
    n_groups = BATCH_GROUPS if bsz % BATCH_GROUPS == 0 else 1
    hs = _mixer(x, n_groups, norm_mix_g[0], norm_ffn_g[0], w_in[0], conv_w[0], shift_mu[0], decay_w0[0], decay_w2[0],
                iclr_a0[0], iclr_a2[0], gate_g2[0], k_k[0], k_a[0], r_k[0], ln_x_g[0], ln_x_b[0], w_out[0])
    out = None
    for g, (h, hp) in enumerate(hs):
        out = _moe(h.reshape(-1, d), hp.reshape(-1, d // 2), norm_ffn_g[0], router_w[0], router_b[0], exp_w1[0], exp_b1[0],
                   exp_w2[0], exp_b2[0], norm_final_g, out, g, n_groups)
    return out.reshape(bsz, s, d)
```

```python
import functools

import jax
import jax.numpy as jnp
from jax import lax
from jax.experimental import pallas as pl
from jax.experimental.pallas import tpu as pltpu
from jax.experimental.pallas import tpu_sc as plsc

HEAD = 64
DECAY_LORA = 64
AAA_LORA = 64
GATE_LORA = 160
TOP_K = 4
SWIGLU_ALPHA = 1.702
SWIGLU_LIMIT = 7.0
NORM_EPS = 1e-5
GN_EPS = 64e-5
DECAY_SCALE = 0.6065306597126334

LANES = 128
SUBLANES = 8
VMEM_LIMIT_BYTES = 56 * 1024 * 1024

RWKV_CHUNK = 64
MIX_TOKENS = 256
ROUTE_TOKENS = 512
MOVE_TOKENS = 256
GATHER_WINDOW = 64
SLOT_TILE = 512
EXPERT_ROWS = 256
BATCH_GROUPS = 2

BF16 = jnp.bfloat16
F32 = jnp.float32


def _dot(a, b):
    return jnp.dot(a.astype(BF16), b.astype(BF16), preferred_element_type=F32)


def _dot_nt(a, b):
    return lax.dot_general(a.astype(BF16), b.astype(BF16), (((1,), (1,)), ((), ())),
                           preferred_element_type=F32)


def _dot_tn(a, b):
    return lax.dot_general(a.astype(BF16), b.astype(BF16), (((0,), (0,)), ((), ())),
                           preferred_element_type=F32)


def _sigmoid(x):
    return 0.5 * jnp.tanh(0.5 * x) + 0.5


def _pack_bf16_pairs(x):
    n = x.shape[1] // 2
    bits = pltpu.bitcast(x.astype(BF16).astype(F32), jnp.uint32)
    packed = (bits[:, n:] & jnp.uint32(0xFFFF0000)) | (bits[:, :n] >> 16)
    return pltpu.bitcast(packed, jnp.int32)


def _unpack_bf16_pairs(w):
    bits = pltpu.bitcast(w, jnp.uint32)
    return pltpu.bitcast(bits << 16, F32), pltpu.bitcast(bits & jnp.uint32(0xFFFF0000), F32)


def _shift_rows(x, n, carry):
    rolled = pltpu.roll(x, n, axis=0)
    row = lax.broadcasted_iota(jnp.int32, x.shape, 0)
    out = rolled
    for i in range(n):
        out = jnp.where(row == i, carry[SUBLANES - n + i:SUBLANES - n + i + 1, :], out)
    return out


def _mixer_kernel(x_ref, g_ref, wmain_ref, wlora_ref, wgate_ref, cw_ref, mu_rkv_ref, mu_lora_ref,
                  w0_ref, w2_ref, a0_ref, a2_ref, g2_ref, kk_ref, ka_ref, rk_ref, lng_ref, lnb_ref,
                  wout_ref, hsel_ref, hselt_ref, gffn_ref, o_ref, hp_ref,
                  cu_s, cp_s, cl_s, ar_s, bk_s, v_s, y_s, gam_s, *state_s,
                  n_heads, d_model):
    tc = x_ref.shape[1]
    L = RWKV_CHUNK
    n_sub = tc // L
    D = d_model

    @pl.when(pl.program_id(1) == 0)
    def _():
        for st_ref in state_s:
            st_ref[...] = jnp.zeros_like(st_ref)
        cu_s[...] = jnp.zeros_like(cu_s)
        cp_s[...] = jnp.zeros_like(cp_s)
        cl_s[...] = jnp.zeros_like(cl_s)

    def split(t):
        hi = t.astype(BF16)
        return hi, (t - hi.astype(F32)).astype(BF16)

    def head_sums(parts, exact=False):
        per_head = sum(jnp.dot(p, hsel_ref[...], preferred_element_type=F32) for p in parts)
        back = split(per_head) if exact else [per_head.astype(BF16)]
        return sum(jnp.dot(p, hselt_ref[...], preferred_element_type=F32) for p in back)

    x = x_ref[0]
    ms = jnp.mean(x * x, axis=-1, keepdims=True)
    xn = (x * lax.rsqrt(ms + NORM_EPS) * g_ref[...]).astype(BF16)

    pc = jnp.dot(xn, wmain_ref[:, 0:3 * D], preferred_element_type=F32)
    u = pc[:, D:2 * D] * pc[:, 2 * D:3 * D]
    cu = cu_s[...]
    conv = (cw_ref[0:1, :] * _shift_rows(u, 2, cu) + cw_ref[1:2, :] * _shift_rows(u, 1, cu)
            + cw_ref[2:3, :] * u)
    y_conv = pc[:, 0:D] * conv
    cu_s[...] = u[tc - SUBLANES:tc, :]

    pr = jnp.dot(xn, wmain_ref[:, 3 * D:6 * D], preferred_element_type=F32)
    cp = cp_s[...]
    cp_s[...] = pr[tc - SUBLANES:tc, :]
    pr = pr + (_shift_rows(pr, 1, cp) - pr) * mu_rkv_ref[...]
    plo = jnp.dot(xn, wlora_ref[...], preferred_element_type=F32)
    cl = cl_s[...]
    cl_s[...] = plo[tc - SUBLANES:tc, :]
    plo = plo + (_shift_rows(plo, 1, cl) - plo) * mu_lora_ref[...]

    r = pr[:, 0:D]
    k = pr[:, D:2 * D]
    v = pr[:, 2 * D:3 * D]
    wd = plo[:, 0:LANES]
    ad = plo[:, LANES:2 * LANES]
    gd = plo[:, 2 * LANES:4 * LANES]

    lw = -DECAY_SCALE * _sigmoid(w0_ref[...] + _dot(jnp.tanh(wd), w2_ref[...]))
    a = _sigmoid(a0_ref[...] + _dot(ad, a2_ref[...]))
    g = _dot(_sigmoid(gd), g2_ref[...])

    row = lax.broadcasted_iota(jnp.int32, (tc, tc), 0)
    col = lax.broadcasted_iota(jnp.int32, (tc, tc), 1)
    tri = jnp.where((row >= col) & ((row // L) == (col // L)), 1.0, 0.0).astype(BF16)
    cum = sum(jnp.dot(tri, p, preferred_element_type=F32) for p in split(lw))
    e_inv = jnp.exp(-cum)

    kkraw = k * kk_ref[...]
    ss = head_sums([(kkraw * kkraw).astype(BF16)])
    kkn = kkraw * jnp.minimum(lax.rsqrt(ss), 1e12)

    k2 = k * (1.0 + (a - 1.0) * ka_ref[...])
    a_t = kkn * jnp.exp(cum - lw)
    r_t = r * jnp.exp(cum)
    b_t = kkn * a * e_inv
    k_t = k2 * e_inv
    for c in range(n_sub):
        rows = slice(c * L, (c + 1) * L)
        ar_s[c, 0:L, :] = a_t[rows]
        ar_s[c, L:2 * L, :] = r_t[rows]
        bk_s[c, 0:L, :] = b_t[rows]
        bk_s[c, L:2 * L, :] = k_t[rows]
        gam_s[c] = jnp.exp(cum[(c + 1) * L - 1:(c + 1) * L, :])
    v_s[...] = v
    bonus = head_sums([(r * k2 * rk_ref[...]).astype(BF16)]) * v

    ri = lax.broadcasted_iota(jnp.int32, (L, L), 0)
    ci = lax.broadcasted_iota(jnp.int32, (L, L), 1)
    strict = ri > ci
    eye = jnp.where(ri == ci, 1.0, 0.0)
    ri2 = lax.broadcasted_iota(jnp.int32, (L, 2 * L), 0)
    ci2 = lax.broadcasted_iota(jnp.int32, (L, 2 * L), 1)
    incl2 = ri2 >= (ci2 % L)
    n_double = L.bit_length() - 2

    def sub_chunk(c):
        r0 = c * L
        heads = range(n_heads)
        hsl = [slice(h * HEAD, (h + 1) * HEAD) for h in heads]
        ar = [ar_s[c, :, hsl[h]].astype(BF16) for h in heads]
        bk = [bk_s[c, :, hsl[h]].astype(BF16) for h in heads]
        vh = [v_s[pl.ds(r0, L), hsl[h]] for h in heads]
        st = [state_s[h][...] for h in heads]
        gm = [_dot_nt(ar[h], bk[h]) for h in heads]
        nm = [jnp.where(strict, gm[h][0:L, 0:L], 0.0) for h in heads]
        mak = [jnp.where(strict, gm[h][0:L, L:2 * L], 0.0) for h in heads]
        q = [jnp.where(incl2, gm[h][L:2 * L, :], 0.0) for h in heads]
        xinv = [eye - nm[h] for h in heads]
        p = nm
        for _ in range(n_double):
            p = [_dot(p[h], p[h]) for h in heads]
            xinv = [xinv[h] + _dot(xinv[h], p[h]) for h in heads]
        ars = [_dot_nt(ar[h], st[h]) for h in heads]
        mv = [_dot(mak[h], vh[h]) for h in heads]
        uu = [_dot(xinv[h], ars[h][0:L] + mv[h]) for h in heads]
        uv = [jnp.concatenate([-uu[h], vh[h]], axis=0).astype(BF16) for h in heads]
        yh = [ars[h][L:2 * L] + _dot(q[h], uv[h]) for h in heads]
        for h in heads:
            state_s[h][...] = (st[h] + _dot_tn(uv[h], bk[h])) * gam_s[c, :, hsl[h]]
        for h in heads:
            y_s[pl.ds(r0, L), hsl[h]] = yh[h]

    for c in range(n_sub):
        sub_chunk(c)

    y = y_s[...]
    yc = y - head_sums(split(y), exact=True) * (1.0 / HEAD)
    var = head_sums([(yc * yc).astype(BF16)]) * (1.0 / HEAD)
    y_rwkv = (yc * lax.rsqrt(var + GN_EPS) * lng_ref[...] + lnb_ref[...] + bonus) * g
    gates = _sigmoid(jnp.dot(xn, wgate_ref[...], preferred_element_type=F32))
    mix = gates[:, 0:D] * y_conv + gates[:, D:2 * D] * y_rwkv
    h = x + jnp.dot(mix.astype(BF16), wout_ref[...], preferred_element_type=F32)
    o_ref[0] = h
    hn = h * lax.rsqrt(jnp.mean(h * h, axis=-1, keepdims=True) + NORM_EPS) * gffn_ref[...]
    hp_ref[0] = _pack_bf16_pairs(hn)


def _const_spec(shape):
    nd = len(shape)
    return pl.BlockSpec(shape, lambda *_: (0,) * nd, pipeline_mode=pl.Buffered(1))


def _mixer(x, n_groups, norm_g, norm_ffn_g, w_in, conv_w, shift_mu, w0, w2, a0, a2, g2, k_k, k_a, r_k, ln_g, ln_b, w_out):
    bsz, s, d = x.shape
    gb = bsz // n_groups
    n_heads = d // HEAD
    tc = min(MIX_TOKENS, s)
    L = RWKV_CHUNK
    n_sub = tc // L
    lora0 = 6 * d
    w_main = w_in[:, 0:6 * d].astype(BF16)
    pad = lambda t, n: jnp.pad(t, ((0, 0), (0, n - t.shape[1])))
    lora_cols = (DECAY_LORA, AAA_LORA, GATE_LORA)
    lora_pads = (LANES, LANES, 2 * LANES)
    pieces_w, pieces_mu, off = [], [], lora0
    for n, p in zip(lora_cols, lora_pads):
        pieces_w.append(pad(w_in[:, off:off + n], p))
        pieces_mu.append(pad(shift_mu[None, off - 3 * d:off - 3 * d + n], p))
        off += n
    w_lora = jnp.concatenate(pieces_w, axis=1).astype(BF16)
    mu_lora = jnp.concatenate(pieces_mu, axis=1)
    w_gate = w_in[:, off:off + 2 * d].astype(BF16)
    mu_rkv = shift_mu[None, 0:3 * d]
    padr = lambda t, n: jnp.pad(t, ((0, n - t.shape[0]), (0, 0)))
    w2p = padr(w2, LANES).astype(BF16)
    a2p = padr(a2, LANES).astype(BF16)
    g2p = padr(g2, 2 * LANES).astype(BF16)
    row = lambda t: t.reshape(1, -1)
    head_of = jnp.arange(d, dtype=jnp.int32) // HEAD
    head_sel = (head_of[:, None] == jnp.arange(LANES)[None, :]).astype(BF16)
    consts = [row(norm_g), w_main, w_lora, w_gate, conv_w, mu_rkv, mu_lora, row(w0), w2p, row(a0), a2p,
              g2p, row(k_k), row(k_a), row(r_k), row(ln_g), row(ln_b), w_out.astype(BF16), head_sel, head_sel.T,
              row(norm_ffn_g)]
    kern = functools.partial(_mixer_kernel, n_heads=n_heads, d_model=d)
    call = lambda first: pl.pallas_call(
        kern,
        grid=(gb, s // tc),
        in_specs=[pl.BlockSpec((1, tc, d), lambda b, c: (b + first, c, 0))]
        + [_const_spec(t.shape) for t in consts],
        out_specs=[pl.BlockSpec((1, tc, d), lambda b, c: (b, c, 0)),
                   pl.BlockSpec((1, tc, d // 2), lambda b, c: (b, c, 0))],
        out_shape=[jax.ShapeDtypeStruct((gb, s, d), F32), jax.ShapeDtypeStruct((gb, s, d // 2), jnp.int32)],
        scratch_shapes=[
            pltpu.VMEM((SUBLANES, d), F32),
            pltpu.VMEM((SUBLANES, 3 * d), F32),
            pltpu.VMEM((SUBLANES, 4 * LANES), F32),
            pltpu.VMEM((n_sub, 2 * L, d), F32),
            pltpu.VMEM((n_sub, 2 * L, d), F32),
            pltpu.VMEM((tc, d), F32),
            pltpu.VMEM((tc, d), F32),
            pltpu.VMEM((n_sub, 1, d), F32),
        ] + [pltpu.VMEM((HEAD, HEAD), F32)] * n_heads,
        compiler_params=pltpu.CompilerParams(
            dimension_semantics=("arbitrary", "arbitrary"), vmem_limit_bytes=VMEM_LIMIT_BYTES),
        name="mixer",
    )(x, *consts)
    return [call(g * gb) for g in range(n_groups)]


def _router_kernel(h_ref, g_ref, rw_ref, rb_ref, meta_ref, metat_ref, cnt_ref, run_s, *, n_experts):
    tt = h_ref.shape[0]

    @pl.when(pl.program_id(0) == 0)
    def _():
        run_s[...] = jnp.zeros_like(run_s)

    h = h_ref[...]
    ms = jnp.mean(h * h, axis=-1, keepdims=True)
    hn = h * lax.rsqrt(ms + NORM_EPS) * g_ref[...]
    logits = jnp.dot(hn, rw_ref[...], preferred_element_type=F32,
                     precision=lax.Precision.HIGHEST) + rb_ref[...]
    lane = lax.broadcasted_iota(jnp.int32, (tt, LANES), 1)
    neg = jnp.float32(-jnp.inf)
    work = jnp.where(lane < n_experts, logits, neg)
    vals, idxs = [], []
    for _ in range(TOP_K):
        m = jnp.max(work, axis=-1, keepdims=True)
        i = jnp.min(jnp.where(work == m, lane, LANES), axis=-1, keepdims=True)
        vals.append(m)
        idxs.append(i)
        work = jnp.where(lane == i, neg, work)
    ex = [jnp.exp(vv - vals[0]) for vv in vals]
    den = ex[0] + ex[1] + ex[2] + ex[3]
    gates = [e / den for e in ex]

    onehot = jnp.zeros((tt, LANES), jnp.bool_)
    for kk in range(TOP_K):
        onehot = onehot | (lane == (idxs[kk] + kk * n_experts))
    oh = jnp.where(onehot, 1.0, 0.0)
    ri = lax.broadcasted_iota(jnp.int32, (tt, tt), 0)
    ci = lax.broadcasted_iota(jnp.int32, (tt, tt), 1)
    tri = jnp.where(ri > ci, 1.0, 0.0).astype(BF16)
    cnt = jnp.dot(tri, oh.astype(BF16), preferred_element_type=F32)
    tot = jnp.broadcast_to(jnp.sum(oh, axis=0, keepdims=True), (SUBLANES, LANES))
    lane8 = lax.broadcasted_iota(jnp.int32, (SUBLANES, LANES), 1)
    pk = jnp.zeros_like(tot)
    te = tot
    for j in range(1, TOP_K):
        rolled = pltpu.roll(tot, j * n_experts, axis=1)
        pk = pk + jnp.where(lane8 >= j * n_experts, rolled, 0.0)
        te = te + rolled
    before = cnt + (run_s[...] + pk)[0:1, :]
    ranks = [jnp.sum(jnp.where(onehot & (lane // n_experts == kk), before, 0.0), axis=-1, keepdims=True)
             for kk in range(TOP_K)]
    run_s[...] = run_s[...] + te
    cnt_ref[...] = run_s[...]

    meta = jnp.zeros((tt, LANES), F32)
    for kk in range(TOP_K):
        meta = jnp.where(lane == kk, idxs[kk].astype(F32), meta)
        meta = jnp.where(lane == TOP_K + kk, gates[kk], meta)
        meta = jnp.where(lane == 2 * TOP_K + kk, ranks[kk], meta)
    meta_ref[...] = meta
    metat_ref[...] = meta.T[0:2 * SUBLANES, :]


def _router(h2, norm_g, router_w, router_b):
    t, d = h2.shape
    n_experts = router_w.shape[1]
    tt = min(ROUTE_TOKENS, t)
    rw = jnp.pad(router_w, ((0, 0), (0, LANES - n_experts)))
    rb = jnp.pad(router_b, (0, LANES - n_experts)).reshape(1, LANES)
    kern = functools.partial(_router_kernel, n_experts=n_experts)
    return pl.pallas_call(
        kern,
        grid=(t // tt,),
        in_specs=[pl.BlockSpec((tt, d), lambda i: (i, 0)),
                  pl.BlockSpec((1, d), lambda i: (0, 0)),
                  pl.BlockSpec((d, LANES), lambda i: (0, 0)),
                  pl.BlockSpec((1, LANES), lambda i: (0, 0))],
        out_specs=[pl.BlockSpec((tt, LANES), lambda i: (i, 0)),
                   pl.BlockSpec((2 * SUBLANES, tt), lambda i: (0, i)),
                   pl.BlockSpec((SUBLANES, LANES), lambda i: (0, 0))],
        out_shape=[jax.ShapeDtypeStruct((t, LANES), F32),
                   jax.ShapeDtypeStruct((2 * SUBLANES, t), F32),
                   jax.ShapeDtypeStruct((SUBLANES, LANES), F32)],
        scratch_shapes=[pltpu.VMEM((SUBLANES, LANES), F32)],
        compiler_params=pltpu.CompilerParams(
            dimension_semantics=("arbitrary",), vmem_limit_bytes=VMEM_LIMIT_BYTES),
        name="router",
    )(h2, norm_g.reshape(1, d), rw, rb)


def _row_gather(table, idx):
    n_idx = idx.shape[0]
    d = table.shape[1]
    w = GATHER_WINDOW
    sc = plsc.get_sparse_core_info()
    n_workers = sc.num_cores * sc.num_subcores
    per_worker = n_idx // n_workers
    n_win = per_worker // w
    assert per_worker * n_workers == n_idx and n_win * w == per_worker and n_win % 2 == 0
    mesh = plsc.VectorSubcoreMesh(core_axis_name="core", subcore_axis_name="subcore")

    @functools.partial(
        pl.kernel, out_type=jax.ShapeDtypeStruct((n_idx, d), table.dtype), mesh=mesh,
        scratch_types=[pltpu.VMEM((per_worker,), jnp.int32),
                       pltpu.VMEM((w, d), table.dtype), pltpu.VMEM((w, d), table.dtype),
                       pltpu.SemaphoreType.DMA, pltpu.SemaphoreType.DMA,
                       pltpu.SemaphoreType.DMA, pltpu.SemaphoreType.DMA])
    def gather_kernel(table_hbm, idx_hbm, out_hbm, idx_v, buf_a, buf_b, gsem_a, gsem_b, psem_a, psem_b):
        worker = lax.axis_index("subcore") * sc.num_cores + lax.axis_index("core")
        base = worker * per_worker
        pltpu.sync_copy(idx_hbm.at[pl.ds(base, per_worker)], idx_v)

        def gather(j, buf, sem):
            return pltpu.make_async_copy(table_hbm.at[idx_v.at[pl.ds(j * w, w)]], buf, sem)

        def put(j, buf, sem):
            return pltpu.make_async_copy(buf, out_hbm.at[pl.ds(base + j * w, w)], sem)

        gather(0, buf_a, gsem_a).start()

        @pl.loop(0, n_win, step=2)
        def _(j):
            gather(j, buf_a, gsem_a).wait()

            @pl.when(j > 0)
            def _():
                put(j - 1, buf_b, psem_b).wait()

            gather(j + 1, buf_b, gsem_b).start()
            put(j, buf_a, psem_a).start()
            gather(j + 1, buf_b, gsem_b).wait()
            put(j, buf_a, psem_a).wait()

            @pl.when(j + 2 < n_win)
            def _():
                gather(j + 2, buf_a, gsem_a).start()

            put(j + 1, buf_b, psem_b).start()

        put(n_win - 1, buf_b, psem_b).wait()

    return gather_kernel(table, idx)


def _slot_tokens(pos, n_slots):
    n = pos.shape[0]
    sc = plsc.get_sparse_core_info()
    n_cores, n_sub = sc.num_cores, sc.num_subcores
    rows = n // LANES
    rows_per_tile = rows // n_sub
    out_per_worker = n_slots // (n_cores * n_sub)
    init_per_tile = n_slots // n_sub
    assert rows_per_tile * n_sub * LANES == n and out_per_worker * n_cores * n_sub == n_slots
    mesh = plsc.VectorSubcoreMesh(core_axis_name="core", subcore_axis_name="subcore")
    tokens = (jnp.arange(n, dtype=jnp.int32) % (n // TOP_K)).reshape(rows, LANES)
    fill = jnp.arange(n_slots, dtype=jnp.int32) % (n // TOP_K)

    @functools.partial(
        pl.kernel, out_type=jax.ShapeDtypeStruct((n_slots,), jnp.int32), mesh=mesh,
        scratch_types=[pltpu.VMEM((rows_per_tile, LANES), jnp.int32),
                       pltpu.VMEM((rows_per_tile, LANES), jnp.int32),
                       pltpu.VMEM_SHARED((n_slots,), jnp.int32)])
    def invert_kernel(pos_hbm, tok_hbm, fill_hbm, out_hbm, idx_v, val_v, shared):
        cid = lax.axis_index("core")
        sid = lax.axis_index("subcore")
        pltpu.sync_copy(fill_hbm.at[pl.ds(sid * init_per_tile, init_per_tile)],
                        shared.at[pl.ds(sid * init_per_tile, init_per_tile)])
        pltpu.sync_copy(pos_hbm.at[pl.ds(sid * rows_per_tile, rows_per_tile)], idx_v)
        pltpu.sync_copy(tok_hbm.at[pl.ds(sid * rows_per_tile, rows_per_tile)], val_v)
        plsc.subcore_barrier()

        @pl.loop(0, rows_per_tile)
        def _(j):
            pltpu.sync_copy(val_v.at[j], shared.at[idx_v.at[j]])

        plsc.subcore_barrier()
        off = (cid * n_sub + sid) * out_per_worker
        pltpu.sync_copy(shared.at[pl.ds(off, out_per_worker)], out_hbm.at[pl.ds(off, out_per_worker)])

    return invert_kernel(pos.reshape(rows, LANES), tokens, fill)


def _experts_kernel(te_ref, nv_ref, xs_ref, w1_ref, b1_ref, w2_ref, b2_ref, ys_ref, w1_s, w2_s, *, d_ff):
    i = pl.program_id(0)
    prev = te_ref[jnp.maximum(i - 1, 0)]
    changed = (i == 0) | (te_ref[i] != prev)

    @pl.when(changed)
    def _():
        w1_s[...] = w1_ref[0].astype(BF16)
        w2_s[...] = w2_ref[0].astype(BF16)

    nvalid = nv_ref[i]

    @pl.when(nvalid > 0)
    def _():
        tm = xs_ref.shape[0]
        groups = [slice(r, r + EXPERT_ROWS) for r in range(0, tm, EXPERT_ROWS)]
        half = xs_ref.shape[1]
        xg = []
        for rows in groups:
            row = rows.start + lax.broadcasted_iota(jnp.int32, (EXPERT_ROWS, 1), 0)
            lo, hi = _unpack_bf16_pairs(jnp.where(row < nvalid, xs_ref[rows, :], 0))
            xg.append((lo.astype(BF16), hi.astype(BF16)))
        ug = [jnp.dot(lo, w1_s[0:half, :], preferred_element_type=F32)
              + jnp.dot(hi, w1_s[half:2 * half, :], preferred_element_type=F32) + b1_ref[0] for lo, hi in xg]
        ag = []
        for u in ug:
            glu = jnp.minimum(u[:, 0:d_ff], SWIGLU_LIMIT)
            lin = jnp.clip(u[:, d_ff:2 * d_ff], -SWIGLU_LIMIT, SWIGLU_LIMIT)
            ag.append((glu * _sigmoid(SWIGLU_ALPHA * glu) * (lin + 1.0)).astype(BF16))
        for rows, act in zip(groups, ag):
            ys_ref[rows, :] = _pack_bf16_pairs(
                jnp.dot(act, w2_s[...], preferred_element_type=F32) + b2_ref[0])

    @pl.when(nvalid <= 0)
    def _():
        ys_ref[...] = jnp.zeros_like(ys_ref)


def _experts(xs, tile_expert, tile_valid, w1, b1, w2, b2):
    n_slots = xs.shape[0]
    n_exp, d, two_ff = w1.shape
    d_ff = two_ff // 2
    tm = SLOT_TILE
    n_tiles = n_slots // tm
    kern = functools.partial(_experts_kernel, d_ff=d_ff)
    grid_spec = pltpu.PrefetchScalarGridSpec(
        num_scalar_prefetch=2,
        grid=(n_tiles,),
        in_specs=[pl.BlockSpec((tm, d // 2), lambda i, te, nv: (i, 0)),
                  pl.BlockSpec((1, d, two_ff), lambda i, te, nv: (te[i], 0, 0)),
                  pl.BlockSpec((1, 1, two_ff), lambda i, te, nv: (te[i], 0, 0)),
                  pl.BlockSpec((1, d_ff, d), lambda i, te, nv: (te[i], 0, 0)),
                  pl.BlockSpec((1, 1, d), lambda i, te, nv: (te[i], 0, 0))],
        out_specs=pl.BlockSpec((tm, d // 2), lambda i, te, nv: (i, 0)),
        scratch_shapes=[pltpu.VMEM((d, two_ff), BF16), pltpu.VMEM((d_ff, d), BF16)],
    )
    return pl.pallas_call(
        kern,
        grid_spec=grid_spec,
        out_shape=jax.ShapeDtypeStruct((n_slots, d // 2), jnp.int32),
        compiler_params=pltpu.CompilerParams(
            dimension_semantics=("arbitrary",), vmem_limit_bytes=VMEM_LIMIT_BYTES),
        name="experts",
    )(tile_expert, tile_valid, xs, w1, b1.reshape(n_exp, 1, two_ff), w2, b2.reshape(n_exp, 1, d))


def _combine_kernel(h_ref, meta_ref, g_ref, *rest):
    y_refs, o_ref = rest[:TOP_K], rest[-1]
    d = h_ref.shape[1]
    half = d // 2
    acc_lo = h_ref[:, 0:half]
    acc_hi = h_ref[:, half:d]
    for kk in range(TOP_K):
        gate = meta_ref[:, TOP_K + kk:TOP_K + kk + 1]
        lo, hi = _unpack_bf16_pairs(y_refs[kk][...])
        acc_lo = acc_lo + gate * lo
        acc_hi = acc_hi + gate * hi
    ms = (jnp.sum(acc_lo * acc_lo, axis=-1, keepdims=True)
          + jnp.sum(acc_hi * acc_hi, axis=-1, keepdims=True)) * (1.0 / d)
    scale = lax.rsqrt(ms + NORM_EPS)
    o_ref[:, 0:half] = acc_lo * scale * g_ref[:, 0:half]
    o_ref[:, half:d] = acc_hi * scale * g_ref[:, half:d]


def _combine(h2, y4, meta, norm_g, out_prev, group, n_groups):
    t, d = h2.shape
    tt = min(MOVE_TOKENS, t)
    n_blk = t // tt
    prev_specs = [] if out_prev is None else [pl.BlockSpec(memory_space=pl.ANY)]
    prev_args = [] if out_prev is None else [out_prev]
    n_in = 3 + TOP_K
    y_specs = [pl.BlockSpec((tt, d // 2), functools.partial(lambda i, kk: (kk * n_blk + i, 0), kk=kk))
               for kk in range(TOP_K)]
    return pl.pallas_call(
        _combine_kernel,
        grid=(n_blk,),
        in_specs=[pl.BlockSpec((tt, d), lambda i: (i, 0)),
                  pl.BlockSpec((tt, LANES), lambda i: (i, 0)),
                  pl.BlockSpec((1, d), lambda i: (0, 0))] + y_specs + prev_specs,
        out_specs=pl.BlockSpec((tt, d), lambda i: (group * n_blk + i, 0)),
        out_shape=jax.ShapeDtypeStruct((n_groups * t, d), F32),
        input_output_aliases={} if out_prev is None else {n_in: 0},
        compiler_params=pltpu.CompilerParams(
            dimension_semantics=("arbitrary",), vmem_limit_bytes=VMEM_LIMIT_BYTES),
        name="combine",
    )(h2, meta, norm_g.reshape(1, d), *([y4] * TOP_K), *prev_args)


def _moe(h2, hp2, norm_ffn_g, router_w, router_b, w1, b1, w2, b2, norm_final_g, out_prev, group, n_groups):
    t, d = h2.shape
    n_exp = router_w.shape[1]
    tm = SLOT_TILE
    meta, metat, counts = _router(h2, norm_ffn_g, router_w, router_b)
    eidx = metat[0:TOP_K].astype(jnp.int32)
    rank = metat[2 * TOP_K:3 * TOP_K].astype(jnp.int32)
    cnt = counts[0, 0:n_exp].astype(jnp.int32)
    padded = (cnt + tm - 1) // tm * tm
    seg_end = jnp.cumsum(padded)
    seg_start = seg_end - padded
    experts = jnp.arange(n_exp, dtype=jnp.int32)[:, None, None]
    start_of = jnp.sum(jnp.where(eidx[None] == experts, seg_start[:, None, None], 0), axis=0)
    pos = (start_of + rank).reshape(-1)
    n_tiles = -(-(t * TOP_K) // tm) + n_exp
    n_slots = n_tiles * tm
    tile_start = jnp.arange(n_tiles, dtype=jnp.int32) * tm
    tile_expert = jnp.minimum(jnp.sum(tile_start[:, None] >= seg_end[None, :], axis=1), n_exp - 1).astype(jnp.int32)
    tile_valid = jnp.clip(seg_start[tile_expert] + cnt[tile_expert] - tile_start, 0, tm).astype(jnp.int32)
    slot_token = _slot_tokens(pos, n_slots)
    xs = _row_gather(hp2, slot_token)
    ys = _experts(xs, tile_expert, tile_valid, w1, b1, w2, b2)
    y4 = _row_gather(ys, pos)
    return _combine(h2, y4, meta, norm_final_g, out_prev, group, n_groups)


def kernel(x, norm_mix_g, w_in, conv_w, shift_mu, decay_w0, decay_w2, iclr_a0, iclr_a2, gate_g2, k_k, k_a,
           r_k, ln_x_g, ln_x_b, w_out, norm_ffn_g, router_w, router_b, exp_w1, exp_b1, exp_w2, exp_b2,
           norm_final_g):
    bsz, s, d = x.shape
    depth = w_in.shape[0]
    assert depth == 1, "final norm is fused into the last layer's combine kernel"
    n_groups = BATCH_GROUPS if bsz % BATCH_GROUPS == 0 else 1
    hs = _mixer(x, n_groups, norm_mix_g[0], norm_ffn_g[0], w_in[0], conv_w[0], shift_mu[0], decay_w0[0], decay_w2[0],
                iclr_a0[0], iclr_a2[0], gate_g2[0], k_k[0], k_a[0], r_k[0], ln_x_g[0], ln_x_b[0], w_out[0])
    out = None
    for g, (h, hp) in enumerate(hs):
        out = _moe(h.reshape(-1, d), hp.reshape(-1, d // 2), norm_ffn_g[0], router_w[0], router_b[0], exp_w1[0], exp_b1[0],
                   exp_w2[0], exp_b2[0], norm_final_g, out, g, n_groups)
    return out.reshape(bsz, s, d)
```

```python
import functools

import jax
import jax.numpy as jnp
from jax import lax
from jax.experimental import pallas as pl
from jax.experimental.pallas import tpu as pltpu
from jax.experimental.pallas import tpu_sc as plsc

HEAD = 64
DECAY_LORA = 64
AAA_LORA = 64
GATE_LORA = 160
TOP_K = 4
SWIGLU_ALPHA = 1.702
SWIGLU_LIMIT = 7.0
NORM_EPS = 1e-5
GN_EPS = 64e-5
DECAY_SCALE = 0.6065306597126334

LANES = 128
SUBLANES = 8
VMEM_LIMIT_BYTES = 56 * 1024 * 1024

RWKV_CHUNK = 64
MIX_TOKENS = 256
ROUTE_TOKENS = 512
MOVE_TOKENS = 256
GATHER_WINDOW = 64
SLOT_TILE = 512
EXPERT_ROWS = 256
BATCH_GROUPS = 2

BF16 = jnp.bfloat16
F32 = jnp.float32


def _dot(a, b):
    return jnp.dot(a.astype(BF16), b.astype(BF16), preferred_element_type=F32)


def _dot_nt(a, b):
    return lax.dot_general(a.astype(BF16), b.astype(BF16), (((1,), (1,)), ((), ())),
                           preferred_element_type=F32)


def _dot_tn(a, b):
    return lax.dot_general(a.astype(BF16), b.astype(BF16), (((0,), (0,)), ((), ())),
                           preferred_element_type=F32)


def _sigmoid(x):
    return 0.5 * jnp.tanh(0.5 * x) + 0.5


def _pack_bf16_pairs(x):
    n = x.shape[1] // 2
    bits = pltpu.bitcast(x.astype(BF16).astype(F32), jnp.uint32)
    packed = (bits[:, n:] & jnp.uint32(0xFFFF0000)) | (bits[:, :n] >> 16)
    return pltpu.bitcast(packed, jnp.int32)


def _unpack_bf16_pairs(w):
    bits = pltpu.bitcast(w, jnp.uint32)
    return pltpu.bitcast(bits << 16, F32), pltpu.bitcast(bits & jnp.uint32(0xFFFF0000), F32)


def _shift_rows(x, n, carry):
    rolled = pltpu.roll(x, n, axis=0)
    row = lax.broadcasted_iota(jnp.int32, x.shape, 0)
    out = rolled
    for i in range(n):
        out = jnp.where(row == i, carry[SUBLANES - n + i:SUBLANES - n + i + 1, :], out)
    return out


def _mixer_kernel(x_ref, g_ref, wmain_ref, wlora_ref, wgate_ref, cw_ref, mu_rkv_ref, mu_lora_ref,
                  w0_ref, w2_ref, a0_ref, a2_ref, g2_ref, kk_ref, ka_ref, rk_ref, lng_ref, lnb_ref,
                  wout_ref, hsel_ref, hselt_ref, gffn_ref, o_ref, hp_ref,
                  cu_s, cp_s, cl_s, ar_s, bk_s, v_s, y_s, gam_s, *state_s,
                  n_heads, d_model):
    tc = x_ref.shape[1]
    L = RWKV_CHUNK
    n_sub = tc // L
    D = d_model

    @pl.when(pl.program_id(1) == 0)
    def _():
        for st_ref in state_s:
            st_ref[...] = jnp.zeros_like(st_ref)
        cu_s[...] = jnp.zeros_like(cu_s)
        cp_s[...] = jnp.zeros_like(cp_s)
        cl_s[...] = jnp.zeros_like(cl_s)

    def split(t):
        hi = t.astype(BF16)
        return hi, (t - hi.astype(F32)).astype(BF16)

    def head_sums(parts, exact=False):
        per_head = sum(jnp.dot(p, hsel_ref[...], preferred_element_type=F32) for p in parts)
        back = split(per_head) if exact else [per_head.astype(BF16)]
        return sum(jnp.dot(p, hselt_ref[...], preferred_element_type=F32) for p in back)

    x = x_ref[0]
    ms = jnp.mean(x * x, axis=-1, keepdims=True)
    xn = (x * lax.rsqrt(ms + NORM_EPS) * g_ref[...]).astype(BF16)

    pc = jnp.dot(xn, wmain_ref[:, 0:3 * D], preferred_element_type=F32)
    u = pc[:, D:2 * D] * pc[:, 2 * D:3 * D]
    cu = cu_s[...]
    conv = (cw_ref[0:1, :] * _shift_rows(u, 2, cu) + cw_ref[1:2, :] * _shift_rows(u, 1, cu)
            + cw_ref[2:3, :] * u)
    y_conv = pc[:, 0:D] * conv
    cu_s[...] = u[tc - SUBLANES:tc, :]

    pr = jnp.dot(xn, wmain_ref[:, 3 * D:6 * D], preferred_element_type=F32)
    cp = cp_s[...]
    cp_s[...] = pr[tc - SUBLANES:tc, :]
    pr = pr + (_shift_rows(pr, 1, cp) - pr) * mu_rkv_ref[...]
    plo = jnp.dot(xn, wlora_ref[...], preferred_element_type=F32)
    cl = cl_s[...]
    cl_s[...] = plo[tc - SUBLANES:tc, :]
    plo = plo + (_shift_rows(plo, 1, cl) - plo) * mu_lora_ref[...]

    r = pr[:, 0:D]
    k = pr[:, D:2 * D]
    v = pr[:, 2 * D:3 * D]
    wd = plo[:, 0:LANES]
    ad = plo[:, LANES:2 * LANES]
    gd = plo[:, 2 * LANES:4 * LANES]

    lw = -DECAY_SCALE * _sigmoid(w0_ref[...] + _dot(jnp.tanh(wd), w2_ref[...]))
    a = _sigmoid(a0_ref[...] + _dot(ad, a2_ref[...]))
    g = _dot(_sigmoid(gd), g2_ref[...])

    row = lax.broadcasted_iota(jnp.int32, (tc, tc), 0)
    col = lax.broadcasted_iota(jnp.int32, (tc, tc), 1)
    tri = jnp.where((row >= col) & ((row // L) == (col // L)), 1.0, 0.0).astype(BF16)
    cum = sum(jnp.dot(tri, p, preferred_element_type=F32) for p in split(lw))
    e_inv = jnp.exp(-cum)

    kkraw = k * kk_ref[...]
    ss = head_sums([(kkraw * kkraw).astype(BF16)])
    kkn = kkraw * jnp.minimum(lax.rsqrt(ss), 1e12)

    k2 = k * (1.0 + (a - 1.0) * ka_ref[...])
    a_t = kkn * jnp.exp(cum - lw)
    r_t = r * jnp.exp(cum)
    b_t = kkn * a * e_inv
    k_t = k2 * e_inv
    for c in range(n_sub):
        rows = slice(c * L, (c + 1) * L)
        ar_s[c, 0:L, :] = a_t[rows]
        ar_s[c, L:2 * L, :] = r_t[rows]
        bk_s[c, 0:L, :] = b_t[rows]
        bk_s[c, L:2 * L, :] = k_t[rows]
        gam_s[c] = jnp.exp(cum[(c + 1) * L - 1:(c + 1) * L, :])
    v_s[...] = v
    bonus = head_sums([(r * k2 * rk_ref[...]).astype(BF16)]) * v

    pair = 2 * HEAD
    lane = lax.broadcasted_iota(jnp.int32, (1, pair), 1)
    left = lane < HEAD
    row1 = lax.broadcasted_iota(jnp.int32, (L, pair), 0)
    col1 = lax.broadcasted_iota(jnp.int32, (L, pair), 1) % HEAD
    strict = row1 > col1
    eye = jnp.where(row1 == col1, 1.0, 0.0)
    incl2 = (lax.broadcasted_iota(jnp.int32, (L, 2 * pair), 0)
             >= lax.broadcasted_iota(jnp.int32, (L, 2 * pair), 1) % HEAD)
    level_mask = [(row1 // (2 * s) == col1 // (2 * s)) & (row1 % (2 * s) >= s) & (col1 % (2 * s) < s)
                  for s in (1 << i for i in range(L.bit_length() - 1))]

    def blockdiag(t):
        tb = t.astype(BF16)
        return jnp.concatenate([jnp.where(left, tb, 0), jnp.where(left, 0, tb)], axis=0)

    pairs = range(n_heads // 2)
    psl = [slice(p * pair, (p + 1) * pair) for p in pairs]
    work = [(c, p) for c in range(n_sub) for p in pairs]
    ar = [ar_s[c, :, psl[p]].astype(BF16) for c, p in work]
    bk = [bk_s[c, :, psl[p]].astype(BF16) for c, p in work]
    vp = [v_s[c * L:(c + 1) * L, psl[p]] for c, p in work]
    gm = [_dot_nt(ar[i], jnp.concatenate([blockdiag(bk[i][0:L]), blockdiag(bk[i][L:2 * L])], axis=0))
          for i in range(len(work))]
    nm = [jnp.where(strict, g_[0:L, 0:pair], 0.0) for g_ in gm]
    mak = [jnp.where(strict, g_[0:L, pair:2 * pair], 0.0) for g_ in gm]
    q = [jnp.where(incl2, g_[L:2 * L, :], 0.0).astype(BF16) for g_ in gm]
    xinv = [eye - jnp.where(level_mask[0], n_, 0.0) for n_ in nm]
    for lm in level_mask[1:]:
        half = [_dot(jnp.where(lm, n_, 0.0), blockdiag(t)) for n_, t in zip(nm, xinv)]
        xinv = [t - _dot(t, blockdiag(h_)) for t, h_ in zip(xinv, half)]
    xinv = [t.astype(BF16) for t in xinv]
    mv = [_dot(m_, blockdiag(v_)) for m_, v_ in zip(mak, vp)]

    gates = _sigmoid(jnp.dot(xn, wgate_ref[...], preferred_element_type=F32))

    for c in range(n_sub):
        ids = [c * len(pairs) + p for p in pairs]
        st = [state_s[p][...] for p in pairs]
        ars = [_dot_nt(ar[i], blockdiag(st[p])) for p, i in zip(pairs, ids)]
        uu = [_dot(xinv[i], blockdiag(ars[p][0:L] + mv[i])) for p, i in zip(pairs, ids)]
        yh = [ars[p][L:2 * L]
              + _dot(q[i], jnp.concatenate([blockdiag(-uu[p]), blockdiag(vp[i])], axis=0))
              for p, i in zip(pairs, ids)]
        for p, i in zip(pairs, ids):
            upd = _dot_tn(jnp.concatenate([-uu[p], vp[i]], axis=0), bk[i])
            state_s[p][...] = (st[p] + jnp.where(left, upd[0:HEAD], upd[HEAD:pair])) * gam_s[c, :, psl[p]]
        for p in pairs:
            y_s[c * L:(c + 1) * L, psl[p]] = yh[p]

    y = y_s[...]
    yc = y - head_sums(split(y), exact=True) * (1.0 / HEAD)
    var = head_sums([(yc * yc).astype(BF16)]) * (1.0 / HEAD)
    y_rwkv = (yc * lax.rsqrt(var + GN_EPS) * lng_ref[...] + lnb_ref[...] + bonus) * g
    mix = gates[:, 0:D] * y_conv + gates[:, D:2 * D] * y_rwkv
    h = x + jnp.dot(mix.astype(BF16), wout_ref[...], preferred_element_type=F32)
    o_ref[0] = h
    hn = h * lax.rsqrt(jnp.mean(h * h, axis=-1, keepdims=True) + NORM_EPS) * gffn_ref[...]
    hp_ref[0] = _pack_bf16_pairs(hn)


def _const_spec(shape):
    nd = len(shape)
    return pl.BlockSpec(shape, lambda *_: (0,) * nd, pipeline_mode=pl.Buffered(1))


def _mixer(x, n_groups, norm_g, norm_ffn_g, w_in, conv_w, shift_mu, w0, w2, a0, a2, g2, k_k, k_a, r_k, ln_g, ln_b, w_out):
    bsz, s, d = x.shape
    gb = bsz // n_groups
    n_heads = d // HEAD
    tc = min(MIX_TOKENS, s)
    L = RWKV_CHUNK
    n_sub = tc // L
    lora0 = 6 * d
    w_main = w_in[:, 0:6 * d].astype(BF16)
    pad = lambda t, n: jnp.pad(t, ((0, 0), (0, n - t.shape[1])))
    lora_cols = (DECAY_LORA, AAA_LORA, GATE_LORA)
    lora_pads = (LANES, LANES, 2 * LANES)
    pieces_w, pieces_mu, off = [], [], lora0
    for n, p in zip(lora_cols, lora_pads):
        pieces_w.append(pad(w_in[:, off:off + n], p))
        pieces_mu.append(pad(shift_mu[None, off - 3 * d:off - 3 * d + n], p))
        off += n
    w_lora = jnp.concatenate(pieces_w, axis=1).astype(BF16)
    mu_lora = jnp.concatenate(pieces_mu, axis=1)
    w_gate = w_in[:, off:off + 2 * d].astype(BF16)
    mu_rkv = shift_mu[None, 0:3 * d]
    padr = lambda t, n: jnp.pad(t, ((0, n - t.shape[0]), (0, 0)))
    w2p = padr(w2, LANES).astype(BF16)
    a2p = padr(a2, LANES).astype(BF16)
    g2p = padr(g2, 2 * LANES).astype(BF16)
    row = lambda t: t.reshape(1, -1)
    head_of = jnp.arange(d, dtype=jnp.int32) // HEAD
    head_sel = (head_of[:, None] == jnp.arange(LANES)[None, :]).astype(BF16)
    consts = [row(norm_g), w_main, w_lora, w_gate, conv_w, mu_rkv, mu_lora, row(w0), w2p, row(a0), a2p,
              g2p, row(k_k), row(k_a), row(r_k), row(ln_g), row(ln_b), w_out.astype(BF16), head_sel, head_sel.T,
              row(norm_ffn_g)]
    kern = functools.partial(_mixer_kernel, n_heads=n_heads, d_model=d)
    call = lambda first: pl.pallas_call(
        kern,
        grid=(gb, s // tc),
        in_specs=[pl.BlockSpec((1, tc, d), lambda b, c: (b + first, c, 0))]
        + [_const_spec(t.shape) for t in consts],
        out_specs=[pl.BlockSpec((1, tc, d), lambda b, c: (b, c, 0)),
                   pl.BlockSpec((1, tc, d // 2), lambda b, c: (b, c, 0))],
        out_shape=[jax.ShapeDtypeStruct((gb, s, d), F32), jax.ShapeDtypeStruct((gb, s, d // 2), jnp.int32)],
        scratch_shapes=[
            pltpu.VMEM((SUBLANES, d), F32),
            pltpu.VMEM((SUBLANES, 3 * d), F32),
            pltpu.VMEM((SUBLANES, 4 * LANES), F32),
            pltpu.VMEM((n_sub, 2 * L, d), F32),
            pltpu.VMEM((n_sub, 2 * L, d), F32),
            pltpu.VMEM((tc, d), F32),
            pltpu.VMEM((tc, d), F32),
            pltpu.VMEM((n_sub, 1, d), F32),
        ] + [pltpu.VMEM((HEAD, 2 * HEAD), F32)] * (n_heads // 2),
        compiler_params=pltpu.CompilerParams(
            dimension_semantics=("arbitrary", "arbitrary"), vmem_limit_bytes=VMEM_LIMIT_BYTES),
        name="mixer",
    )(x, *consts)
    return [call(g * gb) for g in range(n_groups)]


def _router_kernel(h_ref, g_ref, rw_ref, rb_ref, meta_ref, metat_ref, cnt_ref, run_s, *, n_experts):
    tt = h_ref.shape[0]

    @pl.when(pl.program_id(0) == 0)
    def _():
        run_s[...] = jnp.zeros_like(run_s)

    h = h_ref[...]
    ms = jnp.mean(h * h, axis=-1, keepdims=True)
    hn = h * lax.rsqrt(ms + NORM_EPS) * g_ref[...]
    logits = jnp.dot(hn, rw_ref[...], preferred_element_type=F32,
                     precision=lax.Precision.HIGHEST) + rb_ref[...]
    lane = lax.broadcasted_iota(jnp.int32, (tt, LANES), 1)
    neg = jnp.float32(-jnp.inf)
    work = jnp.where(lane < n_experts, logits, neg)
    vals, idxs = [], []
    for _ in range(TOP_K):
        m = jnp.max(work, axis=-1, keepdims=True)
        i = jnp.min(jnp.where(work == m, lane, LANES), axis=-1, keepdims=True)
        vals.append(m)
        idxs.append(i)
        work = jnp.where(lane == i, neg, work)
    ex = [jnp.exp(vv - vals[0]) for vv in vals]
    den = ex[0] + ex[1] + ex[2] + ex[3]
    gates = [e / den for e in ex]

    onehot = jnp.zeros((tt, LANES), jnp.bool_)
    for kk in range(TOP_K):
        onehot = onehot | (lane == (idxs[kk] + kk * n_experts))
    oh = jnp.where(onehot, 1.0, 0.0)
    ri = lax.broadcasted_iota(jnp.int32, (tt, tt), 0)
    ci = lax.broadcasted_iota(jnp.int32, (tt, tt), 1)
    tri = jnp.where(ri > ci, 1.0, 0.0).astype(BF16)
    cnt = jnp.dot(tri, oh.astype(BF16), preferred_element_type=F32)
    tot = jnp.broadcast_to(jnp.sum(oh, axis=0, keepdims=True), (SUBLANES, LANES))
    lane8 = lax.broadcasted_iota(jnp.int32, (SUBLANES, LANES), 1)
    pk = jnp.zeros_like(tot)
    te = tot
    for j in range(1, TOP_K):
        rolled = pltpu.roll(tot, j * n_experts, axis=1)
        pk = pk + jnp.where(lane8 >= j * n_experts, rolled, 0.0)
        te = te + rolled
    before = cnt + (run_s[...] + pk)[0:1, :]
    ranks = [jnp.sum(jnp.where(onehot & (lane // n_experts == kk), before, 0.0), axis=-1, keepdims=True)
             for kk in range(TOP_K)]
    run_s[...] = run_s[...] + te
    cnt_ref[...] = run_s[...]

    meta = jnp.zeros((tt, LANES), F32)
    for kk in range(TOP_K):
        meta = jnp.where(lane == kk, idxs[kk].astype(F32), meta)
        meta = jnp.where(lane == TOP_K + kk, gates[kk], meta)
        meta = jnp.where(lane == 2 * TOP_K + kk, ranks[kk], meta)
    meta_ref[...] = meta
    metat_ref[...] = meta.T[0:2 * SUBLANES, :]


def _router(h2, norm_g, router_w, router_b):
    t, d = h2.shape
    n_experts = router_w.shape[1]
    tt = min(ROUTE_TOKENS, t)
    rw = jnp.pad(router_w, ((0, 0), (0, LANES - n_experts)))
    rb = jnp.pad(router_b, (0, LANES - n_experts)).reshape(1, LANES)
    kern = functools.partial(_router_kernel, n_experts=n_experts)
    return pl.pallas_call(
        kern,
        grid=(t // tt,),
        in_specs=[pl.BlockSpec((tt, d), lambda i: (i, 0)),
                  pl.BlockSpec((1, d), lambda i: (0, 0)),
                  pl.BlockSpec((d, LANES), lambda i: (0, 0)),
                  pl.BlockSpec((1, LANES), lambda i: (0, 0))],
        out_specs=[pl.BlockSpec((tt, LANES), lambda i: (i, 0)),
                   pl.BlockSpec((2 * SUBLANES, tt), lambda i: (0, i)),
                   pl.BlockSpec((SUBLANES, LANES), lambda i: (0, 0))],
        out_shape=[jax.ShapeDtypeStruct((t, LANES), F32),
                   jax.ShapeDtypeStruct((2 * SUBLANES, t), F32),
                   jax.ShapeDtypeStruct((SUBLANES, LANES), F32)],
        scratch_shapes=[pltpu.VMEM((SUBLANES, LANES), F32)],
        compiler_params=pltpu.CompilerParams(
            dimension_semantics=("arbitrary",), vmem_limit_bytes=VMEM_LIMIT_BYTES),
        name="router",
    )(h2, norm_g.reshape(1, d), rw, rb)


def _row_gather(table, idx):
    n_idx = idx.shape[0]
    d = table.shape[1]
    w = GATHER_WINDOW
    sc = plsc.get_sparse_core_info()
    n_workers = sc.num_cores * sc.num_subcores
    per_worker = n_idx // n_workers
    n_win = per_worker // w
    assert per_worker * n_workers == n_idx and n_win * w == per_worker and n_win % 2 == 0
    mesh = plsc.VectorSubcoreMesh(core_axis_name="core", subcore_axis_name="subcore")

    @functools.partial(
        pl.kernel, out_type=jax.ShapeDtypeStruct((n_idx, d), table.dtype), mesh=mesh,
        scratch_types=[pltpu.VMEM((per_worker,), jnp.int32),
                       pltpu.VMEM((w, d), table.dtype), pltpu.VMEM((w, d), table.dtype),
                       pltpu.SemaphoreType.DMA, pltpu.SemaphoreType.DMA,
                       pltpu.SemaphoreType.DMA, pltpu.SemaphoreType.DMA])
    def gather_kernel(table_hbm, idx_hbm, out_hbm, idx_v, buf_a, buf_b, gsem_a, gsem_b, psem_a, psem_b):
        worker = lax.axis_index("subcore") * sc.num_cores + lax.axis_index("core")
        base = worker * per_worker
        pltpu.sync_copy(idx_hbm.at[pl.ds(base, per_worker)], idx_v)

        def gather(j, buf, sem):
            return pltpu.make_async_copy(table_hbm.at[idx_v.at[pl.ds(j * w, w)]], buf, sem)

        def put(j, buf, sem):
            return pltpu.make_async_copy(buf, out_hbm.at[pl.ds(base + j * w, w)], sem)

        gather(0, buf_a, gsem_a).start()

        @pl.loop(0, n_win, step=2)
        def _(j):
            gather(j, buf_a, gsem_a).wait()

            @pl.when(j > 0)
            def _():
                put(j - 1, buf_b, psem_b).wait()

            gather(j + 1, buf_b, gsem_b).start()
            put(j, buf_a, psem_a).start()
            gather(j + 1, buf_b, gsem_b).wait()
            put(j, buf_a, psem_a).wait()

            @pl.when(j + 2 < n_win)
            def _():
                gather(j + 2, buf_a, gsem_a).start()

            put(j + 1, buf_b, psem_b).start()

        put(n_win - 1, buf_b, psem_b).wait()

    return gather_kernel(table, idx)


def _slot_tokens(pos, n_slots):
    n = pos.shape[0]
    sc = plsc.get_sparse_core_info()
    n_cores, n_sub = sc.num_cores, sc.num_subcores
    rows = n // LANES
    rows_per_tile = rows // n_sub
    out_per_worker = n_slots // (n_cores * n_sub)
    init_per_tile = n_slots // n_sub
    assert rows_per_tile * n_sub * LANES == n and out_per_worker * n_cores * n_sub == n_slots
    mesh = plsc.VectorSubcoreMesh(core_axis_name="core", subcore_axis_name="subcore")
    tokens = (jnp.arange(n, dtype=jnp.int32) % (n // TOP_K)).reshape(rows, LANES)
    fill = jnp.arange(n_slots, dtype=jnp.int32) % (n // TOP_K)

    @functools.partial(
        pl.kernel, out_type=jax.ShapeDtypeStruct((n_slots,), jnp.int32), mesh=mesh,
        scratch_types=[pltpu.VMEM((rows_per_tile, LANES), jnp.int32),
                       pltpu.VMEM((rows_per_tile, LANES), jnp.int32),
                       pltpu.VMEM_SHARED((n_slots,), jnp.int32)])
    def invert_kernel(pos_hbm, tok_hbm, fill_hbm, out_hbm, idx_v, val_v, shared):
        cid = lax.axis_index("core")
        sid = lax.axis_index("subcore")
        pltpu.sync_copy(fill_hbm.at[pl.ds(sid * init_per_tile, init_per_tile)],
                        shared.at[pl.ds(sid * init_per_tile, init_per_tile)])
        pltpu.sync_copy(pos_hbm.at[pl.ds(sid * rows_per_tile, rows_per_tile)], idx_v)
        pltpu.sync_copy(tok_hbm.at[pl.ds(sid * rows_per_tile, rows_per_tile)], val_v)
        plsc.subcore_barrier()

        @pl.loop(0, rows_per_tile)
        def _(j):
            pltpu.sync_copy(val_v.at[j], shared.at[idx_v.at[j]])

        plsc.subcore_barrier()
        off = (cid * n_sub + sid) * out_per_worker
        pltpu.sync_copy(shared.at[pl.ds(off, out_per_worker)], out_hbm.at[pl.ds(off, out_per_worker)])

    return invert_kernel(pos.reshape(rows, LANES), tokens, fill)


def _experts_kernel(te_ref, nv_ref, xs_ref, w1_ref, b1_ref, w2_ref, b2_ref, ys_ref, w1_s, w2_s, *, d_ff):
    i = pl.program_id(0)
    prev = te_ref[jnp.maximum(i - 1, 0)]
    changed = (i == 0) | (te_ref[i] != prev)

    @pl.when(changed)
    def _():
        w1_s[...] = w1_ref[0].astype(BF16)
        w2_s[...] = w2_ref[0].astype(BF16)

    nvalid = nv_ref[i]

    @pl.when(nvalid > 0)
    def _():
        tm = xs_ref.shape[0]
        groups = [slice(r, r + EXPERT_ROWS) for r in range(0, tm, EXPERT_ROWS)]
        half = xs_ref.shape[1]
        xg = []
        for rows in groups:
            row = rows.start + lax.broadcasted_iota(jnp.int32, (EXPERT_ROWS, 1), 0)
            lo, hi = _unpack_bf16_pairs(jnp.where(row < nvalid, xs_ref[rows, :], 0))
            xg.append((lo.astype(BF16), hi.astype(BF16)))
        ug = [jnp.dot(lo, w1_s[0:half, :], preferred_element_type=F32)
              + jnp.dot(hi, w1_s[half:2 * half, :], preferred_element_type=F32) + b1_ref[0] for lo, hi in xg]
        ag = []
        for u in ug:
            glu = jnp.minimum(u[:, 0:d_ff], SWIGLU_LIMIT)
            lin = jnp.clip(u[:, d_ff:2 * d_ff], -SWIGLU_LIMIT, SWIGLU_LIMIT)
            ag.append((glu * _sigmoid(SWIGLU_ALPHA * glu) * (lin + 1.0)).astype(BF16))
        for rows, act in zip(groups, ag):
            ys_ref[rows, :] = _pack_bf16_pairs(
                jnp.dot(act, w2_s[...], preferred_element_type=F32) + b2_ref[0])

    @pl.when(nvalid <= 0)
    def _():
        ys_ref[...] = jnp.zeros_like(ys_ref)


def _experts(xs, tile_expert, tile_valid, w1, b1, w2, b2):
    n_slots = xs.shape[0]
    n_exp, d, two_ff = w1.shape
    d_ff = two_ff // 2
    tm = SLOT_TILE
    n_tiles = n_slots // tm
    kern = functools.partial(_experts_kernel, d_ff=d_ff)
    grid_spec = pltpu.PrefetchScalarGridSpec(
        num_scalar_prefetch=2,
        grid=(n_tiles,),
        in_specs=[pl.BlockSpec((tm, d // 2), lambda i, te, nv: (i, 0)),
                  pl.BlockSpec((1, d, two_ff), lambda i, te, nv: (te[i], 0, 0)),
                  pl.BlockSpec((1, 1, two_ff), lambda i, te, nv: (te[i], 0, 0)),
                  pl.BlockSpec((1, d_ff, d), lambda i, te, nv: (te[i], 0, 0)),
                  pl.BlockSpec((1, 1, d), lambda i, te, nv: (te[i], 0, 0))],
        out_specs=pl.BlockSpec((tm, d // 2), lambda i, te, nv: (i, 0)),
        scratch_shapes=[pltpu.VMEM((d, two_ff), BF16), pltpu.VMEM((d_ff, d), BF16)],
    )
    return pl.pallas_call(
        kern,
        grid_spec=grid_spec,
        out_shape=jax.ShapeDtypeStruct((n_slots, d // 2), jnp.int32),
        compiler_params=pltpu.CompilerParams(
            dimension_semantics=("arbitrary",), vmem_limit_bytes=VMEM_LIMIT_BYTES),
        name="experts",
    )(tile_expert, tile_valid, xs, w1, b1.reshape(n_exp, 1, two_ff), w2, b2.reshape(n_exp, 1, d))


def _combine_kernel(h_ref, meta_ref, g_ref, *rest):
    y_refs, o_ref = rest[:TOP_K], rest[-1]
    d = h_ref.shape[1]
    half = d // 2
    acc_lo = h_ref[:, 0:half]
    acc_hi = h_ref[:, half:d]
    for kk in range(TOP_K):
        gate = meta_ref[:, TOP_K + kk:TOP_K + kk + 1]
        lo, hi = _unpack_bf16_pairs(y_refs[kk][...])
        acc_lo = acc_lo + gate * lo
        acc_hi = acc_hi + gate * hi
    ms = (jnp.sum(acc_lo * acc_lo, axis=-1, keepdims=True)
          + jnp.sum(acc_hi * acc_hi, axis=-1, keepdims=True)) * (1.0 / d)
    scale = lax.rsqrt(ms + NORM_EPS)
    o_ref[:, 0:half] = acc_lo * scale * g_ref[:, 0:half]
    o_ref[:, half:d] = acc_hi * scale * g_ref[:, half:d]


def _combine(h2, y4, meta, norm_g, out_prev, group, n_groups):
    t, d = h2.shape
    tt = min(MOVE_TOKENS, t)
    n_blk = t // tt
    prev_specs = [] if out_prev is None else [pl.BlockSpec(memory_space=pl.ANY)]
    prev_args = [] if out_prev is None else [out_prev]
    n_in = 3 + TOP_K
    y_specs = [pl.BlockSpec((tt, d // 2), functools.partial(lambda i, kk: (kk * n_blk + i, 0), kk=kk))
               for kk in range(TOP_K)]
    return pl.pallas_call(
        _combine_kernel,
        grid=(n_blk,),
        in_specs=[pl.BlockSpec((tt, d), lambda i: (i, 0)),
                  pl.BlockSpec((tt, LANES), lambda i: (i, 0)),
                  pl.BlockSpec((1, d), lambda i: (0, 0))] + y_specs + prev_specs,
        out_specs=pl.BlockSpec((tt, d), lambda i: (group * n_blk + i, 0)),
        out_shape=jax.ShapeDtypeStruct((n_groups * t, d), F32),
        input_output_aliases={} if out_prev is None else {n_in: 0},
        compiler_params=pltpu.CompilerParams(
            dimension_semantics=("arbitrary",), vmem_limit_bytes=VMEM_LIMIT_BYTES),
        name="combine",
    )(h2, meta, norm_g.reshape(1, d), *([y4] * TOP_K), *prev_args)


def _moe(h2, hp2, norm_ffn_g, router_w, router_b, w1, b1, w2, b2, norm_final_g, out_prev, group, n_groups):
    t, d = h2.shape
    n_exp = router_w.shape[1]
    tm = SLOT_TILE
    meta, metat, counts = _router(h2, norm_ffn_g, router_w, router_b)
    eidx = metat[0:TOP_K].astype(jnp.int32)
    rank = metat[2 * TOP_K:3 * TOP_K].astype(jnp.int32)
    cnt = counts[0, 0:n_exp].astype(jnp.int32)
    padded = (cnt + tm - 1) // tm * tm
    seg_end = jnp.cumsum(padded)
    seg_start = seg_end - padded
    experts = jnp.arange(n_exp, dtype=jnp.int32)[:, None, None]
    start_of = jnp.sum(jnp.where(eidx[None] == experts, seg_start[:, None, None], 0), axis=0)
    pos = (start_of + rank).reshape(-1)
    n_tiles = -(-(t * TOP_K) // tm) + n_exp
    n_slots = n_tiles * tm
    tile_start = jnp.arange(n_tiles, dtype=jnp.int32) * tm
    tile_expert = jnp.minimum(jnp.sum(tile_start[:, None] >= seg_end[None, :], axis=1), n_exp - 1).astype(jnp.int32)
    tile_valid = jnp.clip(seg_start[tile_expert] + cnt[tile_expert] - tile_start, 0, tm).astype(jnp.int32)
    slot_token = _slot_tokens(pos, n_slots)
    xs = _row_gather(hp2, slot_token)
    ys = _experts(xs, tile_expert, tile_valid, w1, b1, w2, b2)
    y4 = _row_gather(ys, pos)
    return _combine(h2, y4, meta, norm_final_g, out_prev, group, n_groups)


def kernel(x, norm_mix_g, w_in, conv_w, shift_mu, decay_w0, decay_w2, iclr_a0, iclr_a2, gate_g2, k_k, k_a,
           r_k, ln_x_g, ln_x_b, w_out, norm_ffn_g, router_w, router_b, exp_w1, exp_b1, exp_w2, exp_b2,
           norm_final_g):
    bsz, s, d = x.shape
    depth = w_in.shape[0]
    assert depth == 1, "final norm is fused into the last layer's combine kernel"
    n_groups = BATCH_GROUPS if bsz % BATCH_GROUPS == 0 else 1
    hs = _mixer(x, n_groups, norm_mix_g[0], norm_ffn_g[0], w_in[0], conv_w[0], shift_mu[0], decay_w0[0], decay_w2[0],
                iclr_a0[0], iclr_a2[0], gate_g2[0], k_k[0], k_a[0], r_k[0], ln_x_g[0], ln_x_b[0], w_out[0])
    out = None
    for g, (h, hp) in enumerate(hs):
        out = _moe(h.reshape(-1, d), hp.reshape(-1, d // 2), norm_ffn_g[0], router_w[0], router_b[0], exp_w1[0], exp_b1[0],
                   exp_w2[0], exp_b2[0], norm_final_g, out, g, n_groups)
    return out.reshape(bsz, s, d)
```

```python
import functools

import jax
import jax.numpy as jnp
from jax import lax
from jax.experimental import pallas as pl
from jax.experimental.pallas import tpu as pltpu
from jax.experimental.pallas import tpu_sc as plsc

HEAD = 64
DECAY_LORA = 64
AAA_LORA = 64
GATE_LORA = 160
TOP_K = 4
SWIGLU_ALPHA = 1.702
SWIGLU_LIMIT = 7.0
NORM_EPS = 1e-5
GN_EPS = 64e-5
DECAY_SCALE = 0.6065306597126334

LANES = 128
SUBLANES = 8
VMEM_LIMIT_BYTES = 56 * 1024 * 1024

RWKV_CHUNK = 64
MIX_TOKENS = 256
ROUTE_TOKENS = 512
MOVE_TOKENS = 256
GATHER_WINDOW = 64
SLOT_TILE = 512
EXPERT_ROWS = 256
BATCH_GROUPS = 2

BF16 = jnp.bfloat16
F32 = jnp.float32


def _dot(a, b):
    return jnp.dot(a.astype(BF16), b.astype(BF16), preferred_element_type=F32)


def _dot_nt(a, b):
    return lax.dot_general(a.astype(BF16), b.astype(BF16), (((1,), (1,)), ((), ())),
                           preferred_element_type=F32)


def _dot_tn(a, b):
    return lax.dot_general(a.astype(BF16), b.astype(BF16), (((0,), (0,)), ((), ())),
                           preferred_element_type=F32)


def _sigmoid(x):
    return 0.5 * jnp.tanh(0.5 * x) + 0.5


def _pack_bf16_pairs(x):
    n = x.shape[1] // 2
    bits = pltpu.bitcast(x.astype(BF16).astype(F32), jnp.uint32)
    packed = (bits[:, n:] & jnp.uint32(0xFFFF0000)) | (bits[:, :n] >> 16)
    return pltpu.bitcast(packed, jnp.int32)


def _unpack_bf16_pairs(w):
    bits = pltpu.bitcast(w, jnp.uint32)
    return pltpu.bitcast(bits << 16, F32), pltpu.bitcast(bits & jnp.uint32(0xFFFF0000), F32)


def _shift_rows(x, n, carry):
    rolled = pltpu.roll(x, n, axis=0)
    row = lax.broadcasted_iota(jnp.int32, x.shape, 0)
    out = rolled
    for i in range(n):
        out = jnp.where(row == i, carry[SUBLANES - n + i:SUBLANES - n + i + 1, :], out)
    return out


def _mixer_kernel(x_ref, g_ref, wmain_ref, wlora_ref, wgate_ref, cw_ref, mu_rkv_ref, mu_lora_ref,
                  w0_ref, w2_ref, a0_ref, a2_ref, g2_ref, kk_ref, ka_ref, rk_ref, lng_ref, lnb_ref,
                  wout_ref, hsel_ref, hselt_ref, gffn_ref, o_ref, hp_ref,
                  cu_s, cp_s, cl_s, ar_s, bk_s, v_s, y_s, gam_s, *state_s,
                  n_heads, d_model):
    tc = x_ref.shape[1]
    L = RWKV_CHUNK
    n_sub = tc // L
    D = d_model

    @pl.when(pl.program_id(1) == 0)
    def _():
        for st_ref in state_s:
            st_ref[...] = jnp.zeros_like(st_ref)
        cu_s[...] = jnp.zeros_like(cu_s)
        cp_s[...] = jnp.zeros_like(cp_s)
        cl_s[...] = jnp.zeros_like(cl_s)

    def split(t):
        hi = t.astype(BF16)
        return hi, (t - hi.astype(F32)).astype(BF16)

    def head_sums(parts, exact=False):
        per_head = sum(jnp.dot(p, hsel_ref[...], preferred_element_type=F32) for p in parts)
        back = split(per_head) if exact else [per_head.astype(BF16)]
        return sum(jnp.dot(p, hselt_ref[...], preferred_element_type=F32) for p in back)

    x = x_ref[0]
    ms = jnp.mean(x * x, axis=-1, keepdims=True)
    xn = (x * lax.rsqrt(ms + NORM_EPS) * g_ref[...]).astype(BF16)

    pc = jnp.dot(xn, wmain_ref[:, 0:3 * D], preferred_element_type=F32)
    u = pc[:, D:2 * D] * pc[:, 2 * D:3 * D]
    cu = cu_s[...]
    conv = (cw_ref[0:1, :] * _shift_rows(u, 2, cu) + cw_ref[1:2, :] * _shift_rows(u, 1, cu)
            + cw_ref[2:3, :] * u)
    y_conv = pc[:, 0:D] * conv
    cu_s[...] = u[tc - SUBLANES:tc, :]

    pr = jnp.dot(xn, wmain_ref[:, 3 * D:6 * D], preferred_element_type=F32)
    cp = cp_s[...]
    cp_s[...] = pr[tc - SUBLANES:tc, :]
    pr = pr + (_shift_rows(pr, 1, cp) - pr) * mu_rkv_ref[...]
    plo = jnp.dot(xn, wlora_ref[...], preferred_element_type=F32)
    cl = cl_s[...]
    cl_s[...] = plo[tc - SUBLANES:tc, :]
    plo = plo + (_shift_rows(plo, 1, cl) - plo) * mu_lora_ref[...]

    r = pr[:, 0:D]
    k = pr[:, D:2 * D]
    v = pr[:, 2 * D:3 * D]
    wd = plo[:, 0:LANES]
    ad = plo[:, LANES:2 * LANES]
    gd = plo[:, 2 * LANES:4 * LANES]

    lw = -DECAY_SCALE * _sigmoid(w0_ref[...] + _dot(jnp.tanh(wd), w2_ref[...]))
    a = _sigmoid(a0_ref[...] + _dot(ad, a2_ref[...]))
    g = _dot(_sigmoid(gd), g2_ref[...])

    row = lax.broadcasted_iota(jnp.int32, (tc, tc), 0)
    col = lax.broadcasted_iota(jnp.int32, (tc, tc), 1)
    tri = jnp.where((row >= col) & ((row // L) == (col // L)), 1.0, 0.0).astype(BF16)
    cum = sum(jnp.dot(tri, p, preferred_element_type=F32) for p in split(lw))
    e_inv = jnp.exp(-cum)

    kkraw = k * kk_ref[...]
    ss = head_sums([(kkraw * kkraw).astype(BF16)])
    kkn = kkraw * jnp.minimum(lax.rsqrt(ss), 1e12)

    k2 = k * (1.0 + (a - 1.0) * ka_ref[...])
    a_t = kkn * jnp.exp(cum - lw)
    r_t = r * jnp.exp(cum)
    b_t = kkn * a * e_inv
    k_t = k2 * e_inv
    for c in range(n_sub):
        rows = slice(c * L, (c + 1) * L)
        ar_s[c, 0:L, :] = a_t[rows]
        ar_s[c, L:2 * L, :] = r_t[rows]
        bk_s[c, 0:L, :] = b_t[rows]
        bk_s[c, L:2 * L, :] = k_t[rows]
        gam_s[c] = jnp.exp(cum[(c + 1) * L - 1:(c + 1) * L, :])
    v_s[...] = v
    bonus = head_sums([(r * k2 * rk_ref[...]).astype(BF16)]) * v

    pair = 2 * HEAD
    lane = lax.broadcasted_iota(jnp.int32, (1, pair), 1)
    left = lane < HEAD
    row1 = lax.broadcasted_iota(jnp.int32, (L, pair), 0)
    col1 = lax.broadcasted_iota(jnp.int32, (L, pair), 1) % HEAD
    strict = row1 > col1
    eye = jnp.where(row1 == col1, 1.0, 0.0)
    incl2 = (lax.broadcasted_iota(jnp.int32, (L, 2 * pair), 0)
             >= lax.broadcasted_iota(jnp.int32, (L, 2 * pair), 1) % HEAD)
    level_mask = [(row1 // (2 * s) == col1 // (2 * s)) & (row1 % (2 * s) >= s) & (col1 % (2 * s) < s)
                  for s in (1 << i for i in range(L.bit_length() - 1))]

    def blockdiag(t):
        tb = t.astype(BF16)
        return jnp.concatenate([jnp.where(left, tb, 0), jnp.where(left, 0, tb)], axis=0)

    pairs = range(n_heads // 2)
    psl = [slice(p * pair, (p + 1) * pair) for p in pairs]
    work = [(c, p) for c in range(n_sub) for p in pairs]
    ar = [ar_s[c, :, psl[p]].astype(BF16) for c, p in work]
    bk = [bk_s[c, :, psl[p]].astype(BF16) for c, p in work]
    vp = [v_s[c * L:(c + 1) * L, psl[p]] for c, p in work]
    gm = [_dot_nt(ar[i], jnp.concatenate([blockdiag(bk[i][0:L]), blockdiag(bk[i][L:2 * L])], axis=0))
          for i in range(len(work))]
    nm = [jnp.where(strict, g_[0:L, 0:pair], 0.0) for g_ in gm]
    mak = [jnp.where(strict, g_[0:L, pair:2 * pair], 0.0) for g_ in gm]
    q = [jnp.where(incl2, g_[L:2 * L, :], 0.0).astype(BF16) for g_ in gm]
    xinv = [eye - jnp.where(level_mask[0], n_, 0.0) for n_ in nm]
    for lm in level_mask[1:]:
        half = [_dot(jnp.where(lm, n_, 0.0), blockdiag(t)) for n_, t in zip(nm, xinv)]
        xinv = [t - _dot(t, blockdiag(h_)) for t, h_ in zip(xinv, half)]
    xinv = [t.astype(BF16) for t in xinv]
    mv = [_dot(m_, blockdiag(v_)) for m_, v_ in zip(mak, vp)]

    gates = _sigmoid(jnp.dot(xn, wgate_ref[...], preferred_element_type=F32))

    for c in range(n_sub):
        ids = [c * len(pairs) + p for p in pairs]
        st = [state_s[p][...] for p in pairs]
        ars = [_dot_nt(ar[i], blockdiag(st[p])) for p, i in zip(pairs, ids)]
        uu = [_dot(xinv[i], blockdiag(ars[p][0:L] + mv[i])) for p, i in zip(pairs, ids)]
        yh = [ars[p][L:2 * L]
              + _dot(q[i], jnp.concatenate([blockdiag(-uu[p]), blockdiag(vp[i])], axis=0))
              for p, i in zip(pairs, ids)]
        for p, i in zip(pairs, ids):
            upd = _dot_tn(jnp.concatenate([-uu[p], vp[i]], axis=0), bk[i])
            state_s[p][...] = (st[p] + jnp.where(left, upd[0:HEAD], upd[HEAD:pair])) * gam_s[c, :, psl[p]]
        for p in pairs:
            y_s[c * L:(c + 1) * L, psl[p]] = yh[p]

    y = y_s[...]
    yc = y - head_sums(split(y), exact=True) * (1.0 / HEAD)
    var = head_sums([(yc * yc).astype(BF16)]) * (1.0 / HEAD)
    y_rwkv = (yc * lax.rsqrt(var + GN_EPS) * lng_ref[...] + lnb_ref[...] + bonus) * g
    mix = gates[:, 0:D] * y_conv + gates[:, D:2 * D] * y_rwkv
    h = x + jnp.dot(mix.astype(BF16), wout_ref[...], preferred_element_type=F32)
    o_ref[0] = h
    hn = h * lax.rsqrt(jnp.mean(h * h, axis=-1, keepdims=True) + NORM_EPS) * gffn_ref[...]
    hp_ref[0] = _pack_bf16_pairs(hn)


def _const_spec(shape):
    nd = len(shape)
    return pl.BlockSpec(shape, lambda *_: (0,) * nd, pipeline_mode=pl.Buffered(1))


def _mixer(x, n_groups, norm_g, norm_ffn_g, w_in, conv_w, shift_mu, w0, w2, a0, a2, g2, k_k, k_a, r_k, ln_g, ln_b, w_out):
    bsz, s, d = x.shape
    gb = bsz // n_groups
    n_heads = d // HEAD
    tc = min(MIX_TOKENS, s)
    L = RWKV_CHUNK
    n_sub = tc // L
    lora0 = 6 * d
    w_main = w_in[:, 0:6 * d].astype(BF16)
    pad = lambda t, n: jnp.pad(t, ((0, 0), (0, n - t.shape[1])))
    lora_cols = (DECAY_LORA, AAA_LORA, GATE_LORA)
    lora_pads = (LANES, LANES, 2 * LANES)
    pieces_w, pieces_mu, off = [], [], lora0
    for n, p in zip(lora_cols, lora_pads):
        pieces_w.append(pad(w_in[:, off:off + n], p))
        pieces_mu.append(pad(shift_mu[None, off - 3 * d:off - 3 * d + n], p))
        off += n
    w_lora = jnp.concatenate(pieces_w, axis=1).astype(BF16)
    mu_lora = jnp.concatenate(pieces_mu, axis=1)
    w_gate = w_in[:, off:off + 2 * d].astype(BF16)
    mu_rkv = shift_mu[None, 0:3 * d]
    padr = lambda t, n: jnp.pad(t, ((0, n - t.shape[0]), (0, 0)))
    w2p = padr(w2, LANES).astype(BF16)
    a2p = padr(a2, LANES).astype(BF16)
    g2p = padr(g2, 2 * LANES).astype(BF16)
    row = lambda t: t.reshape(1, -1)
    head_of = jnp.arange(d, dtype=jnp.int32) // HEAD
    head_sel = (head_of[:, None] == jnp.arange(LANES)[None, :]).astype(BF16)
    consts = [row(norm_g), w_main, w_lora, w_gate, conv_w, mu_rkv, mu_lora, row(w0), w2p, row(a0), a2p,
              g2p, row(k_k), row(k_a), row(r_k), row(ln_g), row(ln_b), w_out.astype(BF16), head_sel, head_sel.T,
              row(norm_ffn_g)]
    kern = functools.partial(_mixer_kernel, n_heads=n_heads, d_model=d)
    call = lambda first: pl.pallas_call(
        kern,
        grid=(gb, s // tc),
        in_specs=[pl.BlockSpec((1, tc, d), lambda b, c: (b + first, c, 0))]
        + [_const_spec(t.shape) for t in consts],
        out_specs=[pl.BlockSpec((1, tc, d), lambda b, c: (b, c, 0)),
                   pl.BlockSpec((1, tc, d // 2), lambda b, c: (b, c, 0))],
        out_shape=[jax.ShapeDtypeStruct((gb, s, d), F32), jax.ShapeDtypeStruct((gb, s, d // 2), jnp.int32)],
        scratch_shapes=[
            pltpu.VMEM((SUBLANES, d), F32),
            pltpu.VMEM((SUBLANES, 3 * d), F32),
            pltpu.VMEM((SUBLANES, 4 * LANES), F32),
            pltpu.VMEM((n_sub, 2 * L, d), F32),
            pltpu.VMEM((n_sub, 2 * L, d), F32),
            pltpu.VMEM((tc, d), F32),
            pltpu.VMEM((tc, d), F32),
            pltpu.VMEM((n_sub, 1, d), F32),
        ] + [pltpu.VMEM((HEAD, 2 * HEAD), F32)] * (n_heads // 2),
        compiler_params=pltpu.CompilerParams(
            dimension_semantics=("arbitrary", "arbitrary"), vmem_limit_bytes=VMEM_LIMIT_BYTES),
        name="mixer",
    )(x, *consts)
    return [call(g * gb) for g in range(n_groups)]


def _router_kernel(h_ref, g_ref, rwh_ref, rwl_ref, rb_ref, tri_ref, meta_ref, metat_ref, cnt_ref, run_s, *,
                   n_experts):
    tt = h_ref.shape[0]

    @pl.when(pl.program_id(0) == 0)
    def _():
        run_s[...] = jnp.zeros_like(run_s)

    h = h_ref[...]
    ms = jnp.mean(h * h, axis=-1, keepdims=True)
    hn = h * lax.rsqrt(ms + NORM_EPS) * g_ref[...]
    hn_hi = hn.astype(BF16)
    hn_lo = (hn - hn_hi.astype(F32)).astype(BF16)
    logits = (jnp.dot(hn_hi, rwh_ref[...], preferred_element_type=F32)
              + jnp.dot(hn_lo, rwh_ref[...], preferred_element_type=F32)
              + jnp.dot(hn_hi, rwl_ref[...], preferred_element_type=F32)) + rb_ref[...]
    lane = lax.broadcasted_iota(jnp.int32, (tt, LANES), 1)
    neg = jnp.float32(-jnp.inf)
    work = jnp.where(lane < n_experts, logits, neg)
    vals, idxs = [], []
    for _ in range(TOP_K):
        m = jnp.max(work, axis=-1, keepdims=True)
        i = jnp.min(jnp.where(work == m, lane, LANES), axis=-1, keepdims=True)
        vals.append(m)
        idxs.append(i)
        work = jnp.where(lane == i, neg, work)
    ex = [jnp.exp(vv - vals[0]) for vv in vals]
    den = ex[0] + ex[1] + ex[2] + ex[3]
    gates = [e / den for e in ex]

    onehot = jnp.zeros((tt, LANES), jnp.bool_)
    for kk in range(TOP_K):
        onehot = onehot | (lane == (idxs[kk] + kk * n_experts))
    oh = jnp.where(onehot, 1.0, 0.0)
    cnt = jnp.dot(tri_ref[...], oh.astype(BF16), preferred_element_type=F32)
    tot = jnp.broadcast_to(jnp.sum(oh, axis=0, keepdims=True), (SUBLANES, LANES))
    lane8 = lax.broadcasted_iota(jnp.int32, (SUBLANES, LANES), 1)
    pk = jnp.zeros_like(tot)
    te = tot
    for j in range(1, TOP_K):
        rolled = pltpu.roll(tot, j * n_experts, axis=1)
        pk = pk + jnp.where(lane8 >= j * n_experts, rolled, 0.0)
        te = te + rolled
    before = cnt + (run_s[...] + pk)[0:1, :]
    ranks = [jnp.sum(jnp.where(onehot & (lane // n_experts == kk), before, 0.0), axis=-1, keepdims=True)
             for kk in range(TOP_K)]
    run_s[...] = run_s[...] + te
    cnt_ref[...] = run_s[...]

    meta = jnp.zeros((tt, LANES), F32)
    for kk in range(TOP_K):
        meta = jnp.where(lane == kk, idxs[kk].astype(F32), meta)
        meta = jnp.where(lane == TOP_K + kk, gates[kk], meta)
        meta = jnp.where(lane == 2 * TOP_K + kk, ranks[kk], meta)
    meta_ref[...] = meta
    metat_ref[...] = meta.T[0:2 * SUBLANES, :]


def _router(h2, norm_g, router_w, router_b):
    t, d = h2.shape
    n_experts = router_w.shape[1]
    tt = min(ROUTE_TOKENS, t)
    rw = jnp.pad(router_w, ((0, 0), (0, LANES - n_experts)))
    rw_hi = rw.astype(BF16)
    rw_lo = (rw - rw_hi.astype(F32)).astype(BF16)
    rb = jnp.pad(router_b, (0, LANES - n_experts)).reshape(1, LANES)
    tok = jnp.arange(tt, dtype=jnp.int32)
    tri = (tok[:, None] > tok[None, :]).astype(BF16)
    kern = functools.partial(_router_kernel, n_experts=n_experts)
    return pl.pallas_call(
        kern,
        grid=(t // tt,),
        in_specs=[pl.BlockSpec((tt, d), lambda i: (i, 0)),
                  pl.BlockSpec((1, d), lambda i: (0, 0)),
                  pl.BlockSpec((d, LANES), lambda i: (0, 0)),
                  pl.BlockSpec((d, LANES), lambda i: (0, 0)),
                  pl.BlockSpec((1, LANES), lambda i: (0, 0)),
                  pl.BlockSpec((tt, tt), lambda i: (0, 0))],
        out_specs=[pl.BlockSpec((tt, LANES), lambda i: (i, 0)),
                   pl.BlockSpec((2 * SUBLANES, tt), lambda i: (0, i)),
                   pl.BlockSpec((SUBLANES, LANES), lambda i: (0, 0))],
        out_shape=[jax.ShapeDtypeStruct((t, LANES), F32),
                   jax.ShapeDtypeStruct((2 * SUBLANES, t), F32),
                   jax.ShapeDtypeStruct((SUBLANES, LANES), F32)],
        scratch_shapes=[pltpu.VMEM((SUBLANES, LANES), F32)],
        compiler_params=pltpu.CompilerParams(
            dimension_semantics=("arbitrary",), vmem_limit_bytes=VMEM_LIMIT_BYTES),
        name="router",
    )(h2, norm_g.reshape(1, d), rw_hi, rw_lo, rb, tri)


def _row_gather(table, idx):
    n_idx = idx.shape[0]
    d = table.shape[1]
    w = GATHER_WINDOW
    sc = plsc.get_sparse_core_info()
    n_workers = sc.num_cores * sc.num_subcores
    per_worker = n_idx // n_workers
    n_win = per_worker // w
    assert per_worker * n_workers == n_idx and n_win * w == per_worker and n_win % 2 == 0
    mesh = plsc.VectorSubcoreMesh(core_axis_name="core", subcore_axis_name="subcore")

    @functools.partial(
        pl.kernel, out_type=jax.ShapeDtypeStruct((n_idx, d), table.dtype), mesh=mesh,
        scratch_types=[pltpu.VMEM((per_worker,), jnp.int32),
                       pltpu.VMEM((w, d), table.dtype), pltpu.VMEM((w, d), table.dtype),
                       pltpu.SemaphoreType.DMA, pltpu.SemaphoreType.DMA,
                       pltpu.SemaphoreType.DMA, pltpu.SemaphoreType.DMA])
    def gather_kernel(table_hbm, idx_hbm, out_hbm, idx_v, buf_a, buf_b, gsem_a, gsem_b, psem_a, psem_b):
        worker = lax.axis_index("subcore") * sc.num_cores + lax.axis_index("core")
        base = worker * per_worker
        pltpu.sync_copy(idx_hbm.at[pl.ds(base, per_worker)], idx_v)

        def gather(j, buf, sem):
            return pltpu.make_async_copy(table_hbm.at[idx_v.at[pl.ds(j * w, w)]], buf, sem)

        def put(j, buf, sem):
            return pltpu.make_async_copy(buf, out_hbm.at[pl.ds(base + j * w, w)], sem)

        gather(0, buf_a, gsem_a).start()

        @pl.loop(0, n_win, step=2)
        def _(j):
            gather(j, buf_a, gsem_a).wait()

            @pl.when(j > 0)
            def _():
                put(j - 1, buf_b, psem_b).wait()

            gather(j + 1, buf_b, gsem_b).start()
            put(j, buf_a, psem_a).start()
            gather(j + 1, buf_b, gsem_b).wait()
            put(j, buf_a, psem_a).wait()

            @pl.when(j + 2 < n_win)
            def _():
                gather(j + 2, buf_a, gsem_a).start()

            put(j + 1, buf_b, psem_b).start()

        put(n_win - 1, buf_b, psem_b).wait()

    return gather_kernel(table, idx)


def _slot_tokens(pos, n_slots):
    n = pos.shape[0]
    sc = plsc.get_sparse_core_info()
    n_cores, n_sub = sc.num_cores, sc.num_subcores
    rows = n // LANES
    rows_per_tile = rows // n_sub
    out_per_worker = n_slots // (n_cores * n_sub)
    init_per_tile = n_slots // n_sub
    assert rows_per_tile * n_sub * LANES == n and out_per_worker * n_cores * n_sub == n_slots
    mesh = plsc.VectorSubcoreMesh(core_axis_name="core", subcore_axis_name="subcore")
    tokens = (jnp.arange(n, dtype=jnp.int32) % (n // TOP_K)).reshape(rows, LANES)
    fill = jnp.arange(n_slots, dtype=jnp.int32) % (n // TOP_K)

    @functools.partial(
        pl.kernel, out_type=jax.ShapeDtypeStruct((n_slots,), jnp.int32), mesh=mesh,
        scratch_types=[pltpu.VMEM((rows_per_tile, LANES), jnp.int32),
                       pltpu.VMEM((rows_per_tile, LANES), jnp.int32),
                       pltpu.VMEM_SHARED((n_slots,), jnp.int32)])
    def invert_kernel(pos_hbm, tok_hbm, fill_hbm, out_hbm, idx_v, val_v, shared):
        cid = lax.axis_index("core")
        sid = lax.axis_index("subcore")
        pltpu.sync_copy(fill_hbm.at[pl.ds(sid * init_per_tile, init_per_tile)],
                        shared.at[pl.ds(sid * init_per_tile, init_per_tile)])
        pltpu.sync_copy(pos_hbm.at[pl.ds(sid * rows_per_tile, rows_per_tile)], idx_v)
        pltpu.sync_copy(tok_hbm.at[pl.ds(sid * rows_per_tile, rows_per_tile)], val_v)
        plsc.subcore_barrier()

        @pl.loop(0, rows_per_tile)
        def _(j):
            pltpu.sync_copy(val_v.at[j], shared.at[idx_v.at[j]])

        plsc.subcore_barrier()
        off = (cid * n_sub + sid) * out_per_worker
        pltpu.sync_copy(shared.at[pl.ds(off, out_per_worker)], out_hbm.at[pl.ds(off, out_per_worker)])

    return invert_kernel(pos.reshape(rows, LANES), tokens, fill)


def _experts_kernel(te_ref, nv_ref, xs_ref, w1_ref, b1_ref, w2_ref, b2_ref, ys_ref, w1_s, w2_s, *, d_ff):
    i = pl.program_id(0)
    prev = te_ref[jnp.maximum(i - 1, 0)]
    changed = (i == 0) | (te_ref[i] != prev)

    @pl.when(changed)
    def _():
        w1_s[...] = w1_ref[0].astype(BF16)
        w2_s[...] = w2_ref[0].astype(BF16)

    nvalid = nv_ref[i]

    tm = xs_ref.shape[0]
    all_groups = [slice(r, r + EXPERT_ROWS) for r in range(0, tm, EXPERT_ROWS)]

    def ffn(groups):
        half = xs_ref.shape[1]
        xg = []
        for rows in groups:
            row = rows.start + lax.broadcasted_iota(jnp.int32, (EXPERT_ROWS, 1), 0)
            lo, hi = _unpack_bf16_pairs(jnp.where(row < nvalid, xs_ref[rows, :], 0))
            xg.append((lo.astype(BF16), hi.astype(BF16)))
        ug = [jnp.dot(lo, w1_s[0:half, :], preferred_element_type=F32)
              + jnp.dot(hi, w1_s[half:2 * half, :], preferred_element_type=F32) + b1_ref[0] for lo, hi in xg]
        ag = []
        for u in ug:
            glu = jnp.minimum(u[:, 0:d_ff], SWIGLU_LIMIT)
            lin = jnp.clip(u[:, d_ff:2 * d_ff], -SWIGLU_LIMIT, SWIGLU_LIMIT)
            ag.append((glu * _sigmoid(SWIGLU_ALPHA * glu) * (lin + 1.0)).astype(BF16))
        for rows, act in zip(groups, ag):
            ys_ref[rows, :] = _pack_bf16_pairs(
                jnp.dot(act, w2_s[...], preferred_element_type=F32) + b2_ref[0])
        for rows in all_groups[len(groups):]:
            ys_ref[rows, :] = jnp.zeros((EXPERT_ROWS, ys_ref.shape[1]), ys_ref.dtype)

    for n_used in range(len(all_groups) + 1):
        lo_cnt, hi_cnt = (n_used - 1) * EXPERT_ROWS, n_used * EXPERT_ROWS
        pl.when((nvalid > lo_cnt) & (nvalid <= hi_cnt))(functools.partial(ffn, all_groups[:n_used]))


def _experts(xs, tile_expert, tile_valid, w1, b1, w2, b2):
    n_slots = xs.shape[0]
    n_exp, d, two_ff = w1.shape
    d_ff = two_ff // 2
    tm = SLOT_TILE
    n_tiles = n_slots // tm
    kern = functools.partial(_experts_kernel, d_ff=d_ff)
    grid_spec = pltpu.PrefetchScalarGridSpec(
        num_scalar_prefetch=2,
        grid=(n_tiles,),
        in_specs=[pl.BlockSpec((tm, d // 2), lambda i, te, nv: (i, 0)),
                  pl.BlockSpec((1, d, two_ff), lambda i, te, nv: (te[i], 0, 0)),
                  pl.BlockSpec((1, 1, two_ff), lambda i, te, nv: (te[i], 0, 0)),
                  pl.BlockSpec((1, d_ff, d), lambda i, te, nv: (te[i], 0, 0)),
                  pl.BlockSpec((1, 1, d), lambda i, te, nv: (te[i], 0, 0))],
        out_specs=pl.BlockSpec((tm, d // 2), lambda i, te, nv: (i, 0)),
        scratch_shapes=[pltpu.VMEM((d, two_ff), BF16), pltpu.VMEM((d_ff, d), BF16)],
    )
    return pl.pallas_call(
        kern,
        grid_spec=grid_spec,
        out_shape=jax.ShapeDtypeStruct((n_slots, d // 2), jnp.int32),
        compiler_params=pltpu.CompilerParams(
            dimension_semantics=("arbitrary",), vmem_limit_bytes=VMEM_LIMIT_BYTES),
        name="experts",
    )(tile_expert, tile_valid, xs, w1, b1.reshape(n_exp, 1, two_ff), w2, b2.reshape(n_exp, 1, d))


def _combine_kernel(h_ref, meta_ref, g_ref, *rest):
    y_refs, o_ref = rest[:TOP_K], rest[-1]
    d = h_ref.shape[1]
    half = d // 2
    acc_lo = h_ref[:, 0:half]
    acc_hi = h_ref[:, half:d]
    for kk in range(TOP_K):
        gate = meta_ref[:, TOP_K + kk:TOP_K + kk + 1]
        lo, hi = _unpack_bf16_pairs(y_refs[kk][...])
        acc_lo = acc_lo + gate * lo
        acc_hi = acc_hi + gate * hi
    ms = (jnp.sum(acc_lo * acc_lo, axis=-1, keepdims=True)
          + jnp.sum(acc_hi * acc_hi, axis=-1, keepdims=True)) * (1.0 / d)
    scale = lax.rsqrt(ms + NORM_EPS)
    o_ref[:, 0:half] = acc_lo * scale * g_ref[:, 0:half]
    o_ref[:, half:d] = acc_hi * scale * g_ref[:, half:d]


def _combine(h2, y4, meta, norm_g, out_prev, group, n_groups):
    t, d = h2.shape
    tt = min(MOVE_TOKENS, t)
    n_blk = t // tt
    prev_specs = [] if out_prev is None else [pl.BlockSpec(memory_space=pl.ANY)]
    prev_args = [] if out_prev is None else [out_prev]
    n_in = 3 + TOP_K
    y_specs = [pl.BlockSpec((tt, d // 2), functools.partial(lambda i, kk: (kk * n_blk + i, 0), kk=kk))
               for kk in range(TOP_K)]
    return pl.pallas_call(
        _combine_kernel,
        grid=(n_blk,),
        in_specs=[pl.BlockSpec((tt, d), lambda i: (i, 0)),
                  pl.BlockSpec((tt, LANES), lambda i: (i, 0)),
                  pl.BlockSpec((1, d), lambda i: (0, 0))] + y_specs + prev_specs,
        out_specs=pl.BlockSpec((tt, d), lambda i: (group * n_blk + i, 0)),
        out_shape=jax.ShapeDtypeStruct((n_groups * t, d), F32),
        input_output_aliases={} if out_prev is None else {n_in: 0},
        compiler_params=pltpu.CompilerParams(
            dimension_semantics=("arbitrary",), vmem_limit_bytes=VMEM_LIMIT_BYTES),
        name="combine",
    )(h2, meta, norm_g.reshape(1, d), *([y4] * TOP_K), *prev_args)


def _moe(h2, hp2, norm_ffn_g, router_w, router_b, w1, b1, w2, b2, norm_final_g, out_prev, group, n_groups):
    t, d = h2.shape
    n_exp = router_w.shape[1]
    tm = SLOT_TILE
    meta, metat, counts = _router(h2, norm_ffn_g, router_w, router_b)
    eidx = metat[0:TOP_K].astype(jnp.int32)
    rank = metat[2 * TOP_K:3 * TOP_K].astype(jnp.int32)
    cnt = counts[0, 0:n_exp].astype(jnp.int32)
    padded = (cnt + tm - 1) // tm * tm
    seg_end = jnp.cumsum(padded)
    seg_start = seg_end - padded
    experts = jnp.arange(n_exp, dtype=jnp.int32)[:, None, None]
    start_of = jnp.sum(jnp.where(eidx[None] == experts, seg_start[:, None, None], 0), axis=0)
    pos = (start_of + rank).reshape(-1)
    n_tiles = -(-(t * TOP_K) // tm) + n_exp
    n_slots = n_tiles * tm
    tile_start = jnp.arange(n_tiles, dtype=jnp.int32) * tm
    tile_expert = jnp.minimum(jnp.sum(tile_start[:, None] >= seg_end[None, :], axis=1), n_exp - 1).astype(jnp.int32)
    tile_valid = jnp.clip(seg_start[tile_expert] + cnt[tile_expert] - tile_start, 0, tm).astype(jnp.int32)
    slot_token = _slot_tokens(pos, n_slots)
    xs = _row_gather(hp2, slot_token)
    ys = _experts(xs, tile_expert, tile_valid, w1, b1, w2, b2)
    y4 = _row_gather(ys, pos)
    return _combine(h2, y4, meta, norm_final_g, out_prev, group, n_groups)


def kernel(x, norm_mix_g, w_in, conv_w, shift_mu, decay_w0, decay_w2, iclr_a0, iclr_a2, gate_g2, k_k, k_a,
           r_k, ln_x_g, ln_x_b, w_out, norm_ffn_g, router_w, router_b, exp_w1, exp_b1, exp_w2, exp_b2,
           norm_final_g):
    bsz, s, d = x.shape
    depth = w_in.shape[0]
    assert depth == 1, "final norm is fused into the last layer's combine kernel"
    n_groups = BATCH_GROUPS if bsz % BATCH_GROUPS == 0 else 1
    hs = _mixer(x, n_groups, norm_mix_g[0], norm_ffn_g[0], w_in[0], conv_w[0], shift_mu[0], decay_w0[0], decay_w2[0],
                iclr_a0[0], iclr_a2[0], gate_g2[0], k_k[0], k_a[0], r_k[0], ln_x_g[0], ln_x_b[0], w_out[0])
    out = None
    for g, (h, hp) in enumerate(hs):
        out = _moe(h.reshape(-1, d), hp.reshape(-1, d // 2), norm_ffn_g[0], router_w[0], router_b[0], exp_w1[0], exp_b1[0],
                   exp_w2[0], exp_b2[0], norm_final_g, out, g, n_groups)
    return out.reshape(bsz, s, d)
```

```python
import functools

import jax
import jax.numpy as jnp
from jax import lax
from jax.experimental import pallas as pl
from jax.experimental.pallas import tpu as pltpu
from jax.experimental.pallas import tpu_sc as plsc

HEAD = 64
DECAY_LORA = 64
AAA_LORA = 64
GATE_LORA = 160
TOP_K = 4
SWIGLU_ALPHA = 1.702
SWIGLU_LIMIT = 7.0
NORM_EPS = 1e-5
GN_EPS = 64e-5
DECAY_SCALE = 0.6065306597126334

LANES = 128
SUBLANES = 8
VMEM_LIMIT_BYTES = 56 * 1024 * 1024

RWKV_CHUNK = 64
MIX_ROWS = 4
ROUTE_TOKENS = 512
MOVE_TOKENS = 512
GATHER_WINDOW = 64
SLOT_TILE = 512
EXPERT_ROWS = 256
BATCH_GROUPS = 2

BF16 = jnp.bfloat16
F32 = jnp.float32


def _dot(a, b):
    return jnp.dot(a.astype(BF16), b.astype(BF16), preferred_element_type=F32)


def _dot_nt(a, b):
    return lax.dot_general(a.astype(BF16), b.astype(BF16), (((1,), (1,)), ((), ())),
                           preferred_element_type=F32)


def _dot_tn(a, b):
    return lax.dot_general(a.astype(BF16), b.astype(BF16), (((0,), (0,)), ((), ())),
                           preferred_element_type=F32)


def _sigmoid(x):
    return 0.5 * jnp.tanh(0.5 * x) + 0.5


def _pack_bf16_pairs(x):
    n = x.shape[1] // 2
    bits = pltpu.bitcast(x.astype(BF16).astype(F32), jnp.uint32)
    packed = (bits[:, n:] & jnp.uint32(0xFFFF0000)) | (bits[:, :n] >> 16)
    return pltpu.bitcast(packed, jnp.int32)


def _unpack_bf16_pairs(w):
    bits = pltpu.bitcast(w, jnp.uint32)
    return pltpu.bitcast(bits << 16, F32), pltpu.bitcast(bits & jnp.uint32(0xFFFF0000), F32)


def _shift_rows(x, n, carry, seg):
    rolled = pltpu.roll(x, n, axis=0)
    row = lax.broadcasted_iota(jnp.int32, (SUBLANES, x.shape[1]), 0)
    pieces = []
    for j in range(x.shape[0] // seg):
        head = rolled[j * seg:j * seg + SUBLANES]
        prev = carry[j * SUBLANES:(j + 1) * SUBLANES]
        for i in range(n):
            head = jnp.where(row == i, prev[SUBLANES - n + i:SUBLANES - n + i + 1, :], head)
        pieces += [head, rolled[j * seg + SUBLANES:(j + 1) * seg]]
    return jnp.concatenate(pieces, axis=0)


def _last_rows(x, seg):
    return jnp.concatenate([x[(j + 1) * seg - SUBLANES:(j + 1) * seg] for j in range(x.shape[0] // seg)],
                           axis=0)


def _mixer_kernel(x_ref, g_ref, wmain_ref, wlora_ref, wgate_ref, cw_ref, mu_rkv_ref, mu_lora_ref,
                  w0_ref, w2_ref, a0_ref, a2_ref, g2_ref, kk_ref, ka_ref, rk_ref, lng_ref, lnb_ref,
                  wout_ref, hsel_ref, hselt_ref, gffn_ref, o_ref, hp_ref,
                  cu_s, cp_s, cl_s, ar_s, bk_s, v_s, y_s, gam_s, *state_s,
                  n_heads, d_model):
    n_sub, L = x_ref.shape[0], x_ref.shape[1]
    tc = n_sub * L
    D = d_model

    @pl.when(pl.program_id(1) == 0)
    def _():
        for st_ref in state_s:
            st_ref[...] = jnp.zeros_like(st_ref)
        cu_s[...] = jnp.zeros_like(cu_s)
        cp_s[...] = jnp.zeros_like(cp_s)
        cl_s[...] = jnp.zeros_like(cl_s)

    def split(t):
        hi = t.astype(BF16)
        return hi, (t - hi.astype(F32)).astype(BF16)

    def head_sums(parts, exact=False):
        per_head = sum(jnp.dot(p, hsel_ref[...], preferred_element_type=F32) for p in parts)
        back = split(per_head) if exact else [per_head.astype(BF16)]
        return sum(jnp.dot(p, hselt_ref[...], preferred_element_type=F32) for p in back)

    x = x_ref[...].reshape(tc, D)
    ms = jnp.mean(x * x, axis=-1, keepdims=True)
    xn = (x * lax.rsqrt(ms + NORM_EPS) * g_ref[...]).astype(BF16)

    pc = jnp.dot(xn, wmain_ref[:, 0:3 * D], preferred_element_type=F32)
    u = pc[:, D:2 * D] * pc[:, 2 * D:3 * D]
    cu = cu_s[...]
    conv = (cw_ref[0:1, :] * _shift_rows(u, 2, cu, L) + cw_ref[1:2, :] * _shift_rows(u, 1, cu, L)
            + cw_ref[2:3, :] * u)
    y_conv = pc[:, 0:D] * conv
    cu_s[...] = _last_rows(u, L)

    pr = jnp.dot(xn, wmain_ref[:, 3 * D:6 * D], preferred_element_type=F32)
    cp = cp_s[...]
    cp_s[...] = _last_rows(pr, L)
    pr = pr + (_shift_rows(pr, 1, cp, L) - pr) * mu_rkv_ref[...]
    plo = jnp.dot(xn, wlora_ref[...], preferred_element_type=F32)
    cl = cl_s[...]
    cl_s[...] = _last_rows(plo, L)
    plo = plo + (_shift_rows(plo, 1, cl, L) - plo) * mu_lora_ref[...]

    r = pr[:, 0:D]
    k = pr[:, D:2 * D]
    v = pr[:, 2 * D:3 * D]
    wd = plo[:, 0:LANES]
    ad = plo[:, LANES:2 * LANES]
    gd = plo[:, 2 * LANES:4 * LANES]

    lw = -DECAY_SCALE * _sigmoid(w0_ref[...] + _dot(jnp.tanh(wd), w2_ref[...]))
    a = _sigmoid(a0_ref[...] + _dot(ad, a2_ref[...]))
    g = _dot(_sigmoid(gd), g2_ref[...])

    row = lax.broadcasted_iota(jnp.int32, (tc, tc), 0)
    col = lax.broadcasted_iota(jnp.int32, (tc, tc), 1)
    tri = jnp.where((row >= col) & ((row // L) == (col // L)), 1.0, 0.0).astype(BF16)
    cum = sum(jnp.dot(tri, p, preferred_element_type=F32) for p in split(lw))
    e_inv = jnp.exp(-cum)

    kkraw = k * kk_ref[...]
    ss = head_sums([(kkraw * kkraw).astype(BF16)])
    kkn = kkraw * jnp.minimum(lax.rsqrt(ss), 1e12)

    k2 = k * (1.0 + (a - 1.0) * ka_ref[...])
    a_t = kkn * jnp.exp(cum - lw)
    r_t = r * jnp.exp(cum)
    b_t = kkn * a * e_inv
    k_t = k2 * e_inv
    for c in range(n_sub):
        rows = slice(c * L, (c + 1) * L)
        ar_s[c, 0:L, :] = a_t[rows]
        ar_s[c, L:2 * L, :] = r_t[rows]
        bk_s[c, 0:L, :] = b_t[rows]
        bk_s[c, L:2 * L, :] = k_t[rows]
        gam_s[c] = jnp.exp(cum[(c + 1) * L - 1:(c + 1) * L, :])
    v_s[...] = v
    bonus = head_sums([(r * k2 * rk_ref[...]).astype(BF16)]) * v

    pair = 2 * HEAD
    lane = lax.broadcasted_iota(jnp.int32, (1, pair), 1)
    left = lane < HEAD
    row1 = lax.broadcasted_iota(jnp.int32, (L, pair), 0)
    col1 = lax.broadcasted_iota(jnp.int32, (L, pair), 1) % HEAD
    strict = row1 > col1
    eye = jnp.where(row1 == col1, 1.0, 0.0)
    incl2 = (lax.broadcasted_iota(jnp.int32, (L, 2 * pair), 0)
             >= lax.broadcasted_iota(jnp.int32, (L, 2 * pair), 1) % HEAD)
    level_mask = [(row1 // (2 * s) == col1 // (2 * s)) & (row1 % (2 * s) >= s) & (col1 % (2 * s) < s)
                  for s in (1 << i for i in range(L.bit_length() - 1))]

    def blockdiag(t):
        tb = t.astype(BF16)
        return jnp.concatenate([jnp.where(left, tb, 0), jnp.where(left, 0, tb)], axis=0)

    pairs = range(n_heads // 2)
    psl = [slice(p * pair, (p + 1) * pair) for p in pairs]
    work = [(c, p) for c in range(n_sub) for p in pairs]
    ar = [ar_s[c, :, psl[p]].astype(BF16) for c, p in work]
    bk = [bk_s[c, :, psl[p]].astype(BF16) for c, p in work]
    vp = [v_s[c * L:(c + 1) * L, psl[p]] for c, p in work]
    gm = [_dot_nt(ar[i], jnp.concatenate([blockdiag(bk[i][0:L]), blockdiag(bk[i][L:2 * L])], axis=0))
          for i in range(len(work))]
    nm = [jnp.where(strict, g_[0:L, 0:pair], 0.0) for g_ in gm]
    mak = [jnp.where(strict, g_[0:L, pair:2 * pair], 0.0) for g_ in gm]
    q = [jnp.where(incl2, g_[L:2 * L, :], 0.0).astype(BF16) for g_ in gm]
    xinv = [eye - jnp.where(level_mask[0], n_, 0.0) for n_ in nm]
    for lm in level_mask[1:]:
        half = [_dot(jnp.where(lm, n_, 0.0), blockdiag(t)) for n_, t in zip(nm, xinv)]
        xinv = [t - _dot(t, blockdiag(h_)) for t, h_ in zip(xinv, half)]
    xinv = [t.astype(BF16) for t in xinv]
    mv = [_dot(m_, blockdiag(v_)) for m_, v_ in zip(mak, vp)]

    gates = _sigmoid(jnp.dot(xn, wgate_ref[...], preferred_element_type=F32))

    items = range(len(work))
    st = [state_s[i][...] for i in items]
    ars = [_dot_nt(ar[i], blockdiag(st[i])) for i in items]
    uu = [_dot(xinv[i], blockdiag(ars[i][0:L] + mv[i])) for i in items]
    yh = [ars[i][L:2 * L] + _dot(q[i], jnp.concatenate([blockdiag(-uu[i]), blockdiag(vp[i])], axis=0))
          for i in items]
    for i, (c, p) in enumerate(work):
        upd = _dot_tn(jnp.concatenate([-uu[i], vp[i]], axis=0), bk[i])
        state_s[i][...] = (st[i] + jnp.where(left, upd[0:HEAD], upd[HEAD:pair])) * gam_s[c, :, psl[p]]
    for i, (c, p) in enumerate(work):
        y_s[c * L:(c + 1) * L, psl[p]] = yh[i]

    y = y_s[...]
    yc = y - head_sums(split(y), exact=True) * (1.0 / HEAD)
    var = head_sums([(yc * yc).astype(BF16)]) * (1.0 / HEAD)
    y_rwkv = (yc * lax.rsqrt(var + GN_EPS) * lng_ref[...] + lnb_ref[...] + bonus) * g
    mix = gates[:, 0:D] * y_conv + gates[:, D:2 * D] * y_rwkv
    h = x + jnp.dot(mix.astype(BF16), wout_ref[...], preferred_element_type=F32)
    o_ref[...] = h.reshape(n_sub, L, D)
    hn = h * lax.rsqrt(jnp.mean(h * h, axis=-1, keepdims=True) + NORM_EPS) * gffn_ref[...]
    hp_ref[...] = _pack_bf16_pairs(hn).reshape(n_sub, L, D // 2)


def _const_spec(shape):
    nd = len(shape)
    return pl.BlockSpec(shape, lambda *_: (0,) * nd, pipeline_mode=pl.Buffered(1))


def _mixer(x, n_groups, norm_g, norm_ffn_g, w_in, conv_w, shift_mu, w0, w2, a0, a2, g2, k_k, k_a, r_k, ln_g, ln_b, w_out):
    bsz, s, d = x.shape
    gb = bsz // n_groups
    n_heads = d // HEAD
    L = RWKV_CHUNK
    n_sub = MIX_ROWS
    tc = n_sub * L
    assert gb % n_sub == 0 and s % L == 0
    lora0 = 6 * d
    w_main = w_in[:, 0:6 * d].astype(BF16)
    pad = lambda t, n: jnp.pad(t, ((0, 0), (0, n - t.shape[1])))
    lora_cols = (DECAY_LORA, AAA_LORA, GATE_LORA)
    lora_pads = (LANES, LANES, 2 * LANES)
    pieces_w, pieces_mu, off = [], [], lora0
    for n, p in zip(lora_cols, lora_pads):
        pieces_w.append(pad(w_in[:, off:off + n], p))
        pieces_mu.append(pad(shift_mu[None, off - 3 * d:off - 3 * d + n], p))
        off += n
    w_lora = jnp.concatenate(pieces_w, axis=1).astype(BF16)
    mu_lora = jnp.concatenate(pieces_mu, axis=1)
    w_gate = w_in[:, off:off + 2 * d].astype(BF16)
    mu_rkv = shift_mu[None, 0:3 * d]
    padr = lambda t, n: jnp.pad(t, ((0, n - t.shape[0]), (0, 0)))
    w2p = padr(w2, LANES).astype(BF16)
    a2p = padr(a2, LANES).astype(BF16)
    g2p = padr(g2, 2 * LANES).astype(BF16)
    row = lambda t: t.reshape(1, -1)
    head_of = jnp.arange(d, dtype=jnp.int32) // HEAD
    head_sel = (head_of[:, None] == jnp.arange(LANES)[None, :]).astype(BF16)
    consts = [row(norm_g), w_main, w_lora, w_gate, conv_w, mu_rkv, mu_lora, row(w0), w2p, row(a0), a2p,
              g2p, row(k_k), row(k_a), row(r_k), row(ln_g), row(ln_b), w_out.astype(BF16), head_sel, head_sel.T,
              row(norm_ffn_g)]
    kern = functools.partial(_mixer_kernel, n_heads=n_heads, d_model=d)
    call = lambda first: pl.pallas_call(
        kern,
        grid=(gb // n_sub, s // L),
        in_specs=[pl.BlockSpec((n_sub, L, d), lambda b, c: (b + first // n_sub, c, 0))]
        + [_const_spec(t.shape) for t in consts],
        out_specs=[pl.BlockSpec((n_sub, L, d), lambda b, c: (b, c, 0)),
                   pl.BlockSpec((n_sub, L, d // 2), lambda b, c: (b, c, 0))],
        out_shape=[jax.ShapeDtypeStruct((gb, s, d), F32), jax.ShapeDtypeStruct((gb, s, d // 2), jnp.int32)],
        scratch_shapes=[
            pltpu.VMEM((n_sub * SUBLANES, d), F32),
            pltpu.VMEM((n_sub * SUBLANES, 3 * d), F32),
            pltpu.VMEM((n_sub * SUBLANES, 4 * LANES), F32),
            pltpu.VMEM((n_sub, 2 * L, d), F32),
            pltpu.VMEM((n_sub, 2 * L, d), F32),
            pltpu.VMEM((tc, d), F32),
            pltpu.VMEM((tc, d), F32),
            pltpu.VMEM((n_sub, 1, d), F32),
        ] + [pltpu.VMEM((HEAD, 2 * HEAD), F32)] * (n_sub * n_heads // 2),
        compiler_params=pltpu.CompilerParams(
            dimension_semantics=("arbitrary", "arbitrary"), vmem_limit_bytes=VMEM_LIMIT_BYTES),
        name="mixer",
    )(x, *consts)
    return [call(g * gb) for g in range(n_groups)]


def _router_kernel(h_ref, g_ref, rwh_ref, rwl_ref, rb_ref, tri_ref, meta_ref, metat_ref, cnt_ref, run_s, *,
                   n_experts):
    tt = h_ref.shape[0]

    @pl.when(pl.program_id(0) == 0)
    def _():
        run_s[...] = jnp.zeros_like(run_s)

    h = h_ref[...]
    ms = jnp.mean(h * h, axis=-1, keepdims=True)
    hn = h * lax.rsqrt(ms + NORM_EPS) * g_ref[...]
    hn_hi = hn.astype(BF16)
    hn_lo = (hn - hn_hi.astype(F32)).astype(BF16)
    logits = (jnp.dot(hn_hi, rwh_ref[...], preferred_element_type=F32)
              + jnp.dot(hn_lo, rwh_ref[...], preferred_element_type=F32)
              + jnp.dot(hn_hi, rwl_ref[...], preferred_element_type=F32)) + rb_ref[...]
    lane = lax.broadcasted_iota(jnp.int32, (tt, LANES), 1)
    neg = jnp.float32(-jnp.inf)
    work = jnp.where(lane < n_experts, logits, neg)
    vals, idxs = [], []
    for _ in range(TOP_K):
        m = jnp.max(work, axis=-1, keepdims=True)
        i = jnp.min(jnp.where(work == m, lane, LANES), axis=-1, keepdims=True)
        vals.append(m)
        idxs.append(i)
        work = jnp.where(lane == i, neg, work)
    ex = [jnp.exp(vv - vals[0]) for vv in vals]
    den = ex[0] + ex[1] + ex[2] + ex[3]
    gates = [e / den for e in ex]

    onehot = jnp.zeros((tt, LANES), jnp.bool_)
    for kk in range(TOP_K):
        onehot = onehot | (lane == (idxs[kk] + kk * n_experts))
    oh = jnp.where(onehot, 1.0, 0.0)
    cnt = jnp.dot(tri_ref[...], oh.astype(BF16), preferred_element_type=F32)
    tot = jnp.broadcast_to(jnp.sum(oh, axis=0, keepdims=True), (SUBLANES, LANES))
    lane8 = lax.broadcasted_iota(jnp.int32, (SUBLANES, LANES), 1)
    pk = jnp.zeros_like(tot)
    te = tot
    for j in range(1, TOP_K):
        rolled = pltpu.roll(tot, j * n_experts, axis=1)
        pk = pk + jnp.where(lane8 >= j * n_experts, rolled, 0.0)
        te = te + rolled
    before = cnt + (run_s[...] + pk)[0:1, :]
    ranks = [jnp.sum(jnp.where(onehot & (lane // n_experts == kk), before, 0.0), axis=-1, keepdims=True)
             for kk in range(TOP_K)]
    run_s[...] = run_s[...] + te
    cnt_ref[...] = run_s[...]

    meta = jnp.zeros((tt, LANES), F32)
    for kk in range(TOP_K):
        meta = jnp.where(lane == kk, idxs[kk].astype(F32), meta)
        meta = jnp.where(lane == TOP_K + kk, gates[kk], meta)
        meta = jnp.where(lane == 2 * TOP_K + kk, ranks[kk], meta)
    meta_ref[...] = meta
    metat_ref[...] = meta.T[0:2 * SUBLANES, :]


def _router(h2, norm_g, router_w, router_b):
    t, d = h2.shape
    n_experts = router_w.shape[1]
    tt = min(ROUTE_TOKENS, t)
    rw = jnp.pad(router_w, ((0, 0), (0, LANES - n_experts)))
    rw_hi = rw.astype(BF16)
    rw_lo = (rw - rw_hi.astype(F32)).astype(BF16)
    rb = jnp.pad(router_b, (0, LANES - n_experts)).reshape(1, LANES)
    tok = jnp.arange(tt, dtype=jnp.int32)
    tri = (tok[:, None] > tok[None, :]).astype(BF16)
    kern = functools.partial(_router_kernel, n_experts=n_experts)
    return pl.pallas_call(
        kern,
        grid=(t // tt,),
        in_specs=[pl.BlockSpec((tt, d), lambda i: (i, 0)),
                  pl.BlockSpec((1, d), lambda i: (0, 0)),
                  pl.BlockSpec((d, LANES), lambda i: (0, 0)),
                  pl.BlockSpec((d, LANES), lambda i: (0, 0)),
                  pl.BlockSpec((1, LANES), lambda i: (0, 0)),
                  pl.BlockSpec((tt, tt), lambda i: (0, 0))],
        out_specs=[pl.BlockSpec((tt, LANES), lambda i: (i, 0)),
                   pl.BlockSpec((2 * SUBLANES, tt), lambda i: (0, i)),
                   pl.BlockSpec((SUBLANES, LANES), lambda i: (0, 0))],
        out_shape=[jax.ShapeDtypeStruct((t, LANES), F32),
                   jax.ShapeDtypeStruct((2 * SUBLANES, t), F32),
                   jax.ShapeDtypeStruct((SUBLANES, LANES), F32)],
        scratch_shapes=[pltpu.VMEM((SUBLANES, LANES), F32)],
        compiler_params=pltpu.CompilerParams(
            dimension_semantics=("arbitrary",), vmem_limit_bytes=VMEM_LIMIT_BYTES),
        name="router",
    )(h2, norm_g.reshape(1, d), rw_hi, rw_lo, rb, tri)


def _row_gather(table, idx):
    n_idx = idx.shape[0]
    d = table.shape[1]
    w = GATHER_WINDOW
    sc = plsc.get_sparse_core_info()
    n_workers = sc.num_cores * sc.num_subcores
    per_worker = n_idx // n_workers
    n_win = per_worker // w
    assert per_worker * n_workers == n_idx and n_win * w == per_worker and n_win % 2 == 0
    mesh = plsc.VectorSubcoreMesh(core_axis_name="core", subcore_axis_name="subcore")

    @functools.partial(
        pl.kernel, out_type=jax.ShapeDtypeStruct((n_idx, d), table.dtype), mesh=mesh,
        scratch_types=[pltpu.VMEM((per_worker,), jnp.int32),
                       pltpu.VMEM((w, d), table.dtype), pltpu.VMEM((w, d), table.dtype),
                       pltpu.SemaphoreType.DMA, pltpu.SemaphoreType.DMA,
                       pltpu.SemaphoreType.DMA, pltpu.SemaphoreType.DMA])
    def gather_kernel(table_hbm, idx_hbm, out_hbm, idx_v, buf_a, buf_b, gsem_a, gsem_b, psem_a, psem_b):
        worker = lax.axis_index("subcore") * sc.num_cores + lax.axis_index("core")
        base = worker * per_worker
        pltpu.sync_copy(idx_hbm.at[pl.ds(base, per_worker)], idx_v)

        def gather(j, buf, sem):
            return pltpu.make_async_copy(table_hbm.at[idx_v.at[pl.ds(j * w, w)]], buf, sem)

        def put(j, buf, sem):
            return pltpu.make_async_copy(buf, out_hbm.at[pl.ds(base + j * w, w)], sem)

        gather(0, buf_a, gsem_a).start()

        @pl.loop(0, n_win, step=2)
        def _(j):
            gather(j, buf_a, gsem_a).wait()

            @pl.when(j > 0)
            def _():
                put(j - 1, buf_b, psem_b).wait()

            gather(j + 1, buf_b, gsem_b).start()
            put(j, buf_a, psem_a).start()
            gather(j + 1, buf_b, gsem_b).wait()
            put(j, buf_a, psem_a).wait()

            @pl.when(j + 2 < n_win)
            def _():
                gather(j + 2, buf_a, gsem_a).start()

            put(j + 1, buf_b, psem_b).start()

        put(n_win - 1, buf_b, psem_b).wait()

    return gather_kernel(table, idx)


def _slot_tokens(pos, n_slots):
    n = pos.shape[0]
    sc = plsc.get_sparse_core_info()
    n_cores, n_sub = sc.num_cores, sc.num_subcores
    rows = n // LANES
    rows_per_tile = rows // n_sub
    out_per_worker = n_slots // (n_cores * n_sub)
    init_per_tile = n_slots // n_sub
    assert rows_per_tile * n_sub * LANES == n and out_per_worker * n_cores * n_sub == n_slots
    mesh = plsc.VectorSubcoreMesh(core_axis_name="core", subcore_axis_name="subcore")
    tokens = (jnp.arange(n, dtype=jnp.int32) % (n // TOP_K)).reshape(rows, LANES)
    fill = jnp.arange(n_slots, dtype=jnp.int32) % (n // TOP_K)

    @functools.partial(
        pl.kernel, out_type=jax.ShapeDtypeStruct((n_slots,), jnp.int32), mesh=mesh,
        scratch_types=[pltpu.VMEM((rows_per_tile, LANES), jnp.int32),
                       pltpu.VMEM((rows_per_tile, LANES), jnp.int32),
                       pltpu.VMEM_SHARED((n_slots,), jnp.int32)])
    def invert_kernel(pos_hbm, tok_hbm, fill_hbm, out_hbm, idx_v, val_v, shared):
        cid = lax.axis_index("core")
        sid = lax.axis_index("subcore")
        pltpu.sync_copy(fill_hbm.at[pl.ds(sid * init_per_tile, init_per_tile)],
                        shared.at[pl.ds(sid * init_per_tile, init_per_tile)])
        pltpu.sync_copy(pos_hbm.at[pl.ds(sid * rows_per_tile, rows_per_tile)], idx_v)
        pltpu.sync_copy(tok_hbm.at[pl.ds(sid * rows_per_tile, rows_per_tile)], val_v)
        plsc.subcore_barrier()

        @pl.loop(0, rows_per_tile)
        def _(j):
            pltpu.sync_copy(val_v.at[j], shared.at[idx_v.at[j]])

        plsc.subcore_barrier()
        off = (cid * n_sub + sid) * out_per_worker
        pltpu.sync_copy(shared.at[pl.ds(off, out_per_worker)], out_hbm.at[pl.ds(off, out_per_worker)])

    return invert_kernel(pos.reshape(rows, LANES), tokens, fill)


def _experts_kernel(te_ref, nv_ref, xs_ref, w1_ref, b1_ref, w2_ref, b2_ref, ys_ref, *cast_refs, d_ff):
    i = pl.program_id(0)
    nvalid = nv_ref[i]
    if cast_refs:
        w1_use, w2_use = cast_refs
        prev = te_ref[jnp.maximum(i - 1, 0)]

        @pl.when((i == 0) | (te_ref[i] != prev))
        def _():
            w1_use[0] = w1_ref[0].astype(BF16)
            w2_use[0] = w2_ref[0].astype(BF16)
    else:
        w1_use, w2_use = w1_ref, w2_ref

    tm = xs_ref.shape[0]
    all_groups = [slice(r, r + EXPERT_ROWS) for r in range(0, tm, EXPERT_ROWS)]

    def ffn(groups):
        half = xs_ref.shape[1]
        xg = []
        for rows in groups:
            row = rows.start + lax.broadcasted_iota(jnp.int32, (EXPERT_ROWS, 1), 0)
            lo, hi = _unpack_bf16_pairs(jnp.where(row < nvalid, xs_ref[rows, :], 0))
            xg.append((lo.astype(BF16), hi.astype(BF16)))
        ug = [jnp.dot(lo, w1_use[0, 0:half, :], preferred_element_type=F32)
              + jnp.dot(hi, w1_use[0, half:2 * half, :], preferred_element_type=F32) + b1_ref[0] for lo, hi in xg]
        ag = []
        for u in ug:
            glu = jnp.minimum(u[:, 0:d_ff], SWIGLU_LIMIT)
            lin = jnp.clip(u[:, d_ff:2 * d_ff], -SWIGLU_LIMIT, SWIGLU_LIMIT)
            ag.append((glu * _sigmoid(SWIGLU_ALPHA * glu) * (lin + 1.0)).astype(BF16))
        for rows, act in zip(groups, ag):
            ys_ref[rows, :] = _pack_bf16_pairs(
                jnp.dot(act, w2_use[0], preferred_element_type=F32) + b2_ref[0])
        for rows in all_groups[len(groups):]:
            ys_ref[rows, :] = jnp.zeros((EXPERT_ROWS, ys_ref.shape[1]), ys_ref.dtype)

    for n_used in range(len(all_groups) + 1):
        lo_cnt, hi_cnt = (n_used - 1) * EXPERT_ROWS, n_used * EXPERT_ROWS
        pl.when((nvalid > lo_cnt) & (nvalid <= hi_cnt))(functools.partial(ffn, all_groups[:n_used]))


def _experts(xs, tile_expert, tile_valid, w1, b1, w2, b2):
    cast = w1.dtype != BF16
    n_slots = xs.shape[0]
    n_exp, d, two_ff = w1.shape
    d_ff = two_ff // 2
    tm = SLOT_TILE
    n_tiles = n_slots // tm
    kern = functools.partial(_experts_kernel, d_ff=d_ff)
    grid_spec = pltpu.PrefetchScalarGridSpec(
        num_scalar_prefetch=2,
        grid=(n_tiles,),
        in_specs=[pl.BlockSpec((tm, d // 2), lambda i, te, nv: (i, 0)),
                  pl.BlockSpec((1, d, two_ff), lambda i, te, nv: (te[i], 0, 0)),
                  pl.BlockSpec((1, 1, two_ff), lambda i, te, nv: (te[i], 0, 0)),
                  pl.BlockSpec((1, d_ff, d), lambda i, te, nv: (te[i], 0, 0)),
                  pl.BlockSpec((1, 1, d), lambda i, te, nv: (te[i], 0, 0))],
        out_specs=[pl.BlockSpec((tm, d // 2), lambda i, te, nv: (i, 0))]
        + ([pl.BlockSpec((1, d, two_ff), lambda i, te, nv: (te[i], 0, 0)),
            pl.BlockSpec((1, d_ff, d), lambda i, te, nv: (te[i], 0, 0))] if cast else []),
    )
    outs = pl.pallas_call(
        kern,
        grid_spec=grid_spec,
        out_shape=[jax.ShapeDtypeStruct((n_slots, d // 2), jnp.int32)]
        + ([jax.ShapeDtypeStruct(w1.shape, BF16), jax.ShapeDtypeStruct(w2.shape, BF16)] if cast else []),
        compiler_params=pltpu.CompilerParams(
            dimension_semantics=("arbitrary",), vmem_limit_bytes=VMEM_LIMIT_BYTES),
        name="experts",
    )(tile_expert, tile_valid, xs, w1, b1.reshape(n_exp, 1, two_ff), w2, b2.reshape(n_exp, 1, d))
    return outs if cast else (outs[0], w1, w2)


def _combine_kernel(h_ref, meta_ref, g_ref, *rest):
    y_refs, o_ref = rest[:TOP_K], rest[-1]
    d = h_ref.shape[1]
    half = d // 2
    acc_lo = h_ref[:, 0:half]
    acc_hi = h_ref[:, half:d]
    for kk in range(TOP_K):
        gate = meta_ref[:, TOP_K + kk:TOP_K + kk + 1]
        lo, hi = _unpack_bf16_pairs(y_refs[kk][...])
        acc_lo = acc_lo + gate * lo
        acc_hi = acc_hi + gate * hi
    ms = (jnp.sum(acc_lo * acc_lo, axis=-1, keepdims=True)
          + jnp.sum(acc_hi * acc_hi, axis=-1, keepdims=True)) * (1.0 / d)
    scale = lax.rsqrt(ms + NORM_EPS)
    o_ref[:, 0:half] = acc_lo * scale * g_ref[:, 0:half]
    o_ref[:, half:d] = acc_hi * scale * g_ref[:, half:d]


def _combine(h2, y4, meta, norm_g, out_prev, group, n_groups):
    t, d = h2.shape
    tt = min(MOVE_TOKENS, t)
    n_blk = t // tt
    prev_specs = [] if out_prev is None else [pl.BlockSpec(memory_space=pl.ANY)]
    prev_args = [] if out_prev is None else [out_prev]
    n_in = 3 + TOP_K
    y_specs = [pl.BlockSpec((tt, d // 2), functools.partial(lambda i, kk: (kk * n_blk + i, 0), kk=kk))
               for kk in range(TOP_K)]
    return pl.pallas_call(
        _combine_kernel,
        grid=(n_blk,),
        in_specs=[pl.BlockSpec((tt, d), lambda i: (i, 0)),
                  pl.BlockSpec((tt, LANES), lambda i: (i, 0)),
                  pl.BlockSpec((1, d), lambda i: (0, 0))] + y_specs + prev_specs,
        out_specs=pl.BlockSpec((tt, d), lambda i: (group * n_blk + i, 0)),
        out_shape=jax.ShapeDtypeStruct((n_groups * t, d), F32),
        input_output_aliases={} if out_prev is None else {n_in: 0},
        compiler_params=pltpu.CompilerParams(
            dimension_semantics=("arbitrary",), vmem_limit_bytes=VMEM_LIMIT_BYTES),
        name="combine",
    )(h2, meta, norm_g.reshape(1, d), *([y4] * TOP_K), *prev_args)


def _moe(h2, hp2, norm_ffn_g, router_w, router_b, w1, b1, w2, b2, norm_final_g, out_prev, group, n_groups):
    t, d = h2.shape
    n_exp = router_w.shape[1]
    tm = SLOT_TILE
    meta, metat, counts = _router(h2, norm_ffn_g, router_w, router_b)
    eidx = metat[0:TOP_K].astype(jnp.int32)
    rank = metat[2 * TOP_K:3 * TOP_K].astype(jnp.int32)
    cnt = counts[0, 0:n_exp].astype(jnp.int32)
    padded = jnp.maximum((cnt + tm - 1) // tm, 1) * tm
    seg_end = jnp.cumsum(padded)
    seg_start = seg_end - padded
    experts = jnp.arange(n_exp, dtype=jnp.int32)[:, None, None]
    start_of = jnp.sum(jnp.where(eidx[None] == experts, seg_start[:, None, None], 0), axis=0)
    pos = (start_of + rank).reshape(-1)
    n_tiles = -(-(t * TOP_K) // tm) + n_exp
    n_slots = n_tiles * tm
    tile_start = jnp.arange(n_tiles, dtype=jnp.int32) * tm
    tile_expert = jnp.minimum(jnp.sum(tile_start[:, None] >= seg_end[None, :], axis=1), n_exp - 1).astype(jnp.int32)
    tile_valid = jnp.clip(seg_start[tile_expert] + cnt[tile_expert] - tile_start, 0, tm).astype(jnp.int32)
    slot_token = _slot_tokens(pos, n_slots)
    xs = _row_gather(hp2, slot_token)
    ys, w1_bf, w2_bf = _experts(xs, tile_expert, tile_valid, w1, b1, w2, b2)
    y4 = _row_gather(ys, pos)
    return _combine(h2, y4, meta, norm_final_g, out_prev, group, n_groups), w1_bf, w2_bf


def kernel(x, norm_mix_g, w_in, conv_w, shift_mu, decay_w0, decay_w2, iclr_a0, iclr_a2, gate_g2, k_k, k_a,
           r_k, ln_x_g, ln_x_b, w_out, norm_ffn_g, router_w, router_b, exp_w1, exp_b1, exp_w2, exp_b2,
           norm_final_g):
    bsz, s, d = x.shape
    depth = w_in.shape[0]
    assert depth == 1, "final norm is fused into the last layer's combine kernel"
    n_groups = BATCH_GROUPS if bsz % (BATCH_GROUPS * MIX_ROWS) == 0 else 1
    hs = _mixer(x, n_groups, norm_mix_g[0], norm_ffn_g[0], w_in[0], conv_w[0], shift_mu[0], decay_w0[0], decay_w2[0],
                iclr_a0[0], iclr_a2[0], gate_g2[0], k_k[0], k_a[0], r_k[0], ln_x_g[0], ln_x_b[0], w_out[0])
    w1, w2 = exp_w1[0], exp_w2[0]
    out = None
    for g, (h, hp) in enumerate(hs):
        out, w1, w2 = _moe(h.reshape(-1, d), hp.reshape(-1, d // 2), norm_ffn_g[0], router_w[0], router_b[0],
                           w1, exp_b1[0], w2, exp_b2[0], norm_final_g, out, g, n_groups)
    return out.reshape(bsz, s, d)
```

```python
import functools

import jax
import jax.numpy as jnp
from jax import lax
from jax.experimental import pallas as pl
from jax.experimental.pallas import tpu as pltpu
from jax.experimental.pallas import tpu_sc as plsc

HEAD = 64
DECAY_LORA = 64
AAA_LORA = 64
GATE_LORA = 160
TOP_K = 4
SWIGLU_ALPHA = 1.702
SWIGLU_LIMIT = 7.0
NORM_EPS = 1e-5
GN_EPS = 64e-5
DECAY_SCALE = 0.6065306597126334

LANES = 128
SUBLANES = 8
VMEM_LIMIT_BYTES = 56 * 1024 * 1024

RWKV_CHUNK = 64
MIX_ROWS = 4
ROUTE_TOKENS = 512
MOVE_TOKENS = 512
GATHER_WINDOW = 64
SLOT_TILE = 512
EXPERT_ROWS = 256
BATCH_GROUPS = 2

BF16 = jnp.bfloat16
F32 = jnp.float32


def _dot(a, b):
    return jnp.dot(a.astype(BF16), b.astype(BF16), preferred_element_type=F32)


def _dot_nt(a, b):
    return lax.dot_general(a.astype(BF16), b.astype(BF16), (((1,), (1,)), ((), ())),
                           preferred_element_type=F32)


def _dot_tn(a, b):
    return lax.dot_general(a.astype(BF16), b.astype(BF16), (((0,), (0,)), ((), ())),
                           preferred_element_type=F32)


def _sigmoid(x):
    return 0.5 * jnp.tanh(0.5 * x) + 0.5


def _pack_bf16_pairs(x):
    n = x.shape[1] // 2
    bits = pltpu.bitcast(x.astype(BF16).astype(F32), jnp.uint32)
    packed = (bits[:, n:] & jnp.uint32(0xFFFF0000)) | (bits[:, :n] >> 16)
    return pltpu.bitcast(packed, jnp.int32)


def _unpack_bf16_pairs(w):
    bits = pltpu.bitcast(w, jnp.uint32)
    return pltpu.bitcast(bits << 16, F32), pltpu.bitcast(bits & jnp.uint32(0xFFFF0000), F32)


def _shift_rows(x, n, carry, seg):
    rolled = pltpu.roll(x, n, axis=0)
    row = lax.broadcasted_iota(jnp.int32, (SUBLANES, x.shape[1]), 0)
    pieces = []
    for j in range(x.shape[0] // seg):
        head = rolled[j * seg:j * seg + SUBLANES]
        prev = carry[j * SUBLANES:(j + 1) * SUBLANES]
        for i in range(n):
            head = jnp.where(row == i, prev[SUBLANES - n + i:SUBLANES - n + i + 1, :], head)
        pieces += [head, rolled[j * seg + SUBLANES:(j + 1) * seg]]
    return jnp.concatenate(pieces, axis=0)


def _last_rows(x, seg):
    return jnp.concatenate([x[(j + 1) * seg - SUBLANES:(j + 1) * seg] for j in range(x.shape[0] // seg)],
                           axis=0)


def _mixer_kernel(x_ref, g_ref, wmain_ref, wlora_ref, wgate_ref, cw_ref, mu_rkv_ref, mu_lora_ref,
                  w0_ref, w2_ref, a0_ref, a2_ref, g2_ref, kk_ref, ka_ref, rk_ref, lng_ref, lnb_ref,
                  wout_ref, hsel_ref, hselt_ref, gffn_ref, o_ref, hp_ref,
                  cu_s, cp_s, cl_s, ar_s, bk_s, v_s, y_s, gam_s, *state_s,
                  n_heads, d_model):
    n_sub, L = x_ref.shape[0], x_ref.shape[1]
    tc = n_sub * L
    D = d_model

    @pl.when(pl.program_id(1) == 0)
    def _():
        for st_ref in state_s:
            st_ref[...] = jnp.zeros_like(st_ref)
        cu_s[...] = jnp.zeros_like(cu_s)
        cp_s[...] = jnp.zeros_like(cp_s)
        cl_s[...] = jnp.zeros_like(cl_s)

    def split(t):
        hi = t.astype(BF16)
        return hi, (t - hi.astype(F32)).astype(BF16)

    def head_sums(parts, exact=False):
        per_head = sum(jnp.dot(p, hsel_ref[...], preferred_element_type=F32) for p in parts)
        back = split(per_head) if exact else [per_head.astype(BF16)]
        return sum(jnp.dot(p, hselt_ref[...], preferred_element_type=F32) for p in back)

    x = x_ref[...].reshape(tc, D)
    ms = jnp.mean(x * x, axis=-1, keepdims=True)
    xn = (x * lax.rsqrt(ms + NORM_EPS) * g_ref[...]).astype(BF16)

    pc = jnp.dot(xn, wmain_ref[:, 0:3 * D], preferred_element_type=F32)
    u = pc[:, D:2 * D] * pc[:, 2 * D:3 * D]
    cu = cu_s[...]
    conv = (cw_ref[0:1, :] * _shift_rows(u, 2, cu, L) + cw_ref[1:2, :] * _shift_rows(u, 1, cu, L)
            + cw_ref[2:3, :] * u)
    y_conv = pc[:, 0:D] * conv
    cu_s[...] = _last_rows(u, L)

    pr = jnp.dot(xn, wmain_ref[:, 3 * D:6 * D], preferred_element_type=F32)
    cp = cp_s[...]
    cp_s[...] = _last_rows(pr, L)
    pr = pr + (_shift_rows(pr, 1, cp, L) - pr) * mu_rkv_ref[...]
    plo = jnp.dot(xn, wlora_ref[...], preferred_element_type=F32)
    cl = cl_s[...]
    cl_s[...] = _last_rows(plo, L)
    plo = plo + (_shift_rows(plo, 1, cl, L) - plo) * mu_lora_ref[...]

    r = pr[:, 0:D]
    k = pr[:, D:2 * D]
    v = pr[:, 2 * D:3 * D]
    wd = plo[:, 0:LANES]
    ad = plo[:, LANES:2 * LANES]
    gd = plo[:, 2 * LANES:4 * LANES]

    lw = -DECAY_SCALE * _sigmoid(w0_ref[...] + _dot(jnp.tanh(wd), w2_ref[...]))
    a = _sigmoid(a0_ref[...] + _dot(ad, a2_ref[...]))
    g = _dot(_sigmoid(gd), g2_ref[...])

    row = lax.broadcasted_iota(jnp.int32, (tc, tc), 0)
    col = lax.broadcasted_iota(jnp.int32, (tc, tc), 1)
    tri = jnp.where((row >= col) & ((row // L) == (col // L)), 1.0, 0.0).astype(BF16)
    cum = sum(jnp.dot(tri, p, preferred_element_type=F32) for p in split(lw))
    e_inv = jnp.exp(-cum)

    kkraw = k * kk_ref[...]
    ss = head_sums([(kkraw * kkraw).astype(BF16)])
    kkn = kkraw * jnp.minimum(lax.rsqrt(ss), 1e12)

    k2 = k * (1.0 + (a - 1.0) * ka_ref[...])
    a_t = kkn * jnp.exp(cum - lw)
    r_t = r * jnp.exp(cum)
    b_t = kkn * a * e_inv
    k_t = k2 * e_inv
    for c in range(n_sub):
        rows = slice(c * L, (c + 1) * L)
        ar_s[c, 0:L, :] = a_t[rows]
        ar_s[c, L:2 * L, :] = r_t[rows]
        bk_s[c, 0:L, :] = b_t[rows]
        bk_s[c, L:2 * L, :] = k_t[rows]
        gam_s[c] = jnp.exp(cum[(c + 1) * L - 1:(c + 1) * L, :])
    v_s[...] = v
    bonus = head_sums([(r * k2 * rk_ref[...]).astype(BF16)]) * v

    pair = 2 * HEAD
    lane = lax.broadcasted_iota(jnp.int32, (1, pair), 1)
    left = lane < HEAD
    row1 = lax.broadcasted_iota(jnp.int32, (L, pair), 0)
    col1 = lax.broadcasted_iota(jnp.int32, (L, pair), 1) % HEAD
    strict = row1 > col1
    eye = jnp.where(row1 == col1, 1.0, 0.0)
    incl2 = (lax.broadcasted_iota(jnp.int32, (L, 2 * pair), 0)
             >= lax.broadcasted_iota(jnp.int32, (L, 2 * pair), 1) % HEAD)
    level_mask = [(row1 // (2 * s) == col1 // (2 * s)) & (row1 % (2 * s) >= s) & (col1 % (2 * s) < s)
                  for s in (1 << i for i in range(L.bit_length() - 1))]

    def blockdiag(t):
        tb = t.astype(BF16)
        return jnp.concatenate([jnp.where(left, tb, 0), jnp.where(left, 0, tb)], axis=0)

    pairs = range(n_heads // 2)
    psl = [slice(p * pair, (p + 1) * pair) for p in pairs]
    work = [(c, p) for c in range(n_sub) for p in pairs]
    ar = [ar_s[c, :, psl[p]].astype(BF16) for c, p in work]
    bk = [bk_s[c, :, psl[p]].astype(BF16) for c, p in work]
    vp = [v_s[c * L:(c + 1) * L, psl[p]] for c, p in work]
    gm = [_dot_nt(ar[i], jnp.concatenate([blockdiag(bk[i][0:L]), blockdiag(bk[i][L:2 * L])], axis=0))
          for i in range(len(work))]
    nm = [jnp.where(strict, g_[0:L, 0:pair], 0.0) for g_ in gm]
    mak = [jnp.where(strict, g_[0:L, pair:2 * pair], 0.0) for g_ in gm]
    q = [jnp.where(incl2, g_[L:2 * L, :], 0.0).astype(BF16) for g_ in gm]
    xinv = [eye - jnp.where(level_mask[0], n_, 0.0) for n_ in nm]
    for lm in level_mask[1:]:
        half = [_dot(jnp.where(lm, n_, 0.0), blockdiag(t)) for n_, t in zip(nm, xinv)]
        xinv = [t - _dot(t, blockdiag(h_)) for t, h_ in zip(xinv, half)]
    xinv = [t.astype(BF16) for t in xinv]
    mv = [_dot(m_, blockdiag(v_)) for m_, v_ in zip(mak, vp)]

    gates = _sigmoid(jnp.dot(xn, wgate_ref[...], preferred_element_type=F32))

    items = range(len(work))
    st = [state_s[i][...] for i in items]
    ars = [_dot_nt(ar[i], blockdiag(st[i])) for i in items]
    uu = [_dot(xinv[i], blockdiag(ars[i][0:L] + mv[i])) for i in items]
    yh = [ars[i][L:2 * L] + _dot(q[i], jnp.concatenate([blockdiag(-uu[i]), blockdiag(vp[i])], axis=0))
          for i in items]
    for i, (c, p) in enumerate(work):
        upd = _dot_tn(jnp.concatenate([-uu[i], vp[i]], axis=0), bk[i])
        state_s[i][...] = (st[i] + jnp.where(left, upd[0:HEAD], upd[HEAD:pair])) * gam_s[c, :, psl[p]]
    for i, (c, p) in enumerate(work):
        y_s[c * L:(c + 1) * L, psl[p]] = yh[i]

    y = y_s[...]
    yc = y - head_sums(split(y), exact=True) * (1.0 / HEAD)
    var = head_sums([(yc * yc).astype(BF16)]) * (1.0 / HEAD)
    y_rwkv = (yc * lax.rsqrt(var + GN_EPS) * lng_ref[...] + lnb_ref[...] + bonus) * g
    mix = gates[:, 0:D] * y_conv + gates[:, D:2 * D] * y_rwkv
    h = x + jnp.dot(mix.astype(BF16), wout_ref[...], preferred_element_type=F32)
    o_ref[...] = h.reshape(n_sub, L, D)
    hn = h * lax.rsqrt(jnp.mean(h * h, axis=-1, keepdims=True) + NORM_EPS) * gffn_ref[...]
    hp_ref[...] = _pack_bf16_pairs(hn).reshape(n_sub, L, D // 2)


def _const_spec(shape):
    nd = len(shape)
    return pl.BlockSpec(shape, lambda *_: (0,) * nd, pipeline_mode=pl.Buffered(1))


def _mixer(x, n_groups, norm_g, norm_ffn_g, w_in, conv_w, shift_mu, w0, w2, a0, a2, g2, k_k, k_a, r_k, ln_g, ln_b, w_out):
    bsz, s, d = x.shape
    gb = bsz // n_groups
    n_heads = d // HEAD
    L = RWKV_CHUNK
    n_sub = MIX_ROWS
    tc = n_sub * L
    assert gb % n_sub == 0 and s % L == 0
    lora0 = 6 * d
    w_main = w_in[:, 0:6 * d].astype(BF16)
    pad = lambda t, n: jnp.pad(t, ((0, 0), (0, n - t.shape[1])))
    lora_cols = (DECAY_LORA, AAA_LORA, GATE_LORA)
    lora_pads = (LANES, LANES, 2 * LANES)
    pieces_w, pieces_mu, off = [], [], lora0
    for n, p in zip(lora_cols, lora_pads):
        pieces_w.append(pad(w_in[:, off:off + n], p))
        pieces_mu.append(pad(shift_mu[None, off - 3 * d:off - 3 * d + n], p))
        off += n
    w_lora = jnp.concatenate(pieces_w, axis=1).astype(BF16)
    mu_lora = jnp.concatenate(pieces_mu, axis=1)
    w_gate = w_in[:, off:off + 2 * d].astype(BF16)
    mu_rkv = shift_mu[None, 0:3 * d]
    padr = lambda t, n: jnp.pad(t, ((0, n - t.shape[0]), (0, 0)))
    w2p = padr(w2, LANES).astype(BF16)
    a2p = padr(a2, LANES).astype(BF16)
    g2p = padr(g2, 2 * LANES).astype(BF16)
    row = lambda t: t.reshape(1, -1)
    head_of = jnp.arange(d, dtype=jnp.int32) // HEAD
    head_sel = (head_of[:, None] == jnp.arange(LANES)[None, :]).astype(BF16)
    consts = [row(norm_g), w_main, w_lora, w_gate, conv_w, mu_rkv, mu_lora, row(w0), w2p, row(a0), a2p,
              g2p, row(k_k), row(k_a), row(r_k), row(ln_g), row(ln_b), w_out.astype(BF16), head_sel, head_sel.T,
              row(norm_ffn_g)]
    kern = functools.partial(_mixer_kernel, n_heads=n_heads, d_model=d)
    call = lambda first: pl.pallas_call(
        kern,
        grid=(gb // n_sub, s // L),
        in_specs=[pl.BlockSpec((n_sub, L, d), lambda b, c: (b + first // n_sub, c, 0))]
        + [_const_spec(t.shape) for t in consts],
        out_specs=[pl.BlockSpec((n_sub, L, d), lambda b, c: (b, c, 0)),
                   pl.BlockSpec((n_sub, L, d // 2), lambda b, c: (b, c, 0))],
        out_shape=[jax.ShapeDtypeStruct((gb, s, d), F32), jax.ShapeDtypeStruct((gb, s, d // 2), jnp.int32)],
        scratch_shapes=[
            pltpu.VMEM((n_sub * SUBLANES, d), F32),
            pltpu.VMEM((n_sub * SUBLANES, 3 * d), F32),
            pltpu.VMEM((n_sub * SUBLANES, 4 * LANES), F32),
            pltpu.VMEM((n_sub, 2 * L, d), F32),
            pltpu.VMEM((n_sub, 2 * L, d), F32),
            pltpu.VMEM((tc, d), F32),
            pltpu.VMEM((tc, d), F32),
            pltpu.VMEM((n_sub, 1, d), F32),
        ] + [pltpu.VMEM((HEAD, 2 * HEAD), F32)] * (n_sub * n_heads // 2),
        compiler_params=pltpu.CompilerParams(
            dimension_semantics=("arbitrary", "arbitrary"), vmem_limit_bytes=VMEM_LIMIT_BYTES),
        name="mixer",
    )(x, *consts)
    return [functools.partial(call, g * gb) for g in range(n_groups)]


def _router_kernel(h_ref, g_ref, rwh_ref, rwl_ref, rb_ref, tri_ref, meta_ref, metat_ref, cnt_ref, run_s, *,
                   n_experts):
    tt = h_ref.shape[0]

    @pl.when(pl.program_id(0) == 0)
    def _():
        run_s[...] = jnp.zeros_like(run_s)

    h = h_ref[...]
    ms = jnp.mean(h * h, axis=-1, keepdims=True)
    hn = h * lax.rsqrt(ms + NORM_EPS) * g_ref[...]
    hn_hi = hn.astype(BF16)
    hn_lo = (hn - hn_hi.astype(F32)).astype(BF16)
    logits = (jnp.dot(hn_hi, rwh_ref[...], preferred_element_type=F32)
              + jnp.dot(hn_lo, rwh_ref[...], preferred_element_type=F32)
              + jnp.dot(hn_hi, rwl_ref[...], preferred_element_type=F32)) + rb_ref[...]
    lane = lax.broadcasted_iota(jnp.int32, (tt, LANES), 1)
    neg = jnp.float32(-jnp.inf)
    work = jnp.where(lane < n_experts, logits, neg)
    vals, idxs = [], []
    for _ in range(TOP_K):
        m = jnp.max(work, axis=-1, keepdims=True)
        i = jnp.min(jnp.where(work == m, lane, LANES), axis=-1, keepdims=True)
        vals.append(m)
        idxs.append(i)
        work = jnp.where(lane == i, neg, work)
    ex = [jnp.exp(vv - vals[0]) for vv in vals]
    den = ex[0] + ex[1] + ex[2] + ex[3]
    gates = [e / den for e in ex]

    onehot = jnp.zeros((tt, LANES), jnp.bool_)
    for kk in range(TOP_K):
        onehot = onehot | (lane == (idxs[kk] + kk * n_experts))
    oh = jnp.where(onehot, 1.0, 0.0)
    cnt = jnp.dot(tri_ref[...], oh.astype(BF16), preferred_element_type=F32)
    tot = jnp.broadcast_to(jnp.sum(oh, axis=0, keepdims=True), (SUBLANES, LANES))
    lane8 = lax.broadcasted_iota(jnp.int32, (SUBLANES, LANES), 1)
    pk = jnp.zeros_like(tot)
    te = tot
    for j in range(1, TOP_K):
        rolled = pltpu.roll(tot, j * n_experts, axis=1)
        pk = pk + jnp.where(lane8 >= j * n_experts, rolled, 0.0)
        te = te + rolled
    before = cnt + (run_s[...] + pk)[0:1, :]
    ranks = [jnp.sum(jnp.where(onehot & (lane // n_experts == kk), before, 0.0), axis=-1, keepdims=True)
             for kk in range(TOP_K)]
    run_s[...] = run_s[...] + te
    cnt_ref[...] = run_s[...]

    meta = jnp.zeros((tt, LANES), F32)
    for kk in range(TOP_K):
        meta = jnp.where(lane == kk, idxs[kk].astype(F32), meta)
        meta = jnp.where(lane == TOP_K + kk, gates[kk], meta)
        meta = jnp.where(lane == 2 * TOP_K + kk, ranks[kk], meta)
    meta_ref[...] = meta
    metat_ref[...] = meta.T[0:2 * SUBLANES, :]


def _router(h2, norm_g, router_w, router_b):
    t, d = h2.shape
    n_experts = router_w.shape[1]
    tt = min(ROUTE_TOKENS, t)
    rw = jnp.pad(router_w, ((0, 0), (0, LANES - n_experts)))
    rw_hi = rw.astype(BF16)
    rw_lo = (rw - rw_hi.astype(F32)).astype(BF16)
    rb = jnp.pad(router_b, (0, LANES - n_experts)).reshape(1, LANES)
    tok = jnp.arange(tt, dtype=jnp.int32)
    tri = (tok[:, None] > tok[None, :]).astype(BF16)
    kern = functools.partial(_router_kernel, n_experts=n_experts)
    return pl.pallas_call(
        kern,
        grid=(t // tt,),
        in_specs=[pl.BlockSpec((tt, d), lambda i: (i, 0)),
                  pl.BlockSpec((1, d), lambda i: (0, 0)),
                  pl.BlockSpec((d, LANES), lambda i: (0, 0)),
                  pl.BlockSpec((d, LANES), lambda i: (0, 0)),
                  pl.BlockSpec((1, LANES), lambda i: (0, 0)),
                  pl.BlockSpec((tt, tt), lambda i: (0, 0))],
        out_specs=[pl.BlockSpec((tt, LANES), lambda i: (i, 0)),
                   pl.BlockSpec((2 * SUBLANES, tt), lambda i: (0, i)),
                   pl.BlockSpec((SUBLANES, LANES), lambda i: (0, 0))],
        out_shape=[jax.ShapeDtypeStruct((t, LANES), F32),
                   jax.ShapeDtypeStruct((2 * SUBLANES, t), F32),
                   jax.ShapeDtypeStruct((SUBLANES, LANES), F32)],
        scratch_shapes=[pltpu.VMEM((SUBLANES, LANES), F32)],
        compiler_params=pltpu.CompilerParams(
            dimension_semantics=("arbitrary",), vmem_limit_bytes=VMEM_LIMIT_BYTES),
        name="router",
    )(h2, norm_g.reshape(1, d), rw_hi, rw_lo, rb, tri)


def _row_gather(table, idx):
    n_idx = idx.shape[0]
    d = table.shape[1]
    w = GATHER_WINDOW
    sc = plsc.get_sparse_core_info()
    n_workers = sc.num_cores * sc.num_subcores
    per_worker = n_idx // n_workers
    n_win = per_worker // w
    assert per_worker * n_workers == n_idx and n_win * w == per_worker and n_win % 2 == 0
    mesh = plsc.VectorSubcoreMesh(core_axis_name="core", subcore_axis_name="subcore")

    @functools.partial(
        pl.kernel, out_type=jax.ShapeDtypeStruct((n_idx, d), table.dtype), mesh=mesh,
        scratch_types=[pltpu.VMEM((per_worker,), jnp.int32),
                       pltpu.VMEM((w, d), table.dtype), pltpu.VMEM((w, d), table.dtype),
                       pltpu.SemaphoreType.DMA, pltpu.SemaphoreType.DMA,
                       pltpu.SemaphoreType.DMA, pltpu.SemaphoreType.DMA])
    def gather_kernel(table_hbm, idx_hbm, out_hbm, idx_v, buf_a, buf_b, gsem_a, gsem_b, psem_a, psem_b):
        worker = lax.axis_index("subcore") * sc.num_cores + lax.axis_index("core")
        base = worker * per_worker
        pltpu.sync_copy(idx_hbm.at[pl.ds(base, per_worker)], idx_v)

        def gather(j, buf, sem):
            return pltpu.make_async_copy(table_hbm.at[idx_v.at[pl.ds(j * w, w)]], buf, sem)

        def put(j, buf, sem):
            return pltpu.make_async_copy(buf, out_hbm.at[pl.ds(base + j * w, w)], sem)

        gather(0, buf_a, gsem_a).start()

        @pl.loop(0, n_win, step=2)
        def _(j):
            gather(j, buf_a, gsem_a).wait()

            @pl.when(j > 0)
            def _():
                put(j - 1, buf_b, psem_b).wait()

            gather(j + 1, buf_b, gsem_b).start()
            put(j, buf_a, psem_a).start()
            gather(j + 1, buf_b, gsem_b).wait()
            put(j, buf_a, psem_a).wait()

            @pl.when(j + 2 < n_win)
            def _():
                gather(j + 2, buf_a, gsem_a).start()

            put(j + 1, buf_b, psem_b).start()

        put(n_win - 1, buf_b, psem_b).wait()

    return gather_kernel(table, idx)


def _slot_tokens(pos, n_slots):
    n = pos.shape[0]
    sc = plsc.get_sparse_core_info()
    n_cores, n_sub = sc.num_cores, sc.num_subcores
    rows = n // LANES
    rows_per_tile = rows // n_sub
    out_per_worker = n_slots // (n_cores * n_sub)
    init_per_tile = n_slots // n_sub
    assert rows_per_tile * n_sub * LANES == n and out_per_worker * n_cores * n_sub == n_slots
    mesh = plsc.VectorSubcoreMesh(core_axis_name="core", subcore_axis_name="subcore")
    tokens = (jnp.arange(n, dtype=jnp.int32) % (n // TOP_K)).reshape(rows, LANES)
    fill = jnp.arange(n_slots, dtype=jnp.int32) % (n // TOP_K)

    @functools.partial(
        pl.kernel, out_type=jax.ShapeDtypeStruct((n_slots,), jnp.int32), mesh=mesh,
        scratch_types=[pltpu.VMEM((rows_per_tile, LANES), jnp.int32),
                       pltpu.VMEM((rows_per_tile, LANES), jnp.int32),
                       pltpu.VMEM_SHARED((n_slots,), jnp.int32)])
    def invert_kernel(pos_hbm, tok_hbm, fill_hbm, out_hbm, idx_v, val_v, shared):
        cid = lax.axis_index("core")
        sid = lax.axis_index("subcore")
        pltpu.sync_copy(fill_hbm.at[pl.ds(sid * init_per_tile, init_per_tile)],
                        shared.at[pl.ds(sid * init_per_tile, init_per_tile)])
        pltpu.sync_copy(pos_hbm.at[pl.ds(sid * rows_per_tile, rows_per_tile)], idx_v)
        pltpu.sync_copy(tok_hbm.at[pl.ds(sid * rows_per_tile, rows_per_tile)], val_v)
        plsc.subcore_barrier()

        @pl.loop(0, rows_per_tile)
        def _(j):
            pltpu.sync_copy(val_v.at[j], shared.at[idx_v.at[j]])

        plsc.subcore_barrier()
        off = (cid * n_sub + sid) * out_per_worker
        pltpu.sync_copy(shared.at[pl.ds(off, out_per_worker)], out_hbm.at[pl.ds(off, out_per_worker)])

    return invert_kernel(pos.reshape(rows, LANES), tokens, fill)


def _experts_kernel(te_ref, nv_ref, xs_ref, w1_ref, b1_ref, w2_ref, b2_ref, ys_ref, *cast_refs, d_ff):
    i = pl.program_id(0)
    nvalid = nv_ref[i]
    if cast_refs:
        w1_use, w2_use = cast_refs
        prev = te_ref[jnp.maximum(i - 1, 0)]

        @pl.when((i == 0) | (te_ref[i] != prev))
        def _():
            w1_use[0] = w1_ref[0].astype(BF16)
            w2_use[0] = w2_ref[0].astype(BF16)
    else:
        w1_use, w2_use = w1_ref, w2_ref

    tm = xs_ref.shape[0]
    all_groups = [slice(r, r + EXPERT_ROWS) for r in range(0, tm, EXPERT_ROWS)]

    def ffn(groups):
        half = xs_ref.shape[1]
        xg = []
        for rows in groups:
            row = rows.start + lax.broadcasted_iota(jnp.int32, (EXPERT_ROWS, 1), 0)
            lo, hi = _unpack_bf16_pairs(jnp.where(row < nvalid, xs_ref[rows, :], 0))
            xg.append((lo.astype(BF16), hi.astype(BF16)))
        ug = [jnp.dot(lo, w1_use[0, 0:half, :], preferred_element_type=F32)
              + jnp.dot(hi, w1_use[0, half:2 * half, :], preferred_element_type=F32) + b1_ref[0] for lo, hi in xg]
        ag = []
        for u in ug:
            glu = jnp.minimum(u[:, 0:d_ff], SWIGLU_LIMIT)
            lin = jnp.clip(u[:, d_ff:2 * d_ff], -SWIGLU_LIMIT, SWIGLU_LIMIT)
            ag.append((glu * _sigmoid(SWIGLU_ALPHA * glu) * (lin + 1.0)).astype(BF16))
        for rows, act in zip(groups, ag):
            ys_ref[rows, :] = _pack_bf16_pairs(
                jnp.dot(act, w2_use[0], preferred_element_type=F32) + b2_ref[0])
        for rows in all_groups[len(groups):]:
            ys_ref[rows, :] = jnp.zeros((EXPERT_ROWS, ys_ref.shape[1]), ys_ref.dtype)

    for n_used in range(len(all_groups) + 1):
        lo_cnt, hi_cnt = (n_used - 1) * EXPERT_ROWS, n_used * EXPERT_ROWS
        pl.when((nvalid > lo_cnt) & (nvalid <= hi_cnt))(functools.partial(ffn, all_groups[:n_used]))


def _experts(xs, tile_expert, tile_valid, w1, b1, w2, b2):
    cast = w1.dtype != BF16
    n_slots = xs.shape[0]
    n_exp, d, two_ff = w1.shape
    d_ff = two_ff // 2
    tm = SLOT_TILE
    n_tiles = n_slots // tm
    kern = functools.partial(_experts_kernel, d_ff=d_ff)
    grid_spec = pltpu.PrefetchScalarGridSpec(
        num_scalar_prefetch=2,
        grid=(n_tiles,),
        in_specs=[pl.BlockSpec((tm, d // 2), lambda i, te, nv: (i, 0)),
                  pl.BlockSpec((1, d, two_ff), lambda i, te, nv: (te[i], 0, 0)),
                  pl.BlockSpec((1, 1, two_ff), lambda i, te, nv: (te[i], 0, 0)),
                  pl.BlockSpec((1, d_ff, d), lambda i, te, nv: (te[i], 0, 0)),
                  pl.BlockSpec((1, 1, d), lambda i, te, nv: (te[i], 0, 0))],
        out_specs=[pl.BlockSpec((tm, d // 2), lambda i, te, nv: (i, 0))]
        + ([pl.BlockSpec((1, d, two_ff), lambda i, te, nv: (te[i], 0, 0)),
            pl.BlockSpec((1, d_ff, d), lambda i, te, nv: (te[i], 0, 0))] if cast else []),
    )
    outs = pl.pallas_call(
        kern,
        grid_spec=grid_spec,
        out_shape=[jax.ShapeDtypeStruct((n_slots, d // 2), jnp.int32)]
        + ([jax.ShapeDtypeStruct(w1.shape, BF16), jax.ShapeDtypeStruct(w2.shape, BF16)] if cast else []),
        compiler_params=pltpu.CompilerParams(
            dimension_semantics=("arbitrary",), vmem_limit_bytes=VMEM_LIMIT_BYTES),
        name="experts",
    )(tile_expert, tile_valid, xs, w1, b1.reshape(n_exp, 1, two_ff), w2, b2.reshape(n_exp, 1, d))
    return outs if cast else (outs[0], w1, w2)


def _combine_kernel(h_ref, meta_ref, g_ref, *rest):
    y_refs, o_ref = rest[:TOP_K], rest[-1]
    d = h_ref.shape[1]
    half = d // 2
    acc_lo = h_ref[:, 0:half]
    acc_hi = h_ref[:, half:d]
    for kk in range(TOP_K):
        gate = meta_ref[:, TOP_K + kk:TOP_K + kk + 1]
        lo, hi = _unpack_bf16_pairs(y_refs[kk][...])
        acc_lo = acc_lo + gate * lo
        acc_hi = acc_hi + gate * hi
    ms = (jnp.sum(acc_lo * acc_lo, axis=-1, keepdims=True)
          + jnp.sum(acc_hi * acc_hi, axis=-1, keepdims=True)) * (1.0 / d)
    scale = lax.rsqrt(ms + NORM_EPS)
    o_ref[:, 0:half] = acc_lo * scale * g_ref[:, 0:half]
    o_ref[:, half:d] = acc_hi * scale * g_ref[:, half:d]


def _combine(h2, y4, meta, norm_g, out_prev, group, n_groups):
    t, d = h2.shape
    tt = min(MOVE_TOKENS, t)
    n_blk = t // tt
    prev_specs = [] if out_prev is None else [pl.BlockSpec(memory_space=pl.ANY)]
    prev_args = [] if out_prev is None else [out_prev]
    n_in = 3 + TOP_K
    y_specs = [pl.BlockSpec((tt, d // 2), functools.partial(lambda i, kk: (kk * n_blk + i, 0), kk=kk))
               for kk in range(TOP_K)]
    return pl.pallas_call(
        _combine_kernel,
        grid=(n_blk,),
        in_specs=[pl.BlockSpec((tt, d), lambda i: (i, 0)),
                  pl.BlockSpec((tt, LANES), lambda i: (i, 0)),
                  pl.BlockSpec((1, d), lambda i: (0, 0))] + y_specs + prev_specs,
        out_specs=pl.BlockSpec((tt, d), lambda i: (group * n_blk + i, 0)),
        out_shape=jax.ShapeDtypeStruct((n_groups * t, d), F32),
        input_output_aliases={} if out_prev is None else {n_in: 0},
        compiler_params=pltpu.CompilerParams(
            dimension_semantics=("arbitrary",), vmem_limit_bytes=VMEM_LIMIT_BYTES),
        name="combine",
    )(h2, meta, norm_g.reshape(1, d), *([y4] * TOP_K), *prev_args)


def _route(h2, hp2, norm_ffn_g, router_w, router_b):
    t, d = h2.shape
    n_exp = router_w.shape[1]
    tm = SLOT_TILE
    meta, metat, counts = _router(h2, norm_ffn_g, router_w, router_b)
    eidx = metat[0:TOP_K].astype(jnp.int32)
    rank = metat[2 * TOP_K:3 * TOP_K].astype(jnp.int32)
    cnt = counts[0, 0:n_exp].astype(jnp.int32)
    padded = jnp.maximum((cnt + tm - 1) // tm, 1) * tm
    seg_end = jnp.cumsum(padded)
    seg_start = seg_end - padded
    experts = jnp.arange(n_exp, dtype=jnp.int32)[:, None, None]
    start_of = jnp.sum(jnp.where(eidx[None] == experts, seg_start[:, None, None], 0), axis=0)
    pos = (start_of + rank).reshape(-1)
    n_tiles = -(-(t * TOP_K) // tm) + n_exp
    n_slots = n_tiles * tm
    tile_start = jnp.arange(n_tiles, dtype=jnp.int32) * tm
    tile_expert = jnp.minimum(jnp.sum(tile_start[:, None] >= seg_end[None, :], axis=1), n_exp - 1).astype(jnp.int32)
    tile_valid = jnp.clip(seg_start[tile_expert] + cnt[tile_expert] - tile_start, 0, tm).astype(jnp.int32)
    slot_token = _slot_tokens(pos, n_slots)
    xs = _row_gather(hp2, slot_token)
    return dict(h2=h2, meta=meta, pos=pos, xs=xs, tile_expert=tile_expert, tile_valid=tile_valid)


def kernel(x, norm_mix_g, w_in, conv_w, shift_mu, decay_w0, decay_w2, iclr_a0, iclr_a2, gate_g2, k_k, k_a,
           r_k, ln_x_g, ln_x_b, w_out, norm_ffn_g, router_w, router_b, exp_w1, exp_b1, exp_w2, exp_b2,
           norm_final_g):
    bsz, s, d = x.shape
    depth = w_in.shape[0]
    assert depth == 1, "final norm is fused into the last layer's combine kernel"
    n_groups = BATCH_GROUPS if bsz % (BATCH_GROUPS * MIX_ROWS) == 0 else 1
    mixers = _mixer(x, n_groups, norm_mix_g[0], norm_ffn_g[0], w_in[0], conv_w[0], shift_mu[0], decay_w0[0],
                    decay_w2[0], iclr_a0[0], iclr_a2[0], gate_g2[0], k_k[0], k_a[0], r_k[0], ln_x_g[0], ln_x_b[0],
                    w_out[0])
    routed = []
    for run_mixer in mixers:
        h, hp = run_mixer()
        routed.append(_route(h.reshape(-1, d), hp.reshape(-1, d // 2), norm_ffn_g[0], router_w[0], router_b[0]))
    w1, w2 = exp_w1[0], exp_w2[0]
    gathered = []
    for r in routed:
        ys, w1, w2 = _experts(r["xs"], r["tile_expert"], r["tile_valid"], w1, exp_b1[0], w2, exp_b2[0])
        gathered.append(_row_gather(ys, r["pos"]))
    out = None
    for g, (r, y4) in enumerate(zip(routed, gathered)):
        out = _combine(r["h2"], y4, r["meta"], norm_final_g, out, g, n_groups)
    return out.reshape(bsz, s, d)
```

```python
import functools

import jax
import jax.numpy as jnp
from jax import lax
from jax.experimental import pallas as pl
from jax.experimental.pallas import tpu as pltpu
from jax.experimental.pallas import tpu_sc as plsc

HEAD = 64
DECAY_LORA = 64
AAA_LORA = 64
GATE_LORA = 160
TOP_K = 4
SWIGLU_ALPHA = 1.702
SWIGLU_LIMIT = 7.0
NORM_EPS = 1e-5
GN_EPS = 64e-5
DECAY_SCALE = 0.6065306597126334

LANES = 128
SUBLANES = 8
VMEM_LIMIT_BYTES = 56 * 1024 * 1024

RWKV_CHUNK = 64
MIX_ROWS = 4
ROUTE_TOKENS = 512
MOVE_TOKENS = 512
GATHER_WINDOW = 64
SLOT_TILE = 512
EXPERT_ROWS = 256
BATCH_GROUPS = 2

BF16 = jnp.bfloat16
F32 = jnp.float32


def _dot(a, b):
    return jnp.dot(a.astype(BF16), b.astype(BF16), preferred_element_type=F32)


def _dot_nt(a, b):
    return lax.dot_general(a.astype(BF16), b.astype(BF16), (((1,), (1,)), ((), ())),
                           preferred_element_type=F32)


def _dot_tn(a, b):
    return lax.dot_general(a.astype(BF16), b.astype(BF16), (((0,), (0,)), ((), ())),
                           preferred_element_type=F32)


def _sigmoid(x):
    return 0.5 * jnp.tanh(0.5 * x) + 0.5


def _pack_bf16_pairs(x):
    n = x.shape[1] // 2
    bits = pltpu.bitcast(x.astype(BF16).astype(F32), jnp.uint32)
    packed = (bits[:, n:] & jnp.uint32(0xFFFF0000)) | (bits[:, :n] >> 16)
    return pltpu.bitcast(packed, jnp.int32)


def _unpack_bf16_pairs(w):
    bits = pltpu.bitcast(w, jnp.uint32)
    return pltpu.bitcast(bits << 16, F32), pltpu.bitcast(bits & jnp.uint32(0xFFFF0000), F32)


def _shift_rows(x, n, carry, seg):
    rolled = pltpu.roll(x, n, axis=0)
    row = lax.broadcasted_iota(jnp.int32, (SUBLANES, x.shape[1]), 0)
    pieces = []
    for j in range(x.shape[0] // seg):
        head = rolled[j * seg:j * seg + SUBLANES]
        prev = carry[j * SUBLANES:(j + 1) * SUBLANES]
        for i in range(n):
            head = jnp.where(row == i, prev[SUBLANES - n + i:SUBLANES - n + i + 1, :], head)
        pieces += [head, rolled[j * seg + SUBLANES:(j + 1) * seg]]
    return jnp.concatenate(pieces, axis=0)


def _last_rows(x, seg):
    return jnp.concatenate([x[(j + 1) * seg - SUBLANES:(j + 1) * seg] for j in range(x.shape[0] // seg)],
                           axis=0)


def _mixer_kernel(x_ref, g_ref, wmain_ref, wlora_ref, wgate_ref, cw_ref, mu_rkv_ref, mu_lora_ref,
                  w0_ref, w2_ref, a0_ref, a2_ref, g2_ref, kk_ref, ka_ref, rk_ref, lng_ref, lnb_ref,
                  wout_ref, hsel_ref, hselt_ref, gffn_ref, o_ref, hp_ref,
                  cu_s, cp_s, cl_s, ar_s, bk_s, v_s, y_s, gam_s, *state_s,
                  n_heads, d_model):
    n_sub, L = x_ref.shape[0], x_ref.shape[1]
    tc = n_sub * L
    D = d_model

    @pl.when(pl.program_id(1) == 0)
    def _():
        for st_ref in state_s:
            st_ref[...] = jnp.zeros_like(st_ref)
        cu_s[...] = jnp.zeros_like(cu_s)
        cp_s[...] = jnp.zeros_like(cp_s)
        cl_s[...] = jnp.zeros_like(cl_s)

    def split(t):
        hi = t.astype(BF16)
        return hi, (t - hi.astype(F32)).astype(BF16)

    def head_sums(parts, exact=False):
        per_head = sum(jnp.dot(p, hsel_ref[...], preferred_element_type=F32) for p in parts)
        back = split(per_head) if exact else [per_head.astype(BF16)]
        return sum(jnp.dot(p, hselt_ref[...], preferred_element_type=F32) for p in back)

    x = x_ref[...].reshape(tc, D)
    ms = jnp.mean(x * x, axis=-1, keepdims=True)
    xn = (x * lax.rsqrt(ms + NORM_EPS) * g_ref[...]).astype(BF16)

    pc = jnp.dot(xn, wmain_ref[:, 0:3 * D], preferred_element_type=F32)
    u = pc[:, D:2 * D] * pc[:, 2 * D:3 * D]
    cu = cu_s[...]
    conv = (cw_ref[0:1, :] * _shift_rows(u, 2, cu, L) + cw_ref[1:2, :] * _shift_rows(u, 1, cu, L)
            + cw_ref[2:3, :] * u)
    y_conv = pc[:, 0:D] * conv
    cu_s[...] = _last_rows(u, L)

    pr = jnp.dot(xn, wmain_ref[:, 3 * D:6 * D], preferred_element_type=F32)
    cp = cp_s[...]
    cp_s[...] = _last_rows(pr, L)
    pr = pr + (_shift_rows(pr, 1, cp, L) - pr) * mu_rkv_ref[...]
    plo = jnp.dot(xn, wlora_ref[...], preferred_element_type=F32)
    cl = cl_s[...]
    cl_s[...] = _last_rows(plo, L)
    plo = plo + (_shift_rows(plo, 1, cl, L) - plo) * mu_lora_ref[...]

    r = pr[:, 0:D]
    k = pr[:, D:2 * D]
    v = pr[:, 2 * D:3 * D]
    wd = plo[:, 0:LANES]
    ad = plo[:, LANES:2 * LANES]
    gd = plo[:, 2 * LANES:4 * LANES]

    lw = -DECAY_SCALE * _sigmoid(w0_ref[...] + _dot(jnp.tanh(wd), w2_ref[...]))
    a = _sigmoid(a0_ref[...] + _dot(ad, a2_ref[...]))
    g = _dot(_sigmoid(gd), g2_ref[...])

    row = lax.broadcasted_iota(jnp.int32, (tc, tc), 0)
    col = lax.broadcasted_iota(jnp.int32, (tc, tc), 1)
    tri = jnp.where((row >= col) & ((row // L) == (col // L)), 1.0, 0.0).astype(BF16)
    cum = sum(jnp.dot(tri, p, preferred_element_type=F32) for p in split(lw))
    e_inv = jnp.exp(-cum)

    kkraw = k * kk_ref[...]
    ss = head_sums([(kkraw * kkraw).astype(BF16)])
    kkn = kkraw * jnp.minimum(lax.rsqrt(ss), 1e12)

    k2 = k * (1.0 + (a - 1.0) * ka_ref[...])
    a_t = kkn * jnp.exp(cum - lw)
    r_t = r * jnp.exp(cum)
    b_t = kkn * a * e_inv
    k_t = k2 * e_inv
    for c in range(n_sub):
        rows = slice(c * L, (c + 1) * L)
        ar_s[c, 0:L, :] = a_t[rows]
        ar_s[c, L:2 * L, :] = r_t[rows]
        bk_s[c, 0:L, :] = b_t[rows]
        bk_s[c, L:2 * L, :] = k_t[rows]
        gam_s[c] = jnp.exp(cum[(c + 1) * L - 1:(c + 1) * L, :])
    v_s[...] = v
    bonus = head_sums([(r * k2 * rk_ref[...]).astype(BF16)]) * v

    pair = 2 * HEAD
    lane = lax.broadcasted_iota(jnp.int32, (1, pair), 1)
    left = lane < HEAD
    row1 = lax.broadcasted_iota(jnp.int32, (L, pair), 0)
    col1 = lax.broadcasted_iota(jnp.int32, (L, pair), 1) % HEAD
    strict = row1 > col1
    eye = jnp.where(row1 == col1, 1.0, 0.0)
    incl2 = (lax.broadcasted_iota(jnp.int32, (L, 2 * pair), 0)
             >= lax.broadcasted_iota(jnp.int32, (L, 2 * pair), 1) % HEAD)
    level_mask = [(row1 // (2 * s) == col1 // (2 * s)) & (row1 % (2 * s) >= s) & (col1 % (2 * s) < s)
                  for s in (1 << i for i in range(L.bit_length() - 1))]

    def blockdiag(t):
        tb = t.astype(BF16)
        return jnp.concatenate([jnp.where(left, tb, 0), jnp.where(left, 0, tb)], axis=0)

    pairs = range(n_heads // 2)
    psl = [slice(p * pair, (p + 1) * pair) for p in pairs]
    work = [(c, p) for c in range(n_sub) for p in pairs]
    ar = [ar_s[c, :, psl[p]].astype(BF16) for c, p in work]
    bk = [bk_s[c, :, psl[p]].astype(BF16) for c, p in work]
    vp = [v_s[c * L:(c + 1) * L, psl[p]] for c, p in work]
    gm = [_dot_nt(ar[i], jnp.concatenate([blockdiag(bk[i][0:L]), blockdiag(bk[i][L:2 * L])], axis=0))
          for i in range(len(work))]
    nm = [jnp.where(strict, g_[0:L, 0:pair], 0.0) for g_ in gm]
    mak = [jnp.where(strict, g_[0:L, pair:2 * pair], 0.0) for g_ in gm]
    q = [jnp.where(incl2, g_[L:2 * L, :], 0.0).astype(BF16) for g_ in gm]
    xinv = [eye - jnp.where(level_mask[0], n_, 0.0) for n_ in nm]
    for lm in level_mask[1:]:
        half = [_dot(jnp.where(lm, n_, 0.0), blockdiag(t)) for n_, t in zip(nm, xinv)]
        xinv = [t - _dot(t, blockdiag(h_)) for t, h_ in zip(xinv, half)]
    xinv = [t.astype(BF16) for t in xinv]
    mv = [_dot(m_, blockdiag(v_)) for m_, v_ in zip(mak, vp)]

    gates = _sigmoid(jnp.dot(xn, wgate_ref[...], preferred_element_type=F32))

    items = range(len(work))
    st = [state_s[i][...] for i in items]
    ars = [_dot_nt(ar[i], blockdiag(st[i])) for i in items]
    uu = [_dot(xinv[i], blockdiag(ars[i][0:L] + mv[i])) for i in items]
    yh = [ars[i][L:2 * L] + _dot(q[i], jnp.concatenate([blockdiag(-uu[i]), blockdiag(vp[i])], axis=0))
          for i in items]
    for i, (c, p) in enumerate(work):
        upd = _dot_tn(jnp.concatenate([-uu[i], vp[i]], axis=0), bk[i])
        state_s[i][...] = (st[i] + jnp.where(left, upd[0:HEAD], upd[HEAD:pair])) * gam_s[c, :, psl[p]]
    for i, (c, p) in enumerate(work):
        y_s[c * L:(c + 1) * L, psl[p]] = yh[i]

    y = y_s[...]
    yc = y - head_sums(split(y), exact=True) * (1.0 / HEAD)
    var = head_sums([(yc * yc).astype(BF16)]) * (1.0 / HEAD)
    y_rwkv = (yc * lax.rsqrt(var + GN_EPS) * lng_ref[...] + lnb_ref[...] + bonus) * g
    mix = gates[:, 0:D] * y_conv + gates[:, D:2 * D] * y_rwkv
    h = x + jnp.dot(mix.astype(BF16), wout_ref[...], preferred_element_type=F32)
    o_ref[...] = h.reshape(n_sub, L, D)
    hn = h * lax.rsqrt(jnp.mean(h * h, axis=-1, keepdims=True) + NORM_EPS) * gffn_ref[...]
    hp_ref[...] = _pack_bf16_pairs(hn).reshape(n_sub, L, D // 2)


def _const_spec(shape):
    nd = len(shape)
    return pl.BlockSpec(shape, lambda *_: (0,) * nd, pipeline_mode=pl.Buffered(1))


def _mixer(x, n_groups, norm_g, norm_ffn_g, w_in, conv_w, shift_mu, w0, w2, a0, a2, g2, k_k, k_a, r_k, ln_g, ln_b, w_out):
    bsz, s, d = x.shape
    gb = bsz // n_groups
    n_heads = d // HEAD
    L = RWKV_CHUNK
    n_sub = MIX_ROWS
    tc = n_sub * L
    assert gb % n_sub == 0 and s % L == 0
    lora0 = 6 * d
    w_main = w_in[:, 0:6 * d].astype(BF16)
    pad = lambda t, n: jnp.pad(t, ((0, 0), (0, n - t.shape[1])))
    lora_cols = (DECAY_LORA, AAA_LORA, GATE_LORA)
    lora_pads = (LANES, LANES, 2 * LANES)
    pieces_w, pieces_mu, off = [], [], lora0
    for n, p in zip(lora_cols, lora_pads):
        pieces_w.append(pad(w_in[:, off:off + n], p))
        pieces_mu.append(pad(shift_mu[None, off - 3 * d:off - 3 * d + n], p))
        off += n
    w_lora = jnp.concatenate(pieces_w, axis=1).astype(BF16)
    mu_lora = jnp.concatenate(pieces_mu, axis=1)
    w_gate = w_in[:, off:off + 2 * d].astype(BF16)
    mu_rkv = shift_mu[None, 0:3 * d]
    padr = lambda t, n: jnp.pad(t, ((0, n - t.shape[0]), (0, 0)))
    w2p = padr(w2, LANES).astype(BF16)
    a2p = padr(a2, LANES).astype(BF16)
    g2p = padr(g2, 2 * LANES).astype(BF16)
    row = lambda t: t.reshape(1, -1)
    head_of = jnp.arange(d, dtype=jnp.int32) // HEAD
    head_sel = (head_of[:, None] == jnp.arange(LANES)[None, :]).astype(BF16)
    consts = [row(norm_g), w_main, w_lora, w_gate, conv_w, mu_rkv, mu_lora, row(w0), w2p, row(a0), a2p,
              g2p, row(k_k), row(k_a), row(r_k), row(ln_g), row(ln_b), w_out.astype(BF16), head_sel, head_sel.T,
              row(norm_ffn_g)]
    kern = functools.partial(_mixer_kernel, n_heads=n_heads, d_model=d)
    call = lambda first: pl.pallas_call(
        kern,
        grid=(gb // n_sub, s // L),
        in_specs=[pl.BlockSpec((n_sub, L, d), lambda b, c: (b + first // n_sub, c, 0))]
        + [_const_spec(t.shape) for t in consts],
        out_specs=[pl.BlockSpec((n_sub, L, d), lambda b, c: (b, c, 0)),
                   pl.BlockSpec((n_sub, L, d // 2), lambda b, c: (b, c, 0))],
        out_shape=[jax.ShapeDtypeStruct((gb, s, d), F32), jax.ShapeDtypeStruct((gb, s, d // 2), jnp.int32)],
        scratch_shapes=[
            pltpu.VMEM((n_sub * SUBLANES, d), F32),
            pltpu.VMEM((n_sub * SUBLANES, 3 * d), F32),
            pltpu.VMEM((n_sub * SUBLANES, 4 * LANES), F32),
            pltpu.VMEM((n_sub, 2 * L, d), F32),
            pltpu.VMEM((n_sub, 2 * L, d), F32),
            pltpu.VMEM((tc, d), F32),
            pltpu.VMEM((tc, d), F32),
            pltpu.VMEM((n_sub, 1, d), F32),
        ] + [pltpu.VMEM((HEAD, 2 * HEAD), F32)] * (n_sub * n_heads // 2),
        compiler_params=pltpu.CompilerParams(
            dimension_semantics=("arbitrary", "arbitrary"), vmem_limit_bytes=VMEM_LIMIT_BYTES),
        name="mixer",
    )(x, *consts)
    return [functools.partial(call, g * gb) for g in range(n_groups)]


def _router_kernel(h_ref, g_ref, rwh_ref, rwl_ref, rb_ref, tri_ref, meta_ref, metat_ref, cnt_ref, run_s, *,
                   n_experts):
    tt = h_ref.shape[0]

    @pl.when(pl.program_id(0) == 0)
    def _():
        run_s[...] = jnp.zeros_like(run_s)

    h = h_ref[...]
    ms = jnp.mean(h * h, axis=-1, keepdims=True)
    hn = h * lax.rsqrt(ms + NORM_EPS) * g_ref[...]
    hn_hi = hn.astype(BF16)
    hn_lo = (hn - hn_hi.astype(F32)).astype(BF16)
    logits = (jnp.dot(hn_hi, rwh_ref[...], preferred_element_type=F32)
              + jnp.dot(hn_lo, rwh_ref[...], preferred_element_type=F32)
              + jnp.dot(hn_hi, rwl_ref[...], preferred_element_type=F32)) + rb_ref[...]
    lane = lax.broadcasted_iota(jnp.int32, (tt, LANES), 1)
    neg = jnp.float32(-jnp.inf)
    work = jnp.where(lane < n_experts, logits, neg)
    vals, idxs = [], []
    for _ in range(TOP_K):
        m = jnp.max(work, axis=-1, keepdims=True)
        i = jnp.min(jnp.where(work == m, lane, LANES), axis=-1, keepdims=True)
        vals.append(m)
        idxs.append(i)
        work = jnp.where(lane == i, neg, work)
    ex = [jnp.exp(vv - vals[0]) for vv in vals]
    den = ex[0] + ex[1] + ex[2] + ex[3]
    gates = [e / den for e in ex]

    onehot = jnp.zeros((tt, LANES), jnp.bool_)
    for kk in range(TOP_K):
        onehot = onehot | (lane == (idxs[kk] + kk * n_experts))
    oh = jnp.where(onehot, 1.0, 0.0)
    cnt = jnp.dot(tri_ref[...], oh.astype(BF16), preferred_element_type=F32)
    tot = jnp.broadcast_to(jnp.sum(oh, axis=0, keepdims=True), (SUBLANES, LANES))
    lane8 = lax.broadcasted_iota(jnp.int32, (SUBLANES, LANES), 1)
    pk = jnp.zeros_like(tot)
    te = tot
    for j in range(1, TOP_K):
        rolled = pltpu.roll(tot, j * n_experts, axis=1)
        pk = pk + jnp.where(lane8 >= j * n_experts, rolled, 0.0)
        te = te + rolled
    before = cnt + (run_s[...] + pk)[0:1, :]
    ranks = [jnp.sum(jnp.where(onehot & (lane // n_experts == kk), before, 0.0), axis=-1, keepdims=True)
             for kk in range(TOP_K)]
    run_s[...] = run_s[...] + te
    cnt_ref[...] = run_s[...]

    meta = jnp.zeros((tt, LANES), F32)
    for kk in range(TOP_K):
        meta = jnp.where(lane == kk, idxs[kk].astype(F32), meta)
        meta = jnp.where(lane == TOP_K + kk, gates[kk], meta)
        meta = jnp.where(lane == 2 * TOP_K + kk, ranks[kk], meta)
    meta_ref[...] = meta
    metat_ref[...] = meta.T[0:2 * SUBLANES, :]


def _router(h2, norm_g, router_w, router_b):
    t, d = h2.shape
    n_experts = router_w.shape[1]
    tt = min(ROUTE_TOKENS, t)
    rw = jnp.pad(router_w, ((0, 0), (0, LANES - n_experts)))
    rw_hi = rw.astype(BF16)
    rw_lo = (rw - rw_hi.astype(F32)).astype(BF16)
    rb = jnp.pad(router_b, (0, LANES - n_experts)).reshape(1, LANES)
    tok = jnp.arange(tt, dtype=jnp.int32)
    tri = (tok[:, None] > tok[None, :]).astype(BF16)
    kern = functools.partial(_router_kernel, n_experts=n_experts)
    return pl.pallas_call(
        kern,
        grid=(t // tt,),
        in_specs=[pl.BlockSpec((tt, d), lambda i: (i, 0)),
                  pl.BlockSpec((1, d), lambda i: (0, 0)),
                  pl.BlockSpec((d, LANES), lambda i: (0, 0)),
                  pl.BlockSpec((d, LANES), lambda i: (0, 0)),
                  pl.BlockSpec((1, LANES), lambda i: (0, 0)),
                  pl.BlockSpec((tt, tt), lambda i: (0, 0))],
        out_specs=[pl.BlockSpec((tt, LANES), lambda i: (i, 0)),
                   pl.BlockSpec((2 * SUBLANES, tt), lambda i: (0, i)),
                   pl.BlockSpec((SUBLANES, LANES), lambda i: (0, 0))],
        out_shape=[jax.ShapeDtypeStruct((t, LANES), F32),
                   jax.ShapeDtypeStruct((2 * SUBLANES, t), F32),
                   jax.ShapeDtypeStruct((SUBLANES, LANES), F32)],
        scratch_shapes=[pltpu.VMEM((SUBLANES, LANES), F32)],
        compiler_params=pltpu.CompilerParams(
            dimension_semantics=("arbitrary",), vmem_limit_bytes=VMEM_LIMIT_BYTES),
        name="router",
    )(h2, norm_g.reshape(1, d), rw_hi, rw_lo, rb, tri)


def _gather_scratch(per_worker, d, dtype):
    w = GATHER_WINDOW
    return [pltpu.VMEM((per_worker,), jnp.int32),
            pltpu.VMEM((w, d), dtype), pltpu.VMEM((w, d), dtype),
            pltpu.SemaphoreType.DMA, pltpu.SemaphoreType.DMA, pltpu.SemaphoreType.DMA, pltpu.SemaphoreType.DMA]


def _gather_rows(table_hbm, out_hbm, base, idx_v, buf_a, buf_b, gsem_a, gsem_b, psem_a, psem_b):
    w = GATHER_WINDOW
    n_win = idx_v.shape[0] // w

    def gather(j, buf, sem):
        return pltpu.make_async_copy(table_hbm.at[idx_v.at[pl.ds(j * w, w)]], buf, sem)

    def put(j, buf, sem):
        return pltpu.make_async_copy(buf, out_hbm.at[pl.ds(base + j * w, w)], sem)

    gather(0, buf_a, gsem_a).start()

    @pl.loop(0, n_win, step=2)
    def _(j):
        gather(j, buf_a, gsem_a).wait()

        @pl.when(j > 0)
        def _():
            put(j - 1, buf_b, psem_b).wait()

        gather(j + 1, buf_b, gsem_b).start()
        put(j, buf_a, psem_a).start()
        gather(j + 1, buf_b, gsem_b).wait()
        put(j, buf_a, psem_a).wait()

        @pl.when(j + 2 < n_win)
        def _():
            gather(j + 2, buf_a, gsem_a).start()

        put(j + 1, buf_b, psem_b).start()

    put(n_win - 1, buf_b, psem_b).wait()


def _worker_split(n_rows):
    sc = plsc.get_sparse_core_info()
    n_workers = sc.num_cores * sc.num_subcores
    per_worker = n_rows // n_workers
    n_win = per_worker // GATHER_WINDOW
    assert per_worker * n_workers == n_rows and n_win * GATHER_WINDOW == per_worker and n_win % 2 == 0
    return sc, per_worker


def _row_gather(table, idx):
    n_idx, d = idx.shape[0], table.shape[1]
    sc, per_worker = _worker_split(n_idx)
    mesh = plsc.VectorSubcoreMesh(core_axis_name="core", subcore_axis_name="subcore")

    @functools.partial(pl.kernel, out_type=jax.ShapeDtypeStruct((n_idx, d), table.dtype), mesh=mesh,
                       scratch_types=_gather_scratch(per_worker, d, table.dtype))
    def gather_kernel(table_hbm, idx_hbm, out_hbm, idx_v, *bufs):
        base = (lax.axis_index("subcore") * sc.num_cores + lax.axis_index("core")) * per_worker
        pltpu.sync_copy(idx_hbm.at[pl.ds(base, per_worker)], idx_v)
        _gather_rows(table_hbm, out_hbm, base, idx_v, *bufs)

    return gather_kernel(table, idx)


def _dispatch_gather(table, pos, n_slots):
    n, d = pos.shape[0], table.shape[1]
    sc, per_worker = _worker_split(n_slots)
    n_sub = sc.num_subcores
    rows = n // LANES
    rows_per_tile = rows // n_sub
    init_per_tile = n_slots // n_sub
    assert rows_per_tile * n_sub * LANES == n and init_per_tile * n_sub == n_slots
    mesh = plsc.VectorSubcoreMesh(core_axis_name="core", subcore_axis_name="subcore")
    tokens = (jnp.arange(n, dtype=jnp.int32) % (n // TOP_K)).reshape(rows, LANES)
    fill = jnp.arange(n_slots, dtype=jnp.int32) % (n // TOP_K)

    @functools.partial(
        pl.kernel, out_type=jax.ShapeDtypeStruct((n_slots, d), table.dtype), mesh=mesh,
        scratch_types=[pltpu.VMEM((rows_per_tile, LANES), jnp.int32),
                       pltpu.VMEM((rows_per_tile, LANES), jnp.int32),
                       pltpu.VMEM_SHARED((n_slots,), jnp.int32)] + _gather_scratch(per_worker, d, table.dtype))
    def dispatch_kernel(table_hbm, pos_hbm, tok_hbm, fill_hbm, out_hbm, pos_v, tok_v, slot_token, idx_v, *bufs):
        cid = lax.axis_index("core")
        sid = lax.axis_index("subcore")
        pltpu.sync_copy(fill_hbm.at[pl.ds(sid * init_per_tile, init_per_tile)],
                        slot_token.at[pl.ds(sid * init_per_tile, init_per_tile)])
        pltpu.sync_copy(pos_hbm.at[pl.ds(sid * rows_per_tile, rows_per_tile)], pos_v)
        pltpu.sync_copy(tok_hbm.at[pl.ds(sid * rows_per_tile, rows_per_tile)], tok_v)
        plsc.subcore_barrier()

        @pl.loop(0, rows_per_tile)
        def _(j):
            pltpu.sync_copy(tok_v.at[j], slot_token.at[pos_v.at[j]])

        plsc.subcore_barrier()
        base = (sid * sc.num_cores + cid) * per_worker
        pltpu.sync_copy(slot_token.at[pl.ds(base, per_worker)], idx_v)
        _gather_rows(table_hbm, out_hbm, base, idx_v, *bufs)

    return dispatch_kernel(table, pos.reshape(rows, LANES), tokens, fill)


def _experts_kernel(te_ref, nv_ref, xs_ref, w1_ref, b1_ref, w2_ref, b2_ref, ys_ref, *cast_refs, d_ff):
    i = pl.program_id(0)
    nvalid = nv_ref[i]
    if cast_refs:
        w1_use, w2_use = cast_refs
        prev = te_ref[jnp.maximum(i - 1, 0)]

        @pl.when((i == 0) | (te_ref[i] != prev))
        def _():
            w1_use[0] = w1_ref[0].astype(BF16)
            w2_use[0] = w2_ref[0].astype(BF16)
    else:
        w1_use, w2_use = w1_ref, w2_ref

    tm = xs_ref.shape[0]
    all_groups = [slice(r, r + EXPERT_ROWS) for r in range(0, tm, EXPERT_ROWS)]

    def ffn(groups):
        half = xs_ref.shape[1]
        xg = []
        for rows in groups:
            row = rows.start + lax.broadcasted_iota(jnp.int32, (EXPERT_ROWS, 1), 0)
            lo, hi = _unpack_bf16_pairs(jnp.where(row < nvalid, xs_ref[rows, :], 0))
            xg.append((lo.astype(BF16), hi.astype(BF16)))
        ug = [jnp.dot(lo, w1_use[0, 0:half, :], preferred_element_type=F32)
              + jnp.dot(hi, w1_use[0, half:2 * half, :], preferred_element_type=F32) + b1_ref[0] for lo, hi in xg]
        ag = []
        for u in ug:
            glu = jnp.minimum(u[:, 0:d_ff], SWIGLU_LIMIT)
            lin = jnp.clip(u[:, d_ff:2 * d_ff], -SWIGLU_LIMIT, SWIGLU_LIMIT)
            ag.append((glu * _sigmoid(SWIGLU_ALPHA * glu) * (lin + 1.0)).astype(BF16))
        for rows, act in zip(groups, ag):
            ys_ref[rows, :] = _pack_bf16_pairs(
                jnp.dot(act, w2_use[0], preferred_element_type=F32) + b2_ref[0])
        for rows in all_groups[len(groups):]:
            ys_ref[rows, :] = jnp.zeros((EXPERT_ROWS, ys_ref.shape[1]), ys_ref.dtype)

    for n_used in range(len(all_groups) + 1):
        lo_cnt, hi_cnt = (n_used - 1) * EXPERT_ROWS, n_used * EXPERT_ROWS
        pl.when((nvalid > lo_cnt) & (nvalid <= hi_cnt))(functools.partial(ffn, all_groups[:n_used]))


def _experts(xs, tile_expert, tile_valid, w1, b1, w2, b2):
    cast = w1.dtype != BF16
    n_slots = xs.shape[0]
    n_exp, d, two_ff = w1.shape
    d_ff = two_ff // 2
    tm = SLOT_TILE
    n_tiles = n_slots // tm
    kern = functools.partial(_experts_kernel, d_ff=d_ff)
    grid_spec = pltpu.PrefetchScalarGridSpec(
        num_scalar_prefetch=2,
        grid=(n_tiles,),
        in_specs=[pl.BlockSpec((tm, d // 2), lambda i, te, nv: (i, 0)),
                  pl.BlockSpec((1, d, two_ff), lambda i, te, nv: (te[i], 0, 0)),
                  pl.BlockSpec((1, 1, two_ff), lambda i, te, nv: (te[i], 0, 0)),
                  pl.BlockSpec((1, d_ff, d), lambda i, te, nv: (te[i], 0, 0)),
                  pl.BlockSpec((1, 1, d), lambda i, te, nv: (te[i], 0, 0))],
        out_specs=[pl.BlockSpec((tm, d // 2), lambda i, te, nv: (i, 0))]
        + ([pl.BlockSpec((1, d, two_ff), lambda i, te, nv: (te[i], 0, 0)),
            pl.BlockSpec((1, d_ff, d), lambda i, te, nv: (te[i], 0, 0))] if cast else []),
    )
    outs = pl.pallas_call(
        kern,
        grid_spec=grid_spec,
        out_shape=[jax.ShapeDtypeStruct((n_slots, d // 2), jnp.int32)]
        + ([jax.ShapeDtypeStruct(w1.shape, BF16), jax.ShapeDtypeStruct(w2.shape, BF16)] if cast else []),
        compiler_params=pltpu.CompilerParams(
            dimension_semantics=("arbitrary",), vmem_limit_bytes=VMEM_LIMIT_BYTES),
        name="experts",
    )(tile_expert, tile_valid, xs, w1, b1.reshape(n_exp, 1, two_ff), w2, b2.reshape(n_exp, 1, d))
    return outs if cast else (outs[0], w1, w2)


def _combine_kernel(h_ref, meta_ref, g_ref, *rest):
    y_refs, o_ref = rest[:TOP_K], rest[-1]
    d = h_ref.shape[1]
    half = d // 2
    acc_lo = h_ref[:, 0:half]
    acc_hi = h_ref[:, half:d]
    for kk in range(TOP_K):
        gate = meta_ref[:, TOP_K + kk:TOP_K + kk + 1]
        lo, hi = _unpack_bf16_pairs(y_refs[kk][...])
        acc_lo = acc_lo + gate * lo
        acc_hi = acc_hi + gate * hi
    ms = (jnp.sum(acc_lo * acc_lo, axis=-1, keepdims=True)
          + jnp.sum(acc_hi * acc_hi, axis=-1, keepdims=True)) * (1.0 / d)
    scale = lax.rsqrt(ms + NORM_EPS)
    o_ref[:, 0:half] = acc_lo * scale * g_ref[:, 0:half]
    o_ref[:, half:d] = acc_hi * scale * g_ref[:, half:d]


def _combine(h2, y4, meta, norm_g, out_prev, group, n_groups):
    t, d = h2.shape
    tt = min(MOVE_TOKENS, t)
    n_blk = t // tt
    prev_specs = [] if out_prev is None else [pl.BlockSpec(memory_space=pl.ANY)]
    prev_args = [] if out_prev is None else [out_prev]
    n_in = 3 + TOP_K
    y_specs = [pl.BlockSpec((tt, d // 2), functools.partial(lambda i, kk: (kk * n_blk + i, 0), kk=kk))
               for kk in range(TOP_K)]
    return pl.pallas_call(
        _combine_kernel,
        grid=(n_blk,),
        in_specs=[pl.BlockSpec((tt, d), lambda i: (i, 0)),
                  pl.BlockSpec((tt, LANES), lambda i: (i, 0)),
                  pl.BlockSpec((1, d), lambda i: (0, 0))] + y_specs + prev_specs,
        out_specs=pl.BlockSpec((tt, d), lambda i: (group * n_blk + i, 0)),
        out_shape=jax.ShapeDtypeStruct((n_groups * t, d), F32),
        input_output_aliases={} if out_prev is None else {n_in: 0},
        compiler_params=pltpu.CompilerParams(
            dimension_semantics=("arbitrary",), vmem_limit_bytes=VMEM_LIMIT_BYTES),
        name="combine",
    )(h2, meta, norm_g.reshape(1, d), *([y4] * TOP_K), *prev_args)


def _route(h2, hp2, norm_ffn_g, router_w, router_b):
    t, d = h2.shape
    n_exp = router_w.shape[1]
    tm = SLOT_TILE
    meta, metat, counts = _router(h2, norm_ffn_g, router_w, router_b)
    eidx = metat[0:TOP_K].astype(jnp.int32)
    rank = metat[2 * TOP_K:3 * TOP_K].astype(jnp.int32)
    cnt = counts[0, 0:n_exp].astype(jnp.int32)
    padded = jnp.maximum((cnt + tm - 1) // tm, 1) * tm
    seg_end = jnp.cumsum(padded)
    seg_start = seg_end - padded
    experts = jnp.arange(n_exp, dtype=jnp.int32)[:, None, None]
    start_of = jnp.sum(jnp.where(eidx[None] == experts, seg_start[:, None, None], 0), axis=0)
    pos = (start_of + rank).reshape(-1)
    n_tiles = -(-(t * TOP_K) // tm) + n_exp
    n_slots = n_tiles * tm
    tile_start = jnp.arange(n_tiles, dtype=jnp.int32) * tm
    tile_expert = jnp.minimum(jnp.sum(tile_start[:, None] >= seg_end[None, :], axis=1), n_exp - 1).astype(jnp.int32)
    tile_valid = jnp.clip(seg_start[tile_expert] + cnt[tile_expert] - tile_start, 0, tm).astype(jnp.int32)
    xs = _dispatch_gather(hp2, pos, n_slots)
    return dict(h2=h2, meta=meta, pos=pos, xs=xs, tile_expert=tile_expert, tile_valid=tile_valid)


def kernel(x, norm_mix_g, w_in, conv_w, shift_mu, decay_w0, decay_w2, iclr_a0, iclr_a2, gate_g2, k_k, k_a,
           r_k, ln_x_g, ln_x_b, w_out, norm_ffn_g, router_w, router_b, exp_w1, exp_b1, exp_w2, exp_b2,
           norm_final_g):
    bsz, s, d = x.shape
    depth = w_in.shape[0]
    assert depth == 1, "final norm is fused into the last layer's combine kernel"
    n_groups = BATCH_GROUPS if bsz % (BATCH_GROUPS * MIX_ROWS) == 0 else 1
    mixers = _mixer(x, n_groups, norm_mix_g[0], norm_ffn_g[0], w_in[0], conv_w[0], shift_mu[0], decay_w0[0],
                    decay_w2[0], iclr_a0[0], iclr_a2[0], gate_g2[0], k_k[0], k_a[0], r_k[0], ln_x_g[0], ln_x_b[0],
                    w_out[0])
    routed = []
    for run_mixer in mixers:
        h, hp = run_mixer()
        routed.append(_route(h.reshape(-1, d), hp.reshape(-1, d // 2), norm_ffn_g[0], router_w[0], router_b[0]))
    w1, w2 = exp_w1[0], exp_w2[0]
    gathered = []
    for r in routed:
        ys, w1, w2 = _experts(r["xs"], r["tile_expert"], r["tile_valid"], w1, exp_b1[0], w2, exp_b2[0])
        gathered.append(_row_gather(ys, r["pos"]))
    out = None
    for g, (r, y4) in enumerate(zip(routed, gathered)):
        out = _combine(r["h2"], y4, r["meta"], norm_final_g, out, g, n_groups)
    return out.reshape(bsz, s, d)
```

```python
import functools

import jax
import jax.numpy as jnp
from jax import lax
from jax.experimental import pallas as pl
from jax.experimental.pallas import tpu as pltpu
from jax.experimental.pallas import tpu_sc as plsc

HEAD = 64
DECAY_LORA = 64
AAA_LORA = 64
GATE_LORA = 160
TOP_K = 4
SWIGLU_ALPHA = 1.702
SWIGLU_LIMIT = 7.0
NORM_EPS = 1e-5
GN_EPS = 64e-5
DECAY_SCALE = 0.6065306597126334

LANES = 128
SUBLANES = 8
VMEM_LIMIT_BYTES = 56 * 1024 * 1024

RWKV_CHUNK = 64
MIX_ROWS = 4
ROUTE_TOKENS = 512
MOVE_TOKENS = 512
SLOT_TOKENS = 8192
GATHER_WINDOW = 64
SLOT_TILE = 512
EXPERT_ROWS = 256
BATCH_GROUPS = 2

BF16 = jnp.bfloat16
F32 = jnp.float32


def _dot(a, b):
    return jnp.dot(a.astype(BF16), b.astype(BF16), preferred_element_type=F32)


def _dot_nt(a, b):
    return lax.dot_general(a.astype(BF16), b.astype(BF16), (((1,), (1,)), ((), ())),
                           preferred_element_type=F32)


def _dot_tn(a, b):
    return lax.dot_general(a.astype(BF16), b.astype(BF16), (((0,), (0,)), ((), ())),
                           preferred_element_type=F32)


def _sigmoid(x):
    return 0.5 * jnp.tanh(0.5 * x) + 0.5


def _pack_bf16_pairs(x):
    n = x.shape[1] // 2
    bits = pltpu.bitcast(x.astype(BF16).astype(F32), jnp.uint32)
    packed = (bits[:, n:] & jnp.uint32(0xFFFF0000)) | (bits[:, :n] >> 16)
    return pltpu.bitcast(packed, jnp.int32)


def _unpack_bf16_pairs(w):
    bits = pltpu.bitcast(w, jnp.uint32)
    return pltpu.bitcast(bits << 16, F32), pltpu.bitcast(bits & jnp.uint32(0xFFFF0000), F32)


def _shift_rows(x, n, carry, seg):
    rolled = pltpu.roll(x, n, axis=0)
    row = lax.broadcasted_iota(jnp.int32, (SUBLANES, x.shape[1]), 0)
    pieces = []
    for j in range(x.shape[0] // seg):
        head = rolled[j * seg:j * seg + SUBLANES]
        prev = carry[j * SUBLANES:(j + 1) * SUBLANES]
        for i in range(n):
            head = jnp.where(row == i, prev[SUBLANES - n + i:SUBLANES - n + i + 1, :], head)
        pieces += [head, rolled[j * seg + SUBLANES:(j + 1) * seg]]
    return jnp.concatenate(pieces, axis=0)


def _last_rows(x, seg):
    return jnp.concatenate([x[(j + 1) * seg - SUBLANES:(j + 1) * seg] for j in range(x.shape[0] // seg)],
                           axis=0)


def _mixer_kernel(x_ref, g_ref, wmain_ref, wlora_ref, wgate_ref, cw_ref, mu_rkv_ref, mu_lora_ref,
                  w0_ref, w2_ref, a0_ref, a2_ref, g2_ref, kk_ref, ka_ref, rk_ref, lng_ref, lnb_ref,
                  wout_ref, hsel_ref, hselt_ref, gffn_ref, o_ref, hp_ref,
                  cu_s, cp_s, cl_s, ar_s, bk_s, v_s, y_s, gam_s, *state_s,
                  n_heads, d_model):
    n_sub, L = x_ref.shape[0], x_ref.shape[1]
    tc = n_sub * L
    D = d_model

    @pl.when(pl.program_id(1) == 0)
    def _():
        for st_ref in state_s:
            st_ref[...] = jnp.zeros_like(st_ref)
        cu_s[...] = jnp.zeros_like(cu_s)
        cp_s[...] = jnp.zeros_like(cp_s)
        cl_s[...] = jnp.zeros_like(cl_s)

    def split(t):
        hi = t.astype(BF16)
        return hi, (t - hi.astype(F32)).astype(BF16)

    def head_sums(parts, exact=False):
        per_head = sum(jnp.dot(p, hsel_ref[...], preferred_element_type=F32) for p in parts)
        back = split(per_head) if exact else [per_head.astype(BF16)]
        return sum(jnp.dot(p, hselt_ref[...], preferred_element_type=F32) for p in back)

    x = x_ref[...].reshape(tc, D)
    ms = jnp.mean(x * x, axis=-1, keepdims=True)
    xn = (x * lax.rsqrt(ms + NORM_EPS) * g_ref[...]).astype(BF16)

    pc = jnp.dot(xn, wmain_ref[:, 0:3 * D], preferred_element_type=F32)
    u = pc[:, D:2 * D] * pc[:, 2 * D:3 * D]
    cu = cu_s[...]
    conv = (cw_ref[0:1, :] * _shift_rows(u, 2, cu, L) + cw_ref[1:2, :] * _shift_rows(u, 1, cu, L)
            + cw_ref[2:3, :] * u)
    y_conv = pc[:, 0:D] * conv
    cu_s[...] = _last_rows(u, L)

    pr = jnp.dot(xn, wmain_ref[:, 3 * D:6 * D], preferred_element_type=F32)
    cp = cp_s[...]
    cp_s[...] = _last_rows(pr, L)
    pr = pr + (_shift_rows(pr, 1, cp, L) - pr) * mu_rkv_ref[...]
    plo = jnp.dot(xn, wlora_ref[...], preferred_element_type=F32)
    cl = cl_s[...]
    cl_s[...] = _last_rows(plo, L)
    plo = plo + (_shift_rows(plo, 1, cl, L) - plo) * mu_lora_ref[...]

    r = pr[:, 0:D]
    k = pr[:, D:2 * D]
    v = pr[:, 2 * D:3 * D]
    wd = plo[:, 0:LANES]
    ad = plo[:, LANES:2 * LANES]
    gd = plo[:, 2 * LANES:4 * LANES]

    lw = -DECAY_SCALE * _sigmoid(w0_ref[...] + _dot(jnp.tanh(wd), w2_ref[...]))
    a = _sigmoid(a0_ref[...] + _dot(ad, a2_ref[...]))
    g = _dot(_sigmoid(gd), g2_ref[...])

    row = lax.broadcasted_iota(jnp.int32, (tc, tc), 0)
    col = lax.broadcasted_iota(jnp.int32, (tc, tc), 1)
    tri = jnp.where((row >= col) & ((row // L) == (col // L)), 1.0, 0.0).astype(BF16)
    cum = sum(jnp.dot(tri, p, preferred_element_type=F32) for p in split(lw))
    e_inv = jnp.exp(-cum)

    kkraw = k * kk_ref[...]
    ss = head_sums([(kkraw * kkraw).astype(BF16)])
    kkn = kkraw * jnp.minimum(lax.rsqrt(ss), 1e12)

    k2 = k * (1.0 + (a - 1.0) * ka_ref[...])
    a_t = kkn * jnp.exp(cum - lw)
    r_t = r * jnp.exp(cum)
    b_t = kkn * a * e_inv
    k_t = k2 * e_inv
    for c in range(n_sub):
        rows = slice(c * L, (c + 1) * L)
        ar_s[c, 0:L, :] = a_t[rows]
        ar_s[c, L:2 * L, :] = r_t[rows]
        bk_s[c, 0:L, :] = b_t[rows]
        bk_s[c, L:2 * L, :] = k_t[rows]
        gam_s[c] = jnp.exp(cum[(c + 1) * L - 1:(c + 1) * L, :])
    v_s[...] = v
    bonus = head_sums([(r * k2 * rk_ref[...]).astype(BF16)]) * v

    pair = 2 * HEAD
    lane = lax.broadcasted_iota(jnp.int32, (1, pair), 1)
    left = lane < HEAD
    row1 = lax.broadcasted_iota(jnp.int32, (L, pair), 0)
    col1 = lax.broadcasted_iota(jnp.int32, (L, pair), 1) % HEAD
    strict = row1 > col1
    eye = jnp.where(row1 == col1, 1.0, 0.0)
    incl2 = (lax.broadcasted_iota(jnp.int32, (L, 2 * pair), 0)
             >= lax.broadcasted_iota(jnp.int32, (L, 2 * pair), 1) % HEAD)
    level_mask = [(row1 // (2 * s) == col1 // (2 * s)) & (row1 % (2 * s) >= s) & (col1 % (2 * s) < s)
                  for s in (1 << i for i in range(L.bit_length() - 1))]

    def blockdiag(t):
        tb = t.astype(BF16)
        return jnp.concatenate([jnp.where(left, tb, 0), jnp.where(left, 0, tb)], axis=0)

    pairs = range(n_heads // 2)
    psl = [slice(p * pair, (p + 1) * pair) for p in pairs]
    work = [(c, p) for c in range(n_sub) for p in pairs]
    ar = [ar_s[c, :, psl[p]].astype(BF16) for c, p in work]
    bk = [bk_s[c, :, psl[p]].astype(BF16) for c, p in work]
    vp = [v_s[c * L:(c + 1) * L, psl[p]] for c, p in work]
    gm = [_dot_nt(ar[i], jnp.concatenate([blockdiag(bk[i][0:L]), blockdiag(bk[i][L:2 * L])], axis=0))
          for i in range(len(work))]
    nm = [jnp.where(strict, g_[0:L, 0:pair], 0.0) for g_ in gm]
    mak = [jnp.where(strict, g_[0:L, pair:2 * pair], 0.0) for g_ in gm]
    q = [jnp.where(incl2, g_[L:2 * L, :], 0.0).astype(BF16) for g_ in gm]
    xinv = [eye - jnp.where(level_mask[0], n_, 0.0) for n_ in nm]
    for lm in level_mask[1:]:
        half = [_dot(jnp.where(lm, n_, 0.0), blockdiag(t)) for n_, t in zip(nm, xinv)]
        xinv = [t - _dot(t, blockdiag(h_)) for t, h_ in zip(xinv, half)]
    xinv = [t.astype(BF16) for t in xinv]
    mv = [_dot(m_, blockdiag(v_)) for m_, v_ in zip(mak, vp)]

    gates = _sigmoid(jnp.dot(xn, wgate_ref[...], preferred_element_type=F32))

    items = range(len(work))
    st = [state_s[i][...] for i in items]
    ars = [_dot_nt(ar[i], blockdiag(st[i])) for i in items]
    uu = [_dot(xinv[i], blockdiag(ars[i][0:L] + mv[i])) for i in items]
    yh = [ars[i][L:2 * L] + _dot(q[i], jnp.concatenate([blockdiag(-uu[i]), blockdiag(vp[i])], axis=0))
          for i in items]
    for i, (c, p) in enumerate(work):
        upd = _dot_tn(jnp.concatenate([-uu[i], vp[i]], axis=0), bk[i])
        state_s[i][...] = (st[i] + jnp.where(left, upd[0:HEAD], upd[HEAD:pair])) * gam_s[c, :, psl[p]]
    for i, (c, p) in enumerate(work):
        y_s[c * L:(c + 1) * L, psl[p]] = yh[i]

    y = y_s[...]
    yc = y - head_sums(split(y), exact=True) * (1.0 / HEAD)
    var = head_sums([(yc * yc).astype(BF16)]) * (1.0 / HEAD)
    y_rwkv = (yc * lax.rsqrt(var + GN_EPS) * lng_ref[...] + lnb_ref[...] + bonus) * g
    mix = gates[:, 0:D] * y_conv + gates[:, D:2 * D] * y_rwkv
    h = x + jnp.dot(mix.astype(BF16), wout_ref[...], preferred_element_type=F32)
    o_ref[...] = h.reshape(n_sub, L, D)
    hn = h * lax.rsqrt(jnp.mean(h * h, axis=-1, keepdims=True) + NORM_EPS) * gffn_ref[...]
    hp_ref[...] = _pack_bf16_pairs(hn).reshape(n_sub, L, D // 2)


def _const_spec(shape):
    nd = len(shape)
    return pl.BlockSpec(shape, lambda *_: (0,) * nd, pipeline_mode=pl.Buffered(1))


def _mixer(x, n_groups, norm_g, norm_ffn_g, w_in, conv_w, shift_mu, w0, w2, a0, a2, g2, k_k, k_a, r_k, ln_g, ln_b, w_out):
    bsz, s, d = x.shape
    gb = bsz // n_groups
    n_heads = d // HEAD
    L = RWKV_CHUNK
    n_sub = MIX_ROWS
    tc = n_sub * L
    assert gb % n_sub == 0 and s % L == 0
    lora0 = 6 * d
    w_main = w_in[:, 0:6 * d].astype(BF16)
    pad = lambda t, n: jnp.pad(t, ((0, 0), (0, n - t.shape[1])))
    lora_cols = (DECAY_LORA, AAA_LORA, GATE_LORA)
    lora_pads = (LANES, LANES, 2 * LANES)
    pieces_w, pieces_mu, off = [], [], lora0
    for n, p in zip(lora_cols, lora_pads):
        pieces_w.append(pad(w_in[:, off:off + n], p))
        pieces_mu.append(pad(shift_mu[None, off - 3 * d:off - 3 * d + n], p))
        off += n
    w_lora = jnp.concatenate(pieces_w, axis=1).astype(BF16)
    mu_lora = jnp.concatenate(pieces_mu, axis=1)
    w_gate = w_in[:, off:off + 2 * d].astype(BF16)
    mu_rkv = shift_mu[None, 0:3 * d]
    padr = lambda t, n: jnp.pad(t, ((0, n - t.shape[0]), (0, 0)))
    w2p = padr(w2, LANES).astype(BF16)
    a2p = padr(a2, LANES).astype(BF16)
    g2p = padr(g2, 2 * LANES).astype(BF16)
    row = lambda t: t.reshape(1, -1)
    head_of = jnp.arange(d, dtype=jnp.int32) // HEAD
    head_sel = (head_of[:, None] == jnp.arange(LANES)[None, :]).astype(BF16)
    consts = [row(norm_g), w_main, w_lora, w_gate, conv_w, mu_rkv, mu_lora, row(w0), w2p, row(a0), a2p,
              g2p, row(k_k), row(k_a), row(r_k), row(ln_g), row(ln_b), w_out.astype(BF16), head_sel, head_sel.T,
              row(norm_ffn_g)]
    kern = functools.partial(_mixer_kernel, n_heads=n_heads, d_model=d)
    call = lambda first: pl.pallas_call(
        kern,
        grid=(gb // n_sub, s // L),
        in_specs=[pl.BlockSpec((n_sub, L, d), lambda b, c: (b + first // n_sub, c, 0))]
        + [_const_spec(t.shape) for t in consts],
        out_specs=[pl.BlockSpec((n_sub, L, d), lambda b, c: (b, c, 0)),
                   pl.BlockSpec((n_sub, L, d // 2), lambda b, c: (b, c, 0))],
        out_shape=[jax.ShapeDtypeStruct((gb, s, d), F32), jax.ShapeDtypeStruct((gb, s, d // 2), jnp.int32)],
        scratch_shapes=[
            pltpu.VMEM((n_sub * SUBLANES, d), F32),
            pltpu.VMEM((n_sub * SUBLANES, 3 * d), F32),
            pltpu.VMEM((n_sub * SUBLANES, 4 * LANES), F32),
            pltpu.VMEM((n_sub, 2 * L, d), F32),
            pltpu.VMEM((n_sub, 2 * L, d), F32),
            pltpu.VMEM((tc, d), F32),
            pltpu.VMEM((tc, d), F32),
            pltpu.VMEM((n_sub, 1, d), F32),
        ] + [pltpu.VMEM((HEAD, 2 * HEAD), F32)] * (n_sub * n_heads // 2),
        compiler_params=pltpu.CompilerParams(
            dimension_semantics=("arbitrary", "arbitrary"), vmem_limit_bytes=VMEM_LIMIT_BYTES),
        name="mixer",
    )(x, *consts)
    return [functools.partial(call, g * gb) for g in range(n_groups)]


def _router_kernel(h_ref, g_ref, rwh_ref, rwl_ref, rb_ref, tri_ref, meta_ref, metat_ref, cnt_ref, run_s, *,
                   n_experts):
    tt = h_ref.shape[0]

    @pl.when(pl.program_id(0) == 0)
    def _():
        run_s[...] = jnp.zeros_like(run_s)

    h = h_ref[...]
    ms = jnp.mean(h * h, axis=-1, keepdims=True)
    hn = h * lax.rsqrt(ms + NORM_EPS) * g_ref[...]
    hn_hi = hn.astype(BF16)
    hn_lo = (hn - hn_hi.astype(F32)).astype(BF16)
    logits = (jnp.dot(hn_hi, rwh_ref[...], preferred_element_type=F32)
              + jnp.dot(hn_lo, rwh_ref[...], preferred_element_type=F32)
              + jnp.dot(hn_hi, rwl_ref[...], preferred_element_type=F32)) + rb_ref[...]
    lane = lax.broadcasted_iota(jnp.int32, (tt, LANES), 1)
    neg = jnp.float32(-jnp.inf)
    work = jnp.where(lane < n_experts, logits, neg)
    vals, idxs = [], []
    for _ in range(TOP_K):
        m = jnp.max(work, axis=-1, keepdims=True)
        i = jnp.min(jnp.where(work == m, lane, LANES), axis=-1, keepdims=True)
        vals.append(m)
        idxs.append(i)
        work = jnp.where(lane == i, neg, work)
    ex = [jnp.exp(vv - vals[0]) for vv in vals]
    den = ex[0] + ex[1] + ex[2] + ex[3]
    gates = [e / den for e in ex]

    onehot = jnp.zeros((tt, LANES), jnp.bool_)
    for kk in range(TOP_K):
        onehot = onehot | (lane == (idxs[kk] + kk * n_experts))
    oh = jnp.where(onehot, 1.0, 0.0)
    cnt = jnp.dot(tri_ref[...], oh.astype(BF16), preferred_element_type=F32)
    tot = jnp.broadcast_to(jnp.sum(oh, axis=0, keepdims=True), (SUBLANES, LANES))
    lane8 = lax.broadcasted_iota(jnp.int32, (SUBLANES, LANES), 1)
    pk = jnp.zeros_like(tot)
    te = tot
    for j in range(1, TOP_K):
        rolled = pltpu.roll(tot, j * n_experts, axis=1)
        pk = pk + jnp.where(lane8 >= j * n_experts, rolled, 0.0)
        te = te + rolled
    before = cnt + (run_s[...] + pk)[0:1, :]
    ranks = [jnp.sum(jnp.where(onehot & (lane // n_experts == kk), before, 0.0), axis=-1, keepdims=True)
             for kk in range(TOP_K)]
    run_s[...] = run_s[...] + te
    cnt_ref[...] = run_s[...]

    meta = jnp.zeros((tt, LANES), F32)
    for kk in range(TOP_K):
        meta = jnp.where(lane == kk, idxs[kk].astype(F32), meta)
        meta = jnp.where(lane == TOP_K + kk, gates[kk], meta)
        meta = jnp.where(lane == 2 * TOP_K + kk, ranks[kk], meta)
    meta_ref[...] = meta
    metat_ref[...] = meta.T[0:2 * SUBLANES, :]


def _router(h2, norm_g, router_w, router_b):
    t, d = h2.shape
    n_experts = router_w.shape[1]
    tt = min(ROUTE_TOKENS, t)
    rw = jnp.pad(router_w, ((0, 0), (0, LANES - n_experts)))
    rw_hi = rw.astype(BF16)
    rw_lo = (rw - rw_hi.astype(F32)).astype(BF16)
    rb = jnp.pad(router_b, (0, LANES - n_experts)).reshape(1, LANES)
    tok = jnp.arange(tt, dtype=jnp.int32)
    tri = (tok[:, None] > tok[None, :]).astype(BF16)
    kern = functools.partial(_router_kernel, n_experts=n_experts)
    return pl.pallas_call(
        kern,
        grid=(t // tt,),
        in_specs=[pl.BlockSpec((tt, d), lambda i: (i, 0)),
                  pl.BlockSpec((1, d), lambda i: (0, 0)),
                  pl.BlockSpec((d, LANES), lambda i: (0, 0)),
                  pl.BlockSpec((d, LANES), lambda i: (0, 0)),
                  pl.BlockSpec((1, LANES), lambda i: (0, 0)),
                  pl.BlockSpec((tt, tt), lambda i: (0, 0))],
        out_specs=[pl.BlockSpec((tt, LANES), lambda i: (i, 0)),
                   pl.BlockSpec((2 * SUBLANES, tt), lambda i: (0, i)),
                   pl.BlockSpec((SUBLANES, LANES), lambda i: (0, 0))],
        out_shape=[jax.ShapeDtypeStruct((t, LANES), F32),
                   jax.ShapeDtypeStruct((2 * SUBLANES, t), F32),
                   jax.ShapeDtypeStruct((SUBLANES, LANES), F32)],
        scratch_shapes=[pltpu.VMEM((SUBLANES, LANES), F32)],
        compiler_params=pltpu.CompilerParams(
            dimension_semantics=("arbitrary",), vmem_limit_bytes=VMEM_LIMIT_BYTES),
        name="router",
    )(h2, norm_g.reshape(1, d), rw_hi, rw_lo, rb, tri)


def _gather_scratch(per_worker, d, dtype):
    w = GATHER_WINDOW
    return [pltpu.VMEM((per_worker,), jnp.int32),
            pltpu.VMEM((w, d), dtype), pltpu.VMEM((w, d), dtype),
            pltpu.SemaphoreType.DMA, pltpu.SemaphoreType.DMA, pltpu.SemaphoreType.DMA, pltpu.SemaphoreType.DMA]


def _gather_rows(table_hbm, out_hbm, base, idx_v, buf_a, buf_b, gsem_a, gsem_b, psem_a, psem_b):
    w = GATHER_WINDOW
    n_win = idx_v.shape[0] // w

    def gather(j, buf, sem):
        return pltpu.make_async_copy(table_hbm.at[idx_v.at[pl.ds(j * w, w)]], buf, sem)

    def put(j, buf, sem):
        return pltpu.make_async_copy(buf, out_hbm.at[pl.ds(base + j * w, w)], sem)

    gather(0, buf_a, gsem_a).start()

    @pl.loop(0, n_win, step=2)
    def _(j):
        gather(j, buf_a, gsem_a).wait()

        @pl.when(j > 0)
        def _():
            put(j - 1, buf_b, psem_b).wait()

        gather(j + 1, buf_b, gsem_b).start()
        put(j, buf_a, psem_a).start()
        gather(j + 1, buf_b, gsem_b).wait()
        put(j, buf_a, psem_a).wait()

        @pl.when(j + 2 < n_win)
        def _():
            gather(j + 2, buf_a, gsem_a).start()

        put(j + 1, buf_b, psem_b).start()

    put(n_win - 1, buf_b, psem_b).wait()


def _worker_split(n_rows):
    sc = plsc.get_sparse_core_info()
    n_workers = sc.num_cores * sc.num_subcores
    per_worker = n_rows // n_workers
    n_win = per_worker // GATHER_WINDOW
    assert per_worker * n_workers == n_rows and n_win * GATHER_WINDOW == per_worker and n_win % 2 == 0
    return sc, per_worker


def _row_gather(table, idx):
    n_idx, d = idx.shape[0], table.shape[1]
    sc, per_worker = _worker_split(n_idx)
    mesh = plsc.VectorSubcoreMesh(core_axis_name="core", subcore_axis_name="subcore")

    @functools.partial(pl.kernel, out_type=jax.ShapeDtypeStruct((n_idx, d), table.dtype), mesh=mesh,
                       scratch_types=_gather_scratch(per_worker, d, table.dtype))
    def gather_kernel(table_hbm, idx_hbm, out_hbm, idx_v, *bufs):
        base = (lax.axis_index("subcore") * sc.num_cores + lax.axis_index("core")) * per_worker
        pltpu.sync_copy(idx_hbm.at[pl.ds(base, per_worker)], idx_v)
        _gather_rows(table_hbm, out_hbm, base, idx_v, *bufs)

    return gather_kernel(table, idx)


def _dispatch_gather(table, pos, n_slots):
    n, d = pos.shape[0], table.shape[1]
    sc, per_worker = _worker_split(n_slots)
    n_sub = sc.num_subcores
    rows = n // LANES
    rows_per_tile = rows // n_sub
    init_per_tile = n_slots // n_sub
    assert rows_per_tile * n_sub * LANES == n and init_per_tile * n_sub == n_slots
    mesh = plsc.VectorSubcoreMesh(core_axis_name="core", subcore_axis_name="subcore")
    tokens = (jnp.arange(n, dtype=jnp.int32) % (n // TOP_K)).reshape(rows, LANES)
    fill = jnp.arange(n_slots, dtype=jnp.int32) % (n // TOP_K)

    @functools.partial(
        pl.kernel, out_type=jax.ShapeDtypeStruct((n_slots, d), table.dtype), mesh=mesh,
        scratch_types=[pltpu.VMEM((rows_per_tile, LANES), jnp.int32),
                       pltpu.VMEM((rows_per_tile, LANES), jnp.int32),
                       pltpu.VMEM_SHARED((n_slots,), jnp.int32)] + _gather_scratch(per_worker, d, table.dtype))
    def dispatch_kernel(table_hbm, pos_hbm, tok_hbm, fill_hbm, out_hbm, pos_v, tok_v, slot_token, idx_v, *bufs):
        cid = lax.axis_index("core")
        sid = lax.axis_index("subcore")
        pltpu.sync_copy(fill_hbm.at[pl.ds(sid * init_per_tile, init_per_tile)],
                        slot_token.at[pl.ds(sid * init_per_tile, init_per_tile)])
        pltpu.sync_copy(pos_hbm.at[pl.ds(sid * rows_per_tile, rows_per_tile)], pos_v)
        pltpu.sync_copy(tok_hbm.at[pl.ds(sid * rows_per_tile, rows_per_tile)], tok_v)
        plsc.subcore_barrier()

        @pl.loop(0, rows_per_tile)
        def _(j):
            pltpu.sync_copy(tok_v.at[j], slot_token.at[pos_v.at[j]])

        plsc.subcore_barrier()
        base = (sid * sc.num_cores + cid) * per_worker
        pltpu.sync_copy(slot_token.at[pl.ds(base, per_worker)], idx_v)
        _gather_rows(table_hbm, out_hbm, base, idx_v, *bufs)

    return dispatch_kernel(table, pos.reshape(rows, LANES), tokens, fill)


def _experts_kernel(te_ref, nv_ref, xs_ref, w1_ref, b1_ref, w2_ref, b2_ref, ys_ref, *cast_refs, d_ff):
    i = pl.program_id(0)
    nvalid = nv_ref[i]
    if cast_refs:
        w1_use, w2_use = cast_refs
        prev = te_ref[jnp.maximum(i - 1, 0)]

        @pl.when((i == 0) | (te_ref[i] != prev))
        def _():
            w1_use[0] = w1_ref[0].astype(BF16)
            w2_use[0] = w2_ref[0].astype(BF16)
    else:
        w1_use, w2_use = w1_ref, w2_ref

    tm = xs_ref.shape[0]
    all_groups = [slice(r, r + EXPERT_ROWS) for r in range(0, tm, EXPERT_ROWS)]

    def ffn(groups):
        half = xs_ref.shape[1]
        xg = []
        for rows in groups:
            row = rows.start + lax.broadcasted_iota(jnp.int32, (EXPERT_ROWS, 1), 0)
            lo, hi = _unpack_bf16_pairs(jnp.where(row < nvalid, xs_ref[rows, :], 0))
            xg.append((lo.astype(BF16), hi.astype(BF16)))
        ug = [jnp.dot(lo, w1_use[0, 0:half, :], preferred_element_type=F32)
              + jnp.dot(hi, w1_use[0, half:2 * half, :], preferred_element_type=F32) + b1_ref[0] for lo, hi in xg]
        ag = []
        for u in ug:
            glu = jnp.minimum(u[:, 0:d_ff], SWIGLU_LIMIT)
            lin = jnp.clip(u[:, d_ff:2 * d_ff], -SWIGLU_LIMIT, SWIGLU_LIMIT)
            ag.append((glu * _sigmoid(SWIGLU_ALPHA * glu) * (lin + 1.0)).astype(BF16))
        for rows, act in zip(groups, ag):
            ys_ref[rows, :] = _pack_bf16_pairs(
                jnp.dot(act, w2_use[0], preferred_element_type=F32) + b2_ref[0])
        for rows in all_groups[len(groups):]:
            ys_ref[rows, :] = jnp.zeros((EXPERT_ROWS, ys_ref.shape[1]), ys_ref.dtype)

    for n_used in range(len(all_groups) + 1):
        lo_cnt, hi_cnt = (n_used - 1) * EXPERT_ROWS, n_used * EXPERT_ROWS
        pl.when((nvalid > lo_cnt) & (nvalid <= hi_cnt))(functools.partial(ffn, all_groups[:n_used]))


def _experts(xs, tile_expert, tile_valid, w1, b1, w2, b2):
    cast = w1.dtype != BF16
    n_slots = xs.shape[0]
    n_exp, d, two_ff = w1.shape
    d_ff = two_ff // 2
    tm = SLOT_TILE
    n_tiles = n_slots // tm
    kern = functools.partial(_experts_kernel, d_ff=d_ff)
    grid_spec = pltpu.PrefetchScalarGridSpec(
        num_scalar_prefetch=2,
        grid=(n_tiles,),
        in_specs=[pl.BlockSpec((tm, d // 2), lambda i, te, nv: (i, 0)),
                  pl.BlockSpec((1, d, two_ff), lambda i, te, nv: (te[i], 0, 0)),
                  pl.BlockSpec((1, 1, two_ff), lambda i, te, nv: (te[i], 0, 0)),
                  pl.BlockSpec((1, d_ff, d), lambda i, te, nv: (te[i], 0, 0)),
                  pl.BlockSpec((1, 1, d), lambda i, te, nv: (te[i], 0, 0))],
        out_specs=[pl.BlockSpec((tm, d // 2), lambda i, te, nv: (i, 0))]
        + ([pl.BlockSpec((1, d, two_ff), lambda i, te, nv: (te[i], 0, 0)),
            pl.BlockSpec((1, d_ff, d), lambda i, te, nv: (te[i], 0, 0))] if cast else []),
    )
    outs = pl.pallas_call(
        kern,
        grid_spec=grid_spec,
        out_shape=[jax.ShapeDtypeStruct((n_slots, d // 2), jnp.int32)]
        + ([jax.ShapeDtypeStruct(w1.shape, BF16), jax.ShapeDtypeStruct(w2.shape, BF16)] if cast else []),
        compiler_params=pltpu.CompilerParams(
            dimension_semantics=("arbitrary",), vmem_limit_bytes=VMEM_LIMIT_BYTES),
        name="experts",
    )(tile_expert, tile_valid, xs, w1, b1.reshape(n_exp, 1, two_ff), w2, b2.reshape(n_exp, 1, d))
    return outs if cast else (outs[0], w1, w2)


def _combine_kernel(h_ref, meta_ref, g_ref, *rest):
    y_refs, o_ref = rest[:TOP_K], rest[-1]
    d = h_ref.shape[1]
    half = d // 2
    acc_lo = h_ref[:, 0:half]
    acc_hi = h_ref[:, half:d]
    for kk in range(TOP_K):
        gate = meta_ref[:, TOP_K + kk:TOP_K + kk + 1]
        lo, hi = _unpack_bf16_pairs(y_refs[kk][...])
        acc_lo = acc_lo + gate * lo
        acc_hi = acc_hi + gate * hi
    ms = (jnp.sum(acc_lo * acc_lo, axis=-1, keepdims=True)
          + jnp.sum(acc_hi * acc_hi, axis=-1, keepdims=True)) * (1.0 / d)
    scale = lax.rsqrt(ms + NORM_EPS)
    o_ref[:, 0:half] = acc_lo * scale * g_ref[:, 0:half]
    o_ref[:, half:d] = acc_hi * scale * g_ref[:, half:d]


def _combine(h2, y4, meta, norm_g, out_prev, group, n_groups):
    t, d = h2.shape
    tt = min(MOVE_TOKENS, t)
    n_blk = t // tt
    prev_specs = [] if out_prev is None else [pl.BlockSpec(memory_space=pl.ANY)]
    prev_args = [] if out_prev is None else [out_prev]
    n_in = 3 + TOP_K
    y_specs = [pl.BlockSpec((tt, d // 2), functools.partial(lambda i, kk: (kk * n_blk + i, 0), kk=kk))
               for kk in range(TOP_K)]
    return pl.pallas_call(
        _combine_kernel,
        grid=(n_blk,),
        in_specs=[pl.BlockSpec((tt, d), lambda i: (i, 0)),
                  pl.BlockSpec((tt, LANES), lambda i: (i, 0)),
                  pl.BlockSpec((1, d), lambda i: (0, 0))] + y_specs + prev_specs,
        out_specs=pl.BlockSpec((tt, d), lambda i: (group * n_blk + i, 0)),
        out_shape=jax.ShapeDtypeStruct((n_groups * t, d), F32),
        input_output_aliases={} if out_prev is None else {n_in: 0},
        compiler_params=pltpu.CompilerParams(
            dimension_semantics=("arbitrary",), vmem_limit_bytes=VMEM_LIMIT_BYTES),
        name="combine",
    )(h2, meta, norm_g.reshape(1, d), *([y4] * TOP_K), *prev_args)


def _slots_kernel(start_ref, metat_ref, pos_ref, *, n_experts):
    eidx = metat_ref[0:TOP_K, :].astype(jnp.int32)
    pos = metat_ref[2 * TOP_K:3 * TOP_K, :].astype(jnp.int32)
    for e in range(n_experts):
        pos = pos + jnp.where(eidx == e, start_ref[e], 0)
    pos_ref[...] = pos


def _slots(metat, seg_start):
    t = metat.shape[1]
    n_exp = seg_start.shape[0]
    tt = min(SLOT_TOKENS, t)
    grid_spec = pltpu.PrefetchScalarGridSpec(
        num_scalar_prefetch=1,
        grid=(t // tt,),
        in_specs=[pl.BlockSpec((2 * SUBLANES, tt), lambda i, st: (0, i))],
        out_specs=pl.BlockSpec((TOP_K, tt), lambda i, st: (0, i)),
    )
    return pl.pallas_call(
        functools.partial(_slots_kernel, n_experts=n_exp),
        grid_spec=grid_spec,
        out_shape=jax.ShapeDtypeStruct((TOP_K, t), jnp.int32),
        compiler_params=pltpu.CompilerParams(dimension_semantics=("arbitrary",)),
        name="slots",
    )(seg_start, metat)


def _route(h2, hp2, norm_ffn_g, router_w, router_b):
    t, d = h2.shape
    n_exp = router_w.shape[1]
    tm = SLOT_TILE
    meta, metat, counts = _router(h2, norm_ffn_g, router_w, router_b)
    cnt = counts[0, 0:n_exp].astype(jnp.int32)
    padded = jnp.maximum((cnt + tm - 1) // tm, 1) * tm
    seg_end = jnp.cumsum(padded)
    seg_start = seg_end - padded
    pos = _slots(metat, seg_start).reshape(-1)
    n_tiles = -(-(t * TOP_K) // tm) + n_exp
    n_slots = n_tiles * tm
    tile_start = jnp.arange(n_tiles, dtype=jnp.int32) * tm
    tile_expert = jnp.minimum(jnp.sum(tile_start[:, None] >= seg_end[None, :], axis=1), n_exp - 1).astype(jnp.int32)
    tile_valid = jnp.clip(seg_start[tile_expert] + cnt[tile_expert] - tile_start, 0, tm).astype(jnp.int32)
    xs = _dispatch_gather(hp2, pos, n_slots)
    return dict(h2=h2, meta=meta, pos=pos, xs=xs, tile_expert=tile_expert, tile_valid=tile_valid)


def kernel(x, norm_mix_g, w_in, conv_w, shift_mu, decay_w0, decay_w2, iclr_a0, iclr_a2, gate_g2, k_k, k_a,
           r_k, ln_x_g, ln_x_b, w_out, norm_ffn_g, router_w, router_b, exp_w1, exp_b1, exp_w2, exp_b2,
           norm_final_g):
    bsz, s, d = x.shape
    depth = w_in.shape[0]
    assert depth == 1, "final norm is fused into the last layer's combine kernel"
    n_groups = BATCH_GROUPS if bsz % (BATCH_GROUPS * MIX_ROWS) == 0 else 1
    mixers = _mixer(x, n_groups, norm_mix_g[0], norm_ffn_g[0], w_in[0], conv_w[0], shift_mu[0], decay_w0[0],
                    decay_w2[0], iclr_a0[0], iclr_a2[0], gate_g2[0], k_k[0], k_a[0], r_k[0], ln_x_g[0], ln_x_b[0],
                    w_out[0])
    routed = []
    for run_mixer in mixers:
        h, hp = run_mixer()
        routed.append(_route(h.reshape(-1, d), hp.reshape(-1, d // 2), norm_ffn_g[0], router_w[0], router_b[0]))
    w1, w2 = exp_w1[0], exp_w2[0]
    gathered = []
    for r in routed:
        ys, w1, w2 = _experts(r["xs"], r["tile_expert"], r["tile_valid"], w1, exp_b1[0], w2, exp_b2[0])
        gathered.append(_row_gather(ys, r["pos"]))
    out = None
    for g, (r, y4) in enumerate(zip(routed, gathered)):
        out = _combine(r["h2"], y4, r["meta"], norm_final_g, out, g, n_groups)
    return out.reshape(bsz, s, d)
```

```python
import functools

import jax
import jax.numpy as jnp
from jax import lax
from jax.experimental import pallas as pl
from jax.experimental.pallas import tpu as pltpu
from jax.experimental.pallas import tpu_sc as plsc

HEAD = 64
DECAY_LORA = 64
AAA_LORA = 64
GATE_LORA = 160
TOP_K = 4
SWIGLU_ALPHA = 1.702
SWIGLU_LIMIT = 7.0
NORM_EPS = 1e-5
GN_EPS = 64e-5
DECAY_SCALE = 0.6065306597126334

LANES = 128
SUBLANES = 8
VMEM_LIMIT_BYTES = 56 * 1024 * 1024

RWKV_CHUNK = 64
MIX_ROWS = 4
ROUTE_TOKENS = 512
MOVE_TOKENS = 512
SLOT_TOKENS = 8192
GATHER_WINDOW = 64
SLOT_TILE = 512
EXPERT_ROWS = 256
BATCH_GROUPS = 2

BF16 = jnp.bfloat16
F32 = jnp.float32


def _dot(a, b):
    return jnp.dot(a.astype(BF16), b.astype(BF16), preferred_element_type=F32)


def _dot_nt(a, b):
    return lax.dot_general(a.astype(BF16), b.astype(BF16), (((1,), (1,)), ((), ())),
                           preferred_element_type=F32)


def _dot_tn(a, b):
    return lax.dot_general(a.astype(BF16), b.astype(BF16), (((0,), (0,)), ((), ())),
                           preferred_element_type=F32)


def _sigmoid(x):
    return 0.5 * jnp.tanh(0.5 * x) + 0.5


def _pack_bf16_pairs(x):
    n = x.shape[1] // 2
    bits = pltpu.bitcast(x.astype(BF16).astype(F32), jnp.uint32)
    packed = (bits[:, n:] & jnp.uint32(0xFFFF0000)) | (bits[:, :n] >> 16)
    return pltpu.bitcast(packed, jnp.int32)


def _unpack_bf16_pairs(w):
    bits = pltpu.bitcast(w, jnp.uint32)
    return pltpu.bitcast(bits << 16, F32), pltpu.bitcast(bits & jnp.uint32(0xFFFF0000), F32)


def _shift_rows(x, n, carry, seg):
    rolled = pltpu.roll(x, n, axis=0)
    row = lax.broadcasted_iota(jnp.int32, (SUBLANES, x.shape[1]), 0)
    pieces = []
    for j in range(x.shape[0] // seg):
        head = rolled[j * seg:j * seg + SUBLANES]
        prev = carry[j * SUBLANES:(j + 1) * SUBLANES]
        for i in range(n):
            head = jnp.where(row == i, prev[SUBLANES - n + i:SUBLANES - n + i + 1, :], head)
        pieces += [head, rolled[j * seg + SUBLANES:(j + 1) * seg]]
    return jnp.concatenate(pieces, axis=0)


def _last_rows(x, seg):
    return jnp.concatenate([x[(j + 1) * seg - SUBLANES:(j + 1) * seg] for j in range(x.shape[0] // seg)],
                           axis=0)


def _mixer_kernel(x_ref, g_ref, wmain_ref, wlora_ref, wgate_ref, cw_ref, mu_rkv_ref, mu_lora_ref,
                  w0_ref, w2_ref, a0_ref, a2_ref, g2_ref, kk_ref, ka_ref, rk_ref, lng_ref, lnb_ref,
                  wout_ref, hsel_ref, hselt_ref, gffn_ref, o_ref, hp_ref,
                  cu_s, cp_s, cl_s, ar_s, bk_s, v_s, y_s, gam_s, *state_s,
                  n_heads, d_model):
    n_sub, L = x_ref.shape[0], x_ref.shape[1]
    tc = n_sub * L
    D = d_model

    @pl.when(pl.program_id(1) == 0)
    def _():
        for st_ref in state_s:
            st_ref[...] = jnp.zeros_like(st_ref)
        cu_s[...] = jnp.zeros_like(cu_s)
        cp_s[...] = jnp.zeros_like(cp_s)
        cl_s[...] = jnp.zeros_like(cl_s)

    def split(t):
        hi = t.astype(BF16)
        return hi, (t - hi.astype(F32)).astype(BF16)

    def head_sums(parts, exact=False):
        per_head = sum(jnp.dot(p, hsel_ref[...], preferred_element_type=F32) for p in parts)
        back = split(per_head) if exact else [per_head.astype(BF16)]
        return sum(jnp.dot(p, hselt_ref[...], preferred_element_type=F32) for p in back)

    x = x_ref[...].reshape(tc, D)
    ms = jnp.mean(x * x, axis=-1, keepdims=True)
    xn = (x * lax.rsqrt(ms + NORM_EPS) * g_ref[...]).astype(BF16)

    pc = jnp.dot(xn, wmain_ref[:, 0:3 * D], preferred_element_type=F32)
    u = pc[:, D:2 * D] * pc[:, 2 * D:3 * D]
    cu = cu_s[...]
    conv = (cw_ref[0:1, :] * _shift_rows(u, 2, cu, L) + cw_ref[1:2, :] * _shift_rows(u, 1, cu, L)
            + cw_ref[2:3, :] * u)
    y_conv = pc[:, 0:D] * conv
    cu_s[...] = _last_rows(u, L)

    pr = jnp.dot(xn, wmain_ref[:, 3 * D:6 * D], preferred_element_type=F32)
    cp = cp_s[...]
    cp_s[...] = _last_rows(pr, L)
    pr = pr + (_shift_rows(pr, 1, cp, L) - pr) * mu_rkv_ref[...]
    plo = jnp.dot(xn, wlora_ref[...], preferred_element_type=F32)
    cl = cl_s[...]
    cl_s[...] = _last_rows(plo, L)
    plo = plo + (_shift_rows(plo, 1, cl, L) - plo) * mu_lora_ref[...]

    r = pr[:, 0:D]
    k = pr[:, D:2 * D]
    v = pr[:, 2 * D:3 * D]
    wd = plo[:, 0:LANES]
    ad = plo[:, LANES:2 * LANES]
    gd = plo[:, 2 * LANES:4 * LANES]

    lw = -DECAY_SCALE * _sigmoid(w0_ref[...] + _dot(jnp.tanh(wd), w2_ref[...]))
    a = _sigmoid(a0_ref[...] + _dot(ad, a2_ref[...]))
    g = _dot(_sigmoid(gd), g2_ref[...])

    row = lax.broadcasted_iota(jnp.int32, (tc, tc), 0)
    col = lax.broadcasted_iota(jnp.int32, (tc, tc), 1)
    tri = jnp.where((row >= col) & ((row // L) == (col // L)), 1.0, 0.0).astype(BF16)
    cum = sum(jnp.dot(tri, p, preferred_element_type=F32) for p in split(lw))
    e_inv = jnp.exp(-cum)

    kkraw = k * kk_ref[...]
    ss = head_sums([(kkraw * kkraw).astype(BF16)])
    kkn = kkraw * jnp.minimum(lax.rsqrt(ss), 1e12)

    k2 = k * (1.0 + (a - 1.0) * ka_ref[...])
    a_t = kkn * jnp.exp(cum - lw)
    r_t = r * jnp.exp(cum)
    b_t = kkn * a * e_inv
    k_t = k2 * e_inv
    for c in range(n_sub):
        rows = slice(c * L, (c + 1) * L)
        ar_s[c, 0:L, :] = a_t[rows]
        ar_s[c, L:2 * L, :] = r_t[rows]
        bk_s[c, 0:L, :] = b_t[rows]
        bk_s[c, L:2 * L, :] = k_t[rows]
        gam_s[c] = jnp.exp(cum[(c + 1) * L - 1:(c + 1) * L, :])
    v_s[...] = v
    bonus = head_sums([(r * k2 * rk_ref[...]).astype(BF16)]) * v

    pair = 2 * HEAD
    lane = lax.broadcasted_iota(jnp.int32, (1, pair), 1)
    left = lane < HEAD
    row1 = lax.broadcasted_iota(jnp.int32, (L, pair), 0)
    col1 = lax.broadcasted_iota(jnp.int32, (L, pair), 1) % HEAD
    strict = row1 > col1
    eye = jnp.where(row1 == col1, 1.0, 0.0)
    incl2 = (lax.broadcasted_iota(jnp.int32, (L, 2 * pair), 0)
             >= lax.broadcasted_iota(jnp.int32, (L, 2 * pair), 1) % HEAD)
    level_mask = [(row1 // (2 * s) == col1 // (2 * s)) & (row1 % (2 * s) >= s) & (col1 % (2 * s) < s)
                  for s in (1 << i for i in range(L.bit_length() - 1))]

    def blockdiag(t):
        tb = t.astype(BF16)
        return jnp.concatenate([jnp.where(left, tb, 0), jnp.where(left, 0, tb)], axis=0)

    pairs = range(n_heads // 2)
    psl = [slice(p * pair, (p + 1) * pair) for p in pairs]
    work = [(c, p) for c in range(n_sub) for p in pairs]
    ar = [ar_s[c, :, psl[p]].astype(BF16) for c, p in work]
    bk = [bk_s[c, :, psl[p]].astype(BF16) for c, p in work]
    vp = [v_s[c * L:(c + 1) * L, psl[p]] for c, p in work]
    gm = [_dot_nt(ar[i], jnp.concatenate([blockdiag(bk[i][0:L]), blockdiag(bk[i][L:2 * L])], axis=0))
          for i in range(len(work))]
    nm = [jnp.where(strict, g_[0:L, 0:pair], 0.0) for g_ in gm]
    mak = [jnp.where(strict, g_[0:L, pair:2 * pair], 0.0) for g_ in gm]
    q = [jnp.where(incl2, g_[L:2 * L, :], 0.0).astype(BF16) for g_ in gm]
    xinv = [eye - jnp.where(level_mask[0], n_, 0.0) for n_ in nm]
    for lm in level_mask[1:]:
        half = [_dot(jnp.where(lm, n_, 0.0), blockdiag(t)) for n_, t in zip(nm, xinv)]
        xinv = [t - _dot(t, blockdiag(h_)) for t, h_ in zip(xinv, half)]
    xinv = [t.astype(BF16) for t in xinv]
    mv = [_dot(m_, blockdiag(v_)) for m_, v_ in zip(mak, vp)]

    gates = _sigmoid(jnp.dot(xn, wgate_ref[...], preferred_element_type=F32))

    items = range(len(work))
    st = [state_s[i][...] for i in items]
    ars = [_dot_nt(ar[i], blockdiag(st[i])) for i in items]
    uu = [_dot(xinv[i], blockdiag(ars[i][0:L] + mv[i])) for i in items]
    yh = [ars[i][L:2 * L] + _dot(q[i], jnp.concatenate([blockdiag(-uu[i]), blockdiag(vp[i])], axis=0))
          for i in items]
    for i, (c, p) in enumerate(work):
        upd = _dot_tn(jnp.concatenate([-uu[i], vp[i]], axis=0), bk[i])
        state_s[i][...] = (st[i] + jnp.where(left, upd[0:HEAD], upd[HEAD:pair])) * gam_s[c, :, psl[p]]
    for i, (c, p) in enumerate(work):
        y_s[c * L:(c + 1) * L, psl[p]] = yh[i]

    y = y_s[...]
    yc = y - head_sums(split(y), exact=True) * (1.0 / HEAD)
    var = head_sums([(yc * yc).astype(BF16)]) * (1.0 / HEAD)
    y_rwkv = (yc * lax.rsqrt(var + GN_EPS) * lng_ref[...] + lnb_ref[...] + bonus) * g
    mix = gates[:, 0:D] * y_conv + gates[:, D:2 * D] * y_rwkv
    h = x + jnp.dot(mix.astype(BF16), wout_ref[...], preferred_element_type=F32)
    o_ref[...] = h.reshape(n_sub, L, D)
    hn = h * lax.rsqrt(jnp.mean(h * h, axis=-1, keepdims=True) + NORM_EPS) * gffn_ref[...]
    hp_ref[...] = _pack_bf16_pairs(hn).reshape(n_sub, L, D // 2)


def _const_spec(shape):
    nd = len(shape)
    return pl.BlockSpec(shape, lambda *_: (0,) * nd, pipeline_mode=pl.Buffered(1))


def _mixer(x, n_groups, norm_g, norm_ffn_g, w_in, conv_w, shift_mu, w0, w2, a0, a2, g2, k_k, k_a, r_k, ln_g, ln_b, w_out):
    bsz, s, d = x.shape
    gb = bsz // n_groups
    n_heads = d // HEAD
    L = RWKV_CHUNK
    n_sub = MIX_ROWS
    tc = n_sub * L
    assert gb % n_sub == 0 and s % L == 0
    lora0 = 6 * d
    w_main = w_in[:, 0:6 * d].astype(BF16)
    pad = lambda t, n: jnp.pad(t, ((0, 0), (0, n - t.shape[1])))
    lora_cols = (DECAY_LORA, AAA_LORA, GATE_LORA)
    lora_pads = (LANES, LANES, 2 * LANES)
    pieces_w, pieces_mu, off = [], [], lora0
    for n, p in zip(lora_cols, lora_pads):
        pieces_w.append(pad(w_in[:, off:off + n], p))
        pieces_mu.append(pad(shift_mu[None, off - 3 * d:off - 3 * d + n], p))
        off += n
    w_lora = jnp.concatenate(pieces_w, axis=1).astype(BF16)
    mu_lora = jnp.concatenate(pieces_mu, axis=1)
    w_gate = w_in[:, off:off + 2 * d].astype(BF16)
    mu_rkv = shift_mu[None, 0:3 * d]
    padr = lambda t, n: jnp.pad(t, ((0, n - t.shape[0]), (0, 0)))
    w2p = padr(w2, LANES).astype(BF16)
    a2p = padr(a2, LANES).astype(BF16)
    g2p = padr(g2, 2 * LANES).astype(BF16)
    row = lambda t: t.reshape(1, -1)
    head_of = jnp.arange(d, dtype=jnp.int32) // HEAD
    head_sel = (head_of[:, None] == jnp.arange(LANES)[None, :]).astype(BF16)
    consts = [row(norm_g), w_main, w_lora, w_gate, conv_w, mu_rkv, mu_lora, row(w0), w2p, row(a0), a2p,
              g2p, row(k_k), row(k_a), row(r_k), row(ln_g), row(ln_b), w_out.astype(BF16), head_sel, head_sel.T,
              row(norm_ffn_g)]
    kern = functools.partial(_mixer_kernel, n_heads=n_heads, d_model=d)
    call = lambda first: pl.pallas_call(
        kern,
        grid=(gb // n_sub, s // L),
        in_specs=[pl.BlockSpec((n_sub, L, d), lambda b, c: (b + first // n_sub, c, 0))]
        + [_const_spec(t.shape) for t in consts],
        out_specs=[pl.BlockSpec((n_sub, L, d), lambda b, c: (b, c, 0)),
                   pl.BlockSpec((n_sub, L, d // 2), lambda b, c: (b, c, 0))],
        out_shape=[jax.ShapeDtypeStruct((gb, s, d), F32), jax.ShapeDtypeStruct((gb, s, d // 2), jnp.int32)],
        scratch_shapes=[
            pltpu.VMEM((n_sub * SUBLANES, d), F32),
            pltpu.VMEM((n_sub * SUBLANES, 3 * d), F32),
            pltpu.VMEM((n_sub * SUBLANES, 4 * LANES), F32),
            pltpu.VMEM((n_sub, 2 * L, d), F32),
            pltpu.VMEM((n_sub, 2 * L, d), F32),
            pltpu.VMEM((tc, d), F32),
            pltpu.VMEM((tc, d), F32),
            pltpu.VMEM((n_sub, 1, d), F32),
        ] + [pltpu.VMEM((HEAD, 2 * HEAD), F32)] * (n_sub * n_heads // 2),
        compiler_params=pltpu.CompilerParams(
            dimension_semantics=("arbitrary", "arbitrary"), vmem_limit_bytes=VMEM_LIMIT_BYTES),
        name="mixer",
    )(x, *consts)
    return [functools.partial(call, g * gb) for g in range(n_groups)]


def _router_kernel(h_ref, g_ref, rwh_ref, rwl_ref, rb_ref, tri_ref, meta_ref, metat_ref, cnt_ref, run_s, *,
                   n_experts):
    tt = h_ref.shape[0]

    @pl.when(pl.program_id(0) == 0)
    def _():
        run_s[...] = jnp.zeros_like(run_s)

    h = h_ref[...]
    ms = jnp.mean(h * h, axis=-1, keepdims=True)
    hn = h * lax.rsqrt(ms + NORM_EPS) * g_ref[...]
    hn_hi = hn.astype(BF16)
    hn_lo = (hn - hn_hi.astype(F32)).astype(BF16)
    logits = (jnp.dot(hn_hi, rwh_ref[...], preferred_element_type=F32)
              + jnp.dot(hn_lo, rwh_ref[...], preferred_element_type=F32)
              + jnp.dot(hn_hi, rwl_ref[...], preferred_element_type=F32)) + rb_ref[...]
    lane = lax.broadcasted_iota(jnp.int32, (tt, LANES), 1)
    neg = jnp.float32(-jnp.inf)
    work = jnp.where(lane < n_experts, logits, neg)
    vals, idxs = [], []
    for _ in range(TOP_K):
        m = jnp.max(work, axis=-1, keepdims=True)
        i = jnp.min(jnp.where(work == m, lane, LANES), axis=-1, keepdims=True)
        vals.append(m)
        idxs.append(i)
        work = jnp.where(lane == i, neg, work)
    ex = [jnp.exp(vv - vals[0]) for vv in vals]
    den = ex[0] + ex[1] + ex[2] + ex[3]
    gates = [e / den for e in ex]

    onehot = jnp.zeros((tt, LANES), jnp.bool_)
    for kk in range(TOP_K):
        onehot = onehot | (lane == (idxs[kk] + kk * n_experts))
    oh = jnp.where(onehot, 1.0, 0.0)
    cnt = jnp.dot(tri_ref[...], oh.astype(BF16), preferred_element_type=F32)
    tot = jnp.broadcast_to(jnp.sum(oh, axis=0, keepdims=True), (SUBLANES, LANES))
    lane8 = lax.broadcasted_iota(jnp.int32, (SUBLANES, LANES), 1)
    pk = jnp.zeros_like(tot)
    te = tot
    for j in range(1, TOP_K):
        rolled = pltpu.roll(tot, j * n_experts, axis=1)
        pk = pk + jnp.where(lane8 >= j * n_experts, rolled, 0.0)
        te = te + rolled
    before = cnt + (run_s[...] + pk)[0:1, :]
    ranks = [jnp.sum(jnp.where(onehot & (lane // n_experts == kk), before, 0.0), axis=-1, keepdims=True)
             for kk in range(TOP_K)]
    run_s[...] = run_s[...] + te
    cnt_ref[...] = run_s[...]

    meta = jnp.zeros((tt, LANES), F32)
    for kk in range(TOP_K):
        meta = jnp.where(lane == kk, idxs[kk].astype(F32), meta)
        meta = jnp.where(lane == TOP_K + kk, gates[kk], meta)
        meta = jnp.where(lane == 2 * TOP_K + kk, ranks[kk], meta)
    meta_ref[...] = meta
    metat_ref[...] = meta.T[0:2 * SUBLANES, :]


def _router(h2, norm_g, router_w, router_b):
    t, d = h2.shape
    n_experts = router_w.shape[1]
    tt = min(ROUTE_TOKENS, t)
    rw = jnp.pad(router_w, ((0, 0), (0, LANES - n_experts)))
    rw_hi = rw.astype(BF16)
    rw_lo = (rw - rw_hi.astype(F32)).astype(BF16)
    rb = jnp.pad(router_b, (0, LANES - n_experts)).reshape(1, LANES)
    tok = jnp.arange(tt, dtype=jnp.int32)
    tri = (tok[:, None] > tok[None, :]).astype(BF16)
    kern = functools.partial(_router_kernel, n_experts=n_experts)
    return pl.pallas_call(
        kern,
        grid=(t // tt,),
        in_specs=[pl.BlockSpec((tt, d), lambda i: (i, 0)),
                  pl.BlockSpec((1, d), lambda i: (0, 0)),
                  pl.BlockSpec((d, LANES), lambda i: (0, 0)),
                  pl.BlockSpec((d, LANES), lambda i: (0, 0)),
                  pl.BlockSpec((1, LANES), lambda i: (0, 0)),
                  pl.BlockSpec((tt, tt), lambda i: (0, 0))],
        out_specs=[pl.BlockSpec((tt, LANES), lambda i: (i, 0)),
                   pl.BlockSpec((2 * SUBLANES, tt), lambda i: (0, i)),
                   pl.BlockSpec((SUBLANES, LANES), lambda i: (0, 0))],
        out_shape=[jax.ShapeDtypeStruct((t, LANES), F32),
                   jax.ShapeDtypeStruct((2 * SUBLANES, t), F32),
                   jax.ShapeDtypeStruct((SUBLANES, LANES), F32)],
        scratch_shapes=[pltpu.VMEM((SUBLANES, LANES), F32)],
        compiler_params=pltpu.CompilerParams(
            dimension_semantics=("arbitrary",), vmem_limit_bytes=VMEM_LIMIT_BYTES),
        name="router",
    )(h2, norm_g.reshape(1, d), rw_hi, rw_lo, rb, tri)


def _gather_scratch(per_worker, d, dtype):
    w = GATHER_WINDOW
    return [pltpu.VMEM((per_worker,), jnp.int32),
            pltpu.VMEM((w, d), dtype), pltpu.VMEM((w, d), dtype),
            pltpu.SemaphoreType.DMA, pltpu.SemaphoreType.DMA, pltpu.SemaphoreType.DMA, pltpu.SemaphoreType.DMA]


def _gather_rows(table_hbm, out_hbm, base, idx_v, buf_a, buf_b, gsem_a, gsem_b, psem_a, psem_b):
    w = GATHER_WINDOW
    n_win = idx_v.shape[0] // w

    def gather(j, buf, sem):
        return pltpu.make_async_copy(table_hbm.at[idx_v.at[pl.ds(j * w, w)]], buf, sem)

    def put(j, buf, sem):
        return pltpu.make_async_copy(buf, out_hbm.at[pl.ds(base + j * w, w)], sem)

    gather(0, buf_a, gsem_a).start()

    @pl.loop(0, n_win, step=2)
    def _(j):
        gather(j, buf_a, gsem_a).wait()

        @pl.when(j > 0)
        def _():
            put(j - 1, buf_b, psem_b).wait()

        gather(j + 1, buf_b, gsem_b).start()
        put(j, buf_a, psem_a).start()
        gather(j + 1, buf_b, gsem_b).wait()
        put(j, buf_a, psem_a).wait()

        @pl.when(j + 2 < n_win)
        def _():
            gather(j + 2, buf_a, gsem_a).start()

        put(j + 1, buf_b, psem_b).start()

    put(n_win - 1, buf_b, psem_b).wait()


def _worker_split(n_rows):
    sc = plsc.get_sparse_core_info()
    n_workers = sc.num_cores * sc.num_subcores
    per_worker = n_rows // n_workers
    n_win = per_worker // GATHER_WINDOW
    assert per_worker * n_workers == n_rows and n_win * GATHER_WINDOW == per_worker and n_win % 2 == 0
    return sc, per_worker


def _row_gather(table, idx):
    n_idx, d = idx.shape[0], table.shape[1]
    sc, per_worker = _worker_split(n_idx)
    mesh = plsc.VectorSubcoreMesh(core_axis_name="core", subcore_axis_name="subcore")

    @functools.partial(pl.kernel, out_type=jax.ShapeDtypeStruct((n_idx, d), table.dtype), mesh=mesh,
                       scratch_types=_gather_scratch(per_worker, d, table.dtype))
    def gather_kernel(table_hbm, idx_hbm, out_hbm, idx_v, *bufs):
        base = (lax.axis_index("subcore") * sc.num_cores + lax.axis_index("core")) * per_worker
        pltpu.sync_copy(idx_hbm.at[pl.ds(base, per_worker)], idx_v)
        _gather_rows(table_hbm, out_hbm, base, idx_v, *bufs)

    return gather_kernel(table, idx)


def _dispatch_gather(table, pos, n_slots):
    n, d = pos.shape[0], table.shape[1]
    sc, per_worker = _worker_split(n_slots)
    n_sub = sc.num_subcores
    rows = n // LANES
    rows_per_tile = rows // n_sub
    init_per_tile = n_slots // n_sub
    assert rows_per_tile * n_sub * LANES == n and init_per_tile * n_sub == n_slots
    mesh = plsc.VectorSubcoreMesh(core_axis_name="core", subcore_axis_name="subcore")
    tokens = (jnp.arange(n, dtype=jnp.int32) % (n // TOP_K)).reshape(rows, LANES)
    fill = jnp.arange(n_slots, dtype=jnp.int32) % (n // TOP_K)

    @functools.partial(
        pl.kernel, out_type=jax.ShapeDtypeStruct((n_slots, d), table.dtype), mesh=mesh,
        scratch_types=[pltpu.VMEM((rows_per_tile, LANES), jnp.int32),
                       pltpu.VMEM((rows_per_tile, LANES), jnp.int32),
                       pltpu.VMEM_SHARED((n_slots,), jnp.int32)] + _gather_scratch(per_worker, d, table.dtype))
    def dispatch_kernel(table_hbm, pos_hbm, tok_hbm, fill_hbm, out_hbm, pos_v, tok_v, slot_token, idx_v, *bufs):
        cid = lax.axis_index("core")
        sid = lax.axis_index("subcore")
        pltpu.sync_copy(fill_hbm.at[pl.ds(sid * init_per_tile, init_per_tile)],
                        slot_token.at[pl.ds(sid * init_per_tile, init_per_tile)])
        pltpu.sync_copy(pos_hbm.at[pl.ds(sid * rows_per_tile, rows_per_tile)], pos_v)
        pltpu.sync_copy(tok_hbm.at[pl.ds(sid * rows_per_tile, rows_per_tile)], tok_v)
        plsc.subcore_barrier()

        @pl.loop(0, rows_per_tile)
        def _(j):
            pltpu.sync_copy(tok_v.at[j], slot_token.at[pos_v.at[j]])

        plsc.subcore_barrier()
        base = (sid * sc.num_cores + cid) * per_worker
        pltpu.sync_copy(slot_token.at[pl.ds(base, per_worker)], idx_v)
        _gather_rows(table_hbm, out_hbm, base, idx_v, *bufs)

    return dispatch_kernel(table, pos.reshape(rows, LANES), tokens, fill)


def _experts_kernel(te_ref, nv_ref, xs_ref, w1_ref, b1_ref, w2_ref, b2_ref, ys_ref, *cast_refs, d_ff):
    i = pl.program_id(0)
    nvalid = nv_ref[i]
    if cast_refs:
        w1_use, w2_use = cast_refs
        prev = te_ref[jnp.maximum(i - 1, 0)]

        @pl.when((i == 0) | (te_ref[i] != prev))
        def _():
            w1_use[0] = w1_ref[0].astype(BF16)
            w2_use[0] = w2_ref[0].astype(BF16)
    else:
        w1_use, w2_use = w1_ref, w2_ref

    tm = xs_ref.shape[0]
    all_groups = [slice(r, r + EXPERT_ROWS) for r in range(0, tm, EXPERT_ROWS)]

    def ffn(groups):
        half = xs_ref.shape[1]
        xg = []
        for rows in groups:
            row = rows.start + lax.broadcasted_iota(jnp.int32, (EXPERT_ROWS, 1), 0)
            lo, hi = _unpack_bf16_pairs(jnp.where(row < nvalid, xs_ref[rows, :], 0))
            xg.append((lo.astype(BF16), hi.astype(BF16)))
        ug = [jnp.dot(lo, w1_use[0, 0:half, :], preferred_element_type=F32)
              + jnp.dot(hi, w1_use[0, half:2 * half, :], preferred_element_type=F32) + b1_ref[0] for lo, hi in xg]
        ag = []
        for u in ug:
            glu = jnp.minimum(u[:, 0:d_ff], SWIGLU_LIMIT)
            lin = jnp.clip(u[:, d_ff:2 * d_ff], -SWIGLU_LIMIT, SWIGLU_LIMIT)
            ag.append((glu * _sigmoid(SWIGLU_ALPHA * glu) * (lin + 1.0)).astype(BF16))
        for rows, act in zip(groups, ag):
            ys_ref[rows, :] = _pack_bf16_pairs(
                jnp.dot(act, w2_use[0], preferred_element_type=F32) + b2_ref[0])
        for rows in all_groups[len(groups):]:
            ys_ref[rows, :] = jnp.zeros((EXPERT_ROWS, ys_ref.shape[1]), ys_ref.dtype)

    for n_used in range(len(all_groups) + 1):
        lo_cnt, hi_cnt = (n_used - 1) * EXPERT_ROWS, n_used * EXPERT_ROWS
        pl.when((nvalid > lo_cnt) & (nvalid <= hi_cnt))(functools.partial(ffn, all_groups[:n_used]))


def _experts(xs, tile_expert, tile_valid, w1, b1, w2, b2):
    cast = w1.dtype != BF16
    n_slots = xs.shape[0]
    n_exp, d, two_ff = w1.shape
    d_ff = two_ff // 2
    tm = SLOT_TILE
    n_tiles = n_slots // tm
    kern = functools.partial(_experts_kernel, d_ff=d_ff)
    grid_spec = pltpu.PrefetchScalarGridSpec(
        num_scalar_prefetch=2,
        grid=(n_tiles,),
        in_specs=[pl.BlockSpec((tm, d // 2), lambda i, te, nv: (i, 0)),
                  pl.BlockSpec((1, d, two_ff), lambda i, te, nv: (te[i], 0, 0)),
                  pl.BlockSpec((1, 1, two_ff), lambda i, te, nv: (te[i], 0, 0)),
                  pl.BlockSpec((1, d_ff, d), lambda i, te, nv: (te[i], 0, 0)),
                  pl.BlockSpec((1, 1, d), lambda i, te, nv: (te[i], 0, 0))],
        out_specs=[pl.BlockSpec((tm, d // 2), lambda i, te, nv: (i, 0))]
        + ([pl.BlockSpec((1, d, two_ff), lambda i, te, nv: (te[i], 0, 0)),
            pl.BlockSpec((1, d_ff, d), lambda i, te, nv: (te[i], 0, 0))] if cast else []),
    )
    outs = pl.pallas_call(
        kern,
        grid_spec=grid_spec,
        out_shape=[jax.ShapeDtypeStruct((n_slots, d // 2), jnp.int32)]
        + ([jax.ShapeDtypeStruct(w1.shape, BF16), jax.ShapeDtypeStruct(w2.shape, BF16)] if cast else []),
        compiler_params=pltpu.CompilerParams(
            dimension_semantics=("arbitrary",), vmem_limit_bytes=VMEM_LIMIT_BYTES),
        name="experts",
    )(tile_expert, tile_valid, xs, w1, b1.reshape(n_exp, 1, two_ff), w2, b2.reshape(n_exp, 1, d))
    return outs if cast else (outs[0], w1, w2)


def _combine_kernel(h_ref, meta_ref, g_ref, *rest):
    y_refs, o_ref = rest[:TOP_K], rest[-1]
    d = h_ref.shape[1]
    half = d // 2
    acc_lo = h_ref[:, 0:half]
    acc_hi = h_ref[:, half:d]
    for kk in range(TOP_K):
        gate = meta_ref[:, TOP_K + kk:TOP_K + kk + 1]
        lo, hi = _unpack_bf16_pairs(y_refs[kk][...])
        acc_lo = acc_lo + gate * lo
        acc_hi = acc_hi + gate * hi
    ms = (jnp.sum(acc_lo * acc_lo, axis=-1, keepdims=True)
          + jnp.sum(acc_hi * acc_hi, axis=-1, keepdims=True)) * (1.0 / d)
    scale = lax.rsqrt(ms + NORM_EPS)
    o_ref[:, 0:half] = acc_lo * scale * g_ref[:, 0:half]
    o_ref[:, half:d] = acc_hi * scale * g_ref[:, half:d]


def _combine(h2, y4, meta, norm_g, out_prev, group, n_groups):
    t, d = h2.shape
    tt = min(MOVE_TOKENS, t)
    n_blk = t // tt
    prev_specs = [] if out_prev is None else [pl.BlockSpec(memory_space=pl.ANY)]
    prev_args = [] if out_prev is None else [out_prev]
    n_in = 3 + TOP_K
    y_specs = [pl.BlockSpec((tt, d // 2), functools.partial(lambda i, kk: (kk * n_blk + i, 0), kk=kk))
               for kk in range(TOP_K)]
    return pl.pallas_call(
        _combine_kernel,
        grid=(n_blk,),
        in_specs=[pl.BlockSpec((tt, d), lambda i: (i, 0)),
                  pl.BlockSpec((tt, LANES), lambda i: (i, 0)),
                  pl.BlockSpec((1, d), lambda i: (0, 0))] + y_specs + prev_specs,
        out_specs=pl.BlockSpec((tt, d), lambda i: (group * n_blk + i, 0)),
        out_shape=jax.ShapeDtypeStruct((n_groups * t, d), F32),
        input_output_aliases={} if out_prev is None else {n_in: 0},
        compiler_params=pltpu.CompilerParams(
            dimension_semantics=("arbitrary",), vmem_limit_bytes=VMEM_LIMIT_BYTES),
        name="combine",
    )(h2, meta, norm_g.reshape(1, d), *([y4] * TOP_K), *prev_args)


def _slots_kernel(seg_ref, metat_ref, pos_ref, te_ref, tv_ref, *, n_experts, tile):
    eidx = metat_ref[0:TOP_K, :].astype(jnp.int32)
    pos = metat_ref[2 * TOP_K:3 * TOP_K, :].astype(jnp.int32)
    for e in range(n_experts):
        pos = pos + jnp.where(eidx == e, seg_ref[e], 0)
    pos_ref[...] = pos

    @pl.when(pl.program_id(0) == 0)
    def _():
        first = lax.broadcasted_iota(jnp.int32, te_ref.shape, 1) * tile
        expert = jnp.zeros(te_ref.shape, jnp.int32)
        for e in range(n_experts):
            expert = expert + jnp.where(first >= seg_ref[n_experts + e], 1, 0)
        expert = jnp.minimum(expert, n_experts - 1)
        used_end = jnp.zeros(te_ref.shape, jnp.int32)
        for e in range(n_experts):
            used_end = used_end + jnp.where(expert == e, seg_ref[e] + seg_ref[2 * n_experts + e], 0)
        te_ref[...] = expert
        tv_ref[...] = jnp.clip(used_end - first, 0, tile)


def _slots(metat, seg_start, seg_end, cnt, n_tiles):
    t = metat.shape[1]
    n_exp = seg_start.shape[0]
    tt = min(SLOT_TOKENS, t)
    nt_pad = -(-n_tiles // LANES) * LANES
    grid_spec = pltpu.PrefetchScalarGridSpec(
        num_scalar_prefetch=1,
        grid=(t // tt,),
        in_specs=[pl.BlockSpec((2 * SUBLANES, tt), lambda i, seg: (0, i))],
        out_specs=[pl.BlockSpec((TOP_K, tt), lambda i, seg: (0, i)),
                   pl.BlockSpec((SUBLANES, nt_pad), lambda i, seg: (0, 0)),
                   pl.BlockSpec((SUBLANES, nt_pad), lambda i, seg: (0, 0))],
    )
    pos, tile_expert, tile_valid = pl.pallas_call(
        functools.partial(_slots_kernel, n_experts=n_exp, tile=SLOT_TILE),
        grid_spec=grid_spec,
        out_shape=[jax.ShapeDtypeStruct((TOP_K, t), jnp.int32),
                   jax.ShapeDtypeStruct((SUBLANES, nt_pad), jnp.int32),
                   jax.ShapeDtypeStruct((SUBLANES, nt_pad), jnp.int32)],
        compiler_params=pltpu.CompilerParams(dimension_semantics=("arbitrary",)),
        name="slots",
    )(jnp.concatenate([seg_start, seg_end, cnt]), metat)
    return pos.reshape(-1), tile_expert[0, 0:n_tiles], tile_valid[0, 0:n_tiles]


def _route(h2, hp2, norm_ffn_g, router_w, router_b):
    t, d = h2.shape
    n_exp = router_w.shape[1]
    tm = SLOT_TILE
    meta, metat, counts = _router(h2, norm_ffn_g, router_w, router_b)
    cnt = counts[0, 0:n_exp].astype(jnp.int32)
    padded = jnp.maximum((cnt + tm - 1) // tm, 1) * tm
    seg_end = jnp.cumsum(padded)
    seg_start = seg_end - padded
    n_tiles = -(-(t * TOP_K) // tm) + n_exp
    n_slots = n_tiles * tm
    pos, tile_expert, tile_valid = _slots(metat, seg_start, seg_end, cnt, n_tiles)
    xs = _dispatch_gather(hp2, pos, n_slots)
    return dict(h2=h2, meta=meta, pos=pos, xs=xs, tile_expert=tile_expert, tile_valid=tile_valid)


def kernel(x, norm_mix_g, w_in, conv_w, shift_mu, decay_w0, decay_w2, iclr_a0, iclr_a2, gate_g2, k_k, k_a,
           r_k, ln_x_g, ln_x_b, w_out, norm_ffn_g, router_w, router_b, exp_w1, exp_b1, exp_w2, exp_b2,
           norm_final_g):
    bsz, s, d = x.shape
    depth = w_in.shape[0]
    assert depth == 1, "final norm is fused into the last layer's combine kernel"
    n_groups = BATCH_GROUPS if bsz % (BATCH_GROUPS * MIX_ROWS) == 0 else 1
    mixers = _mixer(x, n_groups, norm_mix_g[0], norm_ffn_g[0], w_in[0], conv_w[0], shift_mu[0], decay_w0[0],
                    decay_w2[0], iclr_a0[0], iclr_a2[0], gate_g2[0], k_k[0], k_a[0], r_k[0], ln_x_g[0], ln_x_b[0],
                    w_out[0])
    routed = []
    for run_mixer in mixers:
        h, hp = run_mixer()
        routed.append(_route(h.reshape(-1, d), hp.reshape(-1, d // 2), norm_ffn_g[0], router_w[0], router_b[0]))
    w1, w2 = exp_w1[0], exp_w2[0]
    gathered = []
    for r in routed:
        ys, w1, w2 = _experts(r["xs"], r["tile_expert"], r["tile_valid"], w1, exp_b1[0], w2, exp_b2[0])
        gathered.append(_row_gather(ys, r["pos"]))
    out = None
    for g, (r, y4) in enumerate(zip(routed, gathered)):
        out = _combine(r["h2"], y4, r["meta"], norm_final_g, out, g, n_groups)
    return out.reshape(bsz, s, d)
```

```python
import functools

import jax
import jax.numpy as jnp
from jax import lax
from jax.experimental import pallas as pl
from jax.experimental.pallas import tpu as pltpu
from jax.experimental.pallas import tpu_sc as plsc

HEAD = 64
DECAY_LORA = 64
AAA_LORA = 64
GATE_LORA = 160
TOP_K = 4
SWIGLU_ALPHA = 1.702
SWIGLU_LIMIT = 7.0
NORM_EPS = 1e-5
GN_EPS = 64e-5
DECAY_SCALE = 0.6065306597126334

LANES = 128
SUBLANES = 8
VMEM_LIMIT_BYTES = 56 * 1024 * 1024

RWKV_CHUNK = 64
MIX_ROWS = 4
ROUTE_TOKENS = 512
MOVE_TOKENS = 512
SLOT_TOKENS = 8192
GATHER_WINDOW = 64
SLOT_TILE = 512
EXPERT_ROWS = 256
BATCH_GROUPS = 2

BF16 = jnp.bfloat16
F32 = jnp.float32


def _dot(a, b):
    return jnp.dot(a.astype(BF16), b.astype(BF16), preferred_element_type=F32)


def _dot_nt(a, b):
    return lax.dot_general(a.astype(BF16), b.astype(BF16), (((1,), (1,)), ((), ())),
                           preferred_element_type=F32)


def _dot_tn(a, b):
    return lax.dot_general(a.astype(BF16), b.astype(BF16), (((0,), (0,)), ((), ())),
                           preferred_element_type=F32)


def _sigmoid(x):
    return 0.5 * jnp.tanh(0.5 * x) + 0.5


def _pack_bf16_pairs(x):
    n = x.shape[1] // 2
    bits = pltpu.bitcast(x.astype(BF16).astype(F32), jnp.uint32)
    packed = (bits[:, n:] & jnp.uint32(0xFFFF0000)) | (bits[:, :n] >> 16)
    return pltpu.bitcast(packed, jnp.int32)


def _unpack_bf16_pairs(w):
    bits = pltpu.bitcast(w, jnp.uint32)
    return pltpu.bitcast(bits << 16, F32), pltpu.bitcast(bits & jnp.uint32(0xFFFF0000), F32)


def _shift_rows(x, n, carry, seg):
    rolled = pltpu.roll(x, n, axis=0)
    row = lax.broadcasted_iota(jnp.int32, (SUBLANES, x.shape[1]), 0)
    pieces = []
    for j in range(x.shape[0] // seg):
        head = rolled[j * seg:j * seg + SUBLANES]
        prev = carry[j * SUBLANES:(j + 1) * SUBLANES]
        for i in range(n):
            head = jnp.where(row == i, prev[SUBLANES - n + i:SUBLANES - n + i + 1, :], head)
        pieces += [head, rolled[j * seg + SUBLANES:(j + 1) * seg]]
    return jnp.concatenate(pieces, axis=0)


def _last_rows(x, seg):
    return jnp.concatenate([x[(j + 1) * seg - SUBLANES:(j + 1) * seg] for j in range(x.shape[0] // seg)],
                           axis=0)


N_MIXER_INPUTS = 22


def _mixer_kernel(*refs, n_heads, d_model, cast_weights):
    n_extra = 2 if cast_weights else 0
    ins, outs = refs[:N_MIXER_INPUTS + n_extra], refs[N_MIXER_INPUTS + n_extra:N_MIXER_INPUTS + 2 + 2 * n_extra]
    scratch = refs[N_MIXER_INPUTS + 2 + 2 * n_extra:]
    (x_ref, g_ref, wmain_ref, wlora_ref, wgate_ref, cw_ref, mu_rkv_ref, mu_lora_ref, w0_ref, w2_ref, a0_ref,
     a2_ref, g2_ref, kk_ref, ka_ref, rk_ref, lng_ref, lnb_ref, wout_ref, hsel_ref, hselt_ref,
     gffn_ref) = ins[:N_MIXER_INPUTS]
    o_ref, hp_ref = outs[:2]
    (cu_s, cp_s, cl_s, ar_s, bk_s, v_s, y_s, gam_s), state_s = scratch[:8], scratch[8:]
    if cast_weights:
        for src, dst in zip(ins[N_MIXER_INPUTS:], outs[2:]):
            dst[...] = src[...].astype(BF16)
    n_sub, L = x_ref.shape[0], x_ref.shape[1]
    tc = n_sub * L
    D = d_model

    @pl.when(pl.program_id(1) == 0)
    def _():
        for st_ref in state_s:
            st_ref[...] = jnp.zeros_like(st_ref)
        cu_s[...] = jnp.zeros_like(cu_s)
        cp_s[...] = jnp.zeros_like(cp_s)
        cl_s[...] = jnp.zeros_like(cl_s)

    def split(t):
        hi = t.astype(BF16)
        return hi, (t - hi.astype(F32)).astype(BF16)

    def head_sums(parts, exact=False):
        per_head = sum(jnp.dot(p, hsel_ref[...], preferred_element_type=F32) for p in parts)
        back = split(per_head) if exact else [per_head.astype(BF16)]
        return sum(jnp.dot(p, hselt_ref[...], preferred_element_type=F32) for p in back)

    x = x_ref[...].reshape(tc, D)
    ms = jnp.mean(x * x, axis=-1, keepdims=True)
    xn = (x * lax.rsqrt(ms + NORM_EPS) * g_ref[...]).astype(BF16)

    pc = jnp.dot(xn, wmain_ref[:, 0:3 * D], preferred_element_type=F32)
    u = pc[:, D:2 * D] * pc[:, 2 * D:3 * D]
    cu = cu_s[...]
    conv = (cw_ref[0:1, :] * _shift_rows(u, 2, cu, L) + cw_ref[1:2, :] * _shift_rows(u, 1, cu, L)
            + cw_ref[2:3, :] * u)
    y_conv = pc[:, 0:D] * conv
    cu_s[...] = _last_rows(u, L)

    pr = jnp.dot(xn, wmain_ref[:, 3 * D:6 * D], preferred_element_type=F32)
    cp = cp_s[...]
    cp_s[...] = _last_rows(pr, L)
    pr = pr + (_shift_rows(pr, 1, cp, L) - pr) * mu_rkv_ref[...]
    plo = jnp.dot(xn, wlora_ref[...], preferred_element_type=F32)
    cl = cl_s[...]
    cl_s[...] = _last_rows(plo, L)
    plo = plo + (_shift_rows(plo, 1, cl, L) - plo) * mu_lora_ref[...]

    r = pr[:, 0:D]
    k = pr[:, D:2 * D]
    v = pr[:, 2 * D:3 * D]
    wd = plo[:, 0:LANES]
    ad = plo[:, LANES:2 * LANES]
    gd = plo[:, 2 * LANES:4 * LANES]

    lw = -DECAY_SCALE * _sigmoid(w0_ref[...] + _dot(jnp.tanh(wd), w2_ref[...]))
    a = _sigmoid(a0_ref[...] + _dot(ad, a2_ref[...]))
    g = _dot(_sigmoid(gd), g2_ref[...])

    row = lax.broadcasted_iota(jnp.int32, (tc, tc), 0)
    col = lax.broadcasted_iota(jnp.int32, (tc, tc), 1)
    tri = jnp.where((row >= col) & ((row // L) == (col // L)), 1.0, 0.0).astype(BF16)
    cum = sum(jnp.dot(tri, p, preferred_element_type=F32) for p in split(lw))
    e_inv = jnp.exp(-cum)

    kkraw = k * kk_ref[...]
    ss = head_sums([(kkraw * kkraw).astype(BF16)])
    kkn = kkraw * jnp.minimum(lax.rsqrt(ss), 1e12)

    k2 = k * (1.0 + (a - 1.0) * ka_ref[...])
    a_t = kkn * jnp.exp(cum - lw)
    r_t = r * jnp.exp(cum)
    b_t = kkn * a * e_inv
    k_t = k2 * e_inv
    for c in range(n_sub):
        rows = slice(c * L, (c + 1) * L)
        ar_s[c, 0:L, :] = a_t[rows]
        ar_s[c, L:2 * L, :] = r_t[rows]
        bk_s[c, 0:L, :] = b_t[rows]
        bk_s[c, L:2 * L, :] = k_t[rows]
        gam_s[c] = jnp.exp(cum[(c + 1) * L - 1:(c + 1) * L, :])
    v_s[...] = v
    bonus = head_sums([(r * k2 * rk_ref[...]).astype(BF16)]) * v

    pair = 2 * HEAD
    lane = lax.broadcasted_iota(jnp.int32, (1, pair), 1)
    left = lane < HEAD
    row1 = lax.broadcasted_iota(jnp.int32, (L, pair), 0)
    col1 = lax.broadcasted_iota(jnp.int32, (L, pair), 1) % HEAD
    strict = row1 > col1
    eye = jnp.where(row1 == col1, 1.0, 0.0)
    incl2 = (lax.broadcasted_iota(jnp.int32, (L, 2 * pair), 0)
             >= lax.broadcasted_iota(jnp.int32, (L, 2 * pair), 1) % HEAD)
    level_mask = [(row1 // (2 * s) == col1 // (2 * s)) & (row1 % (2 * s) >= s) & (col1 % (2 * s) < s)
                  for s in (1 << i for i in range(L.bit_length() - 1))]

    def blockdiag(t):
        tb = t.astype(BF16)
        return jnp.concatenate([jnp.where(left, tb, 0), jnp.where(left, 0, tb)], axis=0)

    pairs = range(n_heads // 2)
    psl = [slice(p * pair, (p + 1) * pair) for p in pairs]
    work = [(c, p) for c in range(n_sub) for p in pairs]
    ar = [ar_s[c, :, psl[p]].astype(BF16) for c, p in work]
    bk = [bk_s[c, :, psl[p]].astype(BF16) for c, p in work]
    vp = [v_s[c * L:(c + 1) * L, psl[p]] for c, p in work]
    gm = [_dot_nt(ar[i], jnp.concatenate([blockdiag(bk[i][0:L]), blockdiag(bk[i][L:2 * L])], axis=0))
          for i in range(len(work))]
    nm = [jnp.where(strict, g_[0:L, 0:pair], 0.0) for g_ in gm]
    mak = [jnp.where(strict, g_[0:L, pair:2 * pair], 0.0) for g_ in gm]
    q = [jnp.where(incl2, g_[L:2 * L, :], 0.0).astype(BF16) for g_ in gm]
    xinv = [eye - jnp.where(level_mask[0], n_, 0.0) for n_ in nm]
    for lm in level_mask[1:]:
        half = [_dot(jnp.where(lm, n_, 0.0), blockdiag(t)) for n_, t in zip(nm, xinv)]
        xinv = [t - _dot(t, blockdiag(h_)) for t, h_ in zip(xinv, half)]
    xinv = [t.astype(BF16) for t in xinv]
    mv = [_dot(m_, blockdiag(v_)) for m_, v_ in zip(mak, vp)]

    gates = _sigmoid(jnp.dot(xn, wgate_ref[...], preferred_element_type=F32))

    items = range(len(work))
    st = [state_s[i][...] for i in items]
    ars = [_dot_nt(ar[i], blockdiag(st[i])) for i in items]
    uu = [_dot(xinv[i], blockdiag(ars[i][0:L] + mv[i])) for i in items]
    yh = [ars[i][L:2 * L] + _dot(q[i], jnp.concatenate([blockdiag(-uu[i]), blockdiag(vp[i])], axis=0))
          for i in items]
    for i, (c, p) in enumerate(work):
        upd = _dot_tn(jnp.concatenate([-uu[i], vp[i]], axis=0), bk[i])
        state_s[i][...] = (st[i] + jnp.where(left, upd[0:HEAD], upd[HEAD:pair])) * gam_s[c, :, psl[p]]
    for i, (c, p) in enumerate(work):
        y_s[c * L:(c + 1) * L, psl[p]] = yh[i]

    y = y_s[...]
    yc = y - head_sums(split(y), exact=True) * (1.0 / HEAD)
    var = head_sums([(yc * yc).astype(BF16)]) * (1.0 / HEAD)
    y_rwkv = (yc * lax.rsqrt(var + GN_EPS) * lng_ref[...] + lnb_ref[...] + bonus) * g
    mix = gates[:, 0:D] * y_conv + gates[:, D:2 * D] * y_rwkv
    h = x + jnp.dot(mix.astype(BF16), wout_ref[...], preferred_element_type=F32)
    o_ref[...] = h.reshape(n_sub, L, D)
    hn = h * lax.rsqrt(jnp.mean(h * h, axis=-1, keepdims=True) + NORM_EPS) * gffn_ref[...]
    hp_ref[...] = _pack_bf16_pairs(hn).reshape(n_sub, L, D // 2)


def _const_spec(shape):
    nd = len(shape)
    return pl.BlockSpec(shape, lambda *_: (0,) * nd, pipeline_mode=pl.Buffered(1))


def _mixer(x, n_groups, norm_g, norm_ffn_g, w_in, conv_w, shift_mu, w0, w2, a0, a2, g2, k_k, k_a, r_k, ln_g, ln_b, w_out,
           expert_weights):
    bsz, s, d = x.shape
    gb = bsz // n_groups
    n_heads = d // HEAD
    L = RWKV_CHUNK
    n_sub = MIX_ROWS
    tc = n_sub * L
    assert gb % n_sub == 0 and s % L == 0
    lora0 = 6 * d
    w_main = w_in[:, 0:6 * d].astype(BF16)
    pad = lambda t, n: jnp.pad(t, ((0, 0), (0, n - t.shape[1])))
    lora_cols = (DECAY_LORA, AAA_LORA, GATE_LORA)
    lora_pads = (LANES, LANES, 2 * LANES)
    pieces_w, pieces_mu, off = [], [], lora0
    for n, p in zip(lora_cols, lora_pads):
        pieces_w.append(pad(w_in[:, off:off + n], p))
        pieces_mu.append(pad(shift_mu[None, off - 3 * d:off - 3 * d + n], p))
        off += n
    w_lora = jnp.concatenate(pieces_w, axis=1).astype(BF16)
    mu_lora = jnp.concatenate(pieces_mu, axis=1)
    w_gate = w_in[:, off:off + 2 * d].astype(BF16)
    mu_rkv = shift_mu[None, 0:3 * d]
    padr = lambda t, n: jnp.pad(t, ((0, n - t.shape[0]), (0, 0)))
    w2p = padr(w2, LANES).astype(BF16)
    a2p = padr(a2, LANES).astype(BF16)
    g2p = padr(g2, 2 * LANES).astype(BF16)
    row = lambda t: t.reshape(1, -1)
    head_of = jnp.arange(d, dtype=jnp.int32) // HEAD
    head_sel = (head_of[:, None] == jnp.arange(LANES)[None, :]).astype(BF16)
    consts = [row(norm_g), w_main, w_lora, w_gate, conv_w, mu_rkv, mu_lora, row(w0), w2p, row(a0), a2p,
              g2p, row(k_k), row(k_a), row(r_k), row(ln_g), row(ln_b), w_out.astype(BF16), head_sel, head_sel.T,
              row(norm_ffn_g)]
    assert len(consts) + 1 == N_MIXER_INPUTS
    n_c = s // L
    n_steps = (gb // n_sub) * n_c
    n_exp = expert_weights[0].shape[0]
    pieces = n_steps // n_exp
    cast_ok = (pieces * n_exp == n_steps
               and all(w.shape[1] % pieces == 0 and (w.shape[1] // pieces) % (2 * SUBLANES) == 0
                       for w in expert_weights))

    def call(first, cast):
        ew = expert_weights if cast else ()
        ew_specs = [pl.BlockSpec((1, w.shape[1] // pieces, w.shape[2]),
                                 lambda b, c: ((b * n_c + c) // pieces, (b * n_c + c) % pieces, 0)) for w in ew]
        return pl.pallas_call(
            functools.partial(_mixer_kernel, n_heads=n_heads, d_model=d, cast_weights=cast),
            grid=(gb // n_sub, n_c),
            in_specs=[pl.BlockSpec((n_sub, L, d), lambda b, c: (b + first // n_sub, c, 0))]
            + [_const_spec(t.shape) for t in consts] + ew_specs,
            out_specs=[pl.BlockSpec((n_sub, L, d), lambda b, c: (b, c, 0)),
                       pl.BlockSpec((n_sub, L, d // 2), lambda b, c: (b, c, 0))] + ew_specs,
            out_shape=[jax.ShapeDtypeStruct((gb, s, d), F32), jax.ShapeDtypeStruct((gb, s, d // 2), jnp.int32)]
            + [jax.ShapeDtypeStruct(w.shape, BF16) for w in ew],
            scratch_shapes=[
                pltpu.VMEM((n_sub * SUBLANES, d), F32),
                pltpu.VMEM((n_sub * SUBLANES, 3 * d), F32),
                pltpu.VMEM((n_sub * SUBLANES, 4 * LANES), F32),
                pltpu.VMEM((n_sub, 2 * L, d), F32),
                pltpu.VMEM((n_sub, 2 * L, d), F32),
                pltpu.VMEM((tc, d), F32),
                pltpu.VMEM((tc, d), F32),
                pltpu.VMEM((n_sub, 1, d), F32),
            ] + [pltpu.VMEM((HEAD, 2 * HEAD), F32)] * (n_sub * n_heads // 2),
            compiler_params=pltpu.CompilerParams(
                dimension_semantics=("arbitrary", "arbitrary"), vmem_limit_bytes=VMEM_LIMIT_BYTES),
            name="mixer",
        )(x, *consts, *ew)

    return [functools.partial(call, g * gb, cast_ok and g == 0) for g in range(n_groups)]


def _router_kernel(h_ref, g_ref, rwh_ref, rwl_ref, rb_ref, tri_ref, meta_ref, metat_ref, cnt_ref, run_s, *,
                   n_experts):
    tt = h_ref.shape[0]

    @pl.when(pl.program_id(0) == 0)
    def _():
        run_s[...] = jnp.zeros_like(run_s)

    h = h_ref[...]
    ms = jnp.mean(h * h, axis=-1, keepdims=True)
    hn = h * lax.rsqrt(ms + NORM_EPS) * g_ref[...]
    hn_hi = hn.astype(BF16)
    hn_lo = (hn - hn_hi.astype(F32)).astype(BF16)
    logits = (jnp.dot(hn_hi, rwh_ref[...], preferred_element_type=F32)
              + jnp.dot(hn_lo, rwh_ref[...], preferred_element_type=F32)
              + jnp.dot(hn_hi, rwl_ref[...], preferred_element_type=F32)) + rb_ref[...]
    lane = lax.broadcasted_iota(jnp.int32, (tt, LANES), 1)
    neg = jnp.float32(-jnp.inf)
    work = jnp.where(lane < n_experts, logits, neg)
    vals, idxs = [], []
    for _ in range(TOP_K):
        m = jnp.max(work, axis=-1, keepdims=True)
        i = jnp.min(jnp.where(work == m, lane, LANES), axis=-1, keepdims=True)
        vals.append(m)
        idxs.append(i)
        work = jnp.where(lane == i, neg, work)
    ex = [jnp.exp(vv - vals[0]) for vv in vals]
    den = ex[0] + ex[1] + ex[2] + ex[3]
    gates = [e / den for e in ex]

    onehot = jnp.zeros((tt, LANES), jnp.bool_)
    for kk in range(TOP_K):
        onehot = onehot | (lane == (idxs[kk] + kk * n_experts))
    oh = jnp.where(onehot, 1.0, 0.0)
    cnt = jnp.dot(tri_ref[...], oh.astype(BF16), preferred_element_type=F32)
    tot = jnp.broadcast_to(jnp.sum(oh, axis=0, keepdims=True), (SUBLANES, LANES))
    lane8 = lax.broadcasted_iota(jnp.int32, (SUBLANES, LANES), 1)
    pk = jnp.zeros_like(tot)
    te = tot
    for j in range(1, TOP_K):
        rolled = pltpu.roll(tot, j * n_experts, axis=1)
        pk = pk + jnp.where(lane8 >= j * n_experts, rolled, 0.0)
        te = te + rolled
    before = cnt + (run_s[...] + pk)[0:1, :]
    ranks = [jnp.sum(jnp.where(onehot & (lane // n_experts == kk), before, 0.0), axis=-1, keepdims=True)
             for kk in range(TOP_K)]
    run_s[...] = run_s[...] + te
    cnt_ref[...] = run_s[...]

    meta = jnp.zeros((tt, LANES), F32)
    for kk in range(TOP_K):
        meta = jnp.where(lane == kk, idxs[kk].astype(F32), meta)
        meta = jnp.where(lane == TOP_K + kk, gates[kk], meta)
        meta = jnp.where(lane == 2 * TOP_K + kk, ranks[kk], meta)
    meta_ref[...] = meta
    metat_ref[...] = meta.T[0:2 * SUBLANES, :]


def _router(h2, norm_g, router_w, router_b):
    t, d = h2.shape
    n_experts = router_w.shape[1]
    tt = min(ROUTE_TOKENS, t)
    rw = jnp.pad(router_w, ((0, 0), (0, LANES - n_experts)))
    rw_hi = rw.astype(BF16)
    rw_lo = (rw - rw_hi.astype(F32)).astype(BF16)
    rb = jnp.pad(router_b, (0, LANES - n_experts)).reshape(1, LANES)
    tok = jnp.arange(tt, dtype=jnp.int32)
    tri = (tok[:, None] > tok[None, :]).astype(BF16)
    kern = functools.partial(_router_kernel, n_experts=n_experts)
    return pl.pallas_call(
        kern,
        grid=(t // tt,),
        in_specs=[pl.BlockSpec((tt, d), lambda i: (i, 0)),
                  pl.BlockSpec((1, d), lambda i: (0, 0)),
                  pl.BlockSpec((d, LANES), lambda i: (0, 0)),
                  pl.BlockSpec((d, LANES), lambda i: (0, 0)),
                  pl.BlockSpec((1, LANES), lambda i: (0, 0)),
                  pl.BlockSpec((tt, tt), lambda i: (0, 0))],
        out_specs=[pl.BlockSpec((tt, LANES), lambda i: (i, 0)),
                   pl.BlockSpec((2 * SUBLANES, tt), lambda i: (0, i)),
                   pl.BlockSpec((SUBLANES, LANES), lambda i: (0, 0))],
        out_shape=[jax.ShapeDtypeStruct((t, LANES), F32),
                   jax.ShapeDtypeStruct((2 * SUBLANES, t), F32),
                   jax.ShapeDtypeStruct((SUBLANES, LANES), F32)],
        scratch_shapes=[pltpu.VMEM((SUBLANES, LANES), F32)],
        compiler_params=pltpu.CompilerParams(
            dimension_semantics=("arbitrary",), vmem_limit_bytes=VMEM_LIMIT_BYTES),
        name="router",
    )(h2, norm_g.reshape(1, d), rw_hi, rw_lo, rb, tri)


def _gather_scratch(per_worker, d, dtype):
    w = GATHER_WINDOW
    return [pltpu.VMEM((per_worker,), jnp.int32),
            pltpu.VMEM((w, d), dtype), pltpu.VMEM((w, d), dtype),
            pltpu.SemaphoreType.DMA, pltpu.SemaphoreType.DMA, pltpu.SemaphoreType.DMA, pltpu.SemaphoreType.DMA]


def _gather_rows(table_hbm, out_hbm, base, idx_v, buf_a, buf_b, gsem_a, gsem_b, psem_a, psem_b):
    w = GATHER_WINDOW
    n_win = idx_v.shape[0] // w

    def gather(j, buf, sem):
        return pltpu.make_async_copy(table_hbm.at[idx_v.at[pl.ds(j * w, w)]], buf, sem)

    def put(j, buf, sem):
        return pltpu.make_async_copy(buf, out_hbm.at[pl.ds(base + j * w, w)], sem)

    gather(0, buf_a, gsem_a).start()

    @pl.loop(0, n_win, step=2)
    def _(j):
        gather(j, buf_a, gsem_a).wait()

        @pl.when(j > 0)
        def _():
            put(j - 1, buf_b, psem_b).wait()

        gather(j + 1, buf_b, gsem_b).start()
        put(j, buf_a, psem_a).start()
        gather(j + 1, buf_b, gsem_b).wait()
        put(j, buf_a, psem_a).wait()

        @pl.when(j + 2 < n_win)
        def _():
            gather(j + 2, buf_a, gsem_a).start()

        put(j + 1, buf_b, psem_b).start()

    put(n_win - 1, buf_b, psem_b).wait()


def _worker_split(n_rows):
    sc = plsc.get_sparse_core_info()
    n_workers = sc.num_cores * sc.num_subcores
    per_worker = n_rows // n_workers
    n_win = per_worker // GATHER_WINDOW
    assert per_worker * n_workers == n_rows and n_win * GATHER_WINDOW == per_worker and n_win % 2 == 0
    return sc, per_worker


def _row_gather(table, idx):
    n_idx, d = idx.shape[0], table.shape[1]
    sc, per_worker = _worker_split(n_idx)
    mesh = plsc.VectorSubcoreMesh(core_axis_name="core", subcore_axis_name="subcore")

    @functools.partial(pl.kernel, out_type=jax.ShapeDtypeStruct((n_idx, d), table.dtype), mesh=mesh,
                       scratch_types=_gather_scratch(per_worker, d, table.dtype))
    def gather_kernel(table_hbm, idx_hbm, out_hbm, idx_v, *bufs):
        base = (lax.axis_index("subcore") * sc.num_cores + lax.axis_index("core")) * per_worker
        pltpu.sync_copy(idx_hbm.at[pl.ds(base, per_worker)], idx_v)
        _gather_rows(table_hbm, out_hbm, base, idx_v, *bufs)

    return gather_kernel(table, idx)


def _dispatch_gather(table, pos, n_slots):
    n, d = pos.shape[0], table.shape[1]
    sc, per_worker = _worker_split(n_slots)
    n_sub = sc.num_subcores
    rows = n // LANES
    rows_per_tile = rows // n_sub
    init_per_tile = n_slots // n_sub
    assert rows_per_tile * n_sub * LANES == n and init_per_tile * n_sub == n_slots
    mesh = plsc.VectorSubcoreMesh(core_axis_name="core", subcore_axis_name="subcore")
    tokens = (jnp.arange(n, dtype=jnp.int32) % (n // TOP_K)).reshape(rows, LANES)
    fill = jnp.arange(n_slots, dtype=jnp.int32) % (n // TOP_K)

    @functools.partial(
        pl.kernel, out_type=jax.ShapeDtypeStruct((n_slots, d), table.dtype), mesh=mesh,
        scratch_types=[pltpu.VMEM((rows_per_tile, LANES), jnp.int32),
                       pltpu.VMEM((rows_per_tile, LANES), jnp.int32),
                       pltpu.VMEM_SHARED((n_slots,), jnp.int32)] + _gather_scratch(per_worker, d, table.dtype))
    def dispatch_kernel(table_hbm, pos_hbm, tok_hbm, fill_hbm, out_hbm, pos_v, tok_v, slot_token, idx_v, *bufs):
        cid = lax.axis_index("core")
        sid = lax.axis_index("subcore")
        pltpu.sync_copy(fill_hbm.at[pl.ds(sid * init_per_tile, init_per_tile)],
                        slot_token.at[pl.ds(sid * init_per_tile, init_per_tile)])
        pltpu.sync_copy(pos_hbm.at[pl.ds(sid * rows_per_tile, rows_per_tile)], pos_v)
        pltpu.sync_copy(tok_hbm.at[pl.ds(sid * rows_per_tile, rows_per_tile)], tok_v)
        plsc.subcore_barrier()

        @pl.loop(0, rows_per_tile)
        def _(j):
            pltpu.sync_copy(tok_v.at[j], slot_token.at[pos_v.at[j]])

        plsc.subcore_barrier()
        base = (sid * sc.num_cores + cid) * per_worker
        pltpu.sync_copy(slot_token.at[pl.ds(base, per_worker)], idx_v)
        _gather_rows(table_hbm, out_hbm, base, idx_v, *bufs)

    return dispatch_kernel(table, pos.reshape(rows, LANES), tokens, fill)


def _experts_kernel(te_ref, nv_ref, xs_ref, w1_ref, b1_ref, w2_ref, b2_ref, ys_ref, *cast_refs, d_ff):
    i = pl.program_id(0)
    nvalid = nv_ref[i]
    if cast_refs:
        w1_use, w2_use = cast_refs
        prev = te_ref[jnp.maximum(i - 1, 0)]

        @pl.when((i == 0) | (te_ref[i] != prev))
        def _():
            w1_use[0] = w1_ref[0].astype(BF16)
            w2_use[0] = w2_ref[0].astype(BF16)
    else:
        w1_use, w2_use = w1_ref, w2_ref

    tm = xs_ref.shape[0]
    all_groups = [slice(r, r + EXPERT_ROWS) for r in range(0, tm, EXPERT_ROWS)]

    def ffn(groups):
        half = xs_ref.shape[1]
        xg = []
        for rows in groups:
            row = rows.start + lax.broadcasted_iota(jnp.int32, (EXPERT_ROWS, 1), 0)
            lo, hi = _unpack_bf16_pairs(jnp.where(row < nvalid, xs_ref[rows, :], 0))
            xg.append((lo.astype(BF16), hi.astype(BF16)))
        ug = [jnp.dot(lo, w1_use[0, 0:half, :], preferred_element_type=F32)
              + jnp.dot(hi, w1_use[0, half:2 * half, :], preferred_element_type=F32) + b1_ref[0] for lo, hi in xg]
        ag = []
        for u in ug:
            glu = jnp.minimum(u[:, 0:d_ff], SWIGLU_LIMIT)
            lin = jnp.clip(u[:, d_ff:2 * d_ff], -SWIGLU_LIMIT, SWIGLU_LIMIT)
            ag.append((glu * _sigmoid(SWIGLU_ALPHA * glu) * (lin + 1.0)).astype(BF16))
        for rows, act in zip(groups, ag):
            ys_ref[rows, :] = _pack_bf16_pairs(
                jnp.dot(act, w2_use[0], preferred_element_type=F32) + b2_ref[0])
        for rows in all_groups[len(groups):]:
            ys_ref[rows, :] = jnp.zeros((EXPERT_ROWS, ys_ref.shape[1]), ys_ref.dtype)

    for n_used in range(len(all_groups) + 1):
        lo_cnt, hi_cnt = (n_used - 1) * EXPERT_ROWS, n_used * EXPERT_ROWS
        pl.when((nvalid > lo_cnt) & (nvalid <= hi_cnt))(functools.partial(ffn, all_groups[:n_used]))


def _experts(xs, tile_expert, tile_valid, w1, b1, w2, b2):
    cast = w1.dtype != BF16
    n_slots = xs.shape[0]
    n_exp, d, two_ff = w1.shape
    d_ff = two_ff // 2
    tm = SLOT_TILE
    n_tiles = n_slots // tm
    kern = functools.partial(_experts_kernel, d_ff=d_ff)
    grid_spec = pltpu.PrefetchScalarGridSpec(
        num_scalar_prefetch=2,
        grid=(n_tiles,),
        in_specs=[pl.BlockSpec((tm, d // 2), lambda i, te, nv: (i, 0)),
                  pl.BlockSpec((1, d, two_ff), lambda i, te, nv: (te[i], 0, 0)),
                  pl.BlockSpec((1, 1, two_ff), lambda i, te, nv: (te[i], 0, 0)),
                  pl.BlockSpec((1, d_ff, d), lambda i, te, nv: (te[i], 0, 0)),
                  pl.BlockSpec((1, 1, d), lambda i, te, nv: (te[i], 0, 0))],
        out_specs=[pl.BlockSpec((tm, d // 2), lambda i, te, nv: (i, 0))]
        + ([pl.BlockSpec((1, d, two_ff), lambda i, te, nv: (te[i], 0, 0)),
            pl.BlockSpec((1, d_ff, d), lambda i, te, nv: (te[i], 0, 0))] if cast else []),
    )
    outs = pl.pallas_call(
        kern,
        grid_spec=grid_spec,
        out_shape=[jax.ShapeDtypeStruct((n_slots, d // 2), jnp.int32)]
        + ([jax.ShapeDtypeStruct(w1.shape, BF16), jax.ShapeDtypeStruct(w2.shape, BF16)] if cast else []),
        compiler_params=pltpu.CompilerParams(
            dimension_semantics=("arbitrary",), vmem_limit_bytes=VMEM_LIMIT_BYTES),
        name="experts",
    )(tile_expert, tile_valid, xs, w1, b1.reshape(n_exp, 1, two_ff), w2, b2.reshape(n_exp, 1, d))
    return outs if cast else (outs[0], w1, w2)


def _combine_kernel(h_ref, meta_ref, g_ref, *rest):
    y_refs, o_ref = rest[:TOP_K], rest[-1]
    d = h_ref.shape[1]
    half = d // 2
    acc_lo = h_ref[:, 0:half]
    acc_hi = h_ref[:, half:d]
    for kk in range(TOP_K):
        gate = meta_ref[:, TOP_K + kk:TOP_K + kk + 1]
        lo, hi = _unpack_bf16_pairs(y_refs[kk][...])
        acc_lo = acc_lo + gate * lo
        acc_hi = acc_hi + gate * hi
    ms = (jnp.sum(acc_lo * acc_lo, axis=-1, keepdims=True)
          + jnp.sum(acc_hi * acc_hi, axis=-1, keepdims=True)) * (1.0 / d)
    scale = lax.rsqrt(ms + NORM_EPS)
    o_ref[:, 0:half] = acc_lo * scale * g_ref[:, 0:half]
    o_ref[:, half:d] = acc_hi * scale * g_ref[:, half:d]


def _combine(h2, y4, meta, norm_g, out_prev, group, n_groups):
    t, d = h2.shape
    tt = min(MOVE_TOKENS, t)
    n_blk = t // tt
    prev_specs = [] if out_prev is None else [pl.BlockSpec(memory_space=pl.ANY)]
    prev_args = [] if out_prev is None else [out_prev]
    n_in = 3 + TOP_K
    y_specs = [pl.BlockSpec((tt, d // 2), functools.partial(lambda i, kk: (kk * n_blk + i, 0), kk=kk))
               for kk in range(TOP_K)]
    return pl.pallas_call(
        _combine_kernel,
        grid=(n_blk,),
        in_specs=[pl.BlockSpec((tt, d), lambda i: (i, 0)),
                  pl.BlockSpec((tt, LANES), lambda i: (i, 0)),
                  pl.BlockSpec((1, d), lambda i: (0, 0))] + y_specs + prev_specs,
        out_specs=pl.BlockSpec((tt, d), lambda i: (group * n_blk + i, 0)),
        out_shape=jax.ShapeDtypeStruct((n_groups * t, d), F32),
        input_output_aliases={} if out_prev is None else {n_in: 0},
        compiler_params=pltpu.CompilerParams(
            dimension_semantics=("arbitrary",), vmem_limit_bytes=VMEM_LIMIT_BYTES),
        name="combine",
    )(h2, meta, norm_g.reshape(1, d), *([y4] * TOP_K), *prev_args)


def _slots_kernel(seg_ref, metat_ref, pos_ref, te_ref, tv_ref, *, n_experts, tile):
    eidx = metat_ref[0:TOP_K, :].astype(jnp.int32)
    pos = metat_ref[2 * TOP_K:3 * TOP_K, :].astype(jnp.int32)
    for e in range(n_experts):
        pos = pos + jnp.where(eidx == e, seg_ref[e], 0)
    pos_ref[...] = pos

    @pl.when(pl.program_id(0) == 0)
    def _():
        first = lax.broadcasted_iota(jnp.int32, te_ref.shape, 1) * tile
        expert = jnp.zeros(te_ref.shape, jnp.int32)
        for e in range(n_experts):
            expert = expert + jnp.where(first >= seg_ref[n_experts + e], 1, 0)
        expert = jnp.minimum(expert, n_experts - 1)
        used_end = jnp.zeros(te_ref.shape, jnp.int32)
        for e in range(n_experts):
            used_end = used_end + jnp.where(expert == e, seg_ref[e] + seg_ref[2 * n_experts + e], 0)
        te_ref[...] = expert
        tv_ref[...] = jnp.clip(used_end - first, 0, tile)


def _slots(metat, seg_start, seg_end, cnt, n_tiles):
    t = metat.shape[1]
    n_exp = seg_start.shape[0]
    tt = min(SLOT_TOKENS, t)
    nt_pad = -(-n_tiles // LANES) * LANES
    grid_spec = pltpu.PrefetchScalarGridSpec(
        num_scalar_prefetch=1,
        grid=(t // tt,),
        in_specs=[pl.BlockSpec((2 * SUBLANES, tt), lambda i, seg: (0, i))],
        out_specs=[pl.BlockSpec((TOP_K, tt), lambda i, seg: (0, i)),
                   pl.BlockSpec((SUBLANES, nt_pad), lambda i, seg: (0, 0)),
                   pl.BlockSpec((SUBLANES, nt_pad), lambda i, seg: (0, 0))],
    )
    pos, tile_expert, tile_valid = pl.pallas_call(
        functools.partial(_slots_kernel, n_experts=n_exp, tile=SLOT_TILE),
        grid_spec=grid_spec,
        out_shape=[jax.ShapeDtypeStruct((TOP_K, t), jnp.int32),
                   jax.ShapeDtypeStruct((SUBLANES, nt_pad), jnp.int32),
                   jax.ShapeDtypeStruct((SUBLANES, nt_pad), jnp.int32)],
        compiler_params=pltpu.CompilerParams(dimension_semantics=("arbitrary",)),
        name="slots",
    )(jnp.concatenate([seg_start, seg_end, cnt]), metat)
    return pos.reshape(-1), tile_expert[0, 0:n_tiles], tile_valid[0, 0:n_tiles]


def _route(h2, hp2, norm_ffn_g, router_w, router_b):
    t, d = h2.shape
    n_exp = router_w.shape[1]
    tm = SLOT_TILE
    meta, metat, counts = _router(h2, norm_ffn_g, router_w, router_b)
    cnt = counts[0, 0:n_exp].astype(jnp.int32)
    padded = jnp.maximum((cnt + tm - 1) // tm, 1) * tm
    seg_end = jnp.cumsum(padded)
    seg_start = seg_end - padded
    n_tiles = -(-(t * TOP_K) // tm) + n_exp
    n_slots = n_tiles * tm
    pos, tile_expert, tile_valid = _slots(metat, seg_start, seg_end, cnt, n_tiles)
    xs = _dispatch_gather(hp2, pos, n_slots)
    return dict(h2=h2, meta=meta, pos=pos, xs=xs, tile_expert=tile_expert, tile_valid=tile_valid)


def kernel(x, norm_mix_g, w_in, conv_w, shift_mu, decay_w0, decay_w2, iclr_a0, iclr_a2, gate_g2, k_k, k_a,
           r_k, ln_x_g, ln_x_b, w_out, norm_ffn_g, router_w, router_b, exp_w1, exp_b1, exp_w2, exp_b2,
           norm_final_g):
    bsz, s, d = x.shape
    depth = w_in.shape[0]
    assert depth == 1, "final norm is fused into the last layer's combine kernel"
    n_groups = BATCH_GROUPS if bsz % (BATCH_GROUPS * MIX_ROWS) == 0 else 1
    mixers = _mixer(x, n_groups, norm_mix_g[0], norm_ffn_g[0], w_in[0], conv_w[0], shift_mu[0], decay_w0[0],
                    decay_w2[0], iclr_a0[0], iclr_a2[0], gate_g2[0], k_k[0], k_a[0], r_k[0], ln_x_g[0], ln_x_b[0],
                    w_out[0], (exp_w1[0], exp_w2[0]))
    w1, w2 = exp_w1[0], exp_w2[0]
    routed = []
    for run_mixer in mixers:
        h, hp, *cast = run_mixer()
        if cast:
            w1, w2 = cast
        routed.append(_route(h.reshape(-1, d), hp.reshape(-1, d // 2), norm_ffn_g[0], router_w[0], router_b[0]))
    gathered = []
    for r in routed:
        ys, w1, w2 = _experts(r["xs"], r["tile_expert"], r["tile_valid"], w1, exp_b1[0], w2, exp_b2[0])
        gathered.append(_row_gather(ys, r["pos"]))
    out = None
    for g, (r, y4) in enumerate(zip(routed, gathered)):
        out = _combine(r["h2"], y4, r["meta"], norm_final_g, out, g, n_groups)
    return out.reshape(bsz, s, d)
```

```python
import functools

import jax
import jax.numpy as jnp
from jax import lax
from jax.experimental import pallas as pl
from jax.experimental.pallas import tpu as pltpu
from jax.experimental.pallas import tpu_sc as plsc

HEAD = 64
DECAY_LORA = 64
AAA_LORA = 64
GATE_LORA = 160
TOP_K = 4
SWIGLU_ALPHA = 1.702
SWIGLU_LIMIT = 7.0
NORM_EPS = 1e-5
GN_EPS = 64e-5
DECAY_SCALE = 0.6065306597126334

LANES = 128
SUBLANES = 8
VMEM_LIMIT_BYTES = 56 * 1024 * 1024

RWKV_CHUNK = 64
MIX_ROWS = 4
ROUTE_TOKENS = 512
MOVE_TOKENS = 512
SLOT_TOKENS = 8192
GATHER_WINDOW = 64
SLOT_TILE = 1024
EXPERT_ROWS = 512
BATCH_GROUPS = 2

BF16 = jnp.bfloat16
F32 = jnp.float32


def _dot(a, b):
    return jnp.dot(a.astype(BF16), b.astype(BF16), preferred_element_type=F32)


def _dot_nt(a, b):
    return lax.dot_general(a.astype(BF16), b.astype(BF16), (((1,), (1,)), ((), ())),
                           preferred_element_type=F32)


def _dot_tn(a, b):
    return lax.dot_general(a.astype(BF16), b.astype(BF16), (((0,), (0,)), ((), ())),
                           preferred_element_type=F32)


def _sigmoid(x):
    return 0.5 * jnp.tanh(0.5 * x) + 0.5


def _pack_bf16_pairs(x):
    n = x.shape[1] // 2
    bits = pltpu.bitcast(x.astype(BF16).astype(F32), jnp.uint32)
    packed = (bits[:, n:] & jnp.uint32(0xFFFF0000)) | (bits[:, :n] >> 16)
    return pltpu.bitcast(packed, jnp.int32)


def _unpack_bf16_pairs(w):
    bits = pltpu.bitcast(w, jnp.uint32)
    return pltpu.bitcast(bits << 16, F32), pltpu.bitcast(bits & jnp.uint32(0xFFFF0000), F32)


def _shift_rows(x, n, carry, seg):
    rolled = pltpu.roll(x, n, axis=0)
    row = lax.broadcasted_iota(jnp.int32, (SUBLANES, x.shape[1]), 0)
    pieces = []
    for j in range(x.shape[0] // seg):
        head = rolled[j * seg:j * seg + SUBLANES]
        prev = carry[j * SUBLANES:(j + 1) * SUBLANES]
        for i in range(n):
            head = jnp.where(row == i, prev[SUBLANES - n + i:SUBLANES - n + i + 1, :], head)
        pieces += [head, rolled[j * seg + SUBLANES:(j + 1) * seg]]
    return jnp.concatenate(pieces, axis=0)


def _last_rows(x, seg):
    return jnp.concatenate([x[(j + 1) * seg - SUBLANES:(j + 1) * seg] for j in range(x.shape[0] // seg)],
                           axis=0)


N_MIXER_INPUTS = 22


def _mixer_kernel(*refs, n_heads, d_model, cast_weights):
    n_extra = 2 if cast_weights else 0
    ins, outs = refs[:N_MIXER_INPUTS + n_extra], refs[N_MIXER_INPUTS + n_extra:N_MIXER_INPUTS + 2 + 2 * n_extra]
    scratch = refs[N_MIXER_INPUTS + 2 + 2 * n_extra:]
    (x_ref, g_ref, wmain_ref, wlora_ref, wgate_ref, cw_ref, mu_rkv_ref, mu_lora_ref, w0_ref, w2_ref, a0_ref,
     a2_ref, g2_ref, kk_ref, ka_ref, rk_ref, lng_ref, lnb_ref, wout_ref, hsel_ref, hselt_ref,
     gffn_ref) = ins[:N_MIXER_INPUTS]
    o_ref, hp_ref = outs[:2]
    (cu_s, cp_s, cl_s, ar_s, bk_s, v_s, y_s, gam_s), state_s = scratch[:8], scratch[8:]
    if cast_weights:
        for src, dst in zip(ins[N_MIXER_INPUTS:], outs[2:]):
            dst[...] = src[...].astype(BF16)
    n_sub, L = x_ref.shape[0], x_ref.shape[1]
    tc = n_sub * L
    D = d_model

    @pl.when(pl.program_id(1) == 0)
    def _():
        for st_ref in state_s:
            st_ref[...] = jnp.zeros_like(st_ref)
        cu_s[...] = jnp.zeros_like(cu_s)
        cp_s[...] = jnp.zeros_like(cp_s)
        cl_s[...] = jnp.zeros_like(cl_s)

    def split(t):
        hi = t.astype(BF16)
        return hi, (t - hi.astype(F32)).astype(BF16)

    def head_sums(parts, exact=False):
        per_head = sum(jnp.dot(p, hsel_ref[...], preferred_element_type=F32) for p in parts)
        back = split(per_head) if exact else [per_head.astype(BF16)]
        return sum(jnp.dot(p, hselt_ref[...], preferred_element_type=F32) for p in back)

    x = x_ref[...].reshape(tc, D)
    ms = jnp.mean(x * x, axis=-1, keepdims=True)
    xn = (x * lax.rsqrt(ms + NORM_EPS) * g_ref[...]).astype(BF16)

    pc = jnp.dot(xn, wmain_ref[:, 0:3 * D], preferred_element_type=F32)
    u = pc[:, D:2 * D] * pc[:, 2 * D:3 * D]
    cu = cu_s[...]
    conv = (cw_ref[0:1, :] * _shift_rows(u, 2, cu, L) + cw_ref[1:2, :] * _shift_rows(u, 1, cu, L)
            + cw_ref[2:3, :] * u)
    y_conv = pc[:, 0:D] * conv
    cu_s[...] = _last_rows(u, L)

    pr = jnp.dot(xn, wmain_ref[:, 3 * D:6 * D], preferred_element_type=F32)
    cp = cp_s[...]
    cp_s[...] = _last_rows(pr, L)
    pr = pr + (_shift_rows(pr, 1, cp, L) - pr) * mu_rkv_ref[...]
    plo = jnp.dot(xn, wlora_ref[...], preferred_element_type=F32)
    cl = cl_s[...]
    cl_s[...] = _last_rows(plo, L)
    plo = plo + (_shift_rows(plo, 1, cl, L) - plo) * mu_lora_ref[...]

    r = pr[:, 0:D]
    k = pr[:, D:2 * D]
    v = pr[:, 2 * D:3 * D]
    wd = plo[:, 0:LANES]
    ad = plo[:, LANES:2 * LANES]
    gd = plo[:, 2 * LANES:4 * LANES]

    lw = -DECAY_SCALE * _sigmoid(w0_ref[...] + _dot(jnp.tanh(wd), w2_ref[...]))
    a = _sigmoid(a0_ref[...] + _dot(ad, a2_ref[...]))
    g = _dot(_sigmoid(gd), g2_ref[...])

    row = lax.broadcasted_iota(jnp.int32, (tc, tc), 0)
    col = lax.broadcasted_iota(jnp.int32, (tc, tc), 1)
    tri = jnp.where((row >= col) & ((row // L) == (col // L)), 1.0, 0.0).astype(BF16)
    cum = sum(jnp.dot(tri, p, preferred_element_type=F32) for p in split(lw))
    e_inv = jnp.exp(-cum)

    kkraw = k * kk_ref[...]
    ss = head_sums([(kkraw * kkraw).astype(BF16)])
    kkn = kkraw * jnp.minimum(lax.rsqrt(ss), 1e12)

    k2 = k * (1.0 + (a - 1.0) * ka_ref[...])
    a_t = kkn * jnp.exp(cum - lw)
    r_t = r * jnp.exp(cum)
    b_t = kkn * a * e_inv
    k_t = k2 * e_inv
    for c in range(n_sub):
        rows = slice(c * L, (c + 1) * L)
        ar_s[c, 0:L, :] = a_t[rows]
        ar_s[c, L:2 * L, :] = r_t[rows]
        bk_s[c, 0:L, :] = b_t[rows]
        bk_s[c, L:2 * L, :] = k_t[rows]
        gam_s[c] = jnp.exp(cum[(c + 1) * L - 1:(c + 1) * L, :])
    v_s[...] = v
    bonus = head_sums([(r * k2 * rk_ref[...]).astype(BF16)]) * v

    pair = 2 * HEAD
    lane = lax.broadcasted_iota(jnp.int32, (1, pair), 1)
    left = lane < HEAD
    row1 = lax.broadcasted_iota(jnp.int32, (L, pair), 0)
    col1 = lax.broadcasted_iota(jnp.int32, (L, pair), 1) % HEAD
    strict = row1 > col1
    eye = jnp.where(row1 == col1, 1.0, 0.0)
    incl2 = (lax.broadcasted_iota(jnp.int32, (L, 2 * pair), 0)
             >= lax.broadcasted_iota(jnp.int32, (L, 2 * pair), 1) % HEAD)
    level_mask = [(row1 // (2 * s) == col1 // (2 * s)) & (row1 % (2 * s) >= s) & (col1 % (2 * s) < s)
                  for s in (1 << i for i in range(L.bit_length() - 1))]

    def blockdiag(t):
        tb = t.astype(BF16)
        return jnp.concatenate([jnp.where(left, tb, 0), jnp.where(left, 0, tb)], axis=0)

    pairs = range(n_heads // 2)
    psl = [slice(p * pair, (p + 1) * pair) for p in pairs]
    work = [(c, p) for c in range(n_sub) for p in pairs]
    ar = [ar_s[c, :, psl[p]].astype(BF16) for c, p in work]
    bk = [bk_s[c, :, psl[p]].astype(BF16) for c, p in work]
    vp = [v_s[c * L:(c + 1) * L, psl[p]] for c, p in work]
    gm = [_dot_nt(ar[i], jnp.concatenate([blockdiag(bk[i][0:L]), blockdiag(bk[i][L:2 * L])], axis=0))
          for i in range(len(work))]
    nm = [jnp.where(strict, g_[0:L, 0:pair], 0.0) for g_ in gm]
    mak = [jnp.where(strict, g_[0:L, pair:2 * pair], 0.0) for g_ in gm]
    q = [jnp.where(incl2, g_[L:2 * L, :], 0.0).astype(BF16) for g_ in gm]
    xinv = [eye - jnp.where(level_mask[0], n_, 0.0) for n_ in nm]
    for lm in level_mask[1:]:
        half = [_dot(jnp.where(lm, n_, 0.0), blockdiag(t)) for n_, t in zip(nm, xinv)]
        xinv = [t - _dot(t, blockdiag(h_)) for t, h_ in zip(xinv, half)]
    xinv = [t.astype(BF16) for t in xinv]
    mv = [_dot(m_, blockdiag(v_)) for m_, v_ in zip(mak, vp)]

    gates = _sigmoid(jnp.dot(xn, wgate_ref[...], preferred_element_type=F32))

    items = range(len(work))
    st = [state_s[i][...] for i in items]
    ars = [_dot_nt(ar[i], blockdiag(st[i])) for i in items]
    uu = [_dot(xinv[i], blockdiag(ars[i][0:L] + mv[i])) for i in items]
    yh = [ars[i][L:2 * L] + _dot(q[i], jnp.concatenate([blockdiag(-uu[i]), blockdiag(vp[i])], axis=0))
          for i in items]
    for i, (c, p) in enumerate(work):
        upd = _dot_tn(jnp.concatenate([-uu[i], vp[i]], axis=0), bk[i])
        state_s[i][...] = (st[i] + jnp.where(left, upd[0:HEAD], upd[HEAD:pair])) * gam_s[c, :, psl[p]]
    for i, (c, p) in enumerate(work):
        y_s[c * L:(c + 1) * L, psl[p]] = yh[i]

    y = y_s[...]
    yc = y - head_sums(split(y), exact=True) * (1.0 / HEAD)
    var = head_sums([(yc * yc).astype(BF16)]) * (1.0 / HEAD)
    y_rwkv = (yc * lax.rsqrt(var + GN_EPS) * lng_ref[...] + lnb_ref[...] + bonus) * g
    mix = gates[:, 0:D] * y_conv + gates[:, D:2 * D] * y_rwkv
    h = x + jnp.dot(mix.astype(BF16), wout_ref[...], preferred_element_type=F32)
    o_ref[...] = h.reshape(n_sub, L, D)
    hn = h * lax.rsqrt(jnp.mean(h * h, axis=-1, keepdims=True) + NORM_EPS) * gffn_ref[...]
    hp_ref[...] = _pack_bf16_pairs(hn).reshape(n_sub, L, D // 2)


def _const_spec(shape):
    nd = len(shape)
    return pl.BlockSpec(shape, lambda *_: (0,) * nd, pipeline_mode=pl.Buffered(1))


def _mixer(x, n_groups, norm_g, norm_ffn_g, w_in, conv_w, shift_mu, w0, w2, a0, a2, g2, k_k, k_a, r_k, ln_g, ln_b, w_out,
           expert_weights):
    bsz, s, d = x.shape
    gb = bsz // n_groups
    n_heads = d // HEAD
    L = RWKV_CHUNK
    n_sub = MIX_ROWS
    tc = n_sub * L
    assert gb % n_sub == 0 and s % L == 0
    lora0 = 6 * d
    w_main = w_in[:, 0:6 * d].astype(BF16)
    pad = lambda t, n: jnp.pad(t, ((0, 0), (0, n - t.shape[1])))
    lora_cols = (DECAY_LORA, AAA_LORA, GATE_LORA)
    lora_pads = (LANES, LANES, 2 * LANES)
    pieces_w, pieces_mu, off = [], [], lora0
    for n, p in zip(lora_cols, lora_pads):
        pieces_w.append(pad(w_in[:, off:off + n], p))
        pieces_mu.append(pad(shift_mu[None, off - 3 * d:off - 3 * d + n], p))
        off += n
    w_lora = jnp.concatenate(pieces_w, axis=1).astype(BF16)
    mu_lora = jnp.concatenate(pieces_mu, axis=1)
    w_gate = w_in[:, off:off + 2 * d].astype(BF16)
    mu_rkv = shift_mu[None, 0:3 * d]
    padr = lambda t, n: jnp.pad(t, ((0, n - t.shape[0]), (0, 0)))
    w2p = padr(w2, LANES).astype(BF16)
    a2p = padr(a2, LANES).astype(BF16)
    g2p = padr(g2, 2 * LANES).astype(BF16)
    row = lambda t: t.reshape(1, -1)
    head_of = jnp.arange(d, dtype=jnp.int32) // HEAD
    head_sel = (head_of[:, None] == jnp.arange(LANES)[None, :]).astype(BF16)
    consts = [row(norm_g), w_main, w_lora, w_gate, conv_w, mu_rkv, mu_lora, row(w0), w2p, row(a0), a2p,
              g2p, row(k_k), row(k_a), row(r_k), row(ln_g), row(ln_b), w_out.astype(BF16), head_sel, head_sel.T,
              row(norm_ffn_g)]
    assert len(consts) + 1 == N_MIXER_INPUTS
    n_c = s // L
    n_steps = (gb // n_sub) * n_c
    n_exp = expert_weights[0].shape[0]
    pieces = n_steps // n_exp
    cast_ok = (pieces * n_exp == n_steps
               and all(w.shape[1] % pieces == 0 and (w.shape[1] // pieces) % (2 * SUBLANES) == 0
                       for w in expert_weights))

    def call(first, cast):
        ew = expert_weights if cast else ()
        ew_specs = [pl.BlockSpec((1, w.shape[1] // pieces, w.shape[2]),
                                 lambda b, c: ((b * n_c + c) // pieces, (b * n_c + c) % pieces, 0)) for w in ew]
        return pl.pallas_call(
            functools.partial(_mixer_kernel, n_heads=n_heads, d_model=d, cast_weights=cast),
            grid=(gb // n_sub, n_c),
            in_specs=[pl.BlockSpec((n_sub, L, d), lambda b, c: (b + first // n_sub, c, 0))]
            + [_const_spec(t.shape) for t in consts] + ew_specs,
            out_specs=[pl.BlockSpec((n_sub, L, d), lambda b, c: (b, c, 0)),
                       pl.BlockSpec((n_sub, L, d // 2), lambda b, c: (b, c, 0))] + ew_specs,
            out_shape=[jax.ShapeDtypeStruct((gb, s, d), F32), jax.ShapeDtypeStruct((gb, s, d // 2), jnp.int32)]
            + [jax.ShapeDtypeStruct(w.shape, BF16) for w in ew],
            scratch_shapes=[
                pltpu.VMEM((n_sub * SUBLANES, d), F32),
                pltpu.VMEM((n_sub * SUBLANES, 3 * d), F32),
                pltpu.VMEM((n_sub * SUBLANES, 4 * LANES), F32),
                pltpu.VMEM((n_sub, 2 * L, d), F32),
                pltpu.VMEM((n_sub, 2 * L, d), F32),
                pltpu.VMEM((tc, d), F32),
                pltpu.VMEM((tc, d), F32),
                pltpu.VMEM((n_sub, 1, d), F32),
            ] + [pltpu.VMEM((HEAD, 2 * HEAD), F32)] * (n_sub * n_heads // 2),
            compiler_params=pltpu.CompilerParams(
                dimension_semantics=("arbitrary", "arbitrary"), vmem_limit_bytes=VMEM_LIMIT_BYTES),
            name="mixer",
        )(x, *consts, *ew)

    return [functools.partial(call, g * gb, cast_ok and g == 0) for g in range(n_groups)]


def _router_kernel(h_ref, g_ref, rwh_ref, rwl_ref, rb_ref, tri_ref, meta_ref, metat_ref, cnt_ref, run_s, *,
                   n_experts):
    tt = h_ref.shape[0]

    @pl.when(pl.program_id(0) == 0)
    def _():
        run_s[...] = jnp.zeros_like(run_s)

    h = h_ref[...]
    ms = jnp.mean(h * h, axis=-1, keepdims=True)
    hn = h * lax.rsqrt(ms + NORM_EPS) * g_ref[...]
    hn_hi = hn.astype(BF16)
    hn_lo = (hn - hn_hi.astype(F32)).astype(BF16)
    logits = (jnp.dot(hn_hi, rwh_ref[...], preferred_element_type=F32)
              + jnp.dot(hn_lo, rwh_ref[...], preferred_element_type=F32)
              + jnp.dot(hn_hi, rwl_ref[...], preferred_element_type=F32)) + rb_ref[...]
    lane = lax.broadcasted_iota(jnp.int32, (tt, LANES), 1)
    neg = jnp.float32(-jnp.inf)
    work = jnp.where(lane < n_experts, logits, neg)
    vals, idxs = [], []
    for _ in range(TOP_K):
        m = jnp.max(work, axis=-1, keepdims=True)
        i = jnp.min(jnp.where(work == m, lane, LANES), axis=-1, keepdims=True)
        vals.append(m)
        idxs.append(i)
        work = jnp.where(lane == i, neg, work)
    ex = [jnp.exp(vv - vals[0]) for vv in vals]
    den = ex[0] + ex[1] + ex[2] + ex[3]
    gates = [e / den for e in ex]

    onehot = jnp.zeros((tt, LANES), jnp.bool_)
    for kk in range(TOP_K):
        onehot = onehot | (lane == (idxs[kk] + kk * n_experts))
    oh = jnp.where(onehot, 1.0, 0.0)
    cnt = jnp.dot(tri_ref[...], oh.astype(BF16), preferred_element_type=F32)
    tot = jnp.broadcast_to(jnp.sum(oh, axis=0, keepdims=True), (SUBLANES, LANES))
    lane8 = lax.broadcasted_iota(jnp.int32, (SUBLANES, LANES), 1)
    pk = jnp.zeros_like(tot)
    te = tot
    for j in range(1, TOP_K):
        rolled = pltpu.roll(tot, j * n_experts, axis=1)
        pk = pk + jnp.where(lane8 >= j * n_experts, rolled, 0.0)
        te = te + rolled
    before = cnt + (run_s[...] + pk)[0:1, :]
    ranks = [jnp.sum(jnp.where(onehot & (lane // n_experts == kk), before, 0.0), axis=-1, keepdims=True)
             for kk in range(TOP_K)]
    run_s[...] = run_s[...] + te
    cnt_ref[...] = run_s[...]

    meta = jnp.zeros((tt, LANES), F32)
    for kk in range(TOP_K):
        meta = jnp.where(lane == kk, idxs[kk].astype(F32), meta)
        meta = jnp.where(lane == TOP_K + kk, gates[kk], meta)
        meta = jnp.where(lane == 2 * TOP_K + kk, ranks[kk], meta)
    meta_ref[...] = meta
    metat_ref[...] = meta.T[0:2 * SUBLANES, :]


def _router(h2, norm_g, router_w, router_b):
    t, d = h2.shape
    n_experts = router_w.shape[1]
    tt = min(ROUTE_TOKENS, t)
    rw = jnp.pad(router_w, ((0, 0), (0, LANES - n_experts)))
    rw_hi = rw.astype(BF16)
    rw_lo = (rw - rw_hi.astype(F32)).astype(BF16)
    rb = jnp.pad(router_b, (0, LANES - n_experts)).reshape(1, LANES)
    tok = jnp.arange(tt, dtype=jnp.int32)
    tri = (tok[:, None] > tok[None, :]).astype(BF16)
    kern = functools.partial(_router_kernel, n_experts=n_experts)
    return pl.pallas_call(
        kern,
        grid=(t // tt,),
        in_specs=[pl.BlockSpec((tt, d), lambda i: (i, 0)),
                  pl.BlockSpec((1, d), lambda i: (0, 0)),
                  pl.BlockSpec((d, LANES), lambda i: (0, 0)),
                  pl.BlockSpec((d, LANES), lambda i: (0, 0)),
                  pl.BlockSpec((1, LANES), lambda i: (0, 0)),
                  pl.BlockSpec((tt, tt), lambda i: (0, 0))],
        out_specs=[pl.BlockSpec((tt, LANES), lambda i: (i, 0)),
                   pl.BlockSpec((2 * SUBLANES, tt), lambda i: (0, i)),
                   pl.BlockSpec((SUBLANES, LANES), lambda i: (0, 0))],
        out_shape=[jax.ShapeDtypeStruct((t, LANES), F32),
                   jax.ShapeDtypeStruct((2 * SUBLANES, t), F32),
                   jax.ShapeDtypeStruct((SUBLANES, LANES), F32)],
        scratch_shapes=[pltpu.VMEM((SUBLANES, LANES), F32)],
        compiler_params=pltpu.CompilerParams(
            dimension_semantics=("arbitrary",), vmem_limit_bytes=VMEM_LIMIT_BYTES),
        name="router",
    )(h2, norm_g.reshape(1, d), rw_hi, rw_lo, rb, tri)


def _gather_scratch(per_worker, d, dtype):
    w = GATHER_WINDOW
    return [pltpu.VMEM((per_worker,), jnp.int32),
            pltpu.VMEM((w, d), dtype), pltpu.VMEM((w, d), dtype),
            pltpu.SemaphoreType.DMA, pltpu.SemaphoreType.DMA, pltpu.SemaphoreType.DMA, pltpu.SemaphoreType.DMA]


def _gather_rows(table_hbm, out_hbm, base, idx_v, buf_a, buf_b, gsem_a, gsem_b, psem_a, psem_b):
    w = GATHER_WINDOW
    n_win = idx_v.shape[0] // w

    def gather(j, buf, sem):
        return pltpu.make_async_copy(table_hbm.at[idx_v.at[pl.ds(j * w, w)]], buf, sem)

    def put(j, buf, sem):
        return pltpu.make_async_copy(buf, out_hbm.at[pl.ds(base + j * w, w)], sem)

    gather(0, buf_a, gsem_a).start()

    @pl.loop(0, n_win, step=2)
    def _(j):
        gather(j, buf_a, gsem_a).wait()

        @pl.when(j > 0)
        def _():
            put(j - 1, buf_b, psem_b).wait()

        gather(j + 1, buf_b, gsem_b).start()
        put(j, buf_a, psem_a).start()
        gather(j + 1, buf_b, gsem_b).wait()
        put(j, buf_a, psem_a).wait()

        @pl.when(j + 2 < n_win)
        def _():
            gather(j + 2, buf_a, gsem_a).start()

        put(j + 1, buf_b, psem_b).start()

    put(n_win - 1, buf_b, psem_b).wait()


def _worker_split(n_rows):
    sc = plsc.get_sparse_core_info()
    n_workers = sc.num_cores * sc.num_subcores
    per_worker = n_rows // n_workers
    n_win = per_worker // GATHER_WINDOW
    assert per_worker * n_workers == n_rows and n_win * GATHER_WINDOW == per_worker and n_win % 2 == 0
    return sc, per_worker


def _row_gather(table, idx):
    n_idx, d = idx.shape[0], table.shape[1]
    sc, per_worker = _worker_split(n_idx)
    mesh = plsc.VectorSubcoreMesh(core_axis_name="core", subcore_axis_name="subcore")

    @functools.partial(pl.kernel, out_type=jax.ShapeDtypeStruct((n_idx, d), table.dtype), mesh=mesh,
                       scratch_types=_gather_scratch(per_worker, d, table.dtype))
    def gather_kernel(table_hbm, idx_hbm, out_hbm, idx_v, *bufs):
        base = (lax.axis_index("subcore") * sc.num_cores + lax.axis_index("core")) * per_worker
        pltpu.sync_copy(idx_hbm.at[pl.ds(base, per_worker)], idx_v)
        _gather_rows(table_hbm, out_hbm, base, idx_v, *bufs)

    return gather_kernel(table, idx)


def _dispatch_gather(table, pos, n_slots):
    n, d = pos.shape[0], table.shape[1]
    sc, per_worker = _worker_split(n_slots)
    n_sub = sc.num_subcores
    rows = n // LANES
    rows_per_tile = rows // n_sub
    init_per_tile = n_slots // n_sub
    assert rows_per_tile * n_sub * LANES == n and init_per_tile * n_sub == n_slots
    mesh = plsc.VectorSubcoreMesh(core_axis_name="core", subcore_axis_name="subcore")
    tokens = (jnp.arange(n, dtype=jnp.int32) % (n // TOP_K)).reshape(rows, LANES)
    fill = jnp.arange(n_slots, dtype=jnp.int32) % (n // TOP_K)

    @functools.partial(
        pl.kernel, out_type=jax.ShapeDtypeStruct((n_slots, d), table.dtype), mesh=mesh,
        scratch_types=[pltpu.VMEM((rows_per_tile, LANES), jnp.int32),
                       pltpu.VMEM((rows_per_tile, LANES), jnp.int32),
                       pltpu.VMEM_SHARED((n_slots,), jnp.int32)] + _gather_scratch(per_worker, d, table.dtype))
    def dispatch_kernel(table_hbm, pos_hbm, tok_hbm, fill_hbm, out_hbm, pos_v, tok_v, slot_token, idx_v, *bufs):
        cid = lax.axis_index("core")
        sid = lax.axis_index("subcore")
        pltpu.sync_copy(fill_hbm.at[pl.ds(sid * init_per_tile, init_per_tile)],
                        slot_token.at[pl.ds(sid * init_per_tile, init_per_tile)])
        pltpu.sync_copy(pos_hbm.at[pl.ds(sid * rows_per_tile, rows_per_tile)], pos_v)
        pltpu.sync_copy(tok_hbm.at[pl.ds(sid * rows_per_tile, rows_per_tile)], tok_v)
        plsc.subcore_barrier()

        @pl.loop(0, rows_per_tile)
        def _(j):
            pltpu.sync_copy(tok_v.at[j], slot_token.at[pos_v.at[j]])

        plsc.subcore_barrier()
        base = (sid * sc.num_cores + cid) * per_worker
        pltpu.sync_copy(slot_token.at[pl.ds(base, per_worker)], idx_v)
        _gather_rows(table_hbm, out_hbm, base, idx_v, *bufs)

    return dispatch_kernel(table, pos.reshape(rows, LANES), tokens, fill)


def _experts_kernel(te_ref, nv_ref, xs_ref, w1_ref, b1_ref, w2_ref, b2_ref, ys_ref, *cast_refs, d_ff):
    i = pl.program_id(0)
    nvalid = nv_ref[i]
    if cast_refs:
        w1_use, w2_use = cast_refs
        prev = te_ref[jnp.maximum(i - 1, 0)]

        @pl.when((i == 0) | (te_ref[i] != prev))
        def _():
            w1_use[0] = w1_ref[0].astype(BF16)
            w2_use[0] = w2_ref[0].astype(BF16)
    else:
        w1_use, w2_use = w1_ref, w2_ref

    tm = xs_ref.shape[0]
    all_groups = [slice(r, r + EXPERT_ROWS) for r in range(0, tm, EXPERT_ROWS)]

    def ffn(groups):
        half = xs_ref.shape[1]
        xg = []
        for rows in groups:
            row = rows.start + lax.broadcasted_iota(jnp.int32, (EXPERT_ROWS, 1), 0)
            lo, hi = _unpack_bf16_pairs(jnp.where(row < nvalid, xs_ref[rows, :], 0))
            xg.append((lo.astype(BF16), hi.astype(BF16)))
        ug = [jnp.dot(lo, w1_use[0, 0:half, :], preferred_element_type=F32)
              + jnp.dot(hi, w1_use[0, half:2 * half, :], preferred_element_type=F32) + b1_ref[0] for lo, hi in xg]
        ag = []
        for u in ug:
            glu = jnp.minimum(u[:, 0:d_ff], SWIGLU_LIMIT)
            lin = jnp.clip(u[:, d_ff:2 * d_ff], -SWIGLU_LIMIT, SWIGLU_LIMIT)
            ag.append((glu * _sigmoid(SWIGLU_ALPHA * glu) * (lin + 1.0)).astype(BF16))
        for rows, act in zip(groups, ag):
            ys_ref[rows, :] = _pack_bf16_pairs(
                jnp.dot(act, w2_use[0], preferred_element_type=F32) + b2_ref[0])
        for rows in all_groups[len(groups):]:
            ys_ref[rows, :] = jnp.zeros((EXPERT_ROWS, ys_ref.shape[1]), ys_ref.dtype)

    for n_used in range(len(all_groups) + 1):
        lo_cnt, hi_cnt = (n_used - 1) * EXPERT_ROWS, n_used * EXPERT_ROWS
        pl.when((nvalid > lo_cnt) & (nvalid <= hi_cnt))(functools.partial(ffn, all_groups[:n_used]))


def _experts(xs, tile_expert, tile_valid, w1, b1, w2, b2):
    cast = w1.dtype != BF16
    n_slots = xs.shape[0]
    n_exp, d, two_ff = w1.shape
    d_ff = two_ff // 2
    tm = SLOT_TILE
    n_tiles = n_slots // tm
    kern = functools.partial(_experts_kernel, d_ff=d_ff)
    grid_spec = pltpu.PrefetchScalarGridSpec(
        num_scalar_prefetch=2,
        grid=(n_tiles,),
        in_specs=[pl.BlockSpec((tm, d // 2), lambda i, te, nv: (i, 0)),
                  pl.BlockSpec((1, d, two_ff), lambda i, te, nv: (te[i], 0, 0)),
                  pl.BlockSpec((1, 1, two_ff), lambda i, te, nv: (te[i], 0, 0)),
                  pl.BlockSpec((1, d_ff, d), lambda i, te, nv: (te[i], 0, 0)),
                  pl.BlockSpec((1, 1, d), lambda i, te, nv: (te[i], 0, 0))],
        out_specs=[pl.BlockSpec((tm, d // 2), lambda i, te, nv: (i, 0))]
        + ([pl.BlockSpec((1, d, two_ff), lambda i, te, nv: (te[i], 0, 0)),
            pl.BlockSpec((1, d_ff, d), lambda i, te, nv: (te[i], 0, 0))] if cast else []),
    )
    outs = pl.pallas_call(
        kern,
        grid_spec=grid_spec,
        out_shape=[jax.ShapeDtypeStruct((n_slots, d // 2), jnp.int32)]
        + ([jax.ShapeDtypeStruct(w1.shape, BF16), jax.ShapeDtypeStruct(w2.shape, BF16)] if cast else []),
        compiler_params=pltpu.CompilerParams(
            dimension_semantics=("arbitrary",), vmem_limit_bytes=VMEM_LIMIT_BYTES),
        name="experts",
    )(tile_expert, tile_valid, xs, w1, b1.reshape(n_exp, 1, two_ff), w2, b2.reshape(n_exp, 1, d))
    return outs if cast else (outs[0], w1, w2)


def _combine_kernel(h_ref, meta_ref, g_ref, *rest):
    y_refs, o_ref = rest[:TOP_K], rest[-1]
    d = h_ref.shape[1]
    half = d // 2
    acc_lo = h_ref[:, 0:half]
    acc_hi = h_ref[:, half:d]
    for kk in range(TOP_K):
        gate = meta_ref[:, TOP_K + kk:TOP_K + kk + 1]
        lo, hi = _unpack_bf16_pairs(y_refs[kk][...])
        acc_lo = acc_lo + gate * lo
        acc_hi = acc_hi + gate * hi
    ms = (jnp.sum(acc_lo * acc_lo, axis=-1, keepdims=True)
          + jnp.sum(acc_hi * acc_hi, axis=-1, keepdims=True)) * (1.0 / d)
    scale = lax.rsqrt(ms + NORM_EPS)
    o_ref[:, 0:half] = acc_lo * scale * g_ref[:, 0:half]
    o_ref[:, half:d] = acc_hi * scale * g_ref[:, half:d]


def _combine(h2, y4, meta, norm_g, out_prev, group, n_groups):
    t, d = h2.shape
    tt = min(MOVE_TOKENS, t)
    n_blk = t // tt
    prev_specs = [] if out_prev is None else [pl.BlockSpec(memory_space=pl.ANY)]
    prev_args = [] if out_prev is None else [out_prev]
    n_in = 3 + TOP_K
    y_specs = [pl.BlockSpec((tt, d // 2), functools.partial(lambda i, kk: (kk * n_blk + i, 0), kk=kk))
               for kk in range(TOP_K)]
    return pl.pallas_call(
        _combine_kernel,
        grid=(n_blk,),
        in_specs=[pl.BlockSpec((tt, d), lambda i: (i, 0)),
                  pl.BlockSpec((tt, LANES), lambda i: (i, 0)),
                  pl.BlockSpec((1, d), lambda i: (0, 0))] + y_specs + prev_specs,
        out_specs=pl.BlockSpec((tt, d), lambda i: (group * n_blk + i, 0)),
        out_shape=jax.ShapeDtypeStruct((n_groups * t, d), F32),
        input_output_aliases={} if out_prev is None else {n_in: 0},
        compiler_params=pltpu.CompilerParams(
            dimension_semantics=("arbitrary",), vmem_limit_bytes=VMEM_LIMIT_BYTES),
        name="combine",
    )(h2, meta, norm_g.reshape(1, d), *([y4] * TOP_K), *prev_args)


def _slots_kernel(seg_ref, metat_ref, pos_ref, te_ref, tv_ref, *, n_experts, tile):
    eidx = metat_ref[0:TOP_K, :].astype(jnp.int32)
    pos = metat_ref[2 * TOP_K:3 * TOP_K, :].astype(jnp.int32)
    for e in range(n_experts):
        pos = pos + jnp.where(eidx == e, seg_ref[e], 0)
    pos_ref[...] = pos

    @pl.when(pl.program_id(0) == 0)
    def _():
        first = lax.broadcasted_iota(jnp.int32, te_ref.shape, 1) * tile
        expert = jnp.zeros(te_ref.shape, jnp.int32)
        for e in range(n_experts):
            expert = expert + jnp.where(first >= seg_ref[n_experts + e], 1, 0)
        expert = jnp.minimum(expert, n_experts - 1)
        used_end = jnp.zeros(te_ref.shape, jnp.int32)
        for e in range(n_experts):
            used_end = used_end + jnp.where(expert == e, seg_ref[e] + seg_ref[2 * n_experts + e], 0)
        te_ref[...] = expert
        tv_ref[...] = jnp.clip(used_end - first, 0, tile)


def _slots(metat, seg_start, seg_end, cnt, n_tiles):
    t = metat.shape[1]
    n_exp = seg_start.shape[0]
    tt = min(SLOT_TOKENS, t)
    nt_pad = -(-n_tiles // LANES) * LANES
    grid_spec = pltpu.PrefetchScalarGridSpec(
        num_scalar_prefetch=1,
        grid=(t // tt,),
        in_specs=[pl.BlockSpec((2 * SUBLANES, tt), lambda i, seg: (0, i))],
        out_specs=[pl.BlockSpec((TOP_K, tt), lambda i, seg: (0, i)),
                   pl.BlockSpec((SUBLANES, nt_pad), lambda i, seg: (0, 0)),
                   pl.BlockSpec((SUBLANES, nt_pad), lambda i, seg: (0, 0))],
    )
    pos, tile_expert, tile_valid = pl.pallas_call(
        functools.partial(_slots_kernel, n_experts=n_exp, tile=SLOT_TILE),
        grid_spec=grid_spec,
        out_shape=[jax.ShapeDtypeStruct((TOP_K, t), jnp.int32),
                   jax.ShapeDtypeStruct((SUBLANES, nt_pad), jnp.int32),
                   jax.ShapeDtypeStruct((SUBLANES, nt_pad), jnp.int32)],
        compiler_params=pltpu.CompilerParams(dimension_semantics=("arbitrary",)),
        name="slots",
    )(jnp.concatenate([seg_start, seg_end, cnt]), metat)
    return pos.reshape(-1), tile_expert[0, 0:n_tiles], tile_valid[0, 0:n_tiles]


def _route(h2, hp2, norm_ffn_g, router_w, router_b):
    t, d = h2.shape
    n_exp = router_w.shape[1]
    tm = SLOT_TILE
    meta, metat, counts = _router(h2, norm_ffn_g, router_w, router_b)
    cnt = counts[0, 0:n_exp].astype(jnp.int32)
    padded = jnp.maximum((cnt + tm - 1) // tm, 1) * tm
    seg_end = jnp.cumsum(padded)
    seg_start = seg_end - padded
    n_tiles = -(-(t * TOP_K) // tm) + n_exp
    n_slots = n_tiles * tm
    pos, tile_expert, tile_valid = _slots(metat, seg_start, seg_end, cnt, n_tiles)
    xs = _dispatch_gather(hp2, pos, n_slots)
    return dict(h2=h2, meta=meta, pos=pos, xs=xs, tile_expert=tile_expert, tile_valid=tile_valid)


def kernel(x, norm_mix_g, w_in, conv_w, shift_mu, decay_w0, decay_w2, iclr_a0, iclr_a2, gate_g2, k_k, k_a,
           r_k, ln_x_g, ln_x_b, w_out, norm_ffn_g, router_w, router_b, exp_w1, exp_b1, exp_w2, exp_b2,
           norm_final_g):
    bsz, s, d = x.shape
    depth = w_in.shape[0]
    assert depth == 1, "final norm is fused into the last layer's combine kernel"
    n_groups = BATCH_GROUPS if bsz % (BATCH_GROUPS * MIX_ROWS) == 0 else 1
    mixers = _mixer(x, n_groups, norm_mix_g[0], norm_ffn_g[0], w_in[0], conv_w[0], shift_mu[0], decay_w0[0],
                    decay_w2[0], iclr_a0[0], iclr_a2[0], gate_g2[0], k_k[0], k_a[0], r_k[0], ln_x_g[0], ln_x_b[0],
                    w_out[0], (exp_w1[0], exp_w2[0]))
    w1, w2 = exp_w1[0], exp_w2[0]
    routed = []
    for run_mixer in mixers:
        h, hp, *cast = run_mixer()
        if cast:
            w1, w2 = cast
        routed.append(_route(h.reshape(-1, d), hp.reshape(-1, d // 2), norm_ffn_g[0], router_w[0], router_b[0]))
    gathered = []
    for r in routed:
        ys, w1, w2 = _experts(r["xs"], r["tile_expert"], r["tile_valid"], w1, exp_b1[0], w2, exp_b2[0])
        gathered.append(_row_gather(ys, r["pos"]))
    out = None
    for g, (r, y4) in enumerate(zip(routed, gathered)):
        out = _combine(r["h2"], y4, r["meta"], norm_final_g, out, g, n_groups)
    return out.reshape(bsz, s, d)
```

```python
import functools

import jax
import jax.numpy as jnp
from jax import lax
from jax.experimental import pallas as pl
from jax.experimental.pallas import tpu as pltpu
from jax.experimental.pallas import tpu_sc as plsc

HEAD = 64
DECAY_LORA = 64
AAA_LORA = 64
GATE_LORA = 160
TOP_K = 4
SWIGLU_ALPHA = 1.702
SWIGLU_LIMIT = 7.0
NORM_EPS = 1e-5
GN_EPS = 64e-5
DECAY_SCALE = 0.6065306597126334

LANES = 128
SUBLANES = 8
V7X_VMEM_BYTES = 64 * 1024 * 1024
VMEM_LIMIT_BYTES = V7X_VMEM_BYTES * 7 // 8

RWKV_CHUNK = 64
MIX_ROWS = 4
ROUTE_TOKENS = 512
MOVE_TOKENS = 512
SLOT_TOKENS = 8192
GATHER_WINDOW = 64
SLOT_TILE = 512
EXPERT_ROWS = 256
BATCH_GROUPS = 2

BF16 = jnp.bfloat16
F32 = jnp.float32


def _dot(a, b):
    return jnp.dot(a.astype(BF16), b.astype(BF16), preferred_element_type=F32)


def _dot_nt(a, b):
    return lax.dot_general(a.astype(BF16), b.astype(BF16), (((1,), (1,)), ((), ())),
                           preferred_element_type=F32)


def _dot_tn(a, b):
    return lax.dot_general(a.astype(BF16), b.astype(BF16), (((0,), (0,)), ((), ())),
                           preferred_element_type=F32)


def _sigmoid(x):
    return 0.5 * jnp.tanh(0.5 * x) + 0.5


def _pack_bf16_pairs(x):
    n = x.shape[1] // 2
    bits = pltpu.bitcast(x.astype(BF16).astype(F32), jnp.uint32)
    packed = (bits[:, n:] & jnp.uint32(0xFFFF0000)) | (bits[:, :n] >> 16)
    return pltpu.bitcast(packed, jnp.int32)


def _unpack_bf16_pairs(w):
    bits = pltpu.bitcast(w, jnp.uint32)
    return pltpu.bitcast(bits << 16, F32), pltpu.bitcast(bits & jnp.uint32(0xFFFF0000), F32)


def _shift_rows(x, n, carry, seg):
    rolled = pltpu.roll(x, n, axis=0)
    row = lax.broadcasted_iota(jnp.int32, (SUBLANES, x.shape[1]), 0)
    pieces = []
    for j in range(x.shape[0] // seg):
        head = rolled[j * seg:j * seg + SUBLANES]
        prev = carry[j * SUBLANES:(j + 1) * SUBLANES]
        for i in range(n):
            head = jnp.where(row == i, prev[SUBLANES - n + i:SUBLANES - n + i + 1, :], head)
        pieces += [head, rolled[j * seg + SUBLANES:(j + 1) * seg]]
    return jnp.concatenate(pieces, axis=0)


def _last_rows(x, seg):
    return jnp.concatenate([x[(j + 1) * seg - SUBLANES:(j + 1) * seg] for j in range(x.shape[0] // seg)],
                           axis=0)


N_MIXER_INPUTS = 22


def _mixer_kernel(*refs, n_heads, d_model, cast_weights):
    n_extra = 2 if cast_weights else 0
    ins, outs = refs[:N_MIXER_INPUTS + n_extra], refs[N_MIXER_INPUTS + n_extra:N_MIXER_INPUTS + 2 + 2 * n_extra]
    scratch = refs[N_MIXER_INPUTS + 2 + 2 * n_extra:]
    (x_ref, g_ref, wmain_ref, wlora_ref, wgate_ref, cw_ref, mu_rkv_ref, mu_lora_ref, w0_ref, w2_ref, a0_ref,
     a2_ref, g2_ref, kk_ref, ka_ref, rk_ref, lng_ref, lnb_ref, wout_ref, hsel_ref, hselt_ref,
     gffn_ref) = ins[:N_MIXER_INPUTS]
    o_ref, hp_ref = outs[:2]
    (cu_s, cp_s, cl_s, ar_s, bk_s, v_s, y_s, gam_s), state_s = scratch[:8], scratch[8:]
    if cast_weights:
        for src, dst in zip(ins[N_MIXER_INPUTS:], outs[2:]):
            dst[...] = src[...].astype(BF16)
    n_sub, L = x_ref.shape[0], x_ref.shape[1]
    tc = n_sub * L
    D = d_model

    @pl.when(pl.program_id(1) == 0)
    def _():
        for st_ref in state_s:
            st_ref[...] = jnp.zeros_like(st_ref)
        cu_s[...] = jnp.zeros_like(cu_s)
        cp_s[...] = jnp.zeros_like(cp_s)
        cl_s[...] = jnp.zeros_like(cl_s)

    def split(t):
        hi = t.astype(BF16)
        return hi, (t - hi.astype(F32)).astype(BF16)

    def head_sums(parts, exact=False):
        per_head = sum(jnp.dot(p, hsel_ref[...], preferred_element_type=F32) for p in parts)
        back = split(per_head) if exact else [per_head.astype(BF16)]
        return sum(jnp.dot(p, hselt_ref[...], preferred_element_type=F32) for p in back)

    x = x_ref[...].reshape(tc, D)
    ms = jnp.mean(x * x, axis=-1, keepdims=True)
    xn = (x * lax.rsqrt(ms + NORM_EPS) * g_ref[...]).astype(BF16)

    pc = jnp.dot(xn, wmain_ref[:, 0:3 * D], preferred_element_type=F32)
    u = pc[:, D:2 * D] * pc[:, 2 * D:3 * D]
    cu = cu_s[...]
    conv = (cw_ref[0:1, :] * _shift_rows(u, 2, cu, L) + cw_ref[1:2, :] * _shift_rows(u, 1, cu, L)
            + cw_ref[2:3, :] * u)
    y_conv = pc[:, 0:D] * conv
    cu_s[...] = _last_rows(u, L)

    pr = jnp.dot(xn, wmain_ref[:, 3 * D:6 * D], preferred_element_type=F32)
    cp = cp_s[...]
    cp_s[...] = _last_rows(pr, L)
    pr = pr + (_shift_rows(pr, 1, cp, L) - pr) * mu_rkv_ref[...]
    plo = jnp.dot(xn, wlora_ref[...], preferred_element_type=F32)
    cl = cl_s[...]
    cl_s[...] = _last_rows(plo, L)
    plo = plo + (_shift_rows(plo, 1, cl, L) - plo) * mu_lora_ref[...]

    r = pr[:, 0:D]
    k = pr[:, D:2 * D]
    v = pr[:, 2 * D:3 * D]
    wd = plo[:, 0:LANES]
    ad = plo[:, LANES:2 * LANES]
    gd = plo[:, 2 * LANES:4 * LANES]

    lw = -DECAY_SCALE * _sigmoid(w0_ref[...] + _dot(jnp.tanh(wd), w2_ref[...]))
    a = _sigmoid(a0_ref[...] + _dot(ad, a2_ref[...]))
    g = _dot(_sigmoid(gd), g2_ref[...])

    row = lax.broadcasted_iota(jnp.int32, (tc, tc), 0)
    col = lax.broadcasted_iota(jnp.int32, (tc, tc), 1)
    tri = jnp.where((row >= col) & ((row // L) == (col // L)), 1.0, 0.0).astype(BF16)
    cum = sum(jnp.dot(tri, p, preferred_element_type=F32) for p in split(lw))
    e_inv = jnp.exp(-cum)

    kkraw = k * kk_ref[...]
    ss = head_sums([(kkraw * kkraw).astype(BF16)])
    kkn = kkraw * jnp.minimum(lax.rsqrt(ss), 1e12)

    k2 = k * (1.0 + (a - 1.0) * ka_ref[...])
    a_t = kkn * jnp.exp(cum - lw)
    r_t = r * jnp.exp(cum)
    b_t = kkn * a * e_inv
    k_t = k2 * e_inv
    for c in range(n_sub):
        rows = slice(c * L, (c + 1) * L)
        ar_s[c, 0:L, :] = a_t[rows]
        ar_s[c, L:2 * L, :] = r_t[rows]
        bk_s[c, 0:L, :] = b_t[rows]
        bk_s[c, L:2 * L, :] = k_t[rows]
        gam_s[c] = jnp.exp(cum[(c + 1) * L - 1:(c + 1) * L, :])
    v_s[...] = v
    bonus = head_sums([(r * k2 * rk_ref[...]).astype(BF16)]) * v

    pair = 2 * HEAD
    lane = lax.broadcasted_iota(jnp.int32, (1, pair), 1)
    left = lane < HEAD
    row1 = lax.broadcasted_iota(jnp.int32, (L, pair), 0)
    col1 = lax.broadcasted_iota(jnp.int32, (L, pair), 1) % HEAD
    strict = row1 > col1
    eye = jnp.where(row1 == col1, 1.0, 0.0)
    incl2 = (lax.broadcasted_iota(jnp.int32, (L, 2 * pair), 0)
             >= lax.broadcasted_iota(jnp.int32, (L, 2 * pair), 1) % HEAD)
    level_mask = [(row1 // (2 * s) == col1 // (2 * s)) & (row1 % (2 * s) >= s) & (col1 % (2 * s) < s)
                  for s in (1 << i for i in range(L.bit_length() - 1))]

    def blockdiag(t):
        tb = t.astype(BF16)
        return jnp.concatenate([jnp.where(left, tb, 0), jnp.where(left, 0, tb)], axis=0)

    pairs = range(n_heads // 2)
    psl = [slice(p * pair, (p + 1) * pair) for p in pairs]
    work = [(c, p) for c in range(n_sub) for p in pairs]
    ar = [ar_s[c, :, psl[p]].astype(BF16) for c, p in work]
    bk = [bk_s[c, :, psl[p]].astype(BF16) for c, p in work]
    vp = [v_s[c * L:(c + 1) * L, psl[p]] for c, p in work]
    gm = [_dot_nt(ar[i], jnp.concatenate([blockdiag(bk[i][0:L]), blockdiag(bk[i][L:2 * L])], axis=0))
          for i in range(len(work))]
    nm = [g_[0:L, 0:pair] for g_ in gm]
    mak = [jnp.where(strict, g_[0:L, pair:2 * pair], 0.0) for g_ in gm]
    q = [jnp.where(incl2, g_[L:2 * L, :], 0.0).astype(BF16) for g_ in gm]
    xinv = [eye - jnp.where(level_mask[0], n_, 0.0) for n_ in nm]
    for lm in level_mask[1:]:
        half = [_dot(jnp.where(lm, n_, 0.0), blockdiag(t)) for n_, t in zip(nm, xinv)]
        xinv = [t - _dot(t, blockdiag(h_)) for t, h_ in zip(xinv, half)]
    xinv = [t.astype(BF16) for t in xinv]
    mv = [_dot(m_, blockdiag(v_)) for m_, v_ in zip(mak, vp)]

    gates = _sigmoid(jnp.dot(xn, wgate_ref[...], preferred_element_type=F32))

    items = range(len(work))
    st = [state_s[i][...] for i in items]
    ars = [_dot_nt(ar[i], blockdiag(st[i])) for i in items]
    uu = [_dot(xinv[i], blockdiag(ars[i][0:L] + mv[i])) for i in items]
    yh = [ars[i][L:2 * L] + _dot(q[i], jnp.concatenate([blockdiag(-uu[i]), blockdiag(vp[i])], axis=0))
          for i in items]
    for i, (c, p) in enumerate(work):
        upd = _dot_tn(jnp.concatenate([-uu[i], vp[i]], axis=0), bk[i])
        state_s[i][...] = (st[i] + jnp.where(left, upd[0:HEAD], upd[HEAD:pair])) * gam_s[c, :, psl[p]]
    for i, (c, p) in enumerate(work):
        y_s[c * L:(c + 1) * L, psl[p]] = yh[i]

    y = y_s[...]
    yc = y - head_sums(split(y), exact=True) * (1.0 / HEAD)
    var = head_sums([(yc * yc).astype(BF16)]) * (1.0 / HEAD)
    y_rwkv = (yc * lax.rsqrt(var + GN_EPS) * lng_ref[...] + lnb_ref[...] + bonus) * g
    mix = gates[:, 0:D] * y_conv + gates[:, D:2 * D] * y_rwkv
    h = x + jnp.dot(mix.astype(BF16), wout_ref[...], preferred_element_type=F32)
    o_ref[...] = h.reshape(n_sub, L, D)
    hn = h * lax.rsqrt(jnp.mean(h * h, axis=-1, keepdims=True) + NORM_EPS) * gffn_ref[...]
    hp_ref[...] = _pack_bf16_pairs(hn).reshape(n_sub, L, D // 2)


def _const_spec(shape):
    nd = len(shape)
    return pl.BlockSpec(shape, lambda *_: (0,) * nd, pipeline_mode=pl.Buffered(1))


def _mixer(x, n_groups, norm_g, norm_ffn_g, w_in, conv_w, shift_mu, w0, w2, a0, a2, g2, k_k, k_a, r_k, ln_g, ln_b, w_out,
           expert_weights):
    bsz, s, d = x.shape
    gb = bsz // n_groups
    n_heads = d // HEAD
    L = RWKV_CHUNK
    n_sub = MIX_ROWS
    tc = n_sub * L
    assert gb % n_sub == 0 and s % L == 0
    lora0 = 6 * d
    w_main = w_in[:, 0:6 * d].astype(BF16)
    pad = lambda t, n: jnp.pad(t, ((0, 0), (0, n - t.shape[1])))
    lora_cols = (DECAY_LORA, AAA_LORA, GATE_LORA)
    lora_pads = (LANES, LANES, 2 * LANES)
    pieces_w, pieces_mu, off = [], [], lora0
    for n, p in zip(lora_cols, lora_pads):
        pieces_w.append(pad(w_in[:, off:off + n], p))
        pieces_mu.append(pad(shift_mu[None, off - 3 * d:off - 3 * d + n], p))
        off += n
    w_lora = jnp.concatenate(pieces_w, axis=1).astype(BF16)
    mu_lora = jnp.concatenate(pieces_mu, axis=1)
    w_gate = w_in[:, off:off + 2 * d].astype(BF16)
    mu_rkv = shift_mu[None, 0:3 * d]
    padr = lambda t, n: jnp.pad(t, ((0, n - t.shape[0]), (0, 0)))
    w2p = padr(w2, LANES).astype(BF16)
    a2p = padr(a2, LANES).astype(BF16)
    g2p = padr(g2, 2 * LANES).astype(BF16)
    row = lambda t: t.reshape(1, -1)
    head_of = jnp.arange(d, dtype=jnp.int32) // HEAD
    head_sel = (head_of[:, None] == jnp.arange(LANES)[None, :]).astype(BF16)
    consts = [row(norm_g), w_main, w_lora, w_gate, conv_w, mu_rkv, mu_lora, row(w0), w2p, row(a0), a2p,
              g2p, row(k_k), row(k_a), row(r_k), row(ln_g), row(ln_b), w_out.astype(BF16), head_sel, head_sel.T,
              row(norm_ffn_g)]
    assert len(consts) + 1 == N_MIXER_INPUTS
    n_c = s // L
    n_steps = (gb // n_sub) * n_c
    n_exp = expert_weights[0].shape[0]
    pieces = n_steps // n_exp
    cast_ok = (pieces * n_exp == n_steps
               and all(w.shape[1] % pieces == 0 and (w.shape[1] // pieces) % (2 * SUBLANES) == 0
                       for w in expert_weights))

    def call(first, cast):
        ew = expert_weights if cast else ()
        ew_specs = [pl.BlockSpec((1, w.shape[1] // pieces, w.shape[2]),
                                 lambda b, c: ((b * n_c + c) // pieces, (b * n_c + c) % pieces, 0)) for w in ew]
        return pl.pallas_call(
            functools.partial(_mixer_kernel, n_heads=n_heads, d_model=d, cast_weights=cast),
            grid=(gb // n_sub, n_c),
            in_specs=[pl.BlockSpec((n_sub, L, d), lambda b, c: (b + first // n_sub, c, 0))]
            + [_const_spec(t.shape) for t in consts] + ew_specs,
            out_specs=[pl.BlockSpec((n_sub, L, d), lambda b, c: (b, c, 0)),
                       pl.BlockSpec((n_sub, L, d // 2), lambda b, c: (b, c, 0))] + ew_specs,
            out_shape=[jax.ShapeDtypeStruct((gb, s, d), F32), jax.ShapeDtypeStruct((gb, s, d // 2), jnp.int32)]
            + [jax.ShapeDtypeStruct(w.shape, BF16) for w in ew],
            scratch_shapes=[
                pltpu.VMEM((n_sub * SUBLANES, d), F32),
                pltpu.VMEM((n_sub * SUBLANES, 3 * d), F32),
                pltpu.VMEM((n_sub * SUBLANES, 4 * LANES), F32),
                pltpu.VMEM((n_sub, 2 * L, d), F32),
                pltpu.VMEM((n_sub, 2 * L, d), F32),
                pltpu.VMEM((tc, d), F32),
                pltpu.VMEM((tc, d), F32),
                pltpu.VMEM((n_sub, 1, d), F32),
            ] + [pltpu.VMEM((HEAD, 2 * HEAD), F32)] * (n_sub * n_heads // 2),
            compiler_params=pltpu.CompilerParams(
                dimension_semantics=("arbitrary", "arbitrary"), vmem_limit_bytes=VMEM_LIMIT_BYTES),
            name="mixer",
        )(x, *consts, *ew)

    return [functools.partial(call, g * gb, cast_ok and g == 0) for g in range(n_groups)]


def _router_kernel(h_ref, g_ref, rwh_ref, rwl_ref, rb_ref, tri_ref, meta_ref, metat_ref, cnt_ref, run_s, *,
                   n_experts):
    tt = h_ref.shape[0]

    @pl.when(pl.program_id(0) == 0)
    def _():
        run_s[...] = jnp.zeros_like(run_s)

    h = h_ref[...]
    ms = jnp.mean(h * h, axis=-1, keepdims=True)
    hn = h * lax.rsqrt(ms + NORM_EPS) * g_ref[...]
    hn_hi = hn.astype(BF16)
    hn_lo = (hn - hn_hi.astype(F32)).astype(BF16)
    logits = (jnp.dot(hn_hi, rwh_ref[...], preferred_element_type=F32)
              + jnp.dot(hn_lo, rwh_ref[...], preferred_element_type=F32)
              + jnp.dot(hn_hi, rwl_ref[...], preferred_element_type=F32)) + rb_ref[...]
    lane = lax.broadcasted_iota(jnp.int32, (tt, LANES), 1)
    neg = jnp.float32(-jnp.inf)
    work = jnp.where(lane < n_experts, logits, neg)
    vals, idxs = [], []
    for _ in range(TOP_K):
        m = jnp.max(work, axis=-1, keepdims=True)
        i = jnp.min(jnp.where(work == m, lane, LANES), axis=-1, keepdims=True)
        vals.append(m)
        idxs.append(i)
        work = jnp.where(lane == i, neg, work)
    ex = [jnp.exp(vv - vals[0]) for vv in vals]
    den = ex[0] + ex[1] + ex[2] + ex[3]
    gates = [e / den for e in ex]

    onehot = jnp.zeros((tt, LANES), jnp.bool_)
    for kk in range(TOP_K):
        onehot = onehot | (lane == (idxs[kk] + kk * n_experts))
    oh = jnp.where(onehot, 1.0, 0.0)
    cnt = jnp.dot(tri_ref[...], oh.astype(BF16), preferred_element_type=F32)
    tot = jnp.broadcast_to(jnp.sum(oh, axis=0, keepdims=True), (SUBLANES, LANES))
    lane8 = lax.broadcasted_iota(jnp.int32, (SUBLANES, LANES), 1)
    pk = jnp.zeros_like(tot)
    te = tot
    for j in range(1, TOP_K):
        rolled = pltpu.roll(tot, j * n_experts, axis=1)
        pk = pk + jnp.where(lane8 >= j * n_experts, rolled, 0.0)
        te = te + rolled
    before = cnt + (run_s[...] + pk)[0:1, :]
    ranks = [jnp.sum(jnp.where(onehot & (lane // n_experts == kk), before, 0.0), axis=-1, keepdims=True)
             for kk in range(TOP_K)]
    run_s[...] = run_s[...] + te
    cnt_ref[...] = run_s[...]

    meta = jnp.zeros((tt, LANES), F32)
    for kk in range(TOP_K):
        meta = jnp.where(lane == kk, idxs[kk].astype(F32), meta)
        meta = jnp.where(lane == TOP_K + kk, gates[kk], meta)
        meta = jnp.where(lane == 2 * TOP_K + kk, ranks[kk], meta)
    meta_ref[...] = meta
    metat_ref[...] = meta.T[0:2 * SUBLANES, :]


def _router(h2, norm_g, router_w, router_b):
    t, d = h2.shape
    n_experts = router_w.shape[1]
    tt = min(ROUTE_TOKENS, t)
    rw = jnp.pad(router_w, ((0, 0), (0, LANES - n_experts)))
    rw_hi = rw.astype(BF16)
    rw_lo = (rw - rw_hi.astype(F32)).astype(BF16)
    rb = jnp.pad(router_b, (0, LANES - n_experts)).reshape(1, LANES)
    tok = jnp.arange(tt, dtype=jnp.int32)
    tri = (tok[:, None] > tok[None, :]).astype(BF16)
    kern = functools.partial(_router_kernel, n_experts=n_experts)
    return pl.pallas_call(
        kern,
        grid=(t // tt,),
        in_specs=[pl.BlockSpec((tt, d), lambda i: (i, 0)),
                  pl.BlockSpec((1, d), lambda i: (0, 0)),
                  pl.BlockSpec((d, LANES), lambda i: (0, 0)),
                  pl.BlockSpec((d, LANES), lambda i: (0, 0)),
                  pl.BlockSpec((1, LANES), lambda i: (0, 0)),
                  pl.BlockSpec((tt, tt), lambda i: (0, 0))],
        out_specs=[pl.BlockSpec((tt, LANES), lambda i: (i, 0)),
                   pl.BlockSpec((2 * SUBLANES, tt), lambda i: (0, i)),
                   pl.BlockSpec((SUBLANES, LANES), lambda i: (0, 0))],
        out_shape=[jax.ShapeDtypeStruct((t, LANES), F32),
                   jax.ShapeDtypeStruct((2 * SUBLANES, t), F32),
                   jax.ShapeDtypeStruct((SUBLANES, LANES), F32)],
        scratch_shapes=[pltpu.VMEM((SUBLANES, LANES), F32)],
        compiler_params=pltpu.CompilerParams(
            dimension_semantics=("arbitrary",), vmem_limit_bytes=VMEM_LIMIT_BYTES),
        name="router",
    )(h2, norm_g.reshape(1, d), rw_hi, rw_lo, rb, tri)


def _gather_scratch(per_worker, d, dtype):
    w = GATHER_WINDOW
    return [pltpu.VMEM((per_worker,), jnp.int32),
            pltpu.VMEM((w, d), dtype), pltpu.VMEM((w, d), dtype),
            pltpu.SemaphoreType.DMA, pltpu.SemaphoreType.DMA, pltpu.SemaphoreType.DMA, pltpu.SemaphoreType.DMA]


def _gather_rows(table_hbm, out_hbm, base, idx_v, buf_a, buf_b, gsem_a, gsem_b, psem_a, psem_b):
    w = GATHER_WINDOW
    n_win = idx_v.shape[0] // w

    def gather(j, buf, sem):
        return pltpu.make_async_copy(table_hbm.at[idx_v.at[pl.ds(j * w, w)]], buf, sem)

    def put(j, buf, sem):
        return pltpu.make_async_copy(buf, out_hbm.at[pl.ds(base + j * w, w)], sem)

    gather(0, buf_a, gsem_a).start()

    @pl.loop(0, n_win, step=2)
    def _(j):
        gather(j, buf_a, gsem_a).wait()

        @pl.when(j > 0)
        def _():
            put(j - 1, buf_b, psem_b).wait()

        gather(j + 1, buf_b, gsem_b).start()
        put(j, buf_a, psem_a).start()
        gather(j + 1, buf_b, gsem_b).wait()
        put(j, buf_a, psem_a).wait()

        @pl.when(j + 2 < n_win)
        def _():
            gather(j + 2, buf_a, gsem_a).start()

        put(j + 1, buf_b, psem_b).start()

    put(n_win - 1, buf_b, psem_b).wait()


def _worker_split(n_rows):
    sc = plsc.get_sparse_core_info()
    n_workers = sc.num_cores * sc.num_subcores
    per_worker = n_rows // n_workers
    n_win = per_worker // GATHER_WINDOW
    assert per_worker * n_workers == n_rows and n_win * GATHER_WINDOW == per_worker and n_win % 2 == 0
    return sc, per_worker


def _row_gather(table, idx):
    n_idx, d = idx.shape[0], table.shape[1]
    sc, per_worker = _worker_split(n_idx)
    mesh = plsc.VectorSubcoreMesh(core_axis_name="core", subcore_axis_name="subcore")

    @functools.partial(pl.kernel, out_type=jax.ShapeDtypeStruct((n_idx, d), table.dtype), mesh=mesh,
                       scratch_types=_gather_scratch(per_worker, d, table.dtype))
    def gather_kernel(table_hbm, idx_hbm, out_hbm, idx_v, *bufs):
        base = (lax.axis_index("subcore") * sc.num_cores + lax.axis_index("core")) * per_worker
        pltpu.sync_copy(idx_hbm.at[pl.ds(base, per_worker)], idx_v)
        _gather_rows(table_hbm, out_hbm, base, idx_v, *bufs)

    return gather_kernel(table, idx)


def _dispatch_gather(table, pos, n_slots):
    n, d = pos.shape[0], table.shape[1]
    sc, per_worker = _worker_split(n_slots)
    n_sub = sc.num_subcores
    rows = n // LANES
    rows_per_tile = rows // n_sub
    init_per_tile = n_slots // n_sub
    assert rows_per_tile * n_sub * LANES == n and init_per_tile * n_sub == n_slots
    mesh = plsc.VectorSubcoreMesh(core_axis_name="core", subcore_axis_name="subcore")
    tokens = (jnp.arange(n, dtype=jnp.int32) % (n // TOP_K)).reshape(rows, LANES)
    fill = jnp.arange(n_slots, dtype=jnp.int32) % (n // TOP_K)

    @functools.partial(
        pl.kernel, out_type=jax.ShapeDtypeStruct((n_slots, d), table.dtype), mesh=mesh,
        scratch_types=[pltpu.VMEM((rows_per_tile, LANES), jnp.int32),
                       pltpu.VMEM((rows_per_tile, LANES), jnp.int32),
                       pltpu.VMEM_SHARED((n_slots,), jnp.int32)] + _gather_scratch(per_worker, d, table.dtype))
    def dispatch_kernel(table_hbm, pos_hbm, tok_hbm, fill_hbm, out_hbm, pos_v, tok_v, slot_token, idx_v, *bufs):
        cid = lax.axis_index("core")
        sid = lax.axis_index("subcore")
        pltpu.sync_copy(fill_hbm.at[pl.ds(sid * init_per_tile, init_per_tile)],
                        slot_token.at[pl.ds(sid * init_per_tile, init_per_tile)])
        pltpu.sync_copy(pos_hbm.at[pl.ds(sid * rows_per_tile, rows_per_tile)], pos_v)
        pltpu.sync_copy(tok_hbm.at[pl.ds(sid * rows_per_tile, rows_per_tile)], tok_v)
        plsc.subcore_barrier()

        @pl.loop(0, rows_per_tile)
        def _(j):
            pltpu.sync_copy(tok_v.at[j], slot_token.at[pos_v.at[j]])

        plsc.subcore_barrier()
        base = (sid * sc.num_cores + cid) * per_worker
        pltpu.sync_copy(slot_token.at[pl.ds(base, per_worker)], idx_v)
        _gather_rows(table_hbm, out_hbm, base, idx_v, *bufs)

    return dispatch_kernel(table, pos.reshape(rows, LANES), tokens, fill)


def _experts_kernel(te_ref, nv_ref, xs_ref, w1_ref, b1_ref, w2_ref, b2_ref, ys_ref, *cast_refs, d_ff):
    i = pl.program_id(0)
    nvalid = nv_ref[i]
    if cast_refs:
        w1_use, w2_use = cast_refs
        prev = te_ref[jnp.maximum(i - 1, 0)]

        @pl.when((i == 0) | (te_ref[i] != prev))
        def _():
            w1_use[0] = w1_ref[0].astype(BF16)
            w2_use[0] = w2_ref[0].astype(BF16)
    else:
        w1_use, w2_use = w1_ref, w2_ref

    tm = xs_ref.shape[0]
    all_groups = [slice(r, r + EXPERT_ROWS) for r in range(0, tm, EXPERT_ROWS)]

    def ffn(groups):
        half = xs_ref.shape[1]
        xg = []
        for rows in groups:
            row = rows.start + lax.broadcasted_iota(jnp.int32, (EXPERT_ROWS, 1), 0)
            lo, hi = _unpack_bf16_pairs(jnp.where(row < nvalid, xs_ref[rows, :], 0))
            xg.append((lo.astype(BF16), hi.astype(BF16)))
        ug = [jnp.dot(lo, w1_use[0, 0:half, :], preferred_element_type=F32)
              + jnp.dot(hi, w1_use[0, half:2 * half, :], preferred_element_type=F32) + b1_ref[0] for lo, hi in xg]
        ag = []
        for u in ug:
            glu = jnp.minimum(u[:, 0:d_ff], SWIGLU_LIMIT)
            lin = jnp.clip(u[:, d_ff:2 * d_ff], -SWIGLU_LIMIT, SWIGLU_LIMIT)
            ag.append((glu * _sigmoid(SWIGLU_ALPHA * glu) * (lin + 1.0)).astype(BF16))
        for rows, act in zip(groups, ag):
            ys_ref[rows, :] = _pack_bf16_pairs(
                jnp.dot(act, w2_use[0], preferred_element_type=F32) + b2_ref[0])
        for rows in all_groups[len(groups):]:
            ys_ref[rows, :] = jnp.zeros((EXPERT_ROWS, ys_ref.shape[1]), ys_ref.dtype)

    for n_used in range(len(all_groups) + 1):
        lo_cnt, hi_cnt = (n_used - 1) * EXPERT_ROWS, n_used * EXPERT_ROWS
        pl.when((nvalid > lo_cnt) & (nvalid <= hi_cnt))(functools.partial(ffn, all_groups[:n_used]))


def _experts(xs, tile_expert, tile_valid, w1, b1, w2, b2):
    cast = w1.dtype != BF16
    n_slots = xs.shape[0]
    n_exp, d, two_ff = w1.shape
    d_ff = two_ff // 2
    tm = SLOT_TILE
    n_tiles = n_slots // tm
    kern = functools.partial(_experts_kernel, d_ff=d_ff)
    grid_spec = pltpu.PrefetchScalarGridSpec(
        num_scalar_prefetch=2,
        grid=(n_tiles,),
        in_specs=[pl.BlockSpec((tm, d // 2), lambda i, te, nv: (i, 0)),
                  pl.BlockSpec((1, d, two_ff), lambda i, te, nv: (te[i], 0, 0)),
                  pl.BlockSpec((1, 1, two_ff), lambda i, te, nv: (te[i], 0, 0)),
                  pl.BlockSpec((1, d_ff, d), lambda i, te, nv: (te[i], 0, 0)),
                  pl.BlockSpec((1, 1, d), lambda i, te, nv: (te[i], 0, 0))],
        out_specs=[pl.BlockSpec((tm, d // 2), lambda i, te, nv: (i, 0))]
        + ([pl.BlockSpec((1, d, two_ff), lambda i, te, nv: (te[i], 0, 0)),
            pl.BlockSpec((1, d_ff, d), lambda i, te, nv: (te[i], 0, 0))] if cast else []),
    )
    outs = pl.pallas_call(
        kern,
        grid_spec=grid_spec,
        out_shape=[jax.ShapeDtypeStruct((n_slots, d // 2), jnp.int32)]
        + ([jax.ShapeDtypeStruct(w1.shape, BF16), jax.ShapeDtypeStruct(w2.shape, BF16)] if cast else []),
        compiler_params=pltpu.CompilerParams(
            dimension_semantics=("arbitrary",), vmem_limit_bytes=VMEM_LIMIT_BYTES),
        name="experts",
    )(tile_expert, tile_valid, xs, w1, b1.reshape(n_exp, 1, two_ff), w2, b2.reshape(n_exp, 1, d))
    return outs if cast else (outs[0], w1, w2)


def _combine_kernel(h_ref, meta_ref, g_ref, *rest):
    y_refs, o_ref = rest[:TOP_K], rest[-1]
    d = h_ref.shape[1]
    half = d // 2
    acc_lo = h_ref[:, 0:half]
    acc_hi = h_ref[:, half:d]
    for kk in range(TOP_K):
        gate = meta_ref[:, TOP_K + kk:TOP_K + kk + 1]
        lo, hi = _unpack_bf16_pairs(y_refs[kk][...])
        acc_lo = acc_lo + gate * lo
        acc_hi = acc_hi + gate * hi
    ms = (jnp.sum(acc_lo * acc_lo, axis=-1, keepdims=True)
          + jnp.sum(acc_hi * acc_hi, axis=-1, keepdims=True)) * (1.0 / d)
    scale = lax.rsqrt(ms + NORM_EPS)
    o_ref[:, 0:half] = acc_lo * scale * g_ref[:, 0:half]
    o_ref[:, half:d] = acc_hi * scale * g_ref[:, half:d]


def _combine(h2, y4, meta, norm_g, out_prev, group, n_groups):
    t, d = h2.shape
    tt = min(MOVE_TOKENS, t)
    n_blk = t // tt
    prev_specs = [] if out_prev is None else [pl.BlockSpec(memory_space=pl.ANY)]
    prev_args = [] if out_prev is None else [out_prev]
    n_in = 3 + TOP_K
    y_specs = [pl.BlockSpec((tt, d // 2), functools.partial(lambda i, kk: (kk * n_blk + i, 0), kk=kk))
               for kk in range(TOP_K)]
    return pl.pallas_call(
        _combine_kernel,
        grid=(n_blk,),
        in_specs=[pl.BlockSpec((tt, d), lambda i: (i, 0)),
                  pl.BlockSpec((tt, LANES), lambda i: (i, 0)),
                  pl.BlockSpec((1, d), lambda i: (0, 0))] + y_specs + prev_specs,
        out_specs=pl.BlockSpec((tt, d), lambda i: (group * n_blk + i, 0)),
        out_shape=jax.ShapeDtypeStruct((n_groups * t, d), F32),
        input_output_aliases={} if out_prev is None else {n_in: 0},
        compiler_params=pltpu.CompilerParams(
            dimension_semantics=("arbitrary",), vmem_limit_bytes=VMEM_LIMIT_BYTES),
        name="combine",
    )(h2, meta, norm_g.reshape(1, d), *([y4] * TOP_K), *prev_args)


def _slots_kernel(seg_ref, metat_ref, pos_ref, te_ref, tv_ref, *, n_experts, tile):
    eidx = metat_ref[0:TOP_K, :].astype(jnp.int32)
    pos = metat_ref[2 * TOP_K:3 * TOP_K, :].astype(jnp.int32)
    for e in range(n_experts):
        pos = pos + jnp.where(eidx == e, seg_ref[e], 0)
    pos_ref[...] = pos

    @pl.when(pl.program_id(0) == 0)
    def _():
        first = lax.broadcasted_iota(jnp.int32, te_ref.shape, 1) * tile
        expert = jnp.zeros(te_ref.shape, jnp.int32)
        for e in range(n_experts):
            expert = expert + jnp.where(first >= seg_ref[n_experts + e], 1, 0)
        expert = jnp.minimum(expert, n_experts - 1)
        used_end = jnp.zeros(te_ref.shape, jnp.int32)
        for e in range(n_experts):
            used_end = used_end + jnp.where(expert == e, seg_ref[e] + seg_ref[2 * n_experts + e], 0)
        te_ref[...] = expert
        tv_ref[...] = jnp.clip(used_end - first, 0, tile)


def _slots(metat, seg_start, seg_end, cnt, n_tiles):
    t = metat.shape[1]
    n_exp = seg_start.shape[0]
    tt = min(SLOT_TOKENS, t)
    nt_pad = -(-n_tiles // LANES) * LANES
    grid_spec = pltpu.PrefetchScalarGridSpec(
        num_scalar_prefetch=1,
        grid=(t // tt,),
        in_specs=[pl.BlockSpec((2 * SUBLANES, tt), lambda i, seg: (0, i))],
        out_specs=[pl.BlockSpec((TOP_K, tt), lambda i, seg: (0, i)),
                   pl.BlockSpec((SUBLANES, nt_pad), lambda i, seg: (0, 0)),
                   pl.BlockSpec((SUBLANES, nt_pad), lambda i, seg: (0, 0))],
    )
    pos, tile_expert, tile_valid = pl.pallas_call(
        functools.partial(_slots_kernel, n_experts=n_exp, tile=SLOT_TILE),
        grid_spec=grid_spec,
        out_shape=[jax.ShapeDtypeStruct((TOP_K, t), jnp.int32),
                   jax.ShapeDtypeStruct((SUBLANES, nt_pad), jnp.int32),
                   jax.ShapeDtypeStruct((SUBLANES, nt_pad), jnp.int32)],
        compiler_params=pltpu.CompilerParams(dimension_semantics=("arbitrary",)),
        name="slots",
    )(jnp.concatenate([seg_start, seg_end, cnt]), metat)
    return pos.reshape(-1), tile_expert[0, 0:n_tiles], tile_valid[0, 0:n_tiles]


def _route(h2, hp2, norm_ffn_g, router_w, router_b):
    t, d = h2.shape
    n_exp = router_w.shape[1]
    tm = SLOT_TILE
    meta, metat, counts = _router(h2, norm_ffn_g, router_w, router_b)
    cnt = counts[0, 0:n_exp].astype(jnp.int32)
    padded = jnp.maximum((cnt + tm - 1) // tm, 1) * tm
    seg_end = jnp.cumsum(padded)
    seg_start = seg_end - padded
    n_tiles = -(-(t * TOP_K) // tm) + n_exp
    n_slots = n_tiles * tm
    pos, tile_expert, tile_valid = _slots(metat, seg_start, seg_end, cnt, n_tiles)
    xs = _dispatch_gather(hp2, pos, n_slots)
    return dict(h2=h2, meta=meta, pos=pos, xs=xs, tile_expert=tile_expert, tile_valid=tile_valid)


def kernel(x, norm_mix_g, w_in, conv_w, shift_mu, decay_w0, decay_w2, iclr_a0, iclr_a2, gate_g2, k_k, k_a,
           r_k, ln_x_g, ln_x_b, w_out, norm_ffn_g, router_w, router_b, exp_w1, exp_b1, exp_w2, exp_b2,
           norm_final_g):
    bsz, s, d = x.shape
    depth = w_in.shape[0]
    assert depth == 1, "final norm is fused into the last layer's combine kernel"
    n_groups = BATCH_GROUPS if bsz % (BATCH_GROUPS * MIX_ROWS) == 0 else 1
    mixers = _mixer(x, n_groups, norm_mix_g[0], norm_ffn_g[0], w_in[0], conv_w[0], shift_mu[0], decay_w0[0],
                    decay_w2[0], iclr_a0[0], iclr_a2[0], gate_g2[0], k_k[0], k_a[0], r_k[0], ln_x_g[0], ln_x_b[0],
                    w_out[0], (exp_w1[0], exp_w2[0]))
    w1, w2 = exp_w1[0], exp_w2[0]
    routed = []
    for run_mixer in mixers:
        h, hp, *cast = run_mixer()
        if cast:
            w1, w2 = cast
        routed.append(_route(h.reshape(-1, d), hp.reshape(-1, d // 2), norm_ffn_g[0], router_w[0], router_b[0]))
    gathered = []
    for r in routed:
        ys, w1, w2 = _experts(r["xs"], r["tile_expert"], r["tile_valid"], w1, exp_b1[0], w2, exp_b2[0])
        gathered.append(_row_gather(ys, r["pos"]))
    out = None
    for g, (r, y4) in enumerate(zip(routed, gathered)):
        out = _combine(r["h2"], y4, r["meta"], norm_final_g, out, g, n_groups)
    return out.reshape(bsz, s, d)
```

```python
import functools

import jax
import jax.numpy as jnp
from jax import lax
from jax.experimental import pallas as pl
from jax.experimental.pallas import tpu as pltpu
from jax.experimental.pallas import tpu_sc as plsc

HEAD = 64
DECAY_LORA = 64
AAA_LORA = 64
GATE_LORA = 160
TOP_K = 4
SWIGLU_ALPHA = 1.702
SWIGLU_LIMIT = 7.0
NORM_EPS = 1e-5
GN_EPS = 64e-5
DECAY_SCALE = 0.6065306597126334

LANES = 128
SUBLANES = 8
V7X_VMEM_BYTES = 64 * 1024 * 1024
VMEM_LIMIT_BYTES = V7X_VMEM_BYTES * 7 // 8

RWKV_CHUNK = 64
MIX_ROWS = 4
ROUTE_TOKENS = 512
MOVE_TOKENS = 1024
SLOT_TOKENS = 8192
GATHER_WINDOW = 64
SLOT_TILE = 512
EXPERT_ROWS = 256
BATCH_GROUPS = 2

BF16 = jnp.bfloat16
F32 = jnp.float32


def _dot(a, b):
    return jnp.dot(a.astype(BF16), b.astype(BF16), preferred_element_type=F32)


def _dot_nt(a, b):
    return lax.dot_general(a.astype(BF16), b.astype(BF16), (((1,), (1,)), ((), ())),
                           preferred_element_type=F32)


def _dot_tn(a, b):
    return lax.dot_general(a.astype(BF16), b.astype(BF16), (((0,), (0,)), ((), ())),
                           preferred_element_type=F32)


def _sigmoid(x):
    return 0.5 * jnp.tanh(0.5 * x) + 0.5


def _pack_bf16_pairs(x):
    n = x.shape[1] // 2
    bits = pltpu.bitcast(x.astype(BF16).astype(F32), jnp.uint32)
    packed = (bits[:, n:] & jnp.uint32(0xFFFF0000)) | (bits[:, :n] >> 16)
    return pltpu.bitcast(packed, jnp.int32)


def _unpack_bf16_pairs(w):
    bits = pltpu.bitcast(w, jnp.uint32)
    return pltpu.bitcast(bits << 16, F32), pltpu.bitcast(bits & jnp.uint32(0xFFFF0000), F32)


def _shift_rows(x, n, carry, seg):
    rolled = pltpu.roll(x, n, axis=0)
    row = lax.broadcasted_iota(jnp.int32, (SUBLANES, x.shape[1]), 0)
    pieces = []
    for j in range(x.shape[0] // seg):
        head = rolled[j * seg:j * seg + SUBLANES]
        prev = carry[j * SUBLANES:(j + 1) * SUBLANES]
        for i in range(n):
            head = jnp.where(row == i, prev[SUBLANES - n + i:SUBLANES - n + i + 1, :], head)
        pieces += [head, rolled[j * seg + SUBLANES:(j + 1) * seg]]
    return jnp.concatenate(pieces, axis=0)


def _last_rows(x, seg):
    return jnp.concatenate([x[(j + 1) * seg - SUBLANES:(j + 1) * seg] for j in range(x.shape[0] // seg)],
                           axis=0)


N_MIXER_INPUTS = 22


def _mixer_kernel(*refs, n_heads, d_model, cast_weights):
    n_extra = 2 if cast_weights else 0
    ins, outs = refs[:N_MIXER_INPUTS + n_extra], refs[N_MIXER_INPUTS + n_extra:N_MIXER_INPUTS + 2 + 2 * n_extra]
    scratch = refs[N_MIXER_INPUTS + 2 + 2 * n_extra:]
    (x_ref, g_ref, wmain_ref, wlora_ref, wgate_ref, cw_ref, mu_rkv_ref, mu_lora_ref, w0_ref, w2_ref, a0_ref,
     a2_ref, g2_ref, kk_ref, ka_ref, rk_ref, lng_ref, lnb_ref, wout_ref, hsel_ref, hselt_ref,
     gffn_ref) = ins[:N_MIXER_INPUTS]
    o_ref, hp_ref = outs[:2]
    (cu_s, cp_s, cl_s, ar_s, bk_s, v_s, y_s, gam_s), state_s = scratch[:8], scratch[8:]
    if cast_weights:
        for src, dst in zip(ins[N_MIXER_INPUTS:], outs[2:]):
            dst[...] = src[...].astype(BF16)
    n_sub, L = x_ref.shape[0], x_ref.shape[1]
    tc = n_sub * L
    D = d_model

    @pl.when(pl.program_id(1) == 0)
    def _():
        for st_ref in state_s:
            st_ref[...] = jnp.zeros_like(st_ref)
        cu_s[...] = jnp.zeros_like(cu_s)
        cp_s[...] = jnp.zeros_like(cp_s)
        cl_s[...] = jnp.zeros_like(cl_s)

    def split(t):
        hi = t.astype(BF16)
        return hi, (t - hi.astype(F32)).astype(BF16)

    def head_sums(parts, exact=False):
        per_head = sum(jnp.dot(p, hsel_ref[...], preferred_element_type=F32) for p in parts)
        back = split(per_head) if exact else [per_head.astype(BF16)]
        return sum(jnp.dot(p, hselt_ref[...], preferred_element_type=F32) for p in back)

    x = x_ref[...].reshape(tc, D)
    ms = jnp.mean(x * x, axis=-1, keepdims=True)
    xn = (x * lax.rsqrt(ms + NORM_EPS) * g_ref[...]).astype(BF16)

    pc = jnp.dot(xn, wmain_ref[:, 0:3 * D], preferred_element_type=F32)
    u = pc[:, D:2 * D] * pc[:, 2 * D:3 * D]
    cu = cu_s[...]
    conv = (cw_ref[0:1, :] * _shift_rows(u, 2, cu, L) + cw_ref[1:2, :] * _shift_rows(u, 1, cu, L)
            + cw_ref[2:3, :] * u)
    y_conv = pc[:, 0:D] * conv
    cu_s[...] = _last_rows(u, L)

    pr = jnp.dot(xn, wmain_ref[:, 3 * D:6 * D], preferred_element_type=F32)
    cp = cp_s[...]
    cp_s[...] = _last_rows(pr, L)
    pr = pr + (_shift_rows(pr, 1, cp, L) - pr) * mu_rkv_ref[...]
    plo = jnp.dot(xn, wlora_ref[...], preferred_element_type=F32)
    cl = cl_s[...]
    cl_s[...] = _last_rows(plo, L)
    plo = plo + (_shift_rows(plo, 1, cl, L) - plo) * mu_lora_ref[...]

    r = pr[:, 0:D]
    k = pr[:, D:2 * D]
    v = pr[:, 2 * D:3 * D]
    wd = plo[:, 0:LANES]
    ad = plo[:, LANES:2 * LANES]
    gd = plo[:, 2 * LANES:4 * LANES]

    lw = -DECAY_SCALE * _sigmoid(w0_ref[...] + _dot(jnp.tanh(wd), w2_ref[...]))
    a = _sigmoid(a0_ref[...] + _dot(ad, a2_ref[...]))
    g = _dot(_sigmoid(gd), g2_ref[...])

    row = lax.broadcasted_iota(jnp.int32, (tc, tc), 0)
    col = lax.broadcasted_iota(jnp.int32, (tc, tc), 1)
    tri = jnp.where((row >= col) & ((row // L) == (col // L)), 1.0, 0.0).astype(BF16)
    cum = sum(jnp.dot(tri, p, preferred_element_type=F32) for p in split(lw))
    e_inv = jnp.exp(-cum)

    kkraw = k * kk_ref[...]
    ss = head_sums([(kkraw * kkraw).astype(BF16)])
    kkn = kkraw * jnp.minimum(lax.rsqrt(ss), 1e12)

    k2 = k * (1.0 + (a - 1.0) * ka_ref[...])
    a_t = kkn * jnp.exp(cum - lw)
    r_t = r * jnp.exp(cum)
    b_t = kkn * a * e_inv
    k_t = k2 * e_inv
    for c in range(n_sub):
        rows = slice(c * L, (c + 1) * L)
        ar_s[c, 0:L, :] = a_t[rows]
        ar_s[c, L:2 * L, :] = r_t[rows]
        bk_s[c, 0:L, :] = b_t[rows]
        bk_s[c, L:2 * L, :] = k_t[rows]
        gam_s[c] = jnp.exp(cum[(c + 1) * L - 1:(c + 1) * L, :])
    v_s[...] = v
    bonus = head_sums([(r * k2 * rk_ref[...]).astype(BF16)]) * v

    pair = 2 * HEAD
    lane = lax.broadcasted_iota(jnp.int32, (1, pair), 1)
    left = lane < HEAD
    row1 = lax.broadcasted_iota(jnp.int32, (L, pair), 0)
    col1 = lax.broadcasted_iota(jnp.int32, (L, pair), 1) % HEAD
    strict = row1 > col1
    eye = jnp.where(row1 == col1, 1.0, 0.0)
    incl2 = (lax.broadcasted_iota(jnp.int32, (L, 2 * pair), 0)
             >= lax.broadcasted_iota(jnp.int32, (L, 2 * pair), 1) % HEAD)
    level_mask = [(row1 // (2 * s) == col1 // (2 * s)) & (row1 % (2 * s) >= s) & (col1 % (2 * s) < s)
                  for s in (1 << i for i in range(L.bit_length() - 1))]

    def blockdiag(t):
        tb = t.astype(BF16)
        return jnp.concatenate([jnp.where(left, tb, 0), jnp.where(left, 0, tb)], axis=0)

    pairs = range(n_heads // 2)
    psl = [slice(p * pair, (p + 1) * pair) for p in pairs]
    work = [(c, p) for c in range(n_sub) for p in pairs]
    ar = [ar_s[c, :, psl[p]].astype(BF16) for c, p in work]
    bk = [bk_s[c, :, psl[p]].astype(BF16) for c, p in work]
    vp = [v_s[c * L:(c + 1) * L, psl[p]] for c, p in work]
    gm = [_dot_nt(ar[i], jnp.concatenate([blockdiag(bk[i][0:L]), blockdiag(bk[i][L:2 * L])], axis=0))
          for i in range(len(work))]
    nm = [g_[0:L, 0:pair] for g_ in gm]
    mak = [jnp.where(strict, g_[0:L, pair:2 * pair], 0.0) for g_ in gm]
    q = [jnp.where(incl2, g_[L:2 * L, :], 0.0).astype(BF16) for g_ in gm]
    xinv = [eye - jnp.where(level_mask[0], n_, 0.0) for n_ in nm]
    for lm in level_mask[1:]:
        half = [_dot(jnp.where(lm, n_, 0.0), blockdiag(t)) for n_, t in zip(nm, xinv)]
        xinv = [t - _dot(t, blockdiag(h_)) for t, h_ in zip(xinv, half)]
    xinv = [t.astype(BF16) for t in xinv]
    mv = [_dot(m_, blockdiag(v_)) for m_, v_ in zip(mak, vp)]

    gates = _sigmoid(jnp.dot(xn, wgate_ref[...], preferred_element_type=F32))

    items = range(len(work))
    st = [state_s[i][...] for i in items]
    ars = [_dot_nt(ar[i], blockdiag(st[i])) for i in items]
    uu = [_dot(xinv[i], blockdiag(ars[i][0:L] + mv[i])) for i in items]
    yh = [ars[i][L:2 * L] + _dot(q[i], jnp.concatenate([blockdiag(-uu[i]), blockdiag(vp[i])], axis=0))
          for i in items]
    for i, (c, p) in enumerate(work):
        upd = _dot_tn(jnp.concatenate([-uu[i], vp[i]], axis=0), bk[i])
        state_s[i][...] = (st[i] + jnp.where(left, upd[0:HEAD], upd[HEAD:pair])) * gam_s[c, :, psl[p]]
    for i, (c, p) in enumerate(work):
        y_s[c * L:(c + 1) * L, psl[p]] = yh[i]

    y = y_s[...]
    yc = y - head_sums(split(y), exact=True) * (1.0 / HEAD)
    var = head_sums([(yc * yc).astype(BF16)]) * (1.0 / HEAD)
    y_rwkv = (yc * lax.rsqrt(var + GN_EPS) * lng_ref[...] + lnb_ref[...] + bonus) * g
    mix = gates[:, 0:D] * y_conv + gates[:, D:2 * D] * y_rwkv
    h = x + jnp.dot(mix.astype(BF16), wout_ref[...], preferred_element_type=F32)
    o_ref[...] = h.reshape(n_sub, L, D)
    hn = h * lax.rsqrt(jnp.mean(h * h, axis=-1, keepdims=True) + NORM_EPS) * gffn_ref[...]
    hp_ref[...] = _pack_bf16_pairs(hn).reshape(n_sub, L, D // 2)


def _const_spec(shape):
    nd = len(shape)
    return pl.BlockSpec(shape, lambda *_: (0,) * nd, pipeline_mode=pl.Buffered(1))


def _mixer(x, n_groups, norm_g, norm_ffn_g, w_in, conv_w, shift_mu, w0, w2, a0, a2, g2, k_k, k_a, r_k, ln_g, ln_b, w_out,
           expert_weights):
    bsz, s, d = x.shape
    gb = bsz // n_groups
    n_heads = d // HEAD
    L = RWKV_CHUNK
    n_sub = MIX_ROWS
    tc = n_sub * L
    assert gb % n_sub == 0 and s % L == 0
    lora0 = 6 * d
    w_main = w_in[:, 0:6 * d].astype(BF16)
    pad = lambda t, n: jnp.pad(t, ((0, 0), (0, n - t.shape[1])))
    lora_cols = (DECAY_LORA, AAA_LORA, GATE_LORA)
    lora_pads = (LANES, LANES, 2 * LANES)
    pieces_w, pieces_mu, off = [], [], lora0
    for n, p in zip(lora_cols, lora_pads):
        pieces_w.append(pad(w_in[:, off:off + n], p))
        pieces_mu.append(pad(shift_mu[None, off - 3 * d:off - 3 * d + n], p))
        off += n
    w_lora = jnp.concatenate(pieces_w, axis=1).astype(BF16)
    mu_lora = jnp.concatenate(pieces_mu, axis=1)
    w_gate = w_in[:, off:off + 2 * d].astype(BF16)
    mu_rkv = shift_mu[None, 0:3 * d]
    padr = lambda t, n: jnp.pad(t, ((0, n - t.shape[0]), (0, 0)))
    w2p = padr(w2, LANES).astype(BF16)
    a2p = padr(a2, LANES).astype(BF16)
    g2p = padr(g2, 2 * LANES).astype(BF16)
    row = lambda t: t.reshape(1, -1)
    head_of = jnp.arange(d, dtype=jnp.int32) // HEAD
    head_sel = (head_of[:, None] == jnp.arange(LANES)[None, :]).astype(BF16)
    consts = [row(norm_g), w_main, w_lora, w_gate, conv_w, mu_rkv, mu_lora, row(w0), w2p, row(a0), a2p,
              g2p, row(k_k), row(k_a), row(r_k), row(ln_g), row(ln_b), w_out.astype(BF16), head_sel, head_sel.T,
              row(norm_ffn_g)]
    assert len(consts) + 1 == N_MIXER_INPUTS
    n_c = s // L
    n_steps = (gb // n_sub) * n_c
    n_exp = expert_weights[0].shape[0]
    pieces = n_steps // n_exp
    cast_ok = (pieces * n_exp == n_steps
               and all(w.shape[1] % pieces == 0 and (w.shape[1] // pieces) % (2 * SUBLANES) == 0
                       for w in expert_weights))

    def call(first, cast):
        ew = expert_weights if cast else ()
        ew_specs = [pl.BlockSpec((1, w.shape[1] // pieces, w.shape[2]),
                                 lambda b, c: ((b * n_c + c) // pieces, (b * n_c + c) % pieces, 0)) for w in ew]
        return pl.pallas_call(
            functools.partial(_mixer_kernel, n_heads=n_heads, d_model=d, cast_weights=cast),
            grid=(gb // n_sub, n_c),
            in_specs=[pl.BlockSpec((n_sub, L, d), lambda b, c: (b + first // n_sub, c, 0))]
            + [_const_spec(t.shape) for t in consts] + ew_specs,
            out_specs=[pl.BlockSpec((n_sub, L, d), lambda b, c: (b, c, 0)),
                       pl.BlockSpec((n_sub, L, d // 2), lambda b, c: (b, c, 0))] + ew_specs,
            out_shape=[jax.ShapeDtypeStruct((gb, s, d), F32), jax.ShapeDtypeStruct((gb, s, d // 2), jnp.int32)]
            + [jax.ShapeDtypeStruct(w.shape, BF16) for w in ew],
            scratch_shapes=[
                pltpu.VMEM((n_sub * SUBLANES, d), F32),
                pltpu.VMEM((n_sub * SUBLANES, 3 * d), F32),
                pltpu.VMEM((n_sub * SUBLANES, 4 * LANES), F32),
                pltpu.VMEM((n_sub, 2 * L, d), F32),
                pltpu.VMEM((n_sub, 2 * L, d), F32),
                pltpu.VMEM((tc, d), F32),
                pltpu.VMEM((tc, d), F32),
                pltpu.VMEM((n_sub, 1, d), F32),
            ] + [pltpu.VMEM((HEAD, 2 * HEAD), F32)] * (n_sub * n_heads // 2),
            compiler_params=pltpu.CompilerParams(
                dimension_semantics=("arbitrary", "arbitrary"), vmem_limit_bytes=VMEM_LIMIT_BYTES),
            name="mixer",
        )(x, *consts, *ew)

    return [functools.partial(call, g * gb, cast_ok and g == 0) for g in range(n_groups)]


def _router_kernel(h_ref, g_ref, rwh_ref, rwl_ref, rb_ref, tri_ref, meta_ref, metat_ref, cnt_ref, run_s, *,
                   n_experts):
    tt = h_ref.shape[0]

    @pl.when(pl.program_id(0) == 0)
    def _():
        run_s[...] = jnp.zeros_like(run_s)

    h = h_ref[...]
    ms = jnp.mean(h * h, axis=-1, keepdims=True)
    hn = h * lax.rsqrt(ms + NORM_EPS) * g_ref[...]
    hn_hi = hn.astype(BF16)
    hn_lo = (hn - hn_hi.astype(F32)).astype(BF16)
    logits = (jnp.dot(hn_hi, rwh_ref[...], preferred_element_type=F32)
              + jnp.dot(hn_lo, rwh_ref[...], preferred_element_type=F32)
              + jnp.dot(hn_hi, rwl_ref[...], preferred_element_type=F32)) + rb_ref[...]
    lane = lax.broadcasted_iota(jnp.int32, (tt, LANES), 1)
    neg = jnp.float32(-jnp.inf)
    work = jnp.where(lane < n_experts, logits, neg)
    vals, idxs = [], []
    for _ in range(TOP_K):
        m = jnp.max(work, axis=-1, keepdims=True)
        i = jnp.min(jnp.where(work == m, lane, LANES), axis=-1, keepdims=True)
        vals.append(m)
        idxs.append(i)
        work = jnp.where(lane == i, neg, work)
    ex = [jnp.exp(vv - vals[0]) for vv in vals]
    den = ex[0] + ex[1] + ex[2] + ex[3]
    gates = [e / den for e in ex]

    onehot = jnp.zeros((tt, LANES), jnp.bool_)
    for kk in range(TOP_K):
        onehot = onehot | (lane == (idxs[kk] + kk * n_experts))
    oh = jnp.where(onehot, 1.0, 0.0)
    cnt = jnp.dot(tri_ref[...], oh.astype(BF16), preferred_element_type=F32)
    tot = jnp.broadcast_to(jnp.sum(oh, axis=0, keepdims=True), (SUBLANES, LANES))
    lane8 = lax.broadcasted_iota(jnp.int32, (SUBLANES, LANES), 1)
    pk = jnp.zeros_like(tot)
    te = tot
    for j in range(1, TOP_K):
        rolled = pltpu.roll(tot, j * n_experts, axis=1)
        pk = pk + jnp.where(lane8 >= j * n_experts, rolled, 0.0)
        te = te + rolled
    before = cnt + (run_s[...] + pk)[0:1, :]
    ranks = [jnp.sum(jnp.where(onehot & (lane // n_experts == kk), before, 0.0), axis=-1, keepdims=True)
             for kk in range(TOP_K)]
    run_s[...] = run_s[...] + te
    cnt_ref[...] = run_s[...]

    meta = jnp.zeros((tt, LANES), F32)
    for kk in range(TOP_K):
        meta = jnp.where(lane == kk, idxs[kk].astype(F32), meta)
        meta = jnp.where(lane == TOP_K + kk, gates[kk], meta)
        meta = jnp.where(lane == 2 * TOP_K + kk, ranks[kk], meta)
    meta_ref[...] = meta
    metat_ref[...] = meta.T[0:2 * SUBLANES, :]


def _router(h2, norm_g, router_w, router_b):
    t, d = h2.shape
    n_experts = router_w.shape[1]
    tt = min(ROUTE_TOKENS, t)
    rw = jnp.pad(router_w, ((0, 0), (0, LANES - n_experts)))
    rw_hi = rw.astype(BF16)
    rw_lo = (rw - rw_hi.astype(F32)).astype(BF16)
    rb = jnp.pad(router_b, (0, LANES - n_experts)).reshape(1, LANES)
    tok = jnp.arange(tt, dtype=jnp.int32)
    tri = (tok[:, None] > tok[None, :]).astype(BF16)
    kern = functools.partial(_router_kernel, n_experts=n_experts)
    return pl.pallas_call(
        kern,
        grid=(t // tt,),
        in_specs=[pl.BlockSpec((tt, d), lambda i: (i, 0)),
                  pl.BlockSpec((1, d), lambda i: (0, 0)),
                  pl.BlockSpec((d, LANES), lambda i: (0, 0)),
                  pl.BlockSpec((d, LANES), lambda i: (0, 0)),
                  pl.BlockSpec((1, LANES), lambda i: (0, 0)),
                  pl.BlockSpec((tt, tt), lambda i: (0, 0))],
        out_specs=[pl.BlockSpec((tt, LANES), lambda i: (i, 0)),
                   pl.BlockSpec((2 * SUBLANES, tt), lambda i: (0, i)),
                   pl.BlockSpec((SUBLANES, LANES), lambda i: (0, 0))],
        out_shape=[jax.ShapeDtypeStruct((t, LANES), F32),
                   jax.ShapeDtypeStruct((2 * SUBLANES, t), F32),
                   jax.ShapeDtypeStruct((SUBLANES, LANES), F32)],
        scratch_shapes=[pltpu.VMEM((SUBLANES, LANES), F32)],
        compiler_params=pltpu.CompilerParams(
            dimension_semantics=("arbitrary",), vmem_limit_bytes=VMEM_LIMIT_BYTES),
        name="router",
    )(h2, norm_g.reshape(1, d), rw_hi, rw_lo, rb, tri)


def _gather_scratch(per_worker, d, dtype):
    w = GATHER_WINDOW
    return [pltpu.VMEM((per_worker,), jnp.int32),
            pltpu.VMEM((w, d), dtype), pltpu.VMEM((w, d), dtype),
            pltpu.SemaphoreType.DMA, pltpu.SemaphoreType.DMA, pltpu.SemaphoreType.DMA, pltpu.SemaphoreType.DMA]


def _gather_rows(table_hbm, out_hbm, base, idx_v, buf_a, buf_b, gsem_a, gsem_b, psem_a, psem_b):
    w = GATHER_WINDOW
    n_win = idx_v.shape[0] // w

    def gather(j, buf, sem):
        return pltpu.make_async_copy(table_hbm.at[idx_v.at[pl.ds(j * w, w)]], buf, sem)

    def put(j, buf, sem):
        return pltpu.make_async_copy(buf, out_hbm.at[pl.ds(base + j * w, w)], sem)

    gather(0, buf_a, gsem_a).start()

    @pl.loop(0, n_win, step=2)
    def _(j):
        gather(j, buf_a, gsem_a).wait()

        @pl.when(j > 0)
        def _():
            put(j - 1, buf_b, psem_b).wait()

        gather(j + 1, buf_b, gsem_b).start()
        put(j, buf_a, psem_a).start()
        gather(j + 1, buf_b, gsem_b).wait()
        put(j, buf_a, psem_a).wait()

        @pl.when(j + 2 < n_win)
        def _():
            gather(j + 2, buf_a, gsem_a).start()

        put(j + 1, buf_b, psem_b).start()

    put(n_win - 1, buf_b, psem_b).wait()


def _worker_split(n_rows):
    sc = plsc.get_sparse_core_info()
    n_workers = sc.num_cores * sc.num_subcores
    per_worker = n_rows // n_workers
    n_win = per_worker // GATHER_WINDOW
    assert per_worker * n_workers == n_rows and n_win * GATHER_WINDOW == per_worker and n_win % 2 == 0
    return sc, per_worker


def _row_gather(table, idx):
    n_idx, d = idx.shape[0], table.shape[1]
    sc, per_worker = _worker_split(n_idx)
    mesh = plsc.VectorSubcoreMesh(core_axis_name="core", subcore_axis_name="subcore")

    @functools.partial(pl.kernel, out_type=jax.ShapeDtypeStruct((n_idx, d), table.dtype), mesh=mesh,
                       scratch_types=_gather_scratch(per_worker, d, table.dtype))
    def gather_kernel(table_hbm, idx_hbm, out_hbm, idx_v, *bufs):
        base = (lax.axis_index("subcore") * sc.num_cores + lax.axis_index("core")) * per_worker
        pltpu.sync_copy(idx_hbm.at[pl.ds(base, per_worker)], idx_v)
        _gather_rows(table_hbm, out_hbm, base, idx_v, *bufs)

    return gather_kernel(table, idx)


def _dispatch_gather(table, pos, n_slots):
    n, d = pos.shape[0], table.shape[1]
    sc, per_worker = _worker_split(n_slots)
    n_sub = sc.num_subcores
    rows = n // LANES
    rows_per_tile = rows // n_sub
    init_per_tile = n_slots // n_sub
    assert rows_per_tile * n_sub * LANES == n and init_per_tile * n_sub == n_slots
    mesh = plsc.VectorSubcoreMesh(core_axis_name="core", subcore_axis_name="subcore")
    tokens = (jnp.arange(n, dtype=jnp.int32) % (n // TOP_K)).reshape(rows, LANES)
    fill = jnp.arange(n_slots, dtype=jnp.int32) % (n // TOP_K)

    @functools.partial(
        pl.kernel, out_type=jax.ShapeDtypeStruct((n_slots, d), table.dtype), mesh=mesh,
        scratch_types=[pltpu.VMEM((rows_per_tile, LANES), jnp.int32),
                       pltpu.VMEM((rows_per_tile, LANES), jnp.int32),
                       pltpu.VMEM_SHARED((n_slots,), jnp.int32)] + _gather_scratch(per_worker, d, table.dtype))
    def dispatch_kernel(table_hbm, pos_hbm, tok_hbm, fill_hbm, out_hbm, pos_v, tok_v, slot_token, idx_v, *bufs):
        cid = lax.axis_index("core")
        sid = lax.axis_index("subcore")
        pltpu.sync_copy(fill_hbm.at[pl.ds(sid * init_per_tile, init_per_tile)],
                        slot_token.at[pl.ds(sid * init_per_tile, init_per_tile)])
        pltpu.sync_copy(pos_hbm.at[pl.ds(sid * rows_per_tile, rows_per_tile)], pos_v)
        pltpu.sync_copy(tok_hbm.at[pl.ds(sid * rows_per_tile, rows_per_tile)], tok_v)
        plsc.subcore_barrier()

        @pl.loop(0, rows_per_tile)
        def _(j):
            pltpu.sync_copy(tok_v.at[j], slot_token.at[pos_v.at[j]])

        plsc.subcore_barrier()
        base = (sid * sc.num_cores + cid) * per_worker
        pltpu.sync_copy(slot_token.at[pl.ds(base, per_worker)], idx_v)
        _gather_rows(table_hbm, out_hbm, base, idx_v, *bufs)

    return dispatch_kernel(table, pos.reshape(rows, LANES), tokens, fill)


def _experts_kernel(te_ref, nv_ref, xs_ref, w1_ref, b1_ref, w2_ref, b2_ref, ys_ref, *cast_refs, d_ff):
    i = pl.program_id(0)
    nvalid = nv_ref[i]
    if cast_refs:
        w1_use, w2_use = cast_refs
        prev = te_ref[jnp.maximum(i - 1, 0)]

        @pl.when((i == 0) | (te_ref[i] != prev))
        def _():
            w1_use[0] = w1_ref[0].astype(BF16)
            w2_use[0] = w2_ref[0].astype(BF16)
    else:
        w1_use, w2_use = w1_ref, w2_ref

    tm = xs_ref.shape[0]
    all_groups = [slice(r, r + EXPERT_ROWS) for r in range(0, tm, EXPERT_ROWS)]

    def ffn(groups):
        half = xs_ref.shape[1]
        xg = []
        for rows in groups:
            row = rows.start + lax.broadcasted_iota(jnp.int32, (EXPERT_ROWS, 1), 0)
            lo, hi = _unpack_bf16_pairs(jnp.where(row < nvalid, xs_ref[rows, :], 0))
            xg.append((lo.astype(BF16), hi.astype(BF16)))
        ug = [jnp.dot(lo, w1_use[0, 0:half, :], preferred_element_type=F32)
              + jnp.dot(hi, w1_use[0, half:2 * half, :], preferred_element_type=F32) + b1_ref[0] for lo, hi in xg]
        ag = []
        for u in ug:
            glu = jnp.minimum(u[:, 0:d_ff], SWIGLU_LIMIT)
            lin = jnp.clip(u[:, d_ff:2 * d_ff], -SWIGLU_LIMIT, SWIGLU_LIMIT)
            ag.append((glu * _sigmoid(SWIGLU_ALPHA * glu) * (lin + 1.0)).astype(BF16))
        for rows, act in zip(groups, ag):
            ys_ref[rows, :] = _pack_bf16_pairs(
                jnp.dot(act, w2_use[0], preferred_element_type=F32) + b2_ref[0])
        for rows in all_groups[len(groups):]:
            ys_ref[rows, :] = jnp.zeros((EXPERT_ROWS, ys_ref.shape[1]), ys_ref.dtype)

    for n_used in range(len(all_groups) + 1):
        lo_cnt, hi_cnt = (n_used - 1) * EXPERT_ROWS, n_used * EXPERT_ROWS
        pl.when((nvalid > lo_cnt) & (nvalid <= hi_cnt))(functools.partial(ffn, all_groups[:n_used]))


def _experts(xs, tile_expert, tile_valid, w1, b1, w2, b2):
    cast = w1.dtype != BF16
    n_slots = xs.shape[0]
    n_exp, d, two_ff = w1.shape
    d_ff = two_ff // 2
    tm = SLOT_TILE
    n_tiles = n_slots // tm
    kern = functools.partial(_experts_kernel, d_ff=d_ff)
    grid_spec = pltpu.PrefetchScalarGridSpec(
        num_scalar_prefetch=2,
        grid=(n_tiles,),
        in_specs=[pl.BlockSpec((tm, d // 2), lambda i, te, nv: (i, 0)),
                  pl.BlockSpec((1, d, two_ff), lambda i, te, nv: (te[i], 0, 0)),
                  pl.BlockSpec((1, 1, two_ff), lambda i, te, nv: (te[i], 0, 0)),
                  pl.BlockSpec((1, d_ff, d), lambda i, te, nv: (te[i], 0, 0)),
                  pl.BlockSpec((1, 1, d), lambda i, te, nv: (te[i], 0, 0))],
        out_specs=[pl.BlockSpec((tm, d // 2), lambda i, te, nv: (i, 0))]
        + ([pl.BlockSpec((1, d, two_ff), lambda i, te, nv: (te[i], 0, 0)),
            pl.BlockSpec((1, d_ff, d), lambda i, te, nv: (te[i], 0, 0))] if cast else []),
    )
    outs = pl.pallas_call(
        kern,
        grid_spec=grid_spec,
        out_shape=[jax.ShapeDtypeStruct((n_slots, d // 2), jnp.int32)]
        + ([jax.ShapeDtypeStruct(w1.shape, BF16), jax.ShapeDtypeStruct(w2.shape, BF16)] if cast else []),
        compiler_params=pltpu.CompilerParams(
            dimension_semantics=("arbitrary",), vmem_limit_bytes=VMEM_LIMIT_BYTES),
        name="experts",
    )(tile_expert, tile_valid, xs, w1, b1.reshape(n_exp, 1, two_ff), w2, b2.reshape(n_exp, 1, d))
    return outs if cast else (outs[0], w1, w2)


def _combine_kernel(h_ref, meta_ref, g_ref, *rest):
    y_refs, o_ref = rest[:TOP_K], rest[-1]
    d = h_ref.shape[1]
    half = d // 2
    acc_lo = h_ref[:, 0:half]
    acc_hi = h_ref[:, half:d]
    for kk in range(TOP_K):
        gate = meta_ref[:, TOP_K + kk:TOP_K + kk + 1]
        lo, hi = _unpack_bf16_pairs(y_refs[kk][...])
        acc_lo = acc_lo + gate * lo
        acc_hi = acc_hi + gate * hi
    ms = (jnp.sum(acc_lo * acc_lo, axis=-1, keepdims=True)
          + jnp.sum(acc_hi * acc_hi, axis=-1, keepdims=True)) * (1.0 / d)
    scale = lax.rsqrt(ms + NORM_EPS)
    o_ref[:, 0:half] = acc_lo * scale * g_ref[:, 0:half]
    o_ref[:, half:d] = acc_hi * scale * g_ref[:, half:d]


def _combine(h2, y4, meta, norm_g, out_prev, group, n_groups):
    t, d = h2.shape
    tt = min(MOVE_TOKENS, t)
    n_blk = t // tt
    prev_specs = [] if out_prev is None else [pl.BlockSpec(memory_space=pl.ANY)]
    prev_args = [] if out_prev is None else [out_prev]
    n_in = 3 + TOP_K
    y_specs = [pl.BlockSpec((tt, d // 2), functools.partial(lambda i, kk: (kk * n_blk + i, 0), kk=kk))
               for kk in range(TOP_K)]
    return pl.pallas_call(
        _combine_kernel,
        grid=(n_blk,),
        in_specs=[pl.BlockSpec((tt, d), lambda i: (i, 0)),
                  pl.BlockSpec((tt, LANES), lambda i: (i, 0)),
                  pl.BlockSpec((1, d), lambda i: (0, 0))] + y_specs + prev_specs,
        out_specs=pl.BlockSpec((tt, d), lambda i: (group * n_blk + i, 0)),
        out_shape=jax.ShapeDtypeStruct((n_groups * t, d), F32),
        input_output_aliases={} if out_prev is None else {n_in: 0},
        compiler_params=pltpu.CompilerParams(
            dimension_semantics=("arbitrary",), vmem_limit_bytes=VMEM_LIMIT_BYTES),
        name="combine",
    )(h2, meta, norm_g.reshape(1, d), *([y4] * TOP_K), *prev_args)


def _slots_kernel(seg_ref, metat_ref, pos_ref, te_ref, tv_ref, *, n_experts, tile):
    eidx = metat_ref[0:TOP_K, :].astype(jnp.int32)
    pos = metat_ref[2 * TOP_K:3 * TOP_K, :].astype(jnp.int32)
    for e in range(n_experts):
        pos = pos + jnp.where(eidx == e, seg_ref[e], 0)
    pos_ref[...] = pos

    @pl.when(pl.program_id(0) == 0)
    def _():
        first = lax.broadcasted_iota(jnp.int32, te_ref.shape, 1) * tile
        expert = jnp.zeros(te_ref.shape, jnp.int32)
        for e in range(n_experts):
            expert = expert + jnp.where(first >= seg_ref[n_experts + e], 1, 0)
        expert = jnp.minimum(expert, n_experts - 1)
        used_end = jnp.zeros(te_ref.shape, jnp.int32)
        for e in range(n_experts):
            used_end = used_end + jnp.where(expert == e, seg_ref[e] + seg_ref[2 * n_experts + e], 0)
        te_ref[...] = expert
        tv_ref[...] = jnp.clip(used_end - first, 0, tile)


def _slots(metat, seg_start, seg_end, cnt, n_tiles):
    t = metat.shape[1]
    n_exp = seg_start.shape[0]
    tt = min(SLOT_TOKENS, t)
    nt_pad = -(-n_tiles // LANES) * LANES
    grid_spec = pltpu.PrefetchScalarGridSpec(
        num_scalar_prefetch=1,
        grid=(t // tt,),
        in_specs=[pl.BlockSpec((2 * SUBLANES, tt), lambda i, seg: (0, i))],
        out_specs=[pl.BlockSpec((TOP_K, tt), lambda i, seg: (0, i)),
                   pl.BlockSpec((SUBLANES, nt_pad), lambda i, seg: (0, 0)),
                   pl.BlockSpec((SUBLANES, nt_pad), lambda i, seg: (0, 0))],
    )
    pos, tile_expert, tile_valid = pl.pallas_call(
        functools.partial(_slots_kernel, n_experts=n_exp, tile=SLOT_TILE),
        grid_spec=grid_spec,
        out_shape=[jax.ShapeDtypeStruct((TOP_K, t), jnp.int32),
                   jax.ShapeDtypeStruct((SUBLANES, nt_pad), jnp.int32),
                   jax.ShapeDtypeStruct((SUBLANES, nt_pad), jnp.int32)],
        compiler_params=pltpu.CompilerParams(dimension_semantics=("arbitrary",)),
        name="slots",
    )(jnp.concatenate([seg_start, seg_end, cnt]), metat)
    return pos.reshape(-1), tile_expert[0, 0:n_tiles], tile_valid[0, 0:n_tiles]


def _route(h2, hp2, norm_ffn_g, router_w, router_b):
    t, d = h2.shape
    n_exp = router_w.shape[1]
    tm = SLOT_TILE
    meta, metat, counts = _router(h2, norm_ffn_g, router_w, router_b)
    cnt = counts[0, 0:n_exp].astype(jnp.int32)
    padded = jnp.maximum((cnt + tm - 1) // tm, 1) * tm
    seg_end = jnp.cumsum(padded)
    seg_start = seg_end - padded
    n_tiles = -(-(t * TOP_K) // tm) + n_exp
    n_slots = n_tiles * tm
    pos, tile_expert, tile_valid = _slots(metat, seg_start, seg_end, cnt, n_tiles)
    xs = _dispatch_gather(hp2, pos, n_slots)
    return dict(h2=h2, meta=meta, pos=pos, xs=xs, tile_expert=tile_expert, tile_valid=tile_valid)


def kernel(x, norm_mix_g, w_in, conv_w, shift_mu, decay_w0, decay_w2, iclr_a0, iclr_a2, gate_g2, k_k, k_a,
           r_k, ln_x_g, ln_x_b, w_out, norm_ffn_g, router_w, router_b, exp_w1, exp_b1, exp_w2, exp_b2,
           norm_final_g):
    bsz, s, d = x.shape
    depth = w_in.shape[0]
    assert depth == 1, "final norm is fused into the last layer's combine kernel"
    n_groups = BATCH_GROUPS if bsz % (BATCH_GROUPS * MIX_ROWS) == 0 else 1
    mixers = _mixer(x, n_groups, norm_mix_g[0], norm_ffn_g[0], w_in[0], conv_w[0], shift_mu[0], decay_w0[0],
                    decay_w2[0], iclr_a0[0], iclr_a2[0], gate_g2[0], k_k[0], k_a[0], r_k[0], ln_x_g[0], ln_x_b[0],
                    w_out[0], (exp_w1[0], exp_w2[0]))
    w1, w2 = exp_w1[0], exp_w2[0]
    routed = []
    for run_mixer in mixers:
        h, hp, *cast = run_mixer()
        if cast:
            w1, w2 = cast
        routed.append(_route(h.reshape(-1, d), hp.reshape(-1, d // 2), norm_ffn_g[0], router_w[0], router_b[0]))
    gathered = []
    for r in routed:
        ys, w1, w2 = _experts(r["xs"], r["tile_expert"], r["tile_valid"], w1, exp_b1[0], w2, exp_b2[0])
        gathered.append(_row_gather(ys, r["pos"]))
    out = None
    for g, (r, y4) in enumerate(zip(routed, gathered)):
        out = _combine(r["h2"], y4, r["meta"], norm_final_g, out, g, n_groups)
    return out.reshape(bsz, s, d)
```

```python
import functools

import jax
import jax.numpy as jnp
from jax import lax
from jax.experimental import pallas as pl
from jax.experimental.pallas import tpu as pltpu
from jax.experimental.pallas import tpu_sc as plsc

HEAD = 64
DECAY_LORA = 64
AAA_LORA = 64
GATE_LORA = 160
TOP_K = 4
SWIGLU_ALPHA = 1.702
SWIGLU_LIMIT = 7.0
NORM_EPS = 1e-5
GN_EPS = 64e-5
DECAY_SCALE = 0.6065306597126334

LANES = 128
SUBLANES = 8
V7X_VMEM_BYTES = 64 * 1024 * 1024
VMEM_LIMIT_BYTES = V7X_VMEM_BYTES * 7 // 8

RWKV_CHUNK = 64
MIX_ROWS = 4
ROUTE_TOKENS = 512
MOVE_TOKENS = 1024
SLOT_TOKENS = 8192
GATHER_WINDOW = 64
SLOT_TILE = 512
EXPERT_ROWS = 256
BATCH_GROUPS = 4

BF16 = jnp.bfloat16
F32 = jnp.float32


def _dot(a, b):
    return jnp.dot(a.astype(BF16), b.astype(BF16), preferred_element_type=F32)


def _dot_nt(a, b):
    return lax.dot_general(a.astype(BF16), b.astype(BF16), (((1,), (1,)), ((), ())),
                           preferred_element_type=F32)


def _dot_tn(a, b):
    return lax.dot_general(a.astype(BF16), b.astype(BF16), (((0,), (0,)), ((), ())),
                           preferred_element_type=F32)


def _sigmoid(x):
    return 0.5 * jnp.tanh(0.5 * x) + 0.5


def _pack_bf16_pairs(x):
    n = x.shape[1] // 2
    bits = pltpu.bitcast(x.astype(BF16).astype(F32), jnp.uint32)
    packed = (bits[:, n:] & jnp.uint32(0xFFFF0000)) | (bits[:, :n] >> 16)
    return pltpu.bitcast(packed, jnp.int32)


def _unpack_bf16_pairs(w):
    bits = pltpu.bitcast(w, jnp.uint32)
    return pltpu.bitcast(bits << 16, F32), pltpu.bitcast(bits & jnp.uint32(0xFFFF0000), F32)


def _shift_rows(x, n, carry, seg):
    rolled = pltpu.roll(x, n, axis=0)
    row = lax.broadcasted_iota(jnp.int32, (SUBLANES, x.shape[1]), 0)
    pieces = []
    for j in range(x.shape[0] // seg):
        head = rolled[j * seg:j * seg + SUBLANES]
        prev = carry[j * SUBLANES:(j + 1) * SUBLANES]
        for i in range(n):
            head = jnp.where(row == i, prev[SUBLANES - n + i:SUBLANES - n + i + 1, :], head)
        pieces += [head, rolled[j * seg + SUBLANES:(j + 1) * seg]]
    return jnp.concatenate(pieces, axis=0)


def _last_rows(x, seg):
    return jnp.concatenate([x[(j + 1) * seg - SUBLANES:(j + 1) * seg] for j in range(x.shape[0] // seg)],
                           axis=0)


N_MIXER_INPUTS = 22


def _mixer_kernel(*refs, n_heads, d_model, cast_weights):
    n_extra = 2 if cast_weights else 0
    ins, outs = refs[:N_MIXER_INPUTS + n_extra], refs[N_MIXER_INPUTS + n_extra:N_MIXER_INPUTS + 2 + 2 * n_extra]
    scratch = refs[N_MIXER_INPUTS + 2 + 2 * n_extra:]
    (x_ref, g_ref, wmain_ref, wlora_ref, wgate_ref, cw_ref, mu_rkv_ref, mu_lora_ref, w0_ref, w2_ref, a0_ref,
     a2_ref, g2_ref, kk_ref, ka_ref, rk_ref, lng_ref, lnb_ref, wout_ref, hsel_ref, hselt_ref,
     gffn_ref) = ins[:N_MIXER_INPUTS]
    o_ref, hp_ref = outs[:2]
    (cu_s, cp_s, cl_s, ar_s, bk_s, v_s, y_s, gam_s), state_s = scratch[:8], scratch[8:]
    if cast_weights:
        for src, dst in zip(ins[N_MIXER_INPUTS:], outs[2:]):
            dst[...] = src[...].astype(BF16)
    n_sub, L = x_ref.shape[0], x_ref.shape[1]
    tc = n_sub * L
    D = d_model

    @pl.when(pl.program_id(1) == 0)
    def _():
        for st_ref in state_s:
            st_ref[...] = jnp.zeros_like(st_ref)
        cu_s[...] = jnp.zeros_like(cu_s)
        cp_s[...] = jnp.zeros_like(cp_s)
        cl_s[...] = jnp.zeros_like(cl_s)

    def split(t):
        hi = t.astype(BF16)
        return hi, (t - hi.astype(F32)).astype(BF16)

    def head_sums(parts, exact=False):
        per_head = sum(jnp.dot(p, hsel_ref[...], preferred_element_type=F32) for p in parts)
        back = split(per_head) if exact else [per_head.astype(BF16)]
        return sum(jnp.dot(p, hselt_ref[...], preferred_element_type=F32) for p in back)

    x = x_ref[...].reshape(tc, D)
    ms = jnp.mean(x * x, axis=-1, keepdims=True)
    xn = (x * lax.rsqrt(ms + NORM_EPS) * g_ref[...]).astype(BF16)

    pc = jnp.dot(xn, wmain_ref[:, 0:3 * D], preferred_element_type=F32)
    u = pc[:, D:2 * D] * pc[:, 2 * D:3 * D]
    cu = cu_s[...]
    conv = (cw_ref[0:1, :] * _shift_rows(u, 2, cu, L) + cw_ref[1:2, :] * _shift_rows(u, 1, cu, L)
            + cw_ref[2:3, :] * u)
    y_conv = pc[:, 0:D] * conv
    cu_s[...] = _last_rows(u, L)

    pr = jnp.dot(xn, wmain_ref[:, 3 * D:6 * D], preferred_element_type=F32)
    cp = cp_s[...]
    cp_s[...] = _last_rows(pr, L)
    pr = pr + (_shift_rows(pr, 1, cp, L) - pr) * mu_rkv_ref[...]
    plo = jnp.dot(xn, wlora_ref[...], preferred_element_type=F32)
    cl = cl_s[...]
    cl_s[...] = _last_rows(plo, L)
    plo = plo + (_shift_rows(plo, 1, cl, L) - plo) * mu_lora_ref[...]

    r = pr[:, 0:D]
    k = pr[:, D:2 * D]
    v = pr[:, 2 * D:3 * D]
    wd = plo[:, 0:LANES]
    ad = plo[:, LANES:2 * LANES]
    gd = plo[:, 2 * LANES:4 * LANES]

    lw = -DECAY_SCALE * _sigmoid(w0_ref[...] + _dot(jnp.tanh(wd), w2_ref[...]))
    a = _sigmoid(a0_ref[...] + _dot(ad, a2_ref[...]))
    g = _dot(_sigmoid(gd), g2_ref[...])

    row = lax.broadcasted_iota(jnp.int32, (tc, tc), 0)
    col = lax.broadcasted_iota(jnp.int32, (tc, tc), 1)
    tri = jnp.where((row >= col) & ((row // L) == (col // L)), 1.0, 0.0).astype(BF16)
    cum = sum(jnp.dot(tri, p, preferred_element_type=F32) for p in split(lw))
    e_inv = jnp.exp(-cum)

    kkraw = k * kk_ref[...]
    ss = head_sums([(kkraw * kkraw).astype(BF16)])
    kkn = kkraw * jnp.minimum(lax.rsqrt(ss), 1e12)

    k2 = k * (1.0 + (a - 1.0) * ka_ref[...])
    a_t = kkn * jnp.exp(cum - lw)
    r_t = r * jnp.exp(cum)
    b_t = kkn * a * e_inv
    k_t = k2 * e_inv
    for c in range(n_sub):
        rows = slice(c * L, (c + 1) * L)
        ar_s[c, 0:L, :] = a_t[rows]
        ar_s[c, L:2 * L, :] = r_t[rows]
        bk_s[c, 0:L, :] = b_t[rows]
        bk_s[c, L:2 * L, :] = k_t[rows]
        gam_s[c] = jnp.exp(cum[(c + 1) * L - 1:(c + 1) * L, :])
    v_s[...] = v
    bonus = head_sums([(r * k2 * rk_ref[...]).astype(BF16)]) * v

    pair = 2 * HEAD
    lane = lax.broadcasted_iota(jnp.int32, (1, pair), 1)
    left = lane < HEAD
    row1 = lax.broadcasted_iota(jnp.int32, (L, pair), 0)
    col1 = lax.broadcasted_iota(jnp.int32, (L, pair), 1) % HEAD
    strict = row1 > col1
    eye = jnp.where(row1 == col1, 1.0, 0.0)
    incl2 = (lax.broadcasted_iota(jnp.int32, (L, 2 * pair), 0)
             >= lax.broadcasted_iota(jnp.int32, (L, 2 * pair), 1) % HEAD)
    level_mask = [(row1 // (2 * s) == col1 // (2 * s)) & (row1 % (2 * s) >= s) & (col1 % (2 * s) < s)
                  for s in (1 << i for i in range(L.bit_length() - 1))]

    def blockdiag(t):
        tb = t.astype(BF16)
        return jnp.concatenate([jnp.where(left, tb, 0), jnp.where(left, 0, tb)], axis=0)

    pairs = range(n_heads // 2)
    psl = [slice(p * pair, (p + 1) * pair) for p in pairs]
    work = [(c, p) for c in range(n_sub) for p in pairs]
    ar = [ar_s[c, :, psl[p]].astype(BF16) for c, p in work]
    bk = [bk_s[c, :, psl[p]].astype(BF16) for c, p in work]
    vp = [v_s[c * L:(c + 1) * L, psl[p]] for c, p in work]
    gm = [_dot_nt(ar[i], jnp.concatenate([blockdiag(bk[i][0:L]), blockdiag(bk[i][L:2 * L])], axis=0))
          for i in range(len(work))]
    nm = [g_[0:L, 0:pair] for g_ in gm]
    mak = [jnp.where(strict, g_[0:L, pair:2 * pair], 0.0) for g_ in gm]
    q = [jnp.where(incl2, g_[L:2 * L, :], 0.0).astype(BF16) for g_ in gm]
    xinv = [eye - jnp.where(level_mask[0], n_, 0.0) for n_ in nm]
    for lm in level_mask[1:]:
        half = [_dot(jnp.where(lm, n_, 0.0), blockdiag(t)) for n_, t in zip(nm, xinv)]
        xinv = [t - _dot(t, blockdiag(h_)) for t, h_ in zip(xinv, half)]
    xinv = [t.astype(BF16) for t in xinv]
    mv = [_dot(m_, blockdiag(v_)) for m_, v_ in zip(mak, vp)]

    gates = _sigmoid(jnp.dot(xn, wgate_ref[...], preferred_element_type=F32))

    items = range(len(work))
    st = [state_s[i][...] for i in items]
    ars = [_dot_nt(ar[i], blockdiag(st[i])) for i in items]
    uu = [_dot(xinv[i], blockdiag(ars[i][0:L] + mv[i])) for i in items]
    yh = [ars[i][L:2 * L] + _dot(q[i], jnp.concatenate([blockdiag(-uu[i]), blockdiag(vp[i])], axis=0))
          for i in items]
    for i, (c, p) in enumerate(work):
        upd = _dot_tn(jnp.concatenate([-uu[i], vp[i]], axis=0), bk[i])
        state_s[i][...] = (st[i] + jnp.where(left, upd[0:HEAD], upd[HEAD:pair])) * gam_s[c, :, psl[p]]
    for i, (c, p) in enumerate(work):
        y_s[c * L:(c + 1) * L, psl[p]] = yh[i]

    y = y_s[...]
    yc = y - head_sums(split(y), exact=True) * (1.0 / HEAD)
    var = head_sums([(yc * yc).astype(BF16)]) * (1.0 / HEAD)
    y_rwkv = (yc * lax.rsqrt(var + GN_EPS) * lng_ref[...] + lnb_ref[...] + bonus) * g
    mix = gates[:, 0:D] * y_conv + gates[:, D:2 * D] * y_rwkv
    h = x + jnp.dot(mix.astype(BF16), wout_ref[...], preferred_element_type=F32)
    o_ref[...] = h.reshape(n_sub, L, D)
    hn = h * lax.rsqrt(jnp.mean(h * h, axis=-1, keepdims=True) + NORM_EPS) * gffn_ref[...]
    hp_ref[...] = _pack_bf16_pairs(hn).reshape(n_sub, L, D // 2)


def _const_spec(shape):
    nd = len(shape)
    return pl.BlockSpec(shape, lambda *_: (0,) * nd, pipeline_mode=pl.Buffered(1))


def _mixer(x, n_groups, norm_g, norm_ffn_g, w_in, conv_w, shift_mu, w0, w2, a0, a2, g2, k_k, k_a, r_k, ln_g, ln_b, w_out,
           expert_weights):
    bsz, s, d = x.shape
    gb = bsz // n_groups
    n_heads = d // HEAD
    L = RWKV_CHUNK
    n_sub = MIX_ROWS
    tc = n_sub * L
    assert gb % n_sub == 0 and s % L == 0
    lora0 = 6 * d
    w_main = w_in[:, 0:6 * d].astype(BF16)
    pad = lambda t, n: jnp.pad(t, ((0, 0), (0, n - t.shape[1])))
    lora_cols = (DECAY_LORA, AAA_LORA, GATE_LORA)
    lora_pads = (LANES, LANES, 2 * LANES)
    pieces_w, pieces_mu, off = [], [], lora0
    for n, p in zip(lora_cols, lora_pads):
        pieces_w.append(pad(w_in[:, off:off + n], p))
        pieces_mu.append(pad(shift_mu[None, off - 3 * d:off - 3 * d + n], p))
        off += n
    w_lora = jnp.concatenate(pieces_w, axis=1).astype(BF16)
    mu_lora = jnp.concatenate(pieces_mu, axis=1)
    w_gate = w_in[:, off:off + 2 * d].astype(BF16)
    mu_rkv = shift_mu[None, 0:3 * d]
    padr = lambda t, n: jnp.pad(t, ((0, n - t.shape[0]), (0, 0)))
    w2p = padr(w2, LANES).astype(BF16)
    a2p = padr(a2, LANES).astype(BF16)
    g2p = padr(g2, 2 * LANES).astype(BF16)
    row = lambda t: t.reshape(1, -1)
    head_of = jnp.arange(d, dtype=jnp.int32) // HEAD
    head_sel = (head_of[:, None] == jnp.arange(LANES)[None, :]).astype(BF16)
    consts = [row(norm_g), w_main, w_lora, w_gate, conv_w, mu_rkv, mu_lora, row(w0), w2p, row(a0), a2p,
              g2p, row(k_k), row(k_a), row(r_k), row(ln_g), row(ln_b), w_out.astype(BF16), head_sel, head_sel.T,
              row(norm_ffn_g)]
    assert len(consts) + 1 == N_MIXER_INPUTS
    n_c = s // L
    n_steps = (gb // n_sub) * n_c
    n_exp = expert_weights[0].shape[0]
    pieces = n_steps // n_exp
    cast_ok = (pieces * n_exp == n_steps
               and all(w.shape[1] % pieces == 0 and (w.shape[1] // pieces) % (2 * SUBLANES) == 0
                       for w in expert_weights))
    if cast_ok:
        slice_bytes = sum(w.shape[1] // pieces * w.shape[2] * (4 + 2) * 2 for w in expert_weights)
        cast_ok = slice_bytes <= V7X_VMEM_BYTES // 6

    def call(first, cast):
        ew = expert_weights if cast else ()
        ew_specs = [pl.BlockSpec((1, w.shape[1] // pieces, w.shape[2]),
                                 lambda b, c: ((b * n_c + c) // pieces, (b * n_c + c) % pieces, 0)) for w in ew]
        return pl.pallas_call(
            functools.partial(_mixer_kernel, n_heads=n_heads, d_model=d, cast_weights=cast),
            grid=(gb // n_sub, n_c),
            in_specs=[pl.BlockSpec((n_sub, L, d), lambda b, c: (b + first // n_sub, c, 0))]
            + [_const_spec(t.shape) for t in consts] + ew_specs,
            out_specs=[pl.BlockSpec((n_sub, L, d), lambda b, c: (b, c, 0)),
                       pl.BlockSpec((n_sub, L, d // 2), lambda b, c: (b, c, 0))] + ew_specs,
            out_shape=[jax.ShapeDtypeStruct((gb, s, d), F32), jax.ShapeDtypeStruct((gb, s, d // 2), jnp.int32)]
            + [jax.ShapeDtypeStruct(w.shape, BF16) for w in ew],
            scratch_shapes=[
                pltpu.VMEM((n_sub * SUBLANES, d), F32),
                pltpu.VMEM((n_sub * SUBLANES, 3 * d), F32),
                pltpu.VMEM((n_sub * SUBLANES, 4 * LANES), F32),
                pltpu.VMEM((n_sub, 2 * L, d), F32),
                pltpu.VMEM((n_sub, 2 * L, d), F32),
                pltpu.VMEM((tc, d), F32),
                pltpu.VMEM((tc, d), F32),
                pltpu.VMEM((n_sub, 1, d), F32),
            ] + [pltpu.VMEM((HEAD, 2 * HEAD), F32)] * (n_sub * n_heads // 2),
            compiler_params=pltpu.CompilerParams(
                dimension_semantics=("arbitrary", "arbitrary"), vmem_limit_bytes=VMEM_LIMIT_BYTES),
            name="mixer",
        )(x, *consts, *ew)

    return [functools.partial(call, g * gb, cast_ok and g == 0) for g in range(n_groups)]


def _router_kernel(h_ref, g_ref, rwh_ref, rwl_ref, rb_ref, tri_ref, meta_ref, metat_ref, cnt_ref, run_s, *,
                   n_experts):
    tt = h_ref.shape[0]

    @pl.when(pl.program_id(0) == 0)
    def _():
        run_s[...] = jnp.zeros_like(run_s)

    h = h_ref[...]
    ms = jnp.mean(h * h, axis=-1, keepdims=True)
    hn = h * lax.rsqrt(ms + NORM_EPS) * g_ref[...]
    hn_hi = hn.astype(BF16)
    hn_lo = (hn - hn_hi.astype(F32)).astype(BF16)
    logits = (jnp.dot(hn_hi, rwh_ref[...], preferred_element_type=F32)
              + jnp.dot(hn_lo, rwh_ref[...], preferred_element_type=F32)
              + jnp.dot(hn_hi, rwl_ref[...], preferred_element_type=F32)) + rb_ref[...]
    lane = lax.broadcasted_iota(jnp.int32, (tt, LANES), 1)
    neg = jnp.float32(-jnp.inf)
    work = jnp.where(lane < n_experts, logits, neg)
    vals, idxs = [], []
    for _ in range(TOP_K):
        m = jnp.max(work, axis=-1, keepdims=True)
        i = jnp.min(jnp.where(work == m, lane, LANES), axis=-1, keepdims=True)
        vals.append(m)
        idxs.append(i)
        work = jnp.where(lane == i, neg, work)
    ex = [jnp.exp(vv - vals[0]) for vv in vals]
    den = ex[0] + ex[1] + ex[2] + ex[3]
    gates = [e / den for e in ex]

    onehot = jnp.zeros((tt, LANES), jnp.bool_)
    for kk in range(TOP_K):
        onehot = onehot | (lane == (idxs[kk] + kk * n_experts))
    oh = jnp.where(onehot, 1.0, 0.0)
    cnt = jnp.dot(tri_ref[...], oh.astype(BF16), preferred_element_type=F32)
    tot = jnp.broadcast_to(jnp.sum(oh, axis=0, keepdims=True), (SUBLANES, LANES))
    lane8 = lax.broadcasted_iota(jnp.int32, (SUBLANES, LANES), 1)
    pk = jnp.zeros_like(tot)
    te = tot
    for j in range(1, TOP_K):
        rolled = pltpu.roll(tot, j * n_experts, axis=1)
        pk = pk + jnp.where(lane8 >= j * n_experts, rolled, 0.0)
        te = te + rolled
    before = cnt + (run_s[...] + pk)[0:1, :]
    ranks = [jnp.sum(jnp.where(onehot & (lane // n_experts == kk), before, 0.0), axis=-1, keepdims=True)
             for kk in range(TOP_K)]
    run_s[...] = run_s[...] + te
    cnt_ref[...] = run_s[...]

    meta = jnp.zeros((tt, LANES), F32)
    for kk in range(TOP_K):
        meta = jnp.where(lane == kk, idxs[kk].astype(F32), meta)
        meta = jnp.where(lane == TOP_K + kk, gates[kk], meta)
        meta = jnp.where(lane == 2 * TOP_K + kk, ranks[kk], meta)
    meta_ref[...] = meta
    metat_ref[...] = meta.T[0:2 * SUBLANES, :]


def _router(h2, norm_g, router_w, router_b):
    t, d = h2.shape
    n_experts = router_w.shape[1]
    tt = min(ROUTE_TOKENS, t)
    rw = jnp.pad(router_w, ((0, 0), (0, LANES - n_experts)))
    rw_hi = rw.astype(BF16)
    rw_lo = (rw - rw_hi.astype(F32)).astype(BF16)
    rb = jnp.pad(router_b, (0, LANES - n_experts)).reshape(1, LANES)
    tok = jnp.arange(tt, dtype=jnp.int32)
    tri = (tok[:, None] > tok[None, :]).astype(BF16)
    kern = functools.partial(_router_kernel, n_experts=n_experts)
    return pl.pallas_call(
        kern,
        grid=(t // tt,),
        in_specs=[pl.BlockSpec((tt, d), lambda i: (i, 0)),
                  pl.BlockSpec((1, d), lambda i: (0, 0)),
                  pl.BlockSpec((d, LANES), lambda i: (0, 0)),
                  pl.BlockSpec((d, LANES), lambda i: (0, 0)),
                  pl.BlockSpec((1, LANES), lambda i: (0, 0)),
                  pl.BlockSpec((tt, tt), lambda i: (0, 0))],
        out_specs=[pl.BlockSpec((tt, LANES), lambda i: (i, 0)),
                   pl.BlockSpec((2 * SUBLANES, tt), lambda i: (0, i)),
                   pl.BlockSpec((SUBLANES, LANES), lambda i: (0, 0))],
        out_shape=[jax.ShapeDtypeStruct((t, LANES), F32),
                   jax.ShapeDtypeStruct((2 * SUBLANES, t), F32),
                   jax.ShapeDtypeStruct((SUBLANES, LANES), F32)],
        scratch_shapes=[pltpu.VMEM((SUBLANES, LANES), F32)],
        compiler_params=pltpu.CompilerParams(
            dimension_semantics=("arbitrary",), vmem_limit_bytes=VMEM_LIMIT_BYTES),
        name="router",
    )(h2, norm_g.reshape(1, d), rw_hi, rw_lo, rb, tri)


def _gather_scratch(per_worker, d, dtype):
    w = GATHER_WINDOW
    return [pltpu.VMEM((per_worker,), jnp.int32),
            pltpu.VMEM((w, d), dtype), pltpu.VMEM((w, d), dtype),
            pltpu.SemaphoreType.DMA, pltpu.SemaphoreType.DMA, pltpu.SemaphoreType.DMA, pltpu.SemaphoreType.DMA]


def _gather_rows(table_hbm, out_hbm, base, idx_v, buf_a, buf_b, gsem_a, gsem_b, psem_a, psem_b):
    w = GATHER_WINDOW
    n_win = idx_v.shape[0] // w

    def gather(j, buf, sem):
        return pltpu.make_async_copy(table_hbm.at[idx_v.at[pl.ds(j * w, w)]], buf, sem)

    def put(j, buf, sem):
        return pltpu.make_async_copy(buf, out_hbm.at[pl.ds(base + j * w, w)], sem)

    gather(0, buf_a, gsem_a).start()

    @pl.loop(0, n_win, step=2)
    def _(j):
        gather(j, buf_a, gsem_a).wait()

        @pl.when(j > 0)
        def _():
            put(j - 1, buf_b, psem_b).wait()

        gather(j + 1, buf_b, gsem_b).start()
        put(j, buf_a, psem_a).start()
        gather(j + 1, buf_b, gsem_b).wait()
        put(j, buf_a, psem_a).wait()

        @pl.when(j + 2 < n_win)
        def _():
            gather(j + 2, buf_a, gsem_a).start()

        put(j + 1, buf_b, psem_b).start()

    put(n_win - 1, buf_b, psem_b).wait()


def _worker_split(n_rows):
    sc = plsc.get_sparse_core_info()
    n_workers = sc.num_cores * sc.num_subcores
    per_worker = n_rows // n_workers
    n_win = per_worker // GATHER_WINDOW
    assert per_worker * n_workers == n_rows and n_win * GATHER_WINDOW == per_worker and n_win % 2 == 0
    return sc, per_worker


def _row_gather(table, idx):
    n_idx, d = idx.shape[0], table.shape[1]
    sc, per_worker = _worker_split(n_idx)
    mesh = plsc.VectorSubcoreMesh(core_axis_name="core", subcore_axis_name="subcore")

    @functools.partial(pl.kernel, out_type=jax.ShapeDtypeStruct((n_idx, d), table.dtype), mesh=mesh,
                       scratch_types=_gather_scratch(per_worker, d, table.dtype))
    def gather_kernel(table_hbm, idx_hbm, out_hbm, idx_v, *bufs):
        base = (lax.axis_index("subcore") * sc.num_cores + lax.axis_index("core")) * per_worker
        pltpu.sync_copy(idx_hbm.at[pl.ds(base, per_worker)], idx_v)
        _gather_rows(table_hbm, out_hbm, base, idx_v, *bufs)

    return gather_kernel(table, idx)


def _dispatch_gather(table, pos, n_slots):
    n, d = pos.shape[0], table.shape[1]
    sc, per_worker = _worker_split(n_slots)
    n_sub = sc.num_subcores
    rows = n // LANES
    rows_per_tile = rows // n_sub
    init_per_tile = n_slots // n_sub
    assert rows_per_tile * n_sub * LANES == n and init_per_tile * n_sub == n_slots
    mesh = plsc.VectorSubcoreMesh(core_axis_name="core", subcore_axis_name="subcore")
    tokens = (jnp.arange(n, dtype=jnp.int32) % (n // TOP_K)).reshape(rows, LANES)
    fill = jnp.arange(n_slots, dtype=jnp.int32) % (n // TOP_K)

    @functools.partial(
        pl.kernel, out_type=jax.ShapeDtypeStruct((n_slots, d), table.dtype), mesh=mesh,
        scratch_types=[pltpu.VMEM((rows_per_tile, LANES), jnp.int32),
                       pltpu.VMEM((rows_per_tile, LANES), jnp.int32),
                       pltpu.VMEM_SHARED((n_slots,), jnp.int32)] + _gather_scratch(per_worker, d, table.dtype))
    def dispatch_kernel(table_hbm, pos_hbm, tok_hbm, fill_hbm, out_hbm, pos_v, tok_v, slot_token, idx_v, *bufs):
        cid = lax.axis_index("core")
        sid = lax.axis_index("subcore")
        pltpu.sync_copy(fill_hbm.at[pl.ds(sid * init_per_tile, init_per_tile)],
                        slot_token.at[pl.ds(sid * init_per_tile, init_per_tile)])
        pltpu.sync_copy(pos_hbm.at[pl.ds(sid * rows_per_tile, rows_per_tile)], pos_v)
        pltpu.sync_copy(tok_hbm.at[pl.ds(sid * rows_per_tile, rows_per_tile)], tok_v)
        plsc.subcore_barrier()

        @pl.loop(0, rows_per_tile)
        def _(j):
            pltpu.sync_copy(tok_v.at[j], slot_token.at[pos_v.at[j]])

        plsc.subcore_barrier()
        base = (sid * sc.num_cores + cid) * per_worker
        pltpu.sync_copy(slot_token.at[pl.ds(base, per_worker)], idx_v)
        _gather_rows(table_hbm, out_hbm, base, idx_v, *bufs)

    return dispatch_kernel(table, pos.reshape(rows, LANES), tokens, fill)


def _experts_kernel(te_ref, nv_ref, xs_ref, w1_ref, b1_ref, w2_ref, b2_ref, ys_ref, *cast_refs, d_ff):
    i = pl.program_id(0)
    nvalid = nv_ref[i]
    if cast_refs:
        w1_use, w2_use = cast_refs
        prev = te_ref[jnp.maximum(i - 1, 0)]

        @pl.when((i == 0) | (te_ref[i] != prev))
        def _():
            w1_use[0] = w1_ref[0].astype(BF16)
            w2_use[0] = w2_ref[0].astype(BF16)
    else:
        w1_use, w2_use = w1_ref, w2_ref

    tm = xs_ref.shape[0]
    all_groups = [slice(r, r + EXPERT_ROWS) for r in range(0, tm, EXPERT_ROWS)]

    def ffn(groups):
        half = xs_ref.shape[1]
        xg = []
        for rows in groups:
            row = rows.start + lax.broadcasted_iota(jnp.int32, (EXPERT_ROWS, 1), 0)
            lo, hi = _unpack_bf16_pairs(jnp.where(row < nvalid, xs_ref[rows, :], 0))
            xg.append((lo.astype(BF16), hi.astype(BF16)))
        ug = [jnp.dot(lo, w1_use[0, 0:half, :], preferred_element_type=F32)
              + jnp.dot(hi, w1_use[0, half:2 * half, :], preferred_element_type=F32) + b1_ref[0] for lo, hi in xg]
        ag = []
        for u in ug:
            glu = jnp.minimum(u[:, 0:d_ff], SWIGLU_LIMIT)
            lin = jnp.clip(u[:, d_ff:2 * d_ff], -SWIGLU_LIMIT, SWIGLU_LIMIT)
            ag.append((glu * _sigmoid(SWIGLU_ALPHA * glu) * (lin + 1.0)).astype(BF16))
        for rows, act in zip(groups, ag):
            ys_ref[rows, :] = _pack_bf16_pairs(
                jnp.dot(act, w2_use[0], preferred_element_type=F32) + b2_ref[0])
        for rows in all_groups[len(groups):]:
            ys_ref[rows, :] = jnp.zeros((EXPERT_ROWS, ys_ref.shape[1]), ys_ref.dtype)

    for n_used in range(len(all_groups) + 1):
        lo_cnt, hi_cnt = (n_used - 1) * EXPERT_ROWS, n_used * EXPERT_ROWS
        pl.when((nvalid > lo_cnt) & (nvalid <= hi_cnt))(functools.partial(ffn, all_groups[:n_used]))


def _experts(xs, tile_expert, tile_valid, w1, b1, w2, b2):
    cast = w1.dtype != BF16
    n_slots = xs.shape[0]
    n_exp, d, two_ff = w1.shape
    d_ff = two_ff // 2
    tm = SLOT_TILE
    n_tiles = n_slots // tm
    kern = functools.partial(_experts_kernel, d_ff=d_ff)
    grid_spec = pltpu.PrefetchScalarGridSpec(
        num_scalar_prefetch=2,
        grid=(n_tiles,),
        in_specs=[pl.BlockSpec((tm, d // 2), lambda i, te, nv: (i, 0)),
                  pl.BlockSpec((1, d, two_ff), lambda i, te, nv: (te[i], 0, 0)),
                  pl.BlockSpec((1, 1, two_ff), lambda i, te, nv: (te[i], 0, 0)),
                  pl.BlockSpec((1, d_ff, d), lambda i, te, nv: (te[i], 0, 0)),
                  pl.BlockSpec((1, 1, d), lambda i, te, nv: (te[i], 0, 0))],
        out_specs=[pl.BlockSpec((tm, d // 2), lambda i, te, nv: (i, 0))]
        + ([pl.BlockSpec((1, d, two_ff), lambda i, te, nv: (te[i], 0, 0)),
            pl.BlockSpec((1, d_ff, d), lambda i, te, nv: (te[i], 0, 0))] if cast else []),
    )
    outs = pl.pallas_call(
        kern,
        grid_spec=grid_spec,
        out_shape=[jax.ShapeDtypeStruct((n_slots, d // 2), jnp.int32)]
        + ([jax.ShapeDtypeStruct(w1.shape, BF16), jax.ShapeDtypeStruct(w2.shape, BF16)] if cast else []),
        compiler_params=pltpu.CompilerParams(
            dimension_semantics=("arbitrary",), vmem_limit_bytes=VMEM_LIMIT_BYTES),
        name="experts",
    )(tile_expert, tile_valid, xs, w1, b1.reshape(n_exp, 1, two_ff), w2, b2.reshape(n_exp, 1, d))
    return outs if cast else (outs[0], w1, w2)


def _combine_kernel(h_ref, meta_ref, g_ref, *rest):
    y_refs, o_ref = rest[:TOP_K], rest[-1]
    d = h_ref.shape[1]
    half = d // 2
    acc_lo = h_ref[:, 0:half]
    acc_hi = h_ref[:, half:d]
    for kk in range(TOP_K):
        gate = meta_ref[:, TOP_K + kk:TOP_K + kk + 1]
        lo, hi = _unpack_bf16_pairs(y_refs[kk][...])
        acc_lo = acc_lo + gate * lo
        acc_hi = acc_hi + gate * hi
    ms = (jnp.sum(acc_lo * acc_lo, axis=-1, keepdims=True)
          + jnp.sum(acc_hi * acc_hi, axis=-1, keepdims=True)) * (1.0 / d)
    scale = lax.rsqrt(ms + NORM_EPS)
    o_ref[:, 0:half] = acc_lo * scale * g_ref[:, 0:half]
    o_ref[:, half:d] = acc_hi * scale * g_ref[:, half:d]


def _combine(h2, y4, meta, norm_g, out_prev, group, n_groups):
    t, d = h2.shape
    tt = min(MOVE_TOKENS, t)
    n_blk = t // tt
    prev_specs = [] if out_prev is None else [pl.BlockSpec(memory_space=pl.ANY)]
    prev_args = [] if out_prev is None else [out_prev]
    n_in = 3 + TOP_K
    y_specs = [pl.BlockSpec((tt, d // 2), functools.partial(lambda i, kk: (kk * n_blk + i, 0), kk=kk))
               for kk in range(TOP_K)]
    return pl.pallas_call(
        _combine_kernel,
        grid=(n_blk,),
        in_specs=[pl.BlockSpec((tt, d), lambda i: (i, 0)),
                  pl.BlockSpec((tt, LANES), lambda i: (i, 0)),
                  pl.BlockSpec((1, d), lambda i: (0, 0))] + y_specs + prev_specs,
        out_specs=pl.BlockSpec((tt, d), lambda i: (group * n_blk + i, 0)),
        out_shape=jax.ShapeDtypeStruct((n_groups * t, d), F32),
        input_output_aliases={} if out_prev is None else {n_in: 0},
        compiler_params=pltpu.CompilerParams(
            dimension_semantics=("arbitrary",), vmem_limit_bytes=VMEM_LIMIT_BYTES),
        name="combine",
    )(h2, meta, norm_g.reshape(1, d), *([y4] * TOP_K), *prev_args)


def _slots_kernel(seg_ref, metat_ref, pos_ref, te_ref, tv_ref, *, n_experts, tile):
    eidx = metat_ref[0:TOP_K, :].astype(jnp.int32)
    pos = metat_ref[2 * TOP_K:3 * TOP_K, :].astype(jnp.int32)
    for e in range(n_experts):
        pos = pos + jnp.where(eidx == e, seg_ref[e], 0)
    pos_ref[...] = pos

    @pl.when(pl.program_id(0) == 0)
    def _():
        first = lax.broadcasted_iota(jnp.int32, te_ref.shape, 1) * tile
        expert = jnp.zeros(te_ref.shape, jnp.int32)
        for e in range(n_experts):
            expert = expert + jnp.where(first >= seg_ref[n_experts + e], 1, 0)
        expert = jnp.minimum(expert, n_experts - 1)
        used_end = jnp.zeros(te_ref.shape, jnp.int32)
        for e in range(n_experts):
            used_end = used_end + jnp.where(expert == e, seg_ref[e] + seg_ref[2 * n_experts + e], 0)
        te_ref[...] = expert
        tv_ref[...] = jnp.clip(used_end - first, 0, tile)


def _slots(metat, seg_start, seg_end, cnt, n_tiles):
    t = metat.shape[1]
    n_exp = seg_start.shape[0]
    tt = min(SLOT_TOKENS, t)
    nt_pad = -(-n_tiles // LANES) * LANES
    grid_spec = pltpu.PrefetchScalarGridSpec(
        num_scalar_prefetch=1,
        grid=(t // tt,),
        in_specs=[pl.BlockSpec((2 * SUBLANES, tt), lambda i, seg: (0, i))],
        out_specs=[pl.BlockSpec((TOP_K, tt), lambda i, seg: (0, i)),
                   pl.BlockSpec((SUBLANES, nt_pad), lambda i, seg: (0, 0)),
                   pl.BlockSpec((SUBLANES, nt_pad), lambda i, seg: (0, 0))],
    )
    pos, tile_expert, tile_valid = pl.pallas_call(
        functools.partial(_slots_kernel, n_experts=n_exp, tile=SLOT_TILE),
        grid_spec=grid_spec,
        out_shape=[jax.ShapeDtypeStruct((TOP_K, t), jnp.int32),
                   jax.ShapeDtypeStruct((SUBLANES, nt_pad), jnp.int32),
                   jax.ShapeDtypeStruct((SUBLANES, nt_pad), jnp.int32)],
        compiler_params=pltpu.CompilerParams(dimension_semantics=("arbitrary",)),
        name="slots",
    )(jnp.concatenate([seg_start, seg_end, cnt]), metat)
    return pos.reshape(-1), tile_expert[0, 0:n_tiles], tile_valid[0, 0:n_tiles]


def _route(h2, hp2, norm_ffn_g, router_w, router_b):
    t, d = h2.shape
    n_exp = router_w.shape[1]
    tm = SLOT_TILE
    meta, metat, counts = _router(h2, norm_ffn_g, router_w, router_b)
    cnt = counts[0, 0:n_exp].astype(jnp.int32)
    padded = jnp.maximum((cnt + tm - 1) // tm, 1) * tm
    seg_end = jnp.cumsum(padded)
    seg_start = seg_end - padded
    n_tiles = -(-(t * TOP_K) // tm) + n_exp
    n_slots = n_tiles * tm
    pos, tile_expert, tile_valid = _slots(metat, seg_start, seg_end, cnt, n_tiles)
    xs = _dispatch_gather(hp2, pos, n_slots)
    return dict(h2=h2, meta=meta, pos=pos, xs=xs, tile_expert=tile_expert, tile_valid=tile_valid)


def kernel(x, norm_mix_g, w_in, conv_w, shift_mu, decay_w0, decay_w2, iclr_a0, iclr_a2, gate_g2, k_k, k_a,
           r_k, ln_x_g, ln_x_b, w_out, norm_ffn_g, router_w, router_b, exp_w1, exp_b1, exp_w2, exp_b2,
           norm_final_g):
    bsz, s, d = x.shape
    depth = w_in.shape[0]
    assert depth == 1, "final norm is fused into the last layer's combine kernel"
    n_groups = BATCH_GROUPS if bsz % (BATCH_GROUPS * MIX_ROWS) == 0 else 1
    mixers = _mixer(x, n_groups, norm_mix_g[0], norm_ffn_g[0], w_in[0], conv_w[0], shift_mu[0], decay_w0[0],
                    decay_w2[0], iclr_a0[0], iclr_a2[0], gate_g2[0], k_k[0], k_a[0], r_k[0], ln_x_g[0], ln_x_b[0],
                    w_out[0], (exp_w1[0], exp_w2[0]))
    w1, w2 = exp_w1[0], exp_w2[0]
    routed = []
    for run_mixer in mixers:
        h, hp, *cast = run_mixer()
        if cast:
            w1, w2 = cast
        routed.append(_route(h.reshape(-1, d), hp.reshape(-1, d // 2), norm_ffn_g[0], router_w[0], router_b[0]))
    gathered = []
    for r in routed:
        ys, w1, w2 = _experts(r["xs"], r["tile_expert"], r["tile_valid"], w1, exp_b1[0], w2, exp_b2[0])
        gathered.append(_row_gather(ys, r["pos"]))
    out = None
    for g, (r, y4) in enumerate(zip(routed, gathered)):
        out = _combine(r["h2"], y4, r["meta"], norm_final_g, out, g, n_groups)
    return out.reshape(bsz, s, d)
```

```python
import functools

import jax
import jax.numpy as jnp
from jax import lax
from jax.experimental import pallas as pl
from jax.experimental.pallas import tpu as pltpu
from jax.experimental.pallas import tpu_sc as plsc

HEAD = 64
DECAY_LORA = 64
AAA_LORA = 64
GATE_LORA = 160
TOP_K = 4
SWIGLU_ALPHA = 1.702
SWIGLU_LIMIT = 7.0
NORM_EPS = 1e-5
GN_EPS = 64e-5
DECAY_SCALE = 0.6065306597126334

LANES = 128
SUBLANES = 8
V7X_VMEM_BYTES = 64 * 1024 * 1024
VMEM_LIMIT_BYTES = V7X_VMEM_BYTES * 7 // 8

RWKV_CHUNK = 64
MIX_ROWS = 4
ROUTE_TOKENS = 1024
MOVE_TOKENS = 1024
SLOT_TOKENS = 8192
GATHER_WINDOW = 64
SLOT_TILE = 512
EXPERT_ROWS = 256
BATCH_GROUPS = 2

BF16 = jnp.bfloat16
F32 = jnp.float32


def _dot(a, b):
    return jnp.dot(a.astype(BF16), b.astype(BF16), preferred_element_type=F32)


def _dot_nt(a, b):
    return lax.dot_general(a.astype(BF16), b.astype(BF16), (((1,), (1,)), ((), ())),
                           preferred_element_type=F32)


def _dot_tn(a, b):
    return lax.dot_general(a.astype(BF16), b.astype(BF16), (((0,), (0,)), ((), ())),
                           preferred_element_type=F32)


def _sigmoid(x):
    return 0.5 * jnp.tanh(0.5 * x) + 0.5


def _pack_bf16_pairs(x):
    n = x.shape[1] // 2
    bits = pltpu.bitcast(x.astype(BF16).astype(F32), jnp.uint32)
    packed = (bits[:, n:] & jnp.uint32(0xFFFF0000)) | (bits[:, :n] >> 16)
    return pltpu.bitcast(packed, jnp.int32)


def _unpack_bf16_pairs(w):
    bits = pltpu.bitcast(w, jnp.uint32)
    return pltpu.bitcast(bits << 16, F32), pltpu.bitcast(bits & jnp.uint32(0xFFFF0000), F32)


def _shift_rows(x, n, carry, seg):
    rolled = pltpu.roll(x, n, axis=0)
    row = lax.broadcasted_iota(jnp.int32, (SUBLANES, x.shape[1]), 0)
    pieces = []
    for j in range(x.shape[0] // seg):
        head = rolled[j * seg:j * seg + SUBLANES]
        prev = carry[j * SUBLANES:(j + 1) * SUBLANES]
        for i in range(n):
            head = jnp.where(row == i, prev[SUBLANES - n + i:SUBLANES - n + i + 1, :], head)
        pieces += [head, rolled[j * seg + SUBLANES:(j + 1) * seg]]
    return jnp.concatenate(pieces, axis=0)


def _last_rows(x, seg):
    return jnp.concatenate([x[(j + 1) * seg - SUBLANES:(j + 1) * seg] for j in range(x.shape[0] // seg)],
                           axis=0)


N_MIXER_INPUTS = 22


def _mixer_kernel(*refs, n_heads, d_model, cast_weights):
    n_extra = 2 if cast_weights else 0
    ins, outs = refs[:N_MIXER_INPUTS + n_extra], refs[N_MIXER_INPUTS + n_extra:N_MIXER_INPUTS + 2 + 2 * n_extra]
    scratch = refs[N_MIXER_INPUTS + 2 + 2 * n_extra:]
    (x_ref, g_ref, wmain_ref, wlora_ref, wgate_ref, cw_ref, mu_rkv_ref, mu_lora_ref, w0_ref, w2_ref, a0_ref,
     a2_ref, g2_ref, kk_ref, ka_ref, rk_ref, lng_ref, lnb_ref, wout_ref, hsel_ref, hselt_ref,
     gffn_ref) = ins[:N_MIXER_INPUTS]
    o_ref, hp_ref = outs[:2]
    (cu_s, cp_s, cl_s, ar_s, bk_s, v_s, y_s, gam_s), state_s = scratch[:8], scratch[8:]
    if cast_weights:
        for src, dst in zip(ins[N_MIXER_INPUTS:], outs[2:]):
            dst[...] = src[...].astype(BF16)
    n_sub, L = x_ref.shape[0], x_ref.shape[1]
    tc = n_sub * L
    D = d_model

    @pl.when(pl.program_id(1) == 0)
    def _():
        for st_ref in state_s:
            st_ref[...] = jnp.zeros_like(st_ref)
        cu_s[...] = jnp.zeros_like(cu_s)
        cp_s[...] = jnp.zeros_like(cp_s)
        cl_s[...] = jnp.zeros_like(cl_s)

    def split(t):
        hi = t.astype(BF16)
        return hi, (t - hi.astype(F32)).astype(BF16)

    def head_sums(parts, exact=False):
        per_head = sum(jnp.dot(p, hsel_ref[...], preferred_element_type=F32) for p in parts)
        back = split(per_head) if exact else [per_head.astype(BF16)]
        return sum(jnp.dot(p, hselt_ref[...], preferred_element_type=F32) for p in back)

    x = x_ref[...].reshape(tc, D)
    ms = jnp.mean(x * x, axis=-1, keepdims=True)
    xn = (x * lax.rsqrt(ms + NORM_EPS) * g_ref[...]).astype(BF16)

    pc = jnp.dot(xn, wmain_ref[:, 0:3 * D], preferred_element_type=F32)
    u = pc[:, D:2 * D] * pc[:, 2 * D:3 * D]
    cu = cu_s[...]
    conv = (cw_ref[0:1, :] * _shift_rows(u, 2, cu, L) + cw_ref[1:2, :] * _shift_rows(u, 1, cu, L)
            + cw_ref[2:3, :] * u)
    y_conv = pc[:, 0:D] * conv
    cu_s[...] = _last_rows(u, L)

    pr = jnp.dot(xn, wmain_ref[:, 3 * D:6 * D], preferred_element_type=F32)
    cp = cp_s[...]
    cp_s[...] = _last_rows(pr, L)
    pr = pr + (_shift_rows(pr, 1, cp, L) - pr) * mu_rkv_ref[...]
    plo = jnp.dot(xn, wlora_ref[...], preferred_element_type=F32)
    cl = cl_s[...]
    cl_s[...] = _last_rows(plo, L)
    plo = plo + (_shift_rows(plo, 1, cl, L) - plo) * mu_lora_ref[...]

    r = pr[:, 0:D]
    k = pr[:, D:2 * D]
    v = pr[:, 2 * D:3 * D]
    wd = plo[:, 0:LANES]
    ad = plo[:, LANES:2 * LANES]
    gd = plo[:, 2 * LANES:4 * LANES]

    lw = -DECAY_SCALE * _sigmoid(w0_ref[...] + _dot(jnp.tanh(wd), w2_ref[...]))
    a = _sigmoid(a0_ref[...] + _dot(ad, a2_ref[...]))
    g = _dot(_sigmoid(gd), g2_ref[...])

    row = lax.broadcasted_iota(jnp.int32, (tc, tc), 0)
    col = lax.broadcasted_iota(jnp.int32, (tc, tc), 1)
    tri = jnp.where((row >= col) & ((row // L) == (col // L)), 1.0, 0.0).astype(BF16)
    cum = sum(jnp.dot(tri, p, preferred_element_type=F32) for p in split(lw))
    e_inv = jnp.exp(-cum)

    kkraw = k * kk_ref[...]
    ss = head_sums([(kkraw * kkraw).astype(BF16)])
    kkn = kkraw * jnp.minimum(lax.rsqrt(ss), 1e12)

    k2 = k * (1.0 + (a - 1.0) * ka_ref[...])
    a_t = kkn * jnp.exp(cum - lw)
    r_t = r * jnp.exp(cum)
    b_t = kkn * a * e_inv
    k_t = k2 * e_inv
    for c in range(n_sub):
        rows = slice(c * L, (c + 1) * L)
        ar_s[c, 0:L, :] = a_t[rows]
        ar_s[c, L:2 * L, :] = r_t[rows]
        bk_s[c, 0:L, :] = b_t[rows]
        bk_s[c, L:2 * L, :] = k_t[rows]
        gam_s[c] = jnp.exp(cum[(c + 1) * L - 1:(c + 1) * L, :])
    v_s[...] = v
    bonus = head_sums([(r * k2 * rk_ref[...]).astype(BF16)]) * v

    pair = 2 * HEAD
    lane = lax.broadcasted_iota(jnp.int32, (1, pair), 1)
    left = lane < HEAD
    row1 = lax.broadcasted_iota(jnp.int32, (L, pair), 0)
    col1 = lax.broadcasted_iota(jnp.int32, (L, pair), 1) % HEAD
    strict = row1 > col1
    eye = jnp.where(row1 == col1, 1.0, 0.0)
    incl2 = (lax.broadcasted_iota(jnp.int32, (L, 2 * pair), 0)
             >= lax.broadcasted_iota(jnp.int32, (L, 2 * pair), 1) % HEAD)
    level_mask = [(row1 // (2 * s) == col1 // (2 * s)) & (row1 % (2 * s) >= s) & (col1 % (2 * s) < s)
                  for s in (1 << i for i in range(L.bit_length() - 1))]

    def blockdiag(t):
        tb = t.astype(BF16)
        return jnp.concatenate([jnp.where(left, tb, 0), jnp.where(left, 0, tb)], axis=0)

    pairs = range(n_heads // 2)
    psl = [slice(p * pair, (p + 1) * pair) for p in pairs]
    work = [(c, p) for c in range(n_sub) for p in pairs]
    ar = [ar_s[c, :, psl[p]].astype(BF16) for c, p in work]
    bk = [bk_s[c, :, psl[p]].astype(BF16) for c, p in work]
    vp = [v_s[c * L:(c + 1) * L, psl[p]] for c, p in work]
    gm = [_dot_nt(ar[i], jnp.concatenate([blockdiag(bk[i][0:L]), blockdiag(bk[i][L:2 * L])], axis=0))
          for i in range(len(work))]
    nm = [g_[0:L, 0:pair] for g_ in gm]
    mak = [jnp.where(strict, g_[0:L, pair:2 * pair], 0.0) for g_ in gm]
    q = [jnp.where(incl2, g_[L:2 * L, :], 0.0).astype(BF16) for g_ in gm]
    xinv = [eye - jnp.where(level_mask[0], n_, 0.0) for n_ in nm]
    for lm in level_mask[1:]:
        half = [_dot(jnp.where(lm, n_, 0.0), blockdiag(t)) for n_, t in zip(nm, xinv)]
        xinv = [t - _dot(t, blockdiag(h_)) for t, h_ in zip(xinv, half)]
    xinv = [t.astype(BF16) for t in xinv]
    mv = [_dot(m_, blockdiag(v_)) for m_, v_ in zip(mak, vp)]

    gates = _sigmoid(jnp.dot(xn, wgate_ref[...], preferred_element_type=F32))

    items = range(len(work))
    st = [state_s[i][...] for i in items]
    ars = [_dot_nt(ar[i], blockdiag(st[i])) for i in items]
    uu = [_dot(xinv[i], blockdiag(ars[i][0:L] + mv[i])) for i in items]
    yh = [ars[i][L:2 * L] + _dot(q[i], jnp.concatenate([blockdiag(-uu[i]), blockdiag(vp[i])], axis=0))
          for i in items]
    for i, (c, p) in enumerate(work):
        upd = _dot_tn(jnp.concatenate([-uu[i], vp[i]], axis=0), bk[i])
        state_s[i][...] = (st[i] + jnp.where(left, upd[0:HEAD], upd[HEAD:pair])) * gam_s[c, :, psl[p]]
    for i, (c, p) in enumerate(work):
        y_s[c * L:(c + 1) * L, psl[p]] = yh[i]

    y = y_s[...]
    yc = y - head_sums(split(y), exact=True) * (1.0 / HEAD)
    var = head_sums([(yc * yc).astype(BF16)]) * (1.0 / HEAD)
    y_rwkv = (yc * lax.rsqrt(var + GN_EPS) * lng_ref[...] + lnb_ref[...] + bonus) * g
    mix = gates[:, 0:D] * y_conv + gates[:, D:2 * D] * y_rwkv
    h = x + jnp.dot(mix.astype(BF16), wout_ref[...], preferred_element_type=F32)
    o_ref[...] = h.reshape(n_sub, L, D)
    hn = h * lax.rsqrt(jnp.mean(h * h, axis=-1, keepdims=True) + NORM_EPS) * gffn_ref[...]
    hp_ref[...] = _pack_bf16_pairs(hn).reshape(n_sub, L, D // 2)


def _const_spec(shape):
    nd = len(shape)
    return pl.BlockSpec(shape, lambda *_: (0,) * nd, pipeline_mode=pl.Buffered(1))


def _mixer(x, n_groups, norm_g, norm_ffn_g, w_in, conv_w, shift_mu, w0, w2, a0, a2, g2, k_k, k_a, r_k, ln_g, ln_b, w_out,
           expert_weights):
    bsz, s, d = x.shape
    gb = bsz // n_groups
    n_heads = d // HEAD
    L = RWKV_CHUNK
    n_sub = MIX_ROWS
    tc = n_sub * L
    assert gb % n_sub == 0 and s % L == 0
    lora0 = 6 * d
    w_main = w_in[:, 0:6 * d].astype(BF16)
    pad = lambda t, n: jnp.pad(t, ((0, 0), (0, n - t.shape[1])))
    lora_cols = (DECAY_LORA, AAA_LORA, GATE_LORA)
    lora_pads = (LANES, LANES, 2 * LANES)
    pieces_w, pieces_mu, off = [], [], lora0
    for n, p in zip(lora_cols, lora_pads):
        pieces_w.append(pad(w_in[:, off:off + n], p))
        pieces_mu.append(pad(shift_mu[None, off - 3 * d:off - 3 * d + n], p))
        off += n
    w_lora = jnp.concatenate(pieces_w, axis=1).astype(BF16)
    mu_lora = jnp.concatenate(pieces_mu, axis=1)
    w_gate = w_in[:, off:off + 2 * d].astype(BF16)
    mu_rkv = shift_mu[None, 0:3 * d]
    padr = lambda t, n: jnp.pad(t, ((0, n - t.shape[0]), (0, 0)))
    w2p = padr(w2, LANES).astype(BF16)
    a2p = padr(a2, LANES).astype(BF16)
    g2p = padr(g2, 2 * LANES).astype(BF16)
    row = lambda t: t.reshape(1, -1)
    head_of = jnp.arange(d, dtype=jnp.int32) // HEAD
    head_sel = (head_of[:, None] == jnp.arange(LANES)[None, :]).astype(BF16)
    consts = [row(norm_g), w_main, w_lora, w_gate, conv_w, mu_rkv, mu_lora, row(w0), w2p, row(a0), a2p,
              g2p, row(k_k), row(k_a), row(r_k), row(ln_g), row(ln_b), w_out.astype(BF16), head_sel, head_sel.T,
              row(norm_ffn_g)]
    assert len(consts) + 1 == N_MIXER_INPUTS
    n_c = s // L
    n_steps = (gb // n_sub) * n_c
    n_exp = expert_weights[0].shape[0]
    pieces = n_steps // n_exp
    cast_ok = (pieces * n_exp == n_steps
               and all(w.shape[1] % pieces == 0 and (w.shape[1] // pieces) % (2 * SUBLANES) == 0
                       for w in expert_weights))

    def call(first, cast):
        ew = expert_weights if cast else ()
        ew_specs = [pl.BlockSpec((1, w.shape[1] // pieces, w.shape[2]),
                                 lambda b, c: ((b * n_c + c) // pieces, (b * n_c + c) % pieces, 0)) for w in ew]
        return pl.pallas_call(
            functools.partial(_mixer_kernel, n_heads=n_heads, d_model=d, cast_weights=cast),
            grid=(gb // n_sub, n_c),
            in_specs=[pl.BlockSpec((n_sub, L, d), lambda b, c: (b + first // n_sub, c, 0))]
            + [_const_spec(t.shape) for t in consts] + ew_specs,
            out_specs=[pl.BlockSpec((n_sub, L, d), lambda b, c: (b, c, 0)),
                       pl.BlockSpec((n_sub, L, d // 2), lambda b, c: (b, c, 0))] + ew_specs,
            out_shape=[jax.ShapeDtypeStruct((gb, s, d), F32), jax.ShapeDtypeStruct((gb, s, d // 2), jnp.int32)]
            + [jax.ShapeDtypeStruct(w.shape, BF16) for w in ew],
            scratch_shapes=[
                pltpu.VMEM((n_sub * SUBLANES, d), F32),
                pltpu.VMEM((n_sub * SUBLANES, 3 * d), F32),
                pltpu.VMEM((n_sub * SUBLANES, 4 * LANES), F32),
                pltpu.VMEM((n_sub, 2 * L, d), F32),
                pltpu.VMEM((n_sub, 2 * L, d), F32),
                pltpu.VMEM((tc, d), F32),
                pltpu.VMEM((tc, d), F32),
                pltpu.VMEM((n_sub, 1, d), F32),
            ] + [pltpu.VMEM((HEAD, 2 * HEAD), F32)] * (n_sub * n_heads // 2),
            compiler_params=pltpu.CompilerParams(
                dimension_semantics=("arbitrary", "arbitrary"), vmem_limit_bytes=VMEM_LIMIT_BYTES),
            name="mixer",
        )(x, *consts, *ew)

    return [functools.partial(call, g * gb, cast_ok and g == 0) for g in range(n_groups)]


def _router_kernel(h_ref, g_ref, rwh_ref, rwl_ref, rb_ref, tri_ref, meta_ref, metat_ref, cnt_ref, run_s, *,
                   n_experts):
    tt = h_ref.shape[0]

    @pl.when(pl.program_id(0) == 0)
    def _():
        run_s[...] = jnp.zeros_like(run_s)

    h = h_ref[...]
    ms = jnp.mean(h * h, axis=-1, keepdims=True)
    hn = h * lax.rsqrt(ms + NORM_EPS) * g_ref[...]
    hn_hi = hn.astype(BF16)
    hn_lo = (hn - hn_hi.astype(F32)).astype(BF16)
    logits = (jnp.dot(hn_hi, rwh_ref[...], preferred_element_type=F32)
              + jnp.dot(hn_lo, rwh_ref[...], preferred_element_type=F32)
              + jnp.dot(hn_hi, rwl_ref[...], preferred_element_type=F32)) + rb_ref[...]
    lane = lax.broadcasted_iota(jnp.int32, (tt, LANES), 1)
    neg = jnp.float32(-jnp.inf)
    work = jnp.where(lane < n_experts, logits, neg)
    vals, idxs = [], []
    for _ in range(TOP_K):
        m = jnp.max(work, axis=-1, keepdims=True)
        i = jnp.min(jnp.where(work == m, lane, LANES), axis=-1, keepdims=True)
        vals.append(m)
        idxs.append(i)
        work = jnp.where(lane == i, neg, work)
    ex = [jnp.exp(vv - vals[0]) for vv in vals]
    den = ex[0] + ex[1] + ex[2] + ex[3]
    gates = [e / den for e in ex]

    onehot = jnp.zeros((tt, LANES), jnp.bool_)
    for kk in range(TOP_K):
        onehot = onehot | (lane == (idxs[kk] + kk * n_experts))
    oh = jnp.where(onehot, 1.0, 0.0)
    cnt = jnp.dot(tri_ref[...], oh.astype(BF16), preferred_element_type=F32)
    tot = jnp.broadcast_to(jnp.sum(oh, axis=0, keepdims=True), (SUBLANES, LANES))
    lane8 = lax.broadcasted_iota(jnp.int32, (SUBLANES, LANES), 1)
    pk = jnp.zeros_like(tot)
    te = tot
    for j in range(1, TOP_K):
        rolled = pltpu.roll(tot, j * n_experts, axis=1)
        pk = pk + jnp.where(lane8 >= j * n_experts, rolled, 0.0)
        te = te + rolled
    before = cnt + (run_s[...] + pk)[0:1, :]
    ranks = [jnp.sum(jnp.where(onehot & (lane // n_experts == kk), before, 0.0), axis=-1, keepdims=True)
             for kk in range(TOP_K)]
    run_s[...] = run_s[...] + te
    cnt_ref[...] = run_s[...]

    meta = jnp.zeros((tt, LANES), F32)
    for kk in range(TOP_K):
        meta = jnp.where(lane == kk, idxs[kk].astype(F32), meta)
        meta = jnp.where(lane == TOP_K + kk, gates[kk], meta)
        meta = jnp.where(lane == 2 * TOP_K + kk, ranks[kk], meta)
    meta_ref[...] = meta
    metat_ref[...] = meta.T[0:2 * SUBLANES, :]


def _router(h2, norm_g, router_w, router_b):
    t, d = h2.shape
    n_experts = router_w.shape[1]
    tt = min(ROUTE_TOKENS, t)
    rw = jnp.pad(router_w, ((0, 0), (0, LANES - n_experts)))
    rw_hi = rw.astype(BF16)
    rw_lo = (rw - rw_hi.astype(F32)).astype(BF16)
    rb = jnp.pad(router_b, (0, LANES - n_experts)).reshape(1, LANES)
    tok = jnp.arange(tt, dtype=jnp.int32)
    tri = (tok[:, None] > tok[None, :]).astype(BF16)
    kern = functools.partial(_router_kernel, n_experts=n_experts)
    return pl.pallas_call(
        kern,
        grid=(t // tt,),
        in_specs=[pl.BlockSpec((tt, d), lambda i: (i, 0)),
                  pl.BlockSpec((1, d), lambda i: (0, 0)),
                  pl.BlockSpec((d, LANES), lambda i: (0, 0)),
                  pl.BlockSpec((d, LANES), lambda i: (0, 0)),
                  pl.BlockSpec((1, LANES), lambda i: (0, 0)),
                  pl.BlockSpec((tt, tt), lambda i: (0, 0))],
        out_specs=[pl.BlockSpec((tt, LANES), lambda i: (i, 0)),
                   pl.BlockSpec((2 * SUBLANES, tt), lambda i: (0, i)),
                   pl.BlockSpec((SUBLANES, LANES), lambda i: (0, 0))],
        out_shape=[jax.ShapeDtypeStruct((t, LANES), F32),
                   jax.ShapeDtypeStruct((2 * SUBLANES, t), F32),
                   jax.ShapeDtypeStruct((SUBLANES, LANES), F32)],
        scratch_shapes=[pltpu.VMEM((SUBLANES, LANES), F32)],
        compiler_params=pltpu.CompilerParams(
            dimension_semantics=("arbitrary",), vmem_limit_bytes=VMEM_LIMIT_BYTES),
        name="router",
    )(h2, norm_g.reshape(1, d), rw_hi, rw_lo, rb, tri)


def _gather_scratch(per_worker, d, dtype):
    w = GATHER_WINDOW
    return [pltpu.VMEM((per_worker,), jnp.int32),
            pltpu.VMEM((w, d), dtype), pltpu.VMEM((w, d), dtype),
            pltpu.SemaphoreType.DMA, pltpu.SemaphoreType.DMA, pltpu.SemaphoreType.DMA, pltpu.SemaphoreType.DMA]


def _gather_rows(table_hbm, out_hbm, base, idx_v, buf_a, buf_b, gsem_a, gsem_b, psem_a, psem_b):
    w = GATHER_WINDOW
    n_win = idx_v.shape[0] // w

    def gather(j, buf, sem):
        return pltpu.make_async_copy(table_hbm.at[idx_v.at[pl.ds(j * w, w)]], buf, sem)

    def put(j, buf, sem):
        return pltpu.make_async_copy(buf, out_hbm.at[pl.ds(base + j * w, w)], sem)

    gather(0, buf_a, gsem_a).start()

    @pl.loop(0, n_win, step=2)
    def _(j):
        gather(j, buf_a, gsem_a).wait()

        @pl.when(j > 0)
        def _():
            put(j - 1, buf_b, psem_b).wait()

        gather(j + 1, buf_b, gsem_b).start()
        put(j, buf_a, psem_a).start()
        gather(j + 1, buf_b, gsem_b).wait()
        put(j, buf_a, psem_a).wait()

        @pl.when(j + 2 < n_win)
        def _():
            gather(j + 2, buf_a, gsem_a).start()

        put(j + 1, buf_b, psem_b).start()

    put(n_win - 1, buf_b, psem_b).wait()


def _worker_split(n_rows):
    sc = plsc.get_sparse_core_info()
    n_workers = sc.num_cores * sc.num_subcores
    per_worker = n_rows // n_workers
    n_win = per_worker // GATHER_WINDOW
    assert per_worker * n_workers == n_rows and n_win * GATHER_WINDOW == per_worker and n_win % 2 == 0
    return sc, per_worker


def _row_gather(table, idx):
    n_idx, d = idx.shape[0], table.shape[1]
    sc, per_worker = _worker_split(n_idx)
    mesh = plsc.VectorSubcoreMesh(core_axis_name="core", subcore_axis_name="subcore")

    @functools.partial(pl.kernel, out_type=jax.ShapeDtypeStruct((n_idx, d), table.dtype), mesh=mesh,
                       scratch_types=_gather_scratch(per_worker, d, table.dtype))
    def gather_kernel(table_hbm, idx_hbm, out_hbm, idx_v, *bufs):
        base = (lax.axis_index("subcore") * sc.num_cores + lax.axis_index("core")) * per_worker
        pltpu.sync_copy(idx_hbm.at[pl.ds(base, per_worker)], idx_v)
        _gather_rows(table_hbm, out_hbm, base, idx_v, *bufs)

    return gather_kernel(table, idx)


def _dispatch_gather(table, pos, n_slots):
    n, d = pos.shape[0], table.shape[1]
    sc, per_worker = _worker_split(n_slots)
    n_sub = sc.num_subcores
    rows = n // LANES
    rows_per_tile = rows // n_sub
    init_per_tile = n_slots // n_sub
    assert rows_per_tile * n_sub * LANES == n and init_per_tile * n_sub == n_slots
    mesh = plsc.VectorSubcoreMesh(core_axis_name="core", subcore_axis_name="subcore")
    tokens = (jnp.arange(n, dtype=jnp.int32) % (n // TOP_K)).reshape(rows, LANES)
    fill = jnp.arange(n_slots, dtype=jnp.int32) % (n // TOP_K)

    @functools.partial(
        pl.kernel, out_type=jax.ShapeDtypeStruct((n_slots, d), table.dtype), mesh=mesh,
        scratch_types=[pltpu.VMEM((rows_per_tile, LANES), jnp.int32),
                       pltpu.VMEM((rows_per_tile, LANES), jnp.int32),
                       pltpu.VMEM_SHARED((n_slots,), jnp.int32)] + _gather_scratch(per_worker, d, table.dtype))
    def dispatch_kernel(table_hbm, pos_hbm, tok_hbm, fill_hbm, out_hbm, pos_v, tok_v, slot_token, idx_v, *bufs):
        cid = lax.axis_index("core")
        sid = lax.axis_index("subcore")
        pltpu.sync_copy(fill_hbm.at[pl.ds(sid * init_per_tile, init_per_tile)],
                        slot_token.at[pl.ds(sid * init_per_tile, init_per_tile)])
        pltpu.sync_copy(pos_hbm.at[pl.ds(sid * rows_per_tile, rows_per_tile)], pos_v)
        pltpu.sync_copy(tok_hbm.at[pl.ds(sid * rows_per_tile, rows_per_tile)], tok_v)
        plsc.subcore_barrier()

        @pl.loop(0, rows_per_tile)
        def _(j):
            pltpu.sync_copy(tok_v.at[j], slot_token.at[pos_v.at[j]])

        plsc.subcore_barrier()
        base = (sid * sc.num_cores + cid) * per_worker
        pltpu.sync_copy(slot_token.at[pl.ds(base, per_worker)], idx_v)
        _gather_rows(table_hbm, out_hbm, base, idx_v, *bufs)

    return dispatch_kernel(table, pos.reshape(rows, LANES), tokens, fill)


def _experts_kernel(te_ref, nv_ref, xs_ref, w1_ref, b1_ref, w2_ref, b2_ref, ys_ref, *cast_refs, d_ff):
    i = pl.program_id(0)
    nvalid = nv_ref[i]
    if cast_refs:
        w1_use, w2_use = cast_refs
        prev = te_ref[jnp.maximum(i - 1, 0)]

        @pl.when((i == 0) | (te_ref[i] != prev))
        def _():
            w1_use[0] = w1_ref[0].astype(BF16)
            w2_use[0] = w2_ref[0].astype(BF16)
    else:
        w1_use, w2_use = w1_ref, w2_ref

    tm = xs_ref.shape[0]
    all_groups = [slice(r, r + EXPERT_ROWS) for r in range(0, tm, EXPERT_ROWS)]

    def ffn(groups):
        half = xs_ref.shape[1]
        xg = []
        for rows in groups:
            row = rows.start + lax.broadcasted_iota(jnp.int32, (EXPERT_ROWS, 1), 0)
            lo, hi = _unpack_bf16_pairs(jnp.where(row < nvalid, xs_ref[rows, :], 0))
            xg.append((lo.astype(BF16), hi.astype(BF16)))
        ug = [jnp.dot(lo, w1_use[0, 0:half, :], preferred_element_type=F32)
              + jnp.dot(hi, w1_use[0, half:2 * half, :], preferred_element_type=F32) + b1_ref[0] for lo, hi in xg]
        ag = []
        for u in ug:
            glu = jnp.minimum(u[:, 0:d_ff], SWIGLU_LIMIT)
            lin = jnp.clip(u[:, d_ff:2 * d_ff], -SWIGLU_LIMIT, SWIGLU_LIMIT)
            ag.append((glu * _sigmoid(SWIGLU_ALPHA * glu) * (lin + 1.0)).astype(BF16))
        for rows, act in zip(groups, ag):
            ys_ref[rows, :] = _pack_bf16_pairs(
                jnp.dot(act, w2_use[0], preferred_element_type=F32) + b2_ref[0])
        for rows in all_groups[len(groups):]:
            ys_ref[rows, :] = jnp.zeros((EXPERT_ROWS, ys_ref.shape[1]), ys_ref.dtype)

    for n_used in range(len(all_groups) + 1):
        lo_cnt, hi_cnt = (n_used - 1) * EXPERT_ROWS, n_used * EXPERT_ROWS
        pl.when((nvalid > lo_cnt) & (nvalid <= hi_cnt))(functools.partial(ffn, all_groups[:n_used]))


def _experts(xs, tile_expert, tile_valid, w1, b1, w2, b2):
    cast = w1.dtype != BF16
    n_slots = xs.shape[0]
    n_exp, d, two_ff = w1.shape
    d_ff = two_ff // 2
    tm = SLOT_TILE
    n_tiles = n_slots // tm
    kern = functools.partial(_experts_kernel, d_ff=d_ff)
    grid_spec = pltpu.PrefetchScalarGridSpec(
        num_scalar_prefetch=2,
        grid=(n_tiles,),
        in_specs=[pl.BlockSpec((tm, d // 2), lambda i, te, nv: (i, 0)),
                  pl.BlockSpec((1, d, two_ff), lambda i, te, nv: (te[i], 0, 0)),
                  pl.BlockSpec((1, 1, two_ff), lambda i, te, nv: (te[i], 0, 0)),
                  pl.BlockSpec((1, d_ff, d), lambda i, te, nv: (te[i], 0, 0)),
                  pl.BlockSpec((1, 1, d), lambda i, te, nv: (te[i], 0, 0))],
        out_specs=[pl.BlockSpec((tm, d // 2), lambda i, te, nv: (i, 0))]
        + ([pl.BlockSpec((1, d, two_ff), lambda i, te, nv: (te[i], 0, 0)),
            pl.BlockSpec((1, d_ff, d), lambda i, te, nv: (te[i], 0, 0))] if cast else []),
    )
    outs = pl.pallas_call(
        kern,
        grid_spec=grid_spec,
        out_shape=[jax.ShapeDtypeStruct((n_slots, d // 2), jnp.int32)]
        + ([jax.ShapeDtypeStruct(w1.shape, BF16), jax.ShapeDtypeStruct(w2.shape, BF16)] if cast else []),
        compiler_params=pltpu.CompilerParams(
            dimension_semantics=("arbitrary",), vmem_limit_bytes=VMEM_LIMIT_BYTES),
        name="experts",
    )(tile_expert, tile_valid, xs, w1, b1.reshape(n_exp, 1, two_ff), w2, b2.reshape(n_exp, 1, d))
    return outs if cast else (outs[0], w1, w2)


def _combine_kernel(h_ref, meta_ref, g_ref, *rest):
    y_refs, o_ref = rest[:TOP_K], rest[-1]
    d = h_ref.shape[1]
    half = d // 2
    acc_lo = h_ref[:, 0:half]
    acc_hi = h_ref[:, half:d]
    for kk in range(TOP_K):
        gate = meta_ref[:, TOP_K + kk:TOP_K + kk + 1]
        lo, hi = _unpack_bf16_pairs(y_refs[kk][...])
        acc_lo = acc_lo + gate * lo
        acc_hi = acc_hi + gate * hi
    ms = (jnp.sum(acc_lo * acc_lo, axis=-1, keepdims=True)
          + jnp.sum(acc_hi * acc_hi, axis=-1, keepdims=True)) * (1.0 / d)
    scale = lax.rsqrt(ms + NORM_EPS)
    o_ref[:, 0:half] = acc_lo * scale * g_ref[:, 0:half]
    o_ref[:, half:d] = acc_hi * scale * g_ref[:, half:d]


def _combine(h2, y4, meta, norm_g, out_prev, group, n_groups):
    t, d = h2.shape
    tt = min(MOVE_TOKENS, t)
    n_blk = t // tt
    prev_specs = [] if out_prev is None else [pl.BlockSpec(memory_space=pl.ANY)]
    prev_args = [] if out_prev is None else [out_prev]
    n_in = 3 + TOP_K
    y_specs = [pl.BlockSpec((tt, d // 2), functools.partial(lambda i, kk: (kk * n_blk + i, 0), kk=kk))
               for kk in range(TOP_K)]
    return pl.pallas_call(
        _combine_kernel,
        grid=(n_blk,),
        in_specs=[pl.BlockSpec((tt, d), lambda i: (i, 0)),
                  pl.BlockSpec((tt, LANES), lambda i: (i, 0)),
                  pl.BlockSpec((1, d), lambda i: (0, 0))] + y_specs + prev_specs,
        out_specs=pl.BlockSpec((tt, d), lambda i: (group * n_blk + i, 0)),
        out_shape=jax.ShapeDtypeStruct((n_groups * t, d), F32),
        input_output_aliases={} if out_prev is None else {n_in: 0},
        compiler_params=pltpu.CompilerParams(
            dimension_semantics=("arbitrary",), vmem_limit_bytes=VMEM_LIMIT_BYTES),
        name="combine",
    )(h2, meta, norm_g.reshape(1, d), *([y4] * TOP_K), *prev_args)


def _slots_kernel(seg_ref, metat_ref, pos_ref, te_ref, tv_ref, *, n_experts, tile):
    eidx = metat_ref[0:TOP_K, :].astype(jnp.int32)
    pos = metat_ref[2 * TOP_K:3 * TOP_K, :].astype(jnp.int32)
    for e in range(n_experts):
        pos = pos + jnp.where(eidx == e, seg_ref[e], 0)
    pos_ref[...] = pos

    @pl.when(pl.program_id(0) == 0)
    def _():
        first = lax.broadcasted_iota(jnp.int32, te_ref.shape, 1) * tile
        expert = jnp.zeros(te_ref.shape, jnp.int32)
        for e in range(n_experts):
            expert = expert + jnp.where(first >= seg_ref[n_experts + e], 1, 0)
        expert = jnp.minimum(expert, n_experts - 1)
        used_end = jnp.zeros(te_ref.shape, jnp.int32)
        for e in range(n_experts):
            used_end = used_end + jnp.where(expert == e, seg_ref[e] + seg_ref[2 * n_experts + e], 0)
        te_ref[...] = expert
        tv_ref[...] = jnp.clip(used_end - first, 0, tile)


def _slots(metat, seg_start, seg_end, cnt, n_tiles):
    t = metat.shape[1]
    n_exp = seg_start.shape[0]
    tt = min(SLOT_TOKENS, t)
    nt_pad = -(-n_tiles // LANES) * LANES
    grid_spec = pltpu.PrefetchScalarGridSpec(
        num_scalar_prefetch=1,
        grid=(t // tt,),
        in_specs=[pl.BlockSpec((2 * SUBLANES, tt), lambda i, seg: (0, i))],
        out_specs=[pl.BlockSpec((TOP_K, tt), lambda i, seg: (0, i)),
                   pl.BlockSpec((SUBLANES, nt_pad), lambda i, seg: (0, 0)),
                   pl.BlockSpec((SUBLANES, nt_pad), lambda i, seg: (0, 0))],
    )
    pos, tile_expert, tile_valid = pl.pallas_call(
        functools.partial(_slots_kernel, n_experts=n_exp, tile=SLOT_TILE),
        grid_spec=grid_spec,
        out_shape=[jax.ShapeDtypeStruct((TOP_K, t), jnp.int32),
                   jax.ShapeDtypeStruct((SUBLANES, nt_pad), jnp.int32),
                   jax.ShapeDtypeStruct((SUBLANES, nt_pad), jnp.int32)],
        compiler_params=pltpu.CompilerParams(dimension_semantics=("arbitrary",)),
        name="slots",
    )(jnp.concatenate([seg_start, seg_end, cnt]), metat)
    return pos.reshape(-1), tile_expert[0, 0:n_tiles], tile_valid[0, 0:n_tiles]


def _route(h2, hp2, norm_ffn_g, router_w, router_b):
    t, d = h2.shape
    n_exp = router_w.shape[1]
    tm = SLOT_TILE
    meta, metat, counts = _router(h2, norm_ffn_g, router_w, router_b)
    cnt = counts[0, 0:n_exp].astype(jnp.int32)
    padded = jnp.maximum((cnt + tm - 1) // tm, 1) * tm
    seg_end = jnp.cumsum(padded)
    seg_start = seg_end - padded
    n_tiles = -(-(t * TOP_K) // tm) + n_exp
    n_slots = n_tiles * tm
    pos, tile_expert, tile_valid = _slots(metat, seg_start, seg_end, cnt, n_tiles)
    xs = _dispatch_gather(hp2, pos, n_slots)
    return dict(h2=h2, meta=meta, pos=pos, xs=xs, tile_expert=tile_expert, tile_valid=tile_valid)


def kernel(x, norm_mix_g, w_in, conv_w, shift_mu, decay_w0, decay_w2, iclr_a0, iclr_a2, gate_g2, k_k, k_a,
           r_k, ln_x_g, ln_x_b, w_out, norm_ffn_g, router_w, router_b, exp_w1, exp_b1, exp_w2, exp_b2,
           norm_final_g):
    bsz, s, d = x.shape
    depth = w_in.shape[0]
    assert depth == 1, "final norm is fused into the last layer's combine kernel"
    n_groups = BATCH_GROUPS if bsz % (BATCH_GROUPS * MIX_ROWS) == 0 else 1
    mixers = _mixer(x, n_groups, norm_mix_g[0], norm_ffn_g[0], w_in[0], conv_w[0], shift_mu[0], decay_w0[0],
                    decay_w2[0], iclr_a0[0], iclr_a2[0], gate_g2[0], k_k[0], k_a[0], r_k[0], ln_x_g[0], ln_x_b[0],
                    w_out[0], (exp_w1[0], exp_w2[0]))
    w1, w2 = exp_w1[0], exp_w2[0]
    routed = []
    for run_mixer in mixers:
        h, hp, *cast = run_mixer()
        if cast:
            w1, w2 = cast
        routed.append(_route(h.reshape(-1, d), hp.reshape(-1, d // 2), norm_ffn_g[0], router_w[0], router_b[0]))
    gathered = []
    for r in routed:
        ys, w1, w2 = _experts(r["xs"], r["tile_expert"], r["tile_valid"], w1, exp_b1[0], w2, exp_b2[0])
        gathered.append(_row_gather(ys, r["pos"]))
    out = None
    for g, (r, y4) in enumerate(zip(routed, gathered)):
        out = _combine(r["h2"], y4, r["meta"], norm_final_g, out, g, n_groups)
    return out.reshape(bsz, s, d)
```

```python
import functools

import jax
import jax.numpy as jnp
from jax import lax
from jax.experimental import pallas as pl
from jax.experimental.pallas import tpu as pltpu
from jax.experimental.pallas import tpu_sc as plsc

HEAD = 64
DECAY_LORA = 64
AAA_LORA = 64
GATE_LORA = 160
TOP_K = 4
SWIGLU_ALPHA = 1.702
SWIGLU_LIMIT = 7.0
NORM_EPS = 1e-5
GN_EPS = 64e-5
DECAY_SCALE = 0.6065306597126334

LANES = 128
SUBLANES = 8
V7X_VMEM_BYTES = 64 * 1024 * 1024
VMEM_LIMIT_BYTES = V7X_VMEM_BYTES * 7 // 8

RWKV_CHUNK = 64
MIX_ROWS = 4
ROUTE_TOKENS = 2048
MOVE_TOKENS = 1024
SLOT_TOKENS = 8192
GATHER_WINDOW = 64
SLOT_TILE = 512
EXPERT_ROWS = 256
BATCH_GROUPS = 2

BF16 = jnp.bfloat16
F32 = jnp.float32


def _dot(a, b):
    return jnp.dot(a.astype(BF16), b.astype(BF16), preferred_element_type=F32)


def _dot_nt(a, b):
    return lax.dot_general(a.astype(BF16), b.astype(BF16), (((1,), (1,)), ((), ())),
                           preferred_element_type=F32)


def _dot_tn(a, b):
    return lax.dot_general(a.astype(BF16), b.astype(BF16), (((0,), (0,)), ((), ())),
                           preferred_element_type=F32)


def _sigmoid(x):
    return 0.5 * jnp.tanh(0.5 * x) + 0.5


def _pack_bf16_pairs(x):
    n = x.shape[1] // 2
    bits = pltpu.bitcast(x.astype(BF16).astype(F32), jnp.uint32)
    packed = (bits[:, n:] & jnp.uint32(0xFFFF0000)) | (bits[:, :n] >> 16)
    return pltpu.bitcast(packed, jnp.int32)


def _unpack_bf16_pairs(w):
    bits = pltpu.bitcast(w, jnp.uint32)
    return pltpu.bitcast(bits << 16, F32), pltpu.bitcast(bits & jnp.uint32(0xFFFF0000), F32)


def _shift_rows(x, n, carry, seg):
    rolled = pltpu.roll(x, n, axis=0)
    row = lax.broadcasted_iota(jnp.int32, (SUBLANES, x.shape[1]), 0)
    pieces = []
    for j in range(x.shape[0] // seg):
        head = rolled[j * seg:j * seg + SUBLANES]
        prev = carry[j * SUBLANES:(j + 1) * SUBLANES]
        for i in range(n):
            head = jnp.where(row == i, prev[SUBLANES - n + i:SUBLANES - n + i + 1, :], head)
        pieces += [head, rolled[j * seg + SUBLANES:(j + 1) * seg]]
    return jnp.concatenate(pieces, axis=0)


def _last_rows(x, seg):
    return jnp.concatenate([x[(j + 1) * seg - SUBLANES:(j + 1) * seg] for j in range(x.shape[0] // seg)],
                           axis=0)


N_MIXER_INPUTS = 22


def _mixer_kernel(*refs, n_heads, d_model, cast_weights):
    n_extra = 2 if cast_weights else 0
    ins, outs = refs[:N_MIXER_INPUTS + n_extra], refs[N_MIXER_INPUTS + n_extra:N_MIXER_INPUTS + 2 + 2 * n_extra]
    scratch = refs[N_MIXER_INPUTS + 2 + 2 * n_extra:]
    (x_ref, g_ref, wmain_ref, wlora_ref, wgate_ref, cw_ref, mu_rkv_ref, mu_lora_ref, w0_ref, w2_ref, a0_ref,
     a2_ref, g2_ref, kk_ref, ka_ref, rk_ref, lng_ref, lnb_ref, wout_ref, hsel_ref, hselt_ref,
     gffn_ref) = ins[:N_MIXER_INPUTS]
    o_ref, hp_ref = outs[:2]
    (cu_s, cp_s, cl_s, ar_s, bk_s, v_s, y_s, gam_s), state_s = scratch[:8], scratch[8:]
    if cast_weights:
        for src, dst in zip(ins[N_MIXER_INPUTS:], outs[2:]):
            dst[...] = src[...].astype(BF16)
    n_sub, L = x_ref.shape[0], x_ref.shape[1]
    tc = n_sub * L
    D = d_model

    @pl.when(pl.program_id(1) == 0)
    def _():
        for st_ref in state_s:
            st_ref[...] = jnp.zeros_like(st_ref)
        cu_s[...] = jnp.zeros_like(cu_s)
        cp_s[...] = jnp.zeros_like(cp_s)
        cl_s[...] = jnp.zeros_like(cl_s)

    def split(t):
        hi = t.astype(BF16)
        return hi, (t - hi.astype(F32)).astype(BF16)

    def head_sums(parts, exact=False):
        per_head = sum(jnp.dot(p, hsel_ref[...], preferred_element_type=F32) for p in parts)
        back = split(per_head) if exact else [per_head.astype(BF16)]
        return sum(jnp.dot(p, hselt_ref[...], preferred_element_type=F32) for p in back)

    x = x_ref[...].reshape(tc, D)
    ms = jnp.mean(x * x, axis=-1, keepdims=True)
    xn = (x * lax.rsqrt(ms + NORM_EPS) * g_ref[...]).astype(BF16)

    pc = jnp.dot(xn, wmain_ref[:, 0:3 * D], preferred_element_type=F32)
    u = pc[:, D:2 * D] * pc[:, 2 * D:3 * D]
    cu = cu_s[...]
    conv = (cw_ref[0:1, :] * _shift_rows(u, 2, cu, L) + cw_ref[1:2, :] * _shift_rows(u, 1, cu, L)
            + cw_ref[2:3, :] * u)
    y_conv = pc[:, 0:D] * conv
    cu_s[...] = _last_rows(u, L)

    pr = jnp.dot(xn, wmain_ref[:, 3 * D:6 * D], preferred_element_type=F32)
    cp = cp_s[...]
    cp_s[...] = _last_rows(pr, L)
    pr = pr + (_shift_rows(pr, 1, cp, L) - pr) * mu_rkv_ref[...]
    plo = jnp.dot(xn, wlora_ref[...], preferred_element_type=F32)
    cl = cl_s[...]
    cl_s[...] = _last_rows(plo, L)
    plo = plo + (_shift_rows(plo, 1, cl, L) - plo) * mu_lora_ref[...]

    r = pr[:, 0:D]
    k = pr[:, D:2 * D]
    v = pr[:, 2 * D:3 * D]
    wd = plo[:, 0:LANES]
    ad = plo[:, LANES:2 * LANES]
    gd = plo[:, 2 * LANES:4 * LANES]

    lw = -DECAY_SCALE * _sigmoid(w0_ref[...] + _dot(jnp.tanh(wd), w2_ref[...]))
    a = _sigmoid(a0_ref[...] + _dot(ad, a2_ref[...]))
    g = _dot(_sigmoid(gd), g2_ref[...])

    row = lax.broadcasted_iota(jnp.int32, (tc, tc), 0)
    col = lax.broadcasted_iota(jnp.int32, (tc, tc), 1)
    tri = jnp.where((row >= col) & ((row // L) == (col // L)), 1.0, 0.0).astype(BF16)
    cum = sum(jnp.dot(tri, p, preferred_element_type=F32) for p in split(lw))
    e_inv = jnp.exp(-cum)

    kkraw = k * kk_ref[...]
    ss = head_sums([(kkraw * kkraw).astype(BF16)])
    kkn = kkraw * jnp.minimum(lax.rsqrt(ss), 1e12)

    k2 = k * (1.0 + (a - 1.0) * ka_ref[...])
    a_t = kkn * jnp.exp(cum - lw)
    r_t = r * jnp.exp(cum)
    b_t = kkn * a * e_inv
    k_t = k2 * e_inv
    for c in range(n_sub):
        rows = slice(c * L, (c + 1) * L)
        ar_s[c, 0:L, :] = a_t[rows]
        ar_s[c, L:2 * L, :] = r_t[rows]
        bk_s[c, 0:L, :] = b_t[rows]
        bk_s[c, L:2 * L, :] = k_t[rows]
        gam_s[c] = jnp.exp(cum[(c + 1) * L - 1:(c + 1) * L, :])
    v_s[...] = v
    bonus = head_sums([(r * k2 * rk_ref[...]).astype(BF16)]) * v

    pair = 2 * HEAD
    lane = lax.broadcasted_iota(jnp.int32, (1, pair), 1)
    left = lane < HEAD
    row1 = lax.broadcasted_iota(jnp.int32, (L, pair), 0)
    col1 = lax.broadcasted_iota(jnp.int32, (L, pair), 1) % HEAD
    strict = row1 > col1
    eye = jnp.where(row1 == col1, 1.0, 0.0)
    incl2 = (lax.broadcasted_iota(jnp.int32, (L, 2 * pair), 0)
             >= lax.broadcasted_iota(jnp.int32, (L, 2 * pair), 1) % HEAD)
    level_mask = [(row1 // (2 * s) == col1 // (2 * s)) & (row1 % (2 * s) >= s) & (col1 % (2 * s) < s)
                  for s in (1 << i for i in range(L.bit_length() - 1))]

    def blockdiag(t):
        tb = t.astype(BF16)
        return jnp.concatenate([jnp.where(left, tb, 0), jnp.where(left, 0, tb)], axis=0)

    pairs = range(n_heads // 2)
    psl = [slice(p * pair, (p + 1) * pair) for p in pairs]
    work = [(c, p) for c in range(n_sub) for p in pairs]
    ar = [ar_s[c, :, psl[p]].astype(BF16) for c, p in work]
    bk = [bk_s[c, :, psl[p]].astype(BF16) for c, p in work]
    vp = [v_s[c * L:(c + 1) * L, psl[p]] for c, p in work]
    gm = [_dot_nt(ar[i], jnp.concatenate([blockdiag(bk[i][0:L]), blockdiag(bk[i][L:2 * L])], axis=0))
          for i in range(len(work))]
    nm = [g_[0:L, 0:pair] for g_ in gm]
    mak = [jnp.where(strict, g_[0:L, pair:2 * pair], 0.0) for g_ in gm]
    q = [jnp.where(incl2, g_[L:2 * L, :], 0.0).astype(BF16) for g_ in gm]
    xinv = [eye - jnp.where(level_mask[0], n_, 0.0) for n_ in nm]
    for lm in level_mask[1:]:
        half = [_dot(jnp.where(lm, n_, 0.0), blockdiag(t)) for n_, t in zip(nm, xinv)]
        xinv = [t - _dot(t, blockdiag(h_)) for t, h_ in zip(xinv, half)]
    xinv = [t.astype(BF16) for t in xinv]
    mv = [_dot(m_, blockdiag(v_)) for m_, v_ in zip(mak, vp)]

    gates = _sigmoid(jnp.dot(xn, wgate_ref[...], preferred_element_type=F32))

    items = range(len(work))
    st = [state_s[i][...] for i in items]
    ars = [_dot_nt(ar[i], blockdiag(st[i])) for i in items]
    uu = [_dot(xinv[i], blockdiag(ars[i][0:L] + mv[i])) for i in items]
    yh = [ars[i][L:2 * L] + _dot(q[i], jnp.concatenate([blockdiag(-uu[i]), blockdiag(vp[i])], axis=0))
          for i in items]
    for i, (c, p) in enumerate(work):
        upd = _dot_tn(jnp.concatenate([-uu[i], vp[i]], axis=0), bk[i])
        state_s[i][...] = (st[i] + jnp.where(left, upd[0:HEAD], upd[HEAD:pair])) * gam_s[c, :, psl[p]]
    for i, (c, p) in enumerate(work):
        y_s[c * L:(c + 1) * L, psl[p]] = yh[i]

    y = y_s[...]
    yc = y - head_sums(split(y), exact=True) * (1.0 / HEAD)
    var = head_sums([(yc * yc).astype(BF16)]) * (1.0 / HEAD)
    y_rwkv = (yc * lax.rsqrt(var + GN_EPS) * lng_ref[...] + lnb_ref[...] + bonus) * g
    mix = gates[:, 0:D] * y_conv + gates[:, D:2 * D] * y_rwkv
    h = x + jnp.dot(mix.astype(BF16), wout_ref[...], preferred_element_type=F32)
    o_ref[...] = h.reshape(n_sub, L, D)
    hn = h * lax.rsqrt(jnp.mean(h * h, axis=-1, keepdims=True) + NORM_EPS) * gffn_ref[...]
    hp_ref[...] = _pack_bf16_pairs(hn).reshape(n_sub, L, D // 2)


def _const_spec(shape):
    nd = len(shape)
    return pl.BlockSpec(shape, lambda *_: (0,) * nd, pipeline_mode=pl.Buffered(1))


def _mixer(x, n_groups, norm_g, norm_ffn_g, w_in, conv_w, shift_mu, w0, w2, a0, a2, g2, k_k, k_a, r_k, ln_g, ln_b, w_out,
           expert_weights):
    bsz, s, d = x.shape
    gb = bsz // n_groups
    n_heads = d // HEAD
    L = RWKV_CHUNK
    n_sub = MIX_ROWS
    tc = n_sub * L
    assert gb % n_sub == 0 and s % L == 0
    lora0 = 6 * d
    w_main = w_in[:, 0:6 * d].astype(BF16)
    pad = lambda t, n: jnp.pad(t, ((0, 0), (0, n - t.shape[1])))
    lora_cols = (DECAY_LORA, AAA_LORA, GATE_LORA)
    lora_pads = (LANES, LANES, 2 * LANES)
    pieces_w, pieces_mu, off = [], [], lora0
    for n, p in zip(lora_cols, lora_pads):
        pieces_w.append(pad(w_in[:, off:off + n], p))
        pieces_mu.append(pad(shift_mu[None, off - 3 * d:off - 3 * d + n], p))
        off += n
    w_lora = jnp.concatenate(pieces_w, axis=1).astype(BF16)
    mu_lora = jnp.concatenate(pieces_mu, axis=1)
    w_gate = w_in[:, off:off + 2 * d].astype(BF16)
    mu_rkv = shift_mu[None, 0:3 * d]
    padr = lambda t, n: jnp.pad(t, ((0, n - t.shape[0]), (0, 0)))
    w2p = padr(w2, LANES).astype(BF16)
    a2p = padr(a2, LANES).astype(BF16)
    g2p = padr(g2, 2 * LANES).astype(BF16)
    row = lambda t: t.reshape(1, -1)
    head_of = jnp.arange(d, dtype=jnp.int32) // HEAD
    head_sel = (head_of[:, None] == jnp.arange(LANES)[None, :]).astype(BF16)
    consts = [row(norm_g), w_main, w_lora, w_gate, conv_w, mu_rkv, mu_lora, row(w0), w2p, row(a0), a2p,
              g2p, row(k_k), row(k_a), row(r_k), row(ln_g), row(ln_b), w_out.astype(BF16), head_sel, head_sel.T,
              row(norm_ffn_g)]
    assert len(consts) + 1 == N_MIXER_INPUTS
    n_c = s // L
    n_steps = (gb // n_sub) * n_c
    n_exp = expert_weights[0].shape[0]
    pieces = n_steps // n_exp
    cast_ok = (pieces * n_exp == n_steps
               and all(w.shape[1] % pieces == 0 and (w.shape[1] // pieces) % (2 * SUBLANES) == 0
                       for w in expert_weights))

    def call(first, cast):
        ew = expert_weights if cast else ()
        ew_specs = [pl.BlockSpec((1, w.shape[1] // pieces, w.shape[2]),
                                 lambda b, c: ((b * n_c + c) // pieces, (b * n_c + c) % pieces, 0)) for w in ew]
        return pl.pallas_call(
            functools.partial(_mixer_kernel, n_heads=n_heads, d_model=d, cast_weights=cast),
            grid=(gb // n_sub, n_c),
            in_specs=[pl.BlockSpec((n_sub, L, d), lambda b, c: (b + first // n_sub, c, 0))]
            + [_const_spec(t.shape) for t in consts] + ew_specs,
            out_specs=[pl.BlockSpec((n_sub, L, d), lambda b, c: (b, c, 0)),
                       pl.BlockSpec((n_sub, L, d // 2), lambda b, c: (b, c, 0))] + ew_specs,
            out_shape=[jax.ShapeDtypeStruct((gb, s, d), F32), jax.ShapeDtypeStruct((gb, s, d // 2), jnp.int32)]
            + [jax.ShapeDtypeStruct(w.shape, BF16) for w in ew],
            scratch_shapes=[
                pltpu.VMEM((n_sub * SUBLANES, d), F32),
                pltpu.VMEM((n_sub * SUBLANES, 3 * d), F32),
                pltpu.VMEM((n_sub * SUBLANES, 4 * LANES), F32),
                pltpu.VMEM((n_sub, 2 * L, d), F32),
                pltpu.VMEM((n_sub, 2 * L, d), F32),
                pltpu.VMEM((tc, d), F32),
                pltpu.VMEM((tc, d), F32),
                pltpu.VMEM((n_sub, 1, d), F32),
            ] + [pltpu.VMEM((HEAD, 2 * HEAD), F32)] * (n_sub * n_heads // 2),
            compiler_params=pltpu.CompilerParams(
                dimension_semantics=("arbitrary", "arbitrary"), vmem_limit_bytes=VMEM_LIMIT_BYTES),
            name="mixer",
        )(x, *consts, *ew)

    return [functools.partial(call, g * gb, cast_ok and g == 0) for g in range(n_groups)]


def _router_kernel(h_ref, g_ref, rwh_ref, rwl_ref, rb_ref, tri_ref, meta_ref, metat_ref, cnt_ref, run_s, *,
                   n_experts):
    tt = h_ref.shape[0]

    @pl.when(pl.program_id(0) == 0)
    def _():
        run_s[...] = jnp.zeros_like(run_s)

    h = h_ref[...]
    ms = jnp.mean(h * h, axis=-1, keepdims=True)
    hn = h * lax.rsqrt(ms + NORM_EPS) * g_ref[...]
    hn_hi = hn.astype(BF16)
    hn_lo = (hn - hn_hi.astype(F32)).astype(BF16)
    logits = (jnp.dot(hn_hi, rwh_ref[...], preferred_element_type=F32)
              + jnp.dot(hn_lo, rwh_ref[...], preferred_element_type=F32)
              + jnp.dot(hn_hi, rwl_ref[...], preferred_element_type=F32)) + rb_ref[...]
    lane = lax.broadcasted_iota(jnp.int32, (tt, LANES), 1)
    neg = jnp.float32(-jnp.inf)
    work = jnp.where(lane < n_experts, logits, neg)
    vals, idxs = [], []
    for _ in range(TOP_K):
        m = jnp.max(work, axis=-1, keepdims=True)
        i = jnp.min(jnp.where(work == m, lane, LANES), axis=-1, keepdims=True)
        vals.append(m)
        idxs.append(i)
        work = jnp.where(lane == i, neg, work)
    ex = [jnp.exp(vv - vals[0]) for vv in vals]
    den = ex[0] + ex[1] + ex[2] + ex[3]
    gates = [e / den for e in ex]

    onehot = jnp.zeros((tt, LANES), jnp.bool_)
    for kk in range(TOP_K):
        onehot = onehot | (lane == (idxs[kk] + kk * n_experts))
    oh = jnp.where(onehot, 1.0, 0.0)
    cnt = jnp.dot(tri_ref[...], oh.astype(BF16), preferred_element_type=F32)
    tot = jnp.broadcast_to(jnp.sum(oh, axis=0, keepdims=True), (SUBLANES, LANES))
    lane8 = lax.broadcasted_iota(jnp.int32, (SUBLANES, LANES), 1)
    pk = jnp.zeros_like(tot)
    te = tot
    for j in range(1, TOP_K):
        rolled = pltpu.roll(tot, j * n_experts, axis=1)
        pk = pk + jnp.where(lane8 >= j * n_experts, rolled, 0.0)
        te = te + rolled
    before = cnt + (run_s[...] + pk)[0:1, :]
    ranks = [jnp.sum(jnp.where(onehot & (lane // n_experts == kk), before, 0.0), axis=-1, keepdims=True)
             for kk in range(TOP_K)]
    run_s[...] = run_s[...] + te
    cnt_ref[...] = run_s[...]

    meta = jnp.zeros((tt, LANES), F32)
    for kk in range(TOP_K):
        meta = jnp.where(lane == kk, idxs[kk].astype(F32), meta)
        meta = jnp.where(lane == TOP_K + kk, gates[kk], meta)
        meta = jnp.where(lane == 2 * TOP_K + kk, ranks[kk], meta)
    meta_ref[...] = meta
    metat_ref[...] = meta.T[0:2 * SUBLANES, :]


def _router(h2, norm_g, router_w, router_b):
    t, d = h2.shape
    n_experts = router_w.shape[1]
    tt = min(ROUTE_TOKENS, t)
    rw = jnp.pad(router_w, ((0, 0), (0, LANES - n_experts)))
    rw_hi = rw.astype(BF16)
    rw_lo = (rw - rw_hi.astype(F32)).astype(BF16)
    rb = jnp.pad(router_b, (0, LANES - n_experts)).reshape(1, LANES)
    tok = jnp.arange(tt, dtype=jnp.int32)
    tri = (tok[:, None] > tok[None, :]).astype(BF16)
    kern = functools.partial(_router_kernel, n_experts=n_experts)
    return pl.pallas_call(
        kern,
        grid=(t // tt,),
        in_specs=[pl.BlockSpec((tt, d), lambda i: (i, 0)),
                  pl.BlockSpec((1, d), lambda i: (0, 0)),
                  pl.BlockSpec((d, LANES), lambda i: (0, 0)),
                  pl.BlockSpec((d, LANES), lambda i: (0, 0)),
                  pl.BlockSpec((1, LANES), lambda i: (0, 0)),
                  pl.BlockSpec((tt, tt), lambda i: (0, 0))],
        out_specs=[pl.BlockSpec((tt, LANES), lambda i: (i, 0)),
                   pl.BlockSpec((2 * SUBLANES, tt), lambda i: (0, i)),
                   pl.BlockSpec((SUBLANES, LANES), lambda i: (0, 0))],
        out_shape=[jax.ShapeDtypeStruct((t, LANES), F32),
                   jax.ShapeDtypeStruct((2 * SUBLANES, t), F32),
                   jax.ShapeDtypeStruct((SUBLANES, LANES), F32)],
        scratch_shapes=[pltpu.VMEM((SUBLANES, LANES), F32)],
        compiler_params=pltpu.CompilerParams(
            dimension_semantics=("arbitrary",), vmem_limit_bytes=VMEM_LIMIT_BYTES),
        name="router",
    )(h2, norm_g.reshape(1, d), rw_hi, rw_lo, rb, tri)


def _gather_scratch(per_worker, d, dtype):
    w = GATHER_WINDOW
    return [pltpu.VMEM((per_worker,), jnp.int32),
            pltpu.VMEM((w, d), dtype), pltpu.VMEM((w, d), dtype),
            pltpu.SemaphoreType.DMA, pltpu.SemaphoreType.DMA, pltpu.SemaphoreType.DMA, pltpu.SemaphoreType.DMA]


def _gather_rows(table_hbm, out_hbm, base, idx_v, buf_a, buf_b, gsem_a, gsem_b, psem_a, psem_b):
    w = GATHER_WINDOW
    n_win = idx_v.shape[0] // w

    def gather(j, buf, sem):
        return pltpu.make_async_copy(table_hbm.at[idx_v.at[pl.ds(j * w, w)]], buf, sem)

    def put(j, buf, sem):
        return pltpu.make_async_copy(buf, out_hbm.at[pl.ds(base + j * w, w)], sem)

    gather(0, buf_a, gsem_a).start()

    @pl.loop(0, n_win, step=2)
    def _(j):
        gather(j, buf_a, gsem_a).wait()

        @pl.when(j > 0)
        def _():
            put(j - 1, buf_b, psem_b).wait()

        gather(j + 1, buf_b, gsem_b).start()
        put(j, buf_a, psem_a).start()
        gather(j + 1, buf_b, gsem_b).wait()
        put(j, buf_a, psem_a).wait()

        @pl.when(j + 2 < n_win)
        def _():
            gather(j + 2, buf_a, gsem_a).start()

        put(j + 1, buf_b, psem_b).start()

    put(n_win - 1, buf_b, psem_b).wait()


def _worker_split(n_rows):
    sc = plsc.get_sparse_core_info()
    n_workers = sc.num_cores * sc.num_subcores
    per_worker = n_rows // n_workers
    n_win = per_worker // GATHER_WINDOW
    assert per_worker * n_workers == n_rows and n_win * GATHER_WINDOW == per_worker and n_win % 2 == 0
    return sc, per_worker


def _row_gather(table, idx):
    n_idx, d = idx.shape[0], table.shape[1]
    sc, per_worker = _worker_split(n_idx)
    mesh = plsc.VectorSubcoreMesh(core_axis_name="core", subcore_axis_name="subcore")

    @functools.partial(pl.kernel, out_type=jax.ShapeDtypeStruct((n_idx, d), table.dtype), mesh=mesh,
                       scratch_types=_gather_scratch(per_worker, d, table.dtype))
    def gather_kernel(table_hbm, idx_hbm, out_hbm, idx_v, *bufs):
        base = (lax.axis_index("subcore") * sc.num_cores + lax.axis_index("core")) * per_worker
        pltpu.sync_copy(idx_hbm.at[pl.ds(base, per_worker)], idx_v)
        _gather_rows(table_hbm, out_hbm, base, idx_v, *bufs)

    return gather_kernel(table, idx)


def _dispatch_gather(table, pos, n_slots):
    n, d = pos.shape[0], table.shape[1]
    sc, per_worker = _worker_split(n_slots)
    n_sub = sc.num_subcores
    rows = n // LANES
    rows_per_tile = rows // n_sub
    init_per_tile = n_slots // n_sub
    assert rows_per_tile * n_sub * LANES == n and init_per_tile * n_sub == n_slots
    mesh = plsc.VectorSubcoreMesh(core_axis_name="core", subcore_axis_name="subcore")
    tokens = (jnp.arange(n, dtype=jnp.int32) % (n // TOP_K)).reshape(rows, LANES)
    fill = jnp.arange(n_slots, dtype=jnp.int32) % (n // TOP_K)

    @functools.partial(
        pl.kernel, out_type=jax.ShapeDtypeStruct((n_slots, d), table.dtype), mesh=mesh,
        scratch_types=[pltpu.VMEM((rows_per_tile, LANES), jnp.int32),
                       pltpu.VMEM((rows_per_tile, LANES), jnp.int32),
                       pltpu.VMEM_SHARED((n_slots,), jnp.int32)] + _gather_scratch(per_worker, d, table.dtype))
    def dispatch_kernel(table_hbm, pos_hbm, tok_hbm, fill_hbm, out_hbm, pos_v, tok_v, slot_token, idx_v, *bufs):
        cid = lax.axis_index("core")
        sid = lax.axis_index("subcore")
        pltpu.sync_copy(fill_hbm.at[pl.ds(sid * init_per_tile, init_per_tile)],
                        slot_token.at[pl.ds(sid * init_per_tile, init_per_tile)])
        pltpu.sync_copy(pos_hbm.at[pl.ds(sid * rows_per_tile, rows_per_tile)], pos_v)
        pltpu.sync_copy(tok_hbm.at[pl.ds(sid * rows_per_tile, rows_per_tile)], tok_v)
        plsc.subcore_barrier()

        @pl.loop(0, rows_per_tile)
        def _(j):
            pltpu.sync_copy(tok_v.at[j], slot_token.at[pos_v.at[j]])

        plsc.subcore_barrier()
        base = (sid * sc.num_cores + cid) * per_worker
        pltpu.sync_copy(slot_token.at[pl.ds(base, per_worker)], idx_v)
        _gather_rows(table_hbm, out_hbm, base, idx_v, *bufs)

    return dispatch_kernel(table, pos.reshape(rows, LANES), tokens, fill)


def _experts_kernel(te_ref, nv_ref, xs_ref, w1_ref, b1_ref, w2_ref, b2_ref, ys_ref, *cast_refs, d_ff):
    i = pl.program_id(0)
    nvalid = nv_ref[i]
    if cast_refs:
        w1_use, w2_use = cast_refs
        prev = te_ref[jnp.maximum(i - 1, 0)]

        @pl.when((i == 0) | (te_ref[i] != prev))
        def _():
            w1_use[0] = w1_ref[0].astype(BF16)
            w2_use[0] = w2_ref[0].astype(BF16)
    else:
        w1_use, w2_use = w1_ref, w2_ref

    tm = xs_ref.shape[0]
    all_groups = [slice(r, r + EXPERT_ROWS) for r in range(0, tm, EXPERT_ROWS)]

    def ffn(groups):
        half = xs_ref.shape[1]
        xg = []
        for rows in groups:
            row = rows.start + lax.broadcasted_iota(jnp.int32, (EXPERT_ROWS, 1), 0)
            lo, hi = _unpack_bf16_pairs(jnp.where(row < nvalid, xs_ref[rows, :], 0))
            xg.append((lo.astype(BF16), hi.astype(BF16)))
        ug = [jnp.dot(lo, w1_use[0, 0:half, :], preferred_element_type=F32)
              + jnp.dot(hi, w1_use[0, half:2 * half, :], preferred_element_type=F32) + b1_ref[0] for lo, hi in xg]
        ag = []
        for u in ug:
            glu = jnp.minimum(u[:, 0:d_ff], SWIGLU_LIMIT)
            lin = jnp.clip(u[:, d_ff:2 * d_ff], -SWIGLU_LIMIT, SWIGLU_LIMIT)
            ag.append((glu * _sigmoid(SWIGLU_ALPHA * glu) * (lin + 1.0)).astype(BF16))
        for rows, act in zip(groups, ag):
            ys_ref[rows, :] = _pack_bf16_pairs(
                jnp.dot(act, w2_use[0], preferred_element_type=F32) + b2_ref[0])
        for rows in all_groups[len(groups):]:
            ys_ref[rows, :] = jnp.zeros((EXPERT_ROWS, ys_ref.shape[1]), ys_ref.dtype)

    for n_used in range(len(all_groups) + 1):
        lo_cnt, hi_cnt = (n_used - 1) * EXPERT_ROWS, n_used * EXPERT_ROWS
        pl.when((nvalid > lo_cnt) & (nvalid <= hi_cnt))(functools.partial(ffn, all_groups[:n_used]))


def _experts(xs, tile_expert, tile_valid, w1, b1, w2, b2):
    cast = w1.dtype != BF16
    n_slots = xs.shape[0]
    n_exp, d, two_ff = w1.shape
    d_ff = two_ff // 2
    tm = SLOT_TILE
    n_tiles = n_slots // tm
    kern = functools.partial(_experts_kernel, d_ff=d_ff)
    grid_spec = pltpu.PrefetchScalarGridSpec(
        num_scalar_prefetch=2,
        grid=(n_tiles,),
        in_specs=[pl.BlockSpec((tm, d // 2), lambda i, te, nv: (i, 0)),
                  pl.BlockSpec((1, d, two_ff), lambda i, te, nv: (te[i], 0, 0)),
                  pl.BlockSpec((1, 1, two_ff), lambda i, te, nv: (te[i], 0, 0)),
                  pl.BlockSpec((1, d_ff, d), lambda i, te, nv: (te[i], 0, 0)),
                  pl.BlockSpec((1, 1, d), lambda i, te, nv: (te[i], 0, 0))],
        out_specs=[pl.BlockSpec((tm, d // 2), lambda i, te, nv: (i, 0))]
        + ([pl.BlockSpec((1, d, two_ff), lambda i, te, nv: (te[i], 0, 0)),
            pl.BlockSpec((1, d_ff, d), lambda i, te, nv: (te[i], 0, 0))] if cast else []),
    )
    outs = pl.pallas_call(
        kern,
        grid_spec=grid_spec,
        out_shape=[jax.ShapeDtypeStruct((n_slots, d // 2), jnp.int32)]
        + ([jax.ShapeDtypeStruct(w1.shape, BF16), jax.ShapeDtypeStruct(w2.shape, BF16)] if cast else []),
        compiler_params=pltpu.CompilerParams(
            dimension_semantics=("arbitrary",), vmem_limit_bytes=VMEM_LIMIT_BYTES),
        name="experts",
    )(tile_expert, tile_valid, xs, w1, b1.reshape(n_exp, 1, two_ff), w2, b2.reshape(n_exp, 1, d))
    return outs if cast else (outs[0], w1, w2)


def _combine_kernel(h_ref, meta_ref, g_ref, *rest):
    y_refs, o_ref = rest[:TOP_K], rest[-1]
    d = h_ref.shape[1]
    half = d // 2
    acc_lo = h_ref[:, 0:half]
    acc_hi = h_ref[:, half:d]
    for kk in range(TOP_K):
        gate = meta_ref[:, TOP_K + kk:TOP_K + kk + 1]
        lo, hi = _unpack_bf16_pairs(y_refs[kk][...])
        acc_lo = acc_lo + gate * lo
        acc_hi = acc_hi + gate * hi
    ms = (jnp.sum(acc_lo * acc_lo, axis=-1, keepdims=True)
          + jnp.sum(acc_hi * acc_hi, axis=-1, keepdims=True)) * (1.0 / d)
    scale = lax.rsqrt(ms + NORM_EPS)
    o_ref[:, 0:half] = acc_lo * scale * g_ref[:, 0:half]
    o_ref[:, half:d] = acc_hi * scale * g_ref[:, half:d]


def _combine(h2, y4, meta, norm_g, out_prev, group, n_groups):
    t, d = h2.shape
    tt = min(MOVE_TOKENS, t)
    n_blk = t // tt
    prev_specs = [] if out_prev is None else [pl.BlockSpec(memory_space=pl.ANY)]
    prev_args = [] if out_prev is None else [out_prev]
    n_in = 3 + TOP_K
    y_specs = [pl.BlockSpec((tt, d // 2), functools.partial(lambda i, kk: (kk * n_blk + i, 0), kk=kk))
               for kk in range(TOP_K)]
    return pl.pallas_call(
        _combine_kernel,
        grid=(n_blk,),
        in_specs=[pl.BlockSpec((tt, d), lambda i: (i, 0)),
                  pl.BlockSpec((tt, LANES), lambda i: (i, 0)),
                  pl.BlockSpec((1, d), lambda i: (0, 0))] + y_specs + prev_specs,
        out_specs=pl.BlockSpec((tt, d), lambda i: (group * n_blk + i, 0)),
        out_shape=jax.ShapeDtypeStruct((n_groups * t, d), F32),
        input_output_aliases={} if out_prev is None else {n_in: 0},
        compiler_params=pltpu.CompilerParams(
            dimension_semantics=("arbitrary",), vmem_limit_bytes=VMEM_LIMIT_BYTES),
        name="combine",
    )(h2, meta, norm_g.reshape(1, d), *([y4] * TOP_K), *prev_args)


def _slots_kernel(seg_ref, metat_ref, pos_ref, te_ref, tv_ref, *, n_experts, tile):
    eidx = metat_ref[0:TOP_K, :].astype(jnp.int32)
    pos = metat_ref[2 * TOP_K:3 * TOP_K, :].astype(jnp.int32)
    for e in range(n_experts):
        pos = pos + jnp.where(eidx == e, seg_ref[e], 0)
    pos_ref[...] = pos

    @pl.when(pl.program_id(0) == 0)
    def _():
        first = lax.broadcasted_iota(jnp.int32, te_ref.shape, 1) * tile
        expert = jnp.zeros(te_ref.shape, jnp.int32)
        for e in range(n_experts):
            expert = expert + jnp.where(first >= seg_ref[n_experts + e], 1, 0)
        expert = jnp.minimum(expert, n_experts - 1)
        used_end = jnp.zeros(te_ref.shape, jnp.int32)
        for e in range(n_experts):
            used_end = used_end + jnp.where(expert == e, seg_ref[e] + seg_ref[2 * n_experts + e], 0)
        te_ref[...] = expert
        tv_ref[...] = jnp.clip(used_end - first, 0, tile)


def _slots(metat, seg_start, seg_end, cnt, n_tiles):
    t = metat.shape[1]
    n_exp = seg_start.shape[0]
    tt = min(SLOT_TOKENS, t)
    nt_pad = -(-n_tiles // LANES) * LANES
    grid_spec = pltpu.PrefetchScalarGridSpec(
        num_scalar_prefetch=1,
        grid=(t // tt,),
        in_specs=[pl.BlockSpec((2 * SUBLANES, tt), lambda i, seg: (0, i))],
        out_specs=[pl.BlockSpec((TOP_K, tt), lambda i, seg: (0, i)),
                   pl.BlockSpec((SUBLANES, nt_pad), lambda i, seg: (0, 0)),
                   pl.BlockSpec((SUBLANES, nt_pad), lambda i, seg: (0, 0))],
    )
    pos, tile_expert, tile_valid = pl.pallas_call(
        functools.partial(_slots_kernel, n_experts=n_exp, tile=SLOT_TILE),
        grid_spec=grid_spec,
        out_shape=[jax.ShapeDtypeStruct((TOP_K, t), jnp.int32),
                   jax.ShapeDtypeStruct((SUBLANES, nt_pad), jnp.int32),
                   jax.ShapeDtypeStruct((SUBLANES, nt_pad), jnp.int32)],
        compiler_params=pltpu.CompilerParams(dimension_semantics=("arbitrary",)),
        name="slots",
    )(jnp.concatenate([seg_start, seg_end, cnt]), metat)
    return pos.reshape(-1), tile_expert[0, 0:n_tiles], tile_valid[0, 0:n_tiles]


def _route(h2, hp2, norm_ffn_g, router_w, router_b):
    t, d = h2.shape
    n_exp = router_w.shape[1]
    tm = SLOT_TILE
    meta, metat, counts = _router(h2, norm_ffn_g, router_w, router_b)
    cnt = counts[0, 0:n_exp].astype(jnp.int32)
    padded = jnp.maximum((cnt + tm - 1) // tm, 1) * tm
    seg_end = jnp.cumsum(padded)
    seg_start = seg_end - padded
    n_tiles = -(-(t * TOP_K) // tm) + n_exp
    n_slots = n_tiles * tm
    pos, tile_expert, tile_valid = _slots(metat, seg_start, seg_end, cnt, n_tiles)
    xs = _dispatch_gather(hp2, pos, n_slots)
    return dict(h2=h2, meta=meta, pos=pos, xs=xs, tile_expert=tile_expert, tile_valid=tile_valid)


def kernel(x, norm_mix_g, w_in, conv_w, shift_mu, decay_w0, decay_w2, iclr_a0, iclr_a2, gate_g2, k_k, k_a,
           r_k, ln_x_g, ln_x_b, w_out, norm_ffn_g, router_w, router_b, exp_w1, exp_b1, exp_w2, exp_b2,
           norm_final_g):
    bsz, s, d = x.shape
    depth = w_in.shape[0]
    assert depth == 1, "final norm is fused into the last layer's combine kernel"
    n_groups = BATCH_GROUPS if bsz % (BATCH_GROUPS * MIX_ROWS) == 0 else 1
    mixers = _mixer(x, n_groups, norm_mix_g[0], norm_ffn_g[0], w_in[0], conv_w[0], shift_mu[0], decay_w0[0],
                    decay_w2[0], iclr_a0[0], iclr_a2[0], gate_g2[0], k_k[0], k_a[0], r_k[0], ln_x_g[0], ln_x_b[0],
                    w_out[0], (exp_w1[0], exp_w2[0]))
    w1, w2 = exp_w1[0], exp_w2[0]
    routed = []
    for run_mixer in mixers:
        h, hp, *cast = run_mixer()
        if cast:
            w1, w2 = cast
        routed.append(_route(h.reshape(-1, d), hp.reshape(-1, d // 2), norm_ffn_g[0], router_w[0], router_b[0]))
    gathered = []
    for r in routed:
        ys, w1, w2 = _experts(r["xs"], r["tile_expert"], r["tile_valid"], w1, exp_b1[0], w2, exp_b2[0])
        gathered.append(_row_gather(ys, r["pos"]))
    out = None
    for g, (r, y4) in enumerate(zip(routed, gathered)):
        out = _combine(r["h2"], y4, r["meta"], norm_final_g, out, g, n_groups)
    return out.reshape(bsz, s, d)
```
